```python
import math
import jax, jax.numpy as jnp
from jax import lax
import numpy as np

D_MODEL = 1024
BATCH = 32
SEQ = 2048
DEPTH = 2

GRID_W = 64
CTX_LEN = 256
N_EVEN = (DEPTH + 1) // 2
N_ODD = DEPTH // 2

MLA_HEADS = 8
MLA_NOPE = 64
MLA_ROPE = 32
MLA_V = 64
MLA_Q_RANK = 384
MLA_KV_RANK = 256
DIFF_HEADS = 4
DIFF_HEAD_DIM = 64
DIFF_V_DIM = 2 * DIFF_HEAD_DIM
WIN_Q_HEADS = 8
WIN_KV_HEADS = 2
WIN_GROUP = WIN_Q_HEADS // WIN_KV_HEADS
WIN_HEAD_DIM = 64
WINDOW = 128
BLOCK = 128
HYENA_CH = 512
HYENA_ORDER = 2
HYENA_BANDS = 16
HYENA_EMB = 1 + 2 * HYENA_BANDS
HYENA_HIDDEN = 64
HYENA_DECAY_TARGET = 1e-2
HYENA_FAST_PCT = 0.3
HYENA_SLOW_PCT = 1.5
D_FF = 2816

ROPE_BASE = 10000.0
NORM_EPS = 1e-6
NEG_INF = -1e30
Q_BLOCK = 128

EVEN_SPLIT = (MLA_Q_RANK, MLA_KV_RANK, MLA_ROPE, DIFF_HEADS * 2 * DIFF_HEAD_DIM, DIFF_HEADS * 2 * DIFF_HEAD_DIM, DIFF_HEADS * DIFF_V_DIM)
ODD_SPLIT = (WIN_Q_HEADS * WIN_HEAD_DIM, WIN_KV_HEADS * WIN_HEAD_DIM, WIN_KV_HEADS * WIN_HEAD_DIM, (HYENA_ORDER + 1) * HYENA_CH)
EVEN_IN = sum(EVEN_SPLIT)
ODD_IN = sum(ODD_SPLIT)
MIX_WIDTH = MLA_HEADS * MLA_V + DIFF_HEADS * DIFF_V_DIM

kernel_name = "hybrid_mla_diff_swa_hyena_prefix_block"


def split_cols(t, widths):
    return jnp.split(t, [int(i) for i in np.cumsum(widths)[:-1]], axis=-1)


def rms_norm(x, g):
    x32 = x.astype(jnp.float32)
    y = x32 * lax.rsqrt(jnp.mean(x32 * x32, axis=-1, keepdims=True) + NORM_EPS)
    return (y * g.astype(jnp.float32)).astype(x.dtype)


def modulate(h, shift, scale):
    return h * (1.0 + scale) + shift


def dwconv3(x, w, b):
    xp = jnp.pad(x, ((0, 0), (1, 1), (0, 0)))
    return xp[:, :-2] * w[0] + xp[:, 1:-1] * w[1] + xp[:, 2:] * w[2] + b


def axial_rope(length, rot_dim):
    rows = length // GRID_W
    row = jnp.repeat(jnp.arange(rows), GRID_W).astype(jnp.float32)
    col = jnp.tile(jnp.arange(GRID_W), rows).astype(jnp.float32)
    quarter = rot_dim // 4
    inv = ROPE_BASE ** (-jnp.arange(quarter, dtype=jnp.float32) / quarter)
    ang = jnp.concatenate([row[:, None] * inv, col[:, None] * inv], axis=-1)
    return jnp.cos(ang), jnp.sin(ang)


def apply_rope(x, cos, sin):
    half = x.shape[-1] // 2
    extra = x.ndim - 3
    cos = cos.reshape(cos.shape[0], *([1] * extra), half).astype(x.dtype)
    sin = sin.reshape(sin.shape[0], *([1] * extra), half).astype(x.dtype)
    x1, x2 = x[..., :half], x[..., half:]
    return jnp.concatenate([x1 * cos - x2 * sin, x1 * sin + x2 * cos], axis=-1)


def attend(q, k, v, scale):
    s = jnp.einsum("bqhd,bkhd->bhqk", q, k).astype(jnp.float32) * scale
    p = jax.nn.softmax(s, axis=-1).astype(v.dtype)
    return jnp.einsum("bhqk,bkhd->bqhd", p, v)


def map_query_blocks(fn, *qs):
    b, length = qs[0].shape[:2]
    nb = length // Q_BLOCK
    blocks = tuple(jnp.swapaxes(q.reshape(b, nb, Q_BLOCK, *q.shape[2:]), 0, 1) for q in qs)
    out = lax.map(lambda args: fn(*args), blocks)
    return jnp.swapaxes(out, 0, 1).reshape(b, length, *out.shape[3:])


def gqa_sink_attend(q, k, v, sink, mask):
    s = jnp.einsum("bqhgd,bkhd->bhgqk", q, k).astype(jnp.float32) * (WIN_HEAD_DIM ** -0.5)
    if mask is not None:
        s = jnp.where(mask, s, NEG_INF)
    sink_col = jnp.broadcast_to(sink.astype(jnp.float32)[None, :, :, None, None], s.shape[:-1] + (1,))
    p = jax.nn.softmax(jnp.concatenate([s, sink_col], axis=-1), axis=-1)[..., :-1].astype(v.dtype)
    return jnp.einsum("bhgqk,bkhd->bqhgd", p, v)


def window_attention(q, k, v, k_ctx, v_ctx, sink):
    b, length = q.shape[:2]
    nb = length // BLOCK
    n_ctx = k_ctx.shape[1]

    def to_blocks(t):
        return jnp.swapaxes(t.reshape(b, nb, BLOCK, *t.shape[2:]), 0, 1)

    def windows(t):
        tp = jnp.pad(t, ((0, 0), (BLOCK, BLOCK), (0, 0), (0, 0))).reshape(b, nb + 2, BLOCK, *t.shape[2:])
        w = jnp.concatenate([tp[:, :-2], tp[:, 1:-1], tp[:, 2:]], axis=2)
        return jnp.swapaxes(w, 0, 1)

    q_pos = jnp.arange(length).reshape(nb, BLOCK)
    k_pos = (jnp.arange(nb)[:, None] - 1) * BLOCK + jnp.arange(3 * BLOCK)[None, :]
    kp = k_pos[:, None, :]
    in_win = (jnp.abs(q_pos[:, :, None] - kp) <= WINDOW) & (kp >= 0) & (kp < length)
    mask = jnp.concatenate([jnp.ones((nb, BLOCK, n_ctx), bool), in_win], axis=-1)

    def block_fn(args):
        qb, kb, vb, mb = args
        kk = jnp.concatenate([k_ctx, kb], axis=1)
        vv = jnp.concatenate([v_ctx, vb], axis=1)
        return gqa_sink_attend(qb, kk, vv, sink, mb)

    out = lax.map(block_fn, (to_blocks(q), windows(k), windows(v), mask))
    return jnp.swapaxes(out, 0, 1).reshape(b, length, *q.shape[2:])


def hyena_filters(length, w1, b1, w2, b2, w3, b3, freq):
    t = jnp.arange(length, dtype=jnp.float32)
    tn = t / max(length - 1, 1)
    bands = jnp.linspace(1e-4, HYENA_BANDS - 1, HYENA_BANDS, dtype=jnp.float32)
    ang = 2.0 * math.pi * bands[None, :] * t[:, None] / length
    feats = jnp.concatenate([tn[:, None], jnp.cos(ang), -jnp.sin(ang)], axis=-1).astype(w1.dtype)
    hid = jnp.sin(freq * (feats @ w1 + b1))
    hid = jnp.sin(freq * (hid @ w2 + b2))
    h = hid @ w3 + b3
    min_decay = math.log(HYENA_DECAY_TARGET) / HYENA_SLOW_PCT
    max_decay = math.log(HYENA_DECAY_TARGET) / HYENA_FAST_PCT
    deltas = jnp.abs(jnp.linspace(min_decay, max_decay, HYENA_CH, dtype=jnp.float32))
    decay = jnp.exp(-tn[:, None] * deltas[None, :]).astype(h.dtype)
    return h.reshape(length, 2, HYENA_ORDER, HYENA_CH) * decay[:, None, None, :]


def bidir_long_conv(z, h_fwd, h_bwd):
    length, ch = z.shape[1], z.shape[2]
    n = 2 * length
    taps = jnp.concatenate([h_fwd, jnp.zeros((1, ch), h_fwd.dtype), h_bwd[1:][::-1]], axis=0).astype(jnp.float32)
    zf = jnp.fft.rfft(z.astype(jnp.float32), n=n, axis=1)
    tf = jnp.fft.rfft(taps, n=n, axis=0)
    y = jnp.fft.irfft(zf * tf[None], n=n, axis=1)[:, :length]
    return y.astype(z.dtype)


def hyena_mix(u, conv_w, conv_b, filt, bias):
    parts = jnp.split(dwconv3(u, conv_w, conv_b), HYENA_ORDER + 1, axis=-1)
    z = parts[0]
    for n in range(HYENA_ORDER):
        z = parts[n + 1] * (bidir_long_conv(z, filt[:, 0, n], filt[:, 1, n]) + bias[n] * z)
    return z


def conv_ffn(h, w_up, conv_w, conv_b, w_down):
    u = dwconv3(h @ w_up, conv_w, conv_b)
    a, g = jnp.split(u, 2, axis=-1)
    return (jax.nn.silu(g) * a) @ w_down


def even_mixer(h_lat, h_ctx, need_ctx, lambda_init, ropes, w_in, q_norm_g, w_uq, kv_norm_g, w_ukv, diff_lambda, diff_subln_g, w_out):
    def project(h, rope):
        b, length = h.shape[:2]
        cq, ckv, k_rope, dq, dk, dv = split_cols(h @ w_in, EVEN_SPLIT)
        q = (rms_norm(cq, q_norm_g) @ w_uq).reshape(b, length, MLA_HEADS, MLA_NOPE + MLA_ROPE)
        kv = (rms_norm(ckv, kv_norm_g) @ w_ukv).reshape(b, length, MLA_HEADS, MLA_NOPE + MLA_V)
        q_nope, q_rope = q[..., :MLA_NOPE], q[..., MLA_NOPE:]
        k_nope, v_mla = kv[..., :MLA_NOPE], kv[..., MLA_NOPE:]
        dq = dq.reshape(b, length, DIFF_HEADS, 2, DIFF_HEAD_DIM)
        dk = dk.reshape(b, length, DIFF_HEADS, 2, DIFF_HEAD_DIM)
        dv = dv.reshape(b, length, DIFF_HEADS, DIFF_V_DIM)
        if rope is not None:
            (cos_m, sin_m), (cos_d, sin_d) = rope
            q_rope = apply_rope(q_rope, cos_m, sin_m)
            k_rope = apply_rope(k_rope, cos_m, sin_m)
            dq = apply_rope(dq, cos_d, sin_d)
            dk = apply_rope(dk, cos_d, sin_d)
        k_rope = jnp.broadcast_to(k_rope[:, :, None, :], (b, length, MLA_HEADS, MLA_ROPE))
        q_mla = jnp.concatenate([q_nope, q_rope], axis=-1)
        k_mla = jnp.concatenate([k_nope, k_rope], axis=-1)
        return (q_mla, k_mla, v_mla, dq[:, :, :, 0], dq[:, :, :, 1], dk[:, :, :, 0], dk[:, :, :, 1], dv)

    lq1, lk1, lq2, lk2 = diff_lambda.astype(jnp.float32)
    lam = jnp.exp(jnp.sum(lq1 * lk1)) - jnp.exp(jnp.sum(lq2 * lk2)) + lambda_init
    mla_scale = (MLA_NOPE + MLA_ROPE) ** -0.5
    diff_scale = DIFF_HEAD_DIM ** -0.5

    def diff_attend(q1, q2, k1, k2, v):
        return attend(q1, k1, v, diff_scale) - lam.astype(v.dtype) * attend(q2, k2, v, diff_scale)

    def merge(o_mla, o_diff):
        b, length = o_mla.shape[:2]
        o_diff = rms_norm(o_diff, diff_subln_g) * (1.0 - lambda_init)
        return jnp.concatenate([o_mla.reshape(b, length, -1), o_diff.reshape(b, length, -1)], axis=-1) @ w_out

    cq_, ck_, cv_, cdq1, cdq2, cdk1, cdk2, cdv = project(h_ctx, None)
    lq_, lk_, lv_, ldq1, ldq2, ldk1, ldk2, ldv = project(h_lat, ropes)

    def cat(a, b):
        return jnp.concatenate([a, b], axis=1)

    k_all, v_all = cat(ck_, lk_), cat(cv_, lv_)
    dk1_all, dk2_all, dv_all = cat(cdk1, ldk1), cat(cdk2, ldk2), cat(cdv, ldv)
    o_mla = map_query_blocks(lambda qb: attend(qb, k_all, v_all, mla_scale), lq_)
    o_diff = map_query_blocks(lambda q1b, q2b: diff_attend(q1b, q2b, dk1_all, dk2_all, dv_all), ldq1, ldq2)
    y_lat = merge(o_mla, o_diff)
    y_ctx = None
    if need_ctx:
        y_ctx = merge(attend(cq_, ck_, cv_, mla_scale), diff_attend(cdq1, cdq2, cdk1, cdk2, cdv))
    return y_lat, y_ctx


def odd_mixer(h_lat, h_ctx, need_ctx, rope, w_in, sink, conv_w, conv_b, f_w1, f_b1, f_w2, f_b2, f_w3, f_b3, f_freq, hy_bias, w_out):
    sink = sink.reshape(WIN_KV_HEADS, WIN_GROUP)

    def project(h, rope_tab):
        b, length = h.shape[:2]
        q, k, v, u = split_cols(h @ w_in, ODD_SPLIT)
        q = q.reshape(b, length, WIN_KV_HEADS, WIN_GROUP, WIN_HEAD_DIM)
        k = k.reshape(b, length, WIN_KV_HEADS, WIN_HEAD_DIM)
        v = v.reshape(b, length, WIN_KV_HEADS, WIN_HEAD_DIM)
        if rope_tab is not None:
            q = apply_rope(q, *rope_tab)
            k = apply_rope(k, *rope_tab)
        return q, k, v, u

    def hyena(u):
        filt = hyena_filters(u.shape[1], f_w1, f_b1, f_w2, f_b2, f_w3, f_b3, f_freq)
        return hyena_mix(u, conv_w, conv_b, filt, hy_bias)

    def merge(o_win, o_hy):
        b, length = o_win.shape[:2]
        return jnp.concatenate([o_win.reshape(b, length, -1), o_hy], axis=-1) @ w_out

    cq, ck, cv, cu = project(h_ctx, None)
    lq, lk, lv, lu = project(h_lat, rope)
    y_lat = merge(window_attention(lq, lk, lv, ck, cv, sink), hyena(lu))
    y_ctx = None
    if need_ctx:
        y_ctx = merge(gqa_sink_attend(cq, ck, cv, sink, None), hyena(cu))
    return y_lat, y_ctx


def setup_inputs(seed: int = 0) -> dict:
    key = jax.random.key(seed)
    ks = iter(jax.random.split(key, 40))

    def nrm(shape, scale):
        return jax.random.normal(next(ks), shape, jnp.float32) * scale

    def gain(shape):
        return 1.0 + nrm(shape, 0.05)

    return {
        "x": nrm((BATCH, SEQ, D_MODEL), 1.0),
        "c": nrm((BATCH, D_MODEL), 1.0),
        "ctx": nrm((BATCH, CTX_LEN, D_MODEL), 1.0),
        "c_ctx": nrm((D_MODEL,), 1.0),
        "ada_w": nrm((DEPTH, D_MODEL, 6 * D_MODEL), 0.5 * D_MODEL ** -0.5),
        "ada_b": nrm((DEPTH, 6 * D_MODEL), 0.02),
        "norm_g": gain((DEPTH, 4, D_MODEL)),
        "mix_w_out": nrm((DEPTH, MIX_WIDTH, D_MODEL), MIX_WIDTH ** -0.5),
        "ffn_w_up": nrm((DEPTH, D_MODEL, 2 * D_FF), D_MODEL ** -0.5),
        "ffn_conv_w": nrm((DEPTH, 3, 2 * D_FF), 3 ** -0.5),
        "ffn_conv_b": nrm((DEPTH, 2 * D_FF), 0.02),
        "ffn_w_down": nrm((DEPTH, D_FF, D_MODEL), D_FF ** -0.5),
        "even_w_in": nrm((N_EVEN, D_MODEL, EVEN_IN), D_MODEL ** -0.5),
        "mla_q_norm_g": gain((N_EVEN, MLA_Q_RANK)),
        "mla_w_uq": nrm((N_EVEN, MLA_Q_RANK, MLA_HEADS * (MLA_NOPE + MLA_ROPE)), MLA_Q_RANK ** -0.5),
        "mla_kv_norm_g": gain((N_EVEN, MLA_KV_RANK)),
        "mla_w_ukv": nrm((N_EVEN, MLA_KV_RANK, MLA_HEADS * (MLA_NOPE + MLA_V)), MLA_KV_RANK ** -0.5),
        "diff_lambda": nrm((N_EVEN, 4, DIFF_HEAD_DIM), 0.1),
        "diff_subln_g": gain((N_EVEN, DIFF_V_DIM)),
        "odd_w_in": nrm((N_ODD, D_MODEL, ODD_IN), D_MODEL ** -0.5),
        "win_sink": nrm((N_ODD, WIN_Q_HEADS), 0.5),
        "hy_conv_w": nrm((N_ODD, 3, (HYENA_ORDER + 1) * HYENA_CH), 3 ** -0.5),
        "hy_conv_b": nrm((N_ODD, (HYENA_ORDER + 1) * HYENA_CH), 0.02),
        "hy_f_w1": nrm((N_ODD, HYENA_EMB, HYENA_HIDDEN), HYENA_EMB ** -0.5),
        "hy_f_b1": nrm((N_ODD, HYENA_HIDDEN), 0.1),
        "hy_f_w2": nrm((N_ODD, HYENA_HIDDEN, HYENA_HIDDEN), HYENA_HIDDEN ** -0.5),
        "hy_f_b2": nrm((N_ODD, HYENA_HIDDEN), 0.1),
        "hy_f_w3": nrm((N_ODD, HYENA_HIDDEN, 2 * HYENA_ORDER * HYENA_CH), 0.1 * HYENA_HIDDEN ** -0.5),
        "hy_f_b3": nrm((N_ODD, 2 * HYENA_ORDER * HYENA_CH), 0.01),
        "hy_f_freq": 1.0 + nrm((N_ODD, HYENA_HIDDEN), 0.1),
        "hy_bias": nrm((N_ODD, HYENA_ORDER, HYENA_CH), 0.5),
    }


def reference(x, c, ctx, c_ctx, ada_w, ada_b, norm_g, mix_w_out, ffn_w_up, ffn_conv_w, ffn_conv_b, ffn_w_down, even_w_in, mla_q_norm_g, mla_w_uq, mla_kv_norm_g, mla_w_ukv, diff_lambda, diff_subln_g, odd_w_in, win_sink, hy_conv_w, hy_conv_b, hy_f_w1, hy_f_b1, hy_f_w2, hy_f_b2, hy_f_w3, hy_f_b3, hy_f_freq, hy_bias):
    seq = x.shape[1]
    rope_mla = axial_rope(seq, MLA_ROPE)
    rope_diff = axial_rope(seq, DIFF_HEAD_DIM)
    rope_win = axial_rope(seq, WIN_HEAD_DIM)
    sc = jax.nn.silu(c)
    sc_ctx = jax.nn.silu(c_ctx)
    xl, xc = x, ctx
    for layer in range(DEPTH):
        need_ctx = layer < DEPTH - 1
        mod_l = jnp.split((sc @ ada_w[layer] + ada_b[layer])[:, None, :], 6, axis=-1)
        mod_c = jnp.split((sc_ctx @ ada_w[layer] + ada_b[layer])[None, None, :], 6, axis=-1)
        g = norm_g[layer]
        h_l = modulate(rms_norm(xl, g[0]), mod_l[0], mod_l[1])
        h_c = modulate(rms_norm(xc, g[0]), mod_c[0], mod_c[1])
        i = layer // 2
        if layer % 2 == 0:
            lambda_init = 0.8 - 0.6 * math.exp(-0.3 * layer)
            y_l, y_c = even_mixer(h_l, h_c, need_ctx, lambda_init, (rope_mla, rope_diff), even_w_in[i], mla_q_norm_g[i], mla_w_uq[i], mla_kv_norm_g[i], mla_w_ukv[i], diff_lambda[i], diff_subln_g[i], mix_w_out[layer])
        else:
            y_l, y_c = odd_mixer(h_l, h_c, need_ctx, rope_win, odd_w_in[i], win_sink[i], hy_conv_w[i], hy_conv_b[i], hy_f_w1[i], hy_f_b1[i], hy_f_w2[i], hy_f_b2[i], hy_f_w3[i], hy_f_b3[i], hy_f_freq[i], hy_bias[i], mix_w_out[layer])
        ffn_args = (ffn_w_up[layer], ffn_conv_w[layer], ffn_conv_b[layer], ffn_w_down[layer])
        xl = xl + mod_l[2] * rms_norm(y_l, g[1])
        h_l = modulate(rms_norm(xl, g[2]), mod_l[3], mod_l[4])
        xl = xl + mod_l[5] * rms_norm(conv_ffn(h_l, *ffn_args), g[3])
        if need_ctx:
            xc = xc + mod_c[2] * rms_norm(y_c, g[1])
            h_c2 = modulate(rms_norm(xc, g[2]), mod_c[3], mod_c[4])
            xc = xc + mod_c[5] * rms_norm(conv_ffn(h_c2, *ffn_args), g[3])
    return xl
```

```python
import functools
import math

import numpy as np
import jax
import jax.numpy as jnp
from jax import lax
from jax.experimental import pallas as pl
from jax.experimental.pallas import tpu as pltpu

D_MODEL = 1024
GRID_W = 64
MLA_HEADS = 8
MLA_NOPE = 64
MLA_ROPE = 32
MLA_V = 64
MLA_Q_RANK = 384
MLA_KV_RANK = 256
DIFF_HEADS = 4
DIFF_HEAD_DIM = 64
DIFF_V_DIM = 128
WIN_Q_HEADS = 8
WIN_KV_HEADS = 2
WIN_HEAD_DIM = 64
WINDOW = 128
HYENA_CH = 512
HYENA_ORDER = 2
HYENA_BANDS = 16
HYENA_DECAY_TARGET = 1e-2
HYENA_FAST_PCT = 0.3
HYENA_SLOW_PCT = 1.5
D_FF = 2816
ROPE_BASE = 10000.0
NORM_EPS = 1e-6
NEG_INF = -1e30

LANES = 128
HALO_ROWS = 16
ROW_TILE = 256
FFN_CHUNK = 256
FREQ_TILE = 256
VMEM_LIMIT = 48 * 1024 * 1024

BF16 = jnp.bfloat16
F32 = jnp.float32


def _params(*semantics):
    return pltpu.CompilerParams(dimension_semantics=semantics, vmem_limit_bytes=VMEM_LIMIT)


def _rms(x, g):
    return x * lax.rsqrt(jnp.mean(x * x, axis=-1, keepdims=True) + NORM_EPS) * g


def _dot(a, b):
    return jnp.dot(a, b, preferred_element_type=F32)


def _dot_nt(a, b):
    return lax.dot_general(a, b, (((1,), (1,)), ((), ())), preferred_element_type=F32)


def _lane_iota(shape):
    return lax.broadcasted_iota(jnp.int32, shape, len(shape) - 1)


def _rope_chunk(x, cos, sin_signed, half):
    first = (_lane_iota(x.shape) % (2 * half)) < half
    partner = jnp.where(first, pltpu.roll(x, LANES - half, 1), pltpu.roll(x, half, 1))
    return x * cos + partner * sin_signed


def _shift_rows(u, prev_row, next_row):
    rows = u.shape[0]
    r = lax.broadcasted_iota(jnp.int32, u.shape, 0)
    down = jnp.where(r == 0, prev_row, pltpu.roll(u, 1, 0))
    up = jnp.where(r == rows - 1, next_row, pltpu.roll(u, rows - 1, 0))
    return down, up


def _ada_kernel(c_ref, w_ref, b_ref, o_ref):
    c = c_ref[...]
    sc = c * jax.nn.sigmoid(c)
    o_ref[0] = _dot(sc.astype(BF16), w_ref[0].astype(BF16)) + b_ref[0]


def _ada_table(cond, ada_w, ada_b):
    depth, d, n = ada_w.shape
    rows = cond.shape[0]
    tn = 1536
    return pl.pallas_call(
        _ada_kernel,
        grid=(depth, n // tn),
        in_specs=[
            pl.BlockSpec((rows, d), lambda l, j: (0, 0)),
            pl.BlockSpec((1, d, tn), lambda l, j: (l, 0, j)),
            pl.BlockSpec((1, 1, tn), lambda l, j: (l, 0, j)),
        ],
        out_specs=pl.BlockSpec((1, rows, tn), lambda l, j: (l, 0, j)),
        out_shape=jax.ShapeDtypeStruct((depth, rows, n), F32),
        compiler_params=_params("arbitrary", "arbitrary"),
        name="ada_table",
    )(cond, ada_w, ada_b.reshape(depth, 1, n))


def _proj_even_kernel(x_ref, mod_ref, g_ref, win_ref, qg_ref, wuq_ref, kvg_ref, wukv_ref,
                      cm_ref, sm_ref, cd_ref, sd_ref,
                      qm_ref, km_ref, vm_ref, dq_ref, dk_ref, dv_ref):
    x = x_ref[0]
    mod = mod_ref[0]
    h = _rms(x, g_ref[...]) * (1.0 + mod[1:2]) + mod[0:1]
    p = _dot(h.astype(BF16), win_ref[...])
    o = 0
    cq = p[:, o:o + MLA_Q_RANK]; o += MLA_Q_RANK
    ckv = p[:, o:o + MLA_KV_RANK]; o += MLA_KV_RANK
    kr = p[:, o:o + LANES]; o += LANES
    dq = p[:, o:o + 512]; o += 512
    dk = p[:, o:o + 512]; o += 512
    dv = p[:, o:o + 512]
    q = _dot(_rms(cq, qg_ref[...]).astype(BF16), wuq_ref[...])
    kv = _dot(_rms(ckv, kvg_ref[...]).astype(BF16), wukv_ref[...])
    cm, sm, cd, sd = cm_ref[...], sm_ref[...], cd_ref[...], sd_ref[...]
    mla_scale = (MLA_NOPE + MLA_ROPE) ** -0.5
    diff_scale = DIFF_HEAD_DIM ** -0.5
    kr = _rope_chunk(kr, cm, sm, MLA_ROPE // 2)
    for hd in range(MLA_HEADS):
        c = slice(hd * LANES, (hd + 1) * LANES)
        qm_ref[0, :, c] = (_rope_chunk(q[:, c], cm, sm, MLA_ROPE // 2) * mla_scale).astype(BF16)
        km_ref[0, :, c] = (kv[:, c] + kr).astype(BF16)
    vm_ref[0] = kv[:, MLA_HEADS * LANES:].astype(BF16)
    for hd in range(DIFF_HEADS):
        c = slice(hd * LANES, (hd + 1) * LANES)
        dq_ref[0, :, c] = (_rope_chunk(dq[:, c], cd, sd, DIFF_HEAD_DIM // 2) * diff_scale).astype(BF16)
        dk_ref[0, :, c] = _rope_chunk(dk[:, c], cd, sd, DIFF_HEAD_DIM // 2).astype(BF16)
    dv_ref[0] = dv.astype(BF16)


def _proj_even(xcat, mod, g, w_in, qg, w_uq, kvg, w_ukv, cm, sm, cd, sd, n_lat_tiles):
    b, rows, d = xcat.shape
    nt = rows // ROW_TILE
    ctx_row = mod.shape[0] - 1
    const = lambda *shape: pl.BlockSpec(shape, lambda bi, i: (0,) * len(shape))
    tab = pl.BlockSpec((ROW_TILE, LANES), lambda bi, i: (i, 0))
    out = lambda w: pl.BlockSpec((1, ROW_TILE, w), lambda bi, i: (bi, i, 0))
    shp = lambda w: jax.ShapeDtypeStruct((b, rows, w), BF16)
    return pl.pallas_call(
        _proj_even_kernel,
        grid=(b, nt),
        in_specs=[
            pl.BlockSpec((1, ROW_TILE, d), lambda bi, i: (bi, i, 0)),
            pl.BlockSpec((1, 6, d), lambda bi, i: (jnp.where(i < n_lat_tiles, bi, ctx_row), 0, 0)),
            const(1, d), const(*w_in.shape), const(1, MLA_Q_RANK), const(*w_uq.shape),
            const(1, MLA_KV_RANK), const(*w_ukv.shape), tab, tab, tab, tab,
        ],
        out_specs=[out(1024), out(1024), out(1024), out(512), out(512), out(512)],
        out_shape=[shp(1024), shp(1024), shp(1024), shp(512), shp(512), shp(512)],
        compiler_params=_params("parallel", "arbitrary"),
        name="proj_even",
    )(xcat, mod, g, w_in, qg, w_uq, kvg, w_ukv, cm, sm, cd, sd)


def _softmax_pv(s, v):
    m = jnp.max(s, axis=-1, keepdims=True)
    p = jnp.exp(s - m)
    l = jnp.sum(p, axis=-1, keepdims=True)
    return _dot(p.astype(BF16), v) / l


def _mla_kernel(q_ref, k_ref, v_ref, o_ref, *, n_lat_tiles, n_lat):
    i = pl.program_id(2)

    def run(lo, hi):
        acc = None
        for hd in range(2):
            c = slice(hd * LANES, (hd + 1) * LANES)
            s = _dot_nt(q_ref[0, :, c], k_ref[0, lo:hi, c])
            o = _softmax_pv(s, v_ref[0, lo:hi, c])
            acc = o if acc is None else acc + o
        o_ref[0] = acc.astype(o_ref.dtype)

    @pl.when(i < n_lat_tiles)
    def _():
        run(0, k_ref.shape[1])

    @pl.when(i >= n_lat_tiles)
    def _():
        run(n_lat, k_ref.shape[1])


def _mla_attention(qm, km, vm, n_lat):
    b, rows, _ = qm.shape
    nt = rows // ROW_TILE
    pairs = MLA_HEADS // 2
    return pl.pallas_call(
        functools.partial(_mla_kernel, n_lat_tiles=n_lat // ROW_TILE, n_lat=n_lat),
        grid=(b, pairs, nt),
        in_specs=[
            pl.BlockSpec((1, ROW_TILE, 2 * LANES), lambda bi, p, i: (bi, i, p)),
            pl.BlockSpec((1, rows, 2 * LANES), lambda bi, p, i: (bi, 0, p)),
            pl.BlockSpec((1, rows, 2 * LANES), lambda bi, p, i: (bi, 0, p)),
        ],
        out_specs=pl.BlockSpec((1, ROW_TILE, LANES), lambda bi, p, i: (bi, i, p)),
        out_shape=jax.ShapeDtypeStruct((b, rows, pairs * LANES), BF16),
        compiler_params=_params("parallel", "arbitrary", "arbitrary"),
        name="mla_attention",
    )(qm, km, vm)


def _diff_kernel(q_ref, k_ref, v_ref, lam_ref, g_ref, o_ref, *, n_lat_tiles, n_lat, lambda_init):
    i = pl.program_id(2)
    lv = lam_ref[...]
    lam = (jnp.exp(jnp.sum(lv[0:1] * lv[1:2], axis=-1, keepdims=True))
           - jnp.exp(jnp.sum(lv[2:3] * lv[3:4], axis=-1, keepdims=True)) + lambda_init)

    def run(lo, hi):
        q = q_ref[0]
        k = k_ref[0, lo:hi, :]
        v = v_ref[0, lo:hi, :]
        first = _lane_iota(q.shape) < DIFF_HEAD_DIM
        zero = jnp.zeros_like(q)
        o1 = _softmax_pv(_dot_nt(jnp.where(first, q, zero), k), v)
        o2 = _softmax_pv(_dot_nt(jnp.where(first, zero, q), k), v)
        o = o1 - lam * o2
        o_ref[0] = (_rms(o, g_ref[...]) * (1.0 - lambda_init)).astype(o_ref.dtype)

    @pl.when(i < n_lat_tiles)
    def _():
        run(0, k_ref.shape[1])

    @pl.when(i >= n_lat_tiles)
    def _():
        run(n_lat, k_ref.shape[1])


def _diff_attention(dq, dk, dv, diff_lambda, subln_g, n_lat, lambda_init):
    b, rows, _ = dq.shape
    nt = rows // ROW_TILE
    return pl.pallas_call(
        functools.partial(_diff_kernel, n_lat_tiles=n_lat // ROW_TILE, n_lat=n_lat, lambda_init=lambda_init),
        grid=(b, DIFF_HEADS, nt),
        in_specs=[
            pl.BlockSpec((1, ROW_TILE, LANES), lambda bi, h, i: (bi, i, h)),
            pl.BlockSpec((1, rows, LANES), lambda bi, h, i: (bi, 0, h)),
            pl.BlockSpec((1, rows, LANES), lambda bi, h, i: (bi, 0, h)),
            pl.BlockSpec((4, DIFF_HEAD_DIM), lambda bi, h, i: (0, 0)),
            pl.BlockSpec((1, DIFF_V_DIM), lambda bi, h, i: (0, 0)),
        ],
        out_specs=pl.BlockSpec((1, ROW_TILE, LANES), lambda bi, h, i: (bi, i, h)),
        out_shape=jax.ShapeDtypeStruct((b, rows, DIFF_HEADS * LANES), BF16),
        compiler_params=_params("parallel", "arbitrary", "arbitrary"),
        name="diff_attention",
    )(dq, dk, dv, diff_lambda, subln_g)


def _merge_kernel(a_ref, b_ref, x_ref, mod_ref, w_ref, g1_ref, g2_ref, x1_ref, h2_ref):
    half = a_ref.shape[-1]
    mod = mod_ref[0]
    y = _dot(a_ref[0], w_ref[0:half, :]) + _dot(b_ref[0], w_ref[half:, :])
    x1 = x_ref[0] + mod[2:3] * _rms(y, g1_ref[...])
    x1_ref[0] = x1
    h2_ref[0] = (_rms(x1, g2_ref[...]) * (1.0 + mod[4:5]) + mod[3:4]).astype(BF16)


def _merge(a, bb, x, mod, w_out, g1, g2, n_lat_tiles, nt):
    b, _, half = a.shape
    d = x.shape[-1]
    rows = nt * ROW_TILE
    ctx_row = mod.shape[0] - 1
    tile = lambda w: pl.BlockSpec((1, ROW_TILE, w), lambda bi, i: (bi, i, 0))
    const = lambda *shape: pl.BlockSpec(shape, lambda bi, i: (0,) * len(shape))
    return pl.pallas_call(
        _merge_kernel,
        grid=(b, nt),
        in_specs=[
            tile(half), tile(half), tile(d),
            pl.BlockSpec((1, 6, d), lambda bi, i: (jnp.where(i < n_lat_tiles, bi, ctx_row), 0, 0)),
            const(*w_out.shape), const(1, d), const(1, d),
        ],
        out_specs=[tile(d), tile(d)],
        out_shape=[jax.ShapeDtypeStruct((b, rows, d), F32), jax.ShapeDtypeStruct((b, rows, d), BF16)],
        compiler_params=_params("parallel", "arbitrary"),
        name="merge",
    )(a, bb, x, mod, w_out, g1, g2)


def _ffn_kernel(h_ref, hp_ref, hn_ref, x_ref, mod_ref, wa_ref, wg_ref, cwa_ref, cwg_ref, wd_ref, g_ref,
                o_ref, acc_ref, *, n_lat_tiles, n_tiles):
    i = pl.program_id(1)
    h = h_ref[0]
    prev_ok = jnp.where((i == 0) | (i == n_lat_tiles), 0.0, 1.0)
    next_ok = jnp.where((i == n_lat_tiles - 1) | (i == n_tiles - 1), 0.0, 1.0)
    hp = hp_ref[0]
    hn = hn_ref[0]
    acc_ref[...] = jnp.zeros_like(acc_ref)

    def conv_up(w, cw):
        u = _dot(h, w)
        up = _dot(hp, w)[HALO_ROWS - 1:HALO_ROWS] * prev_ok
        un = _dot(hn, w)[0:1] * next_ok
        down, upw = _shift_rows(u, up, un)
        return down * cw[0:1] + u * cw[1:2] + upw * cw[2:3] + cw[3:4]

    def body(j, carry):
        a = conv_up(wa_ref[j], cwa_ref[j])
        g = conv_up(wg_ref[j], cwg_ref[j])
        act = (g * jax.nn.sigmoid(g)) * a
        acc_ref[...] += _dot(act.astype(BF16), wd_ref[j])
        return carry

    lax.fori_loop(0, wa_ref.shape[0], body, 0)
    mod = mod_ref[0]
    o_ref[0] = x_ref[0] + mod[5:6] * _rms(acc_ref[...], g_ref[...])


def _ffn(h2, x1, mod, wa, wg, cwa, cwg, wd, g3, n_lat_tiles, n_tiles):
    b, _, d = h2.shape
    ctx_row = mod.shape[0] - 1
    per = ROW_TILE // HALO_ROWS
    last_halo = n_tiles * per - 1
    const = lambda *shape: pl.BlockSpec(shape, lambda bi, i: (0,) * len(shape))
    tile = pl.BlockSpec((1, ROW_TILE, d), lambda bi, i: (bi, i, 0))
    return pl.pallas_call(
        functools.partial(_ffn_kernel, n_lat_tiles=n_lat_tiles, n_tiles=n_tiles),
        grid=(b, n_tiles),
        in_specs=[
            tile,
            pl.BlockSpec((1, HALO_ROWS, d), lambda bi, i: (bi, jnp.maximum(i * per - 1, 0), 0)),
            pl.BlockSpec((1, HALO_ROWS, d), lambda bi, i: (bi, jnp.minimum((i + 1) * per, last_halo), 0)),
            tile,
            pl.BlockSpec((1, 6, d), lambda bi, i: (jnp.where(i < n_lat_tiles, bi, ctx_row), 0, 0)),
            const(*wa.shape), const(*wg.shape), const(*cwa.shape), const(*cwg.shape), const(*wd.shape),
            const(1, d),
        ],
        out_specs=tile,
        out_shape=jax.ShapeDtypeStruct((b, n_tiles * ROW_TILE, d), F32),
        scratch_shapes=[pltpu.VMEM((ROW_TILE, d), F32)],
        compiler_params=_params("parallel", "arbitrary"),
        name="conv_ffn",
    )(h2, h2, h2, x1, mod, wa, wg, cwa, cwg, wd, g3)


def _proj_odd_kernel(x_ref, xp_ref, xn_ref, mod_ref, g_ref, wqkv_ref, wu_ref, cw_ref, cd_ref, sd_ref,
                     q_ref, k_ref, v_ref, u_ref, *, n_tiles):
    i = pl.program_id(1)
    mod = mod_ref[0]

    def normed(x):
        return (_rms(x, g_ref[...]) * (1.0 + mod[1:2]) + mod[0:1]).astype(BF16)

    h = normed(x_ref[0])
    p = _dot(h, wqkv_ref[...])
    cd, sd = cd_ref[...], sd_ref[...]
    scale = WIN_HEAD_DIM ** -0.5
    for c in range(4):
        cs = slice(c * LANES, (c + 1) * LANES)
        q_ref[0, :, cs] = (_rope_chunk(p[:, cs], cd, sd, WIN_HEAD_DIM // 2) * scale).astype(BF16)
    for c in range(2):
        cs = slice(512 + c * LANES, 512 + (c + 1) * LANES)
        k_ref[0, :, c * LANES:(c + 1) * LANES] = _rope_chunk(p[:, cs], cd, sd, WIN_HEAD_DIM // 2).astype(BF16)
    v_ref[0] = p[:, 768:1024].astype(BF16)
    prev_ok = jnp.where(i == 0, 0.0, 1.0)
    next_ok = jnp.where(i == n_tiles - 1, 0.0, 1.0)
    wu = wu_ref[...]
    u = _dot(h, wu)
    up = _dot(normed(xp_ref[0]), wu)[HALO_ROWS - 1:HALO_ROWS] * prev_ok
    un = _dot(normed(xn_ref[0]), wu)[0:1] * next_ok
    down, upw = _shift_rows(u, up, un)
    cw = cw_ref[...]
    u_ref[0] = down * cw[0:1] + u * cw[1:2] + upw * cw[2:3] + cw[3:4]


def _proj_odd(xcat, mod, g, wqkv, wu, cw, cd, sd, n_tiles):
    b, _, d = xcat.shape
    per = ROW_TILE // HALO_ROWS
    last_halo = n_tiles * per - 1
    rows = n_tiles * ROW_TILE
    const = lambda *shape: pl.BlockSpec(shape, lambda bi, i: (0,) * len(shape))
    tab = pl.BlockSpec((ROW_TILE, LANES), lambda bi, i: (i, 0))
    out = lambda w: pl.BlockSpec((1, ROW_TILE, w), lambda bi, i: (bi, i, 0))
    return pl.pallas_call(
        functools.partial(_proj_odd_kernel, n_tiles=n_tiles),
        grid=(b, n_tiles),
        in_specs=[
            pl.BlockSpec((1, ROW_TILE, d), lambda bi, i: (bi, i, 0)),
            pl.BlockSpec((1, HALO_ROWS, d), lambda bi, i: (bi, jnp.maximum(i * per - 1, 0), 0)),
            pl.BlockSpec((1, HALO_ROWS, d), lambda bi, i: (bi, jnp.minimum((i + 1) * per, last_halo), 0)),
            pl.BlockSpec((1, 6, d), lambda bi, i: (bi, 0, 0)),
            const(1, d), const(*wqkv.shape), const(*wu.shape), const(*cw.shape), tab, tab,
        ],
        out_specs=[out(512), out(256), out(256), out(3 * HYENA_CH)],
        out_shape=[jax.ShapeDtypeStruct((b, rows, 512), BF16), jax.ShapeDtypeStruct((b, rows, 256), BF16),
                   jax.ShapeDtypeStruct((b, rows, 256), BF16), jax.ShapeDtypeStruct((b, rows, 3 * HYENA_CH), F32)],
        compiler_params=_params("parallel", "arbitrary"),
        name="proj_odd",
    )(xcat, xcat, xcat, mod, g, wqkv, wu, cw, cd, sd)


def _proj_ctx_kv_kernel(x_ref, mod_ref, g_ref, w_ref, k_ref, v_ref):
    mod = mod_ref[0]
    h = (_rms(x_ref[0], g_ref[...]) * (1.0 + mod[1:2]) + mod[0:1]).astype(BF16)
    p = _dot(h, w_ref[...])
    k_ref[0] = p[:, :2 * LANES].astype(BF16)
    v_ref[0] = p[:, 2 * LANES:].astype(BF16)


def _proj_ctx_kv(xcat, mod, g, wkv, ctx_tile, n_ctx):
    b, _, d = xcat.shape
    ctx_row = mod.shape[0] - 1
    out = pl.BlockSpec((1, n_ctx, 2 * LANES), lambda bi: (bi, 0, 0))
    return pl.pallas_call(
        _proj_ctx_kv_kernel,
        grid=(b,),
        in_specs=[
            pl.BlockSpec((1, n_ctx, d), lambda bi: (bi, ctx_tile, 0)),
            pl.BlockSpec((1, 6, d), lambda bi: (ctx_row, 0, 0)),
            pl.BlockSpec((1, d), lambda bi: (0, 0)),
            pl.BlockSpec(wkv.shape, lambda bi: (0, 0)),
        ],
        out_specs=[out, out],
        out_shape=[jax.ShapeDtypeStruct((b, n_ctx, 2 * LANES), BF16)] * 2,
        compiler_params=_params("parallel"),
        name="proj_ctx_kv",
    )(xcat, mod, g, wkv)


def _window_kernel(sink_ref, q_ref, kc_ref, vc_ref, kp_ref, km_ref, kn_ref, vp_ref, vm_ref, vn_ref, o_ref,
                   *, n_lat):
    i = pl.program_id(1)
    tq = q_ref.shape[1]
    n_ctx = kc_ref.shape[1]
    nk = n_ctx + tq + 2 * WINDOW
    col = lax.broadcasted_iota(jnp.int32, (tq, nk), 1)
    row = lax.broadcasted_iota(jnp.int32, (tq, nk), 0)
    k_pos = i * tq - WINDOW + (col - n_ctx)
    q_pos = i * tq + row
    visible = (col < n_ctx) | ((jnp.abs(q_pos - k_pos) <= WINDOW) & (k_pos >= 0) & (k_pos < n_lat))
    for hk in range(WIN_KV_HEADS):
        c = slice(hk * LANES, (hk + 1) * LANES)
        k_all = jnp.concatenate([kc_ref[0, :, c], kp_ref[0, :, c], km_ref[0, :, c], kn_ref[0, :, c]], axis=0)
        v_all = jnp.concatenate([vc_ref[0, :, c], vp_ref[0, :, c], vm_ref[0, :, c], vn_ref[0, :, c]], axis=0)
        first_v = _lane_iota(v_all.shape) < WIN_HEAD_DIM
        v_half = (jnp.where(first_v, v_all, jnp.zeros_like(v_all)), jnp.where(first_v, jnp.zeros_like(v_all), v_all))
        for pair in range(2):
            chunk = hk * 2 + pair
            q = q_ref[0, :, chunk * LANES:(chunk + 1) * LANES]
            first_q = _lane_iota(q.shape) < WIN_HEAD_DIM
            out = None
            for half in range(2):
                qh = jnp.where(first_q, q, jnp.zeros_like(q)) if half == 0 else jnp.where(first_q, jnp.zeros_like(q), q)
                s = jnp.where(visible, _dot_nt(qh, k_all), NEG_INF)
                sink = sink_ref[chunk * 2 + half]
                m = jnp.maximum(jnp.max(s, axis=-1, keepdims=True), sink)
                p = jnp.exp(s - m)
                l = jnp.sum(p, axis=-1, keepdims=True) + jnp.exp(sink - m)
                o = _dot(p.astype(BF16), v_half[half]) / l
                out = o if out is None else out + o
            o_ref[0, :, chunk * LANES:(chunk + 1) * LANES] = out.astype(o_ref.dtype)


def _window_attention(q, kd, vd, kc, vc, sink):
    b, n_lat, _ = q.shape
    tq = ROW_TILE
    n_ctx = kc.shape[1]
    per = tq // WINDOW
    last = n_lat // WINDOW - 1
    main = pl.BlockSpec((1, tq, 2 * LANES), lambda bi, i: (bi, i, 0))
    prev = pl.BlockSpec((1, WINDOW, 2 * LANES), lambda bi, i: (bi, jnp.maximum(i * per - 1, 0), 0))
    nxt = pl.BlockSpec((1, WINDOW, 2 * LANES), lambda bi, i: (bi, jnp.minimum((i + 1) * per, last), 0))
    ctx = pl.BlockSpec((1, n_ctx, 2 * LANES), lambda bi, i: (bi, 0, 0))
    return pl.pallas_call(
        functools.partial(_window_kernel, n_lat=n_lat),
        grid=(b, n_lat // tq),
        in_specs=[
            pl.BlockSpec(memory_space=pltpu.SMEM),
            pl.BlockSpec((1, tq, 4 * LANES), lambda bi, i: (bi, i, 0)),
            ctx, ctx, prev, main, nxt, prev, main, nxt,
        ],
        out_specs=pl.BlockSpec((1, tq, 4 * LANES), lambda bi, i: (bi, i, 0)),
        out_shape=jax.ShapeDtypeStruct((b, n_lat, 4 * LANES), BF16),
        compiler_params=_params("parallel", "arbitrary"),
        name="window_attention",
    )(sink, q, kc, vc, kd, kd, kd, vd, vd, vd)


def _hyena_filter_kernel(feat_ref, w1_ref, b1_ref, w2_ref, b2_ref, w3_ref, b3_ref, freq_ref, decay_ref, h_ref,
                         *, n_forward):
    j = pl.program_id(0)
    hp = lax.Precision.HIGHEST
    freq = freq_ref[...]
    hid = jnp.sin(freq * (jnp.dot(feat_ref[...], w1_ref[...], precision=hp, preferred_element_type=F32) + b1_ref[...]))
    hid = jnp.sin(freq * (jnp.dot(hid, w2_ref[...], precision=hp, preferred_element_type=F32) + b2_ref[...]))
    h = jnp.dot(hid, w3_ref[...], precision=hp, preferred_element_type=F32) + b3_ref[...]
    h = h * decay_ref[...]
    row = lax.broadcasted_iota(jnp.int32, h.shape, 0)
    h_ref[...] = jnp.where((row == 0) & (j >= n_forward), 0.0, h).astype(h_ref.dtype)


def _hyena_filters(feats, w1, b1, w2, b2, w3, b3, freq, decay):
    length = feats.shape[0]
    n = w3.shape[1]
    ch = decay.shape[1]
    full = lambda a: pl.BlockSpec(a.shape, lambda j: (0, 0))
    return pl.pallas_call(
        functools.partial(_hyena_filter_kernel, n_forward=HYENA_ORDER),
        grid=(n // ch,),
        in_specs=[full(feats), full(w1), full(b1), full(w2), full(b2),
                  pl.BlockSpec((w3.shape[0], ch), lambda j: (0, j)), pl.BlockSpec((1, ch), lambda j: (0, j)),
                  full(freq), full(decay)],
        out_specs=pl.BlockSpec((length, ch), lambda j: (0, j)),
        out_shape=jax.ShapeDtypeStruct((length, n), BF16),
        compiler_params=_params("arbitrary"),
        name="hyena_filters",
    )(feats, w1, b1, w2, b2, w3, b3, freq, decay)


def _spectrum_kernel(f_ref, h_ref, re_ref, im_ref):
    j = pl.program_id(0)
    tk = re_ref.shape[0]
    n = re_ref.shape[1]
    a = _dot(f_ref[0], h_ref[...])
    re_ref[...] = a[:tk, :n] + a[:tk, n:]
    row = lax.broadcasted_iota(jnp.int32, (tk, n), 0)
    sign = jnp.where((row == 0) & (j == 0), 1.0, -1.0)
    im_ref[...] = a[tk:, :n] + sign * a[tk:, n:]


def _filter_spectrum(f3, hcat):
    nt, tk2, length = f3.shape
    n = hcat.shape[1] // 2
    tk = tk2 // 2
    out = pl.BlockSpec((tk, n), lambda j: (j, 0))
    return pl.pallas_call(
        _spectrum_kernel,
        grid=(nt,),
        in_specs=[pl.BlockSpec((1, tk2, length), lambda j: (j, 0, 0)),
                  pl.BlockSpec(hcat.shape, lambda j: (0, 0))],
        out_specs=[out, out],
        out_shape=[jax.ShapeDtypeStruct((nt * tk, n), F32)] * 2,
        compiler_params=_params("arbitrary"),
        name="filter_spectrum",
    )(f3, hcat)


def _long_conv_kernel(z_ref, x_ref, f_ref, g_ref, hre_ref, him_ref, bias_ref, o_ref, zb_ref, acc_ref):
    j = pl.program_id(1)
    tk = hre_ref.shape[0]

    @pl.when(j == 0)
    def _():
        zb_ref[...] = z_ref[0].astype(BF16)
        acc_ref[...] = jnp.zeros_like(acc_ref)

    a = _dot(f_ref[0], zb_ref[...])
    a_re, a_im = a[:tk], a[tk:]
    h_re, h_im = hre_ref[...], him_ref[...]
    row = lax.broadcasted_iota(jnp.int32, a_re.shape, 0)
    packed = (row == 0) & (j == 0)
    ii = a_im * h_im
    p_re = a_re * h_re - jnp.where(packed, 0.0, ii)
    p_im = jnp.where(packed, ii, a_re * h_im + a_im * h_re)
    p = jnp.concatenate([p_re, p_im], axis=0).astype(BF16)
    acc_ref[...] += _dot(g_ref[0], p)

    @pl.when(j == pl.num_programs(1) - 1)
    def _():
        o_ref[0] = (x_ref[0] * (acc_ref[...] + bias_ref[...] * z_ref[0])).astype(o_ref.dtype)


def _long_conv(z_src, z_col, x_src, x_col, f3, g3, hre, him, h_col, bias, out_dtype):
    b, length, _ = z_src.shape
    nt, tk2, _ = f3.shape
    tk = tk2 // 2
    ch = HYENA_CH
    return pl.pallas_call(
        _long_conv_kernel,
        grid=(b, nt),
        in_specs=[
            pl.BlockSpec((1, length, ch), lambda bi, j: (bi, 0, z_col)),
            pl.BlockSpec((1, length, ch), lambda bi, j: (bi, 0, x_col)),
            pl.BlockSpec((1, tk2, length), lambda bi, j: (j, 0, 0)),
            pl.BlockSpec((1, length, tk2), lambda bi, j: (j, 0, 0)),
            pl.BlockSpec((tk, ch), lambda bi, j: (j, h_col)),
            pl.BlockSpec((tk, ch), lambda bi, j: (j, h_col)),
            pl.BlockSpec((1, ch), lambda bi, j: (0, 0)),
        ],
        out_specs=pl.BlockSpec((1, length, ch), lambda bi, j: (bi, 0, 0)),
        out_shape=jax.ShapeDtypeStruct((b, length, ch), out_dtype),
        scratch_shapes=[pltpu.VMEM((length, ch), BF16), pltpu.VMEM((length, ch), F32)],
        compiler_params=_params("parallel", "arbitrary"),
        name="long_conv",
    )(z_src, x_src, f3, g3, hre, him, bias)


def _axial_rope(length, rot_dim):
    rows = length // GRID_W
    row = jnp.repeat(jnp.arange(rows), GRID_W).astype(F32)
    col = jnp.tile(jnp.arange(GRID_W), rows).astype(F32)
    quarter = rot_dim // 4
    inv = ROPE_BASE ** (-jnp.arange(quarter, dtype=F32) / quarter)
    ang = jnp.concatenate([row[:, None] * inv, col[:, None] * inv], axis=-1)
    return jnp.cos(ang), jnp.sin(ang)


def _rope_tables(length, n_ctx):
    cos_m, sin_m = _axial_rope(length, MLA_ROPE)
    cos_d, sin_d = _axial_rope(length, DIFF_HEAD_DIM)
    ones = lambda w: jnp.ones((length, w), F32)
    zeros = lambda w: jnp.zeros((length, w), F32)
    cm = jnp.concatenate([ones(MLA_NOPE), cos_m, cos_m, ones(LANES - MLA_NOPE - MLA_ROPE)], axis=1)
    sm = jnp.concatenate([zeros(MLA_NOPE), -sin_m, sin_m, zeros(LANES - MLA_NOPE - MLA_ROPE)], axis=1)
    cd = jnp.concatenate([cos_d] * 4, axis=1)
    sd = jnp.concatenate([-sin_d, sin_d] * 2, axis=1)
    pad = lambda t, v: jnp.concatenate([t, jnp.full((n_ctx, LANES), v, F32)], axis=0)
    return pad(cm, 1.0), pad(sm, 0.0), pad(cd, 1.0), pad(sd, 0.0)


def _dft_matrices(length):
    n = 2 * length
    k = np.arange(length, dtype=np.int64)
    ang = 2.0 * np.pi * ((k[:, None] * k[None, :]) % n).astype(np.float64) / n
    alt = np.where(k % 2 == 0, 1.0, -1.0)
    f_re = np.cos(ang)
    f_im = -np.sin(ang)
    f_im[0, :] = alt
    g_re = (2.0 / n) * np.cos(ang.T)
    g_im = -(2.0 / n) * np.sin(ang.T)
    g_re[:, 0] = 1.0 / n
    g_im[:, 0] = alt / n
    nt = length // FREQ_TILE
    f3 = np.concatenate([f_re.reshape(nt, FREQ_TILE, length), f_im.reshape(nt, FREQ_TILE, length)], axis=1)
    g3 = np.concatenate([g_re.reshape(length, nt, FREQ_TILE), g_im.reshape(length, nt, FREQ_TILE)], axis=2)
    g3 = np.transpose(g3, (1, 0, 2))
    return jnp.asarray(f3, dtype=BF16), jnp.asarray(g3, dtype=BF16)


def _hyena_features(length):
    t = jnp.arange(length, dtype=F32)
    tn = t / max(length - 1, 1)
    bands = jnp.linspace(1e-4, HYENA_BANDS - 1, HYENA_BANDS, dtype=F32)
    ang = 2.0 * math.pi * bands[None, :] * t[:, None] / length
    feats = jnp.concatenate([tn[:, None], jnp.cos(ang), -jnp.sin(ang)], axis=-1)
    min_decay = math.log(HYENA_DECAY_TARGET) / HYENA_SLOW_PCT
    max_decay = math.log(HYENA_DECAY_TARGET) / HYENA_FAST_PCT
    deltas = jnp.abs(jnp.linspace(min_decay, max_decay, HYENA_CH, dtype=F32))
    decay = jnp.exp(-tn[:, None] * deltas[None, :])
    return feats, decay


def _pad_cols(w, left, right):
    return jnp.pad(w, ((0, 0),) * (w.ndim - 1) + ((left, right),))


def _even_weights(w_in, w_uq, w_ukv):
    d = w_in.shape[0]
    o = MLA_Q_RANK + MLA_KV_RANK
    w_kr = _pad_cols(w_in[:, o:o + MLA_ROPE], MLA_NOPE, LANES - MLA_NOPE - MLA_ROPE)
    w_in_ext = jnp.concatenate([w_in[:, :o], w_kr, w_in[:, o + MLA_ROPE:]], axis=1)
    uq = _pad_cols(w_uq.reshape(MLA_Q_RANK, MLA_HEADS, MLA_NOPE + MLA_ROPE), 0, LANES - MLA_NOPE - MLA_ROPE)
    ukv = w_ukv.reshape(MLA_KV_RANK, MLA_HEADS, MLA_NOPE + MLA_V)
    uk = _pad_cols(ukv[:, :, :MLA_NOPE], 0, LANES - MLA_NOPE)
    uv = ukv[:, :, MLA_NOPE:].reshape(MLA_KV_RANK, MLA_HEADS // 2, 2, MLA_V)
    uv = jnp.stack([_pad_cols(uv[:, :, 0], 0, LANES - MLA_V), _pad_cols(uv[:, :, 1], LANES - MLA_V, 0)], axis=2)
    w_ukv_ext = jnp.concatenate([uk.reshape(MLA_KV_RANK, -1), uv.reshape(MLA_KV_RANK, -1)], axis=1)
    return w_in_ext.astype(BF16), uq.reshape(MLA_Q_RANK, -1).astype(BF16), w_ukv_ext.astype(BF16)


def _odd_weights(w_in):
    d = w_in.shape[0]
    nq = WIN_Q_HEADS * WIN_HEAD_DIM
    nkv = WIN_KV_HEADS * WIN_HEAD_DIM
    dup = lambda w: jnp.concatenate([w.reshape(d, WIN_KV_HEADS, WIN_HEAD_DIM)] * 2, axis=-1).reshape(d, -1)
    wk = dup(w_in[:, nq:nq + nkv])
    wv = dup(w_in[:, nq + nkv:nq + 2 * nkv])
    wqkv = jnp.concatenate([w_in[:, :nq], wk, wv], axis=1).astype(BF16)
    wkv = jnp.concatenate([wk, wv], axis=1).astype(BF16)
    wu = w_in[:, nq + 2 * nkv:].astype(BF16)
    return wqkv, wkv, wu


def _ffn_weights(w_up, conv_w, conv_b, w_down):
    d = w_up.shape[0]
    nc = D_FF // FFN_CHUNK
    chunks = lambda w: jnp.transpose(w.reshape(w.shape[0], nc, FFN_CHUNK), (1, 0, 2))
    cw = jnp.concatenate([conv_w, conv_b[None, :]], axis=0)
    return (chunks(w_up[:, :D_FF]).astype(BF16), chunks(w_up[:, D_FF:]).astype(BF16),
            chunks(cw[:, :D_FF]), chunks(cw[:, D_FF:]),
            w_down.reshape(nc, FFN_CHUNK, d).astype(BF16))


def kernel(x, c, ctx, c_ctx, ada_w, ada_b, norm_g, mix_w_out, ffn_w_up, ffn_conv_w, ffn_conv_b, ffn_w_down, even_w_in, mla_q_norm_g, mla_w_uq, mla_kv_norm_g, mla_w_ukv, diff_lambda, diff_subln_g, odd_w_in, win_sink, hy_conv_w, hy_conv_b, hy_f_w1, hy_f_b1, hy_f_w2, hy_f_b2, hy_f_w3, hy_f_b3, hy_f_freq, hy_bias):
    b, seq, d = x.shape
    n_ctx = ctx.shape[1]
    depth = ada_w.shape[0]
    assert depth == 2 and d == D_MODEL and seq % ROW_TILE == 0 and n_ctx == ROW_TILE
    n_lat_tiles = seq // ROW_TILE
    n_tiles = n_lat_tiles + 1

    pad_rows = (-(b + 1)) % 8
    cond = jnp.concatenate([c, c_ctx[None, :], jnp.zeros((pad_rows, d), F32)], axis=0)
    mod = _ada_table(cond, ada_w, ada_b).reshape(depth, b + 1 + pad_rows, 6, d)[:, :b + 1]
    g = norm_g.reshape(depth, 4, 1, d)
    cm, sm, cd, sd = _rope_tables(seq, n_ctx)
    xcat = jnp.concatenate([x, ctx], axis=1)

    lambda_init = 0.8 - 0.6 * math.exp(-0.3 * 0)
    w_in_e, w_uq_e, w_ukv_e = _even_weights(even_w_in[0], mla_w_uq[0], mla_w_ukv[0])
    qm, km, vm, dq, dk, dv = _proj_even(
        xcat, mod[0], g[0, 0], w_in_e, mla_q_norm_g[0][None, :], w_uq_e, mla_kv_norm_g[0][None, :], w_ukv_e,
        cm, sm, cd, sd, n_lat_tiles)
    o_mla = _mla_attention(qm, km, vm, seq)
    o_diff = _diff_attention(dq, dk, dv, diff_lambda[0], diff_subln_g[0][None, :], seq, lambda_init)
    w_out0 = mix_w_out[0].astype(BF16)
    x1, h2 = _merge(o_mla, o_diff, xcat, mod[0], w_out0, g[0, 1], g[0, 2], n_lat_tiles, n_tiles)
    ffn0 = _ffn_weights(ffn_w_up[0], ffn_conv_w[0], ffn_conv_b[0], ffn_w_down[0])
    xcat = _ffn(h2, x1, mod[0], *ffn0, g[0, 3], n_lat_tiles, n_tiles)

    wqkv, wkv, wu = _odd_weights(odd_w_in[0])
    cw_hy = jnp.concatenate([hy_conv_w[0], hy_conv_b[0][None, :]], axis=0)
    q, kd, vd, u = _proj_odd(xcat, mod[1], g[1, 0], wqkv, wu, cw_hy, cd, sd, n_lat_tiles)
    kc, vc = _proj_ctx_kv(xcat, mod[1], g[1, 0], wkv, n_lat_tiles, n_ctx)
    o_win = _window_attention(q, kd, vd, kc, vc, win_sink[0])

    feats, decay = _hyena_features(seq)
    emb = feats.shape[1]
    feats = _pad_cols(feats, 0, LANES - emb)
    w1 = jnp.pad(hy_f_w1[0], ((0, LANES - emb), (0, 0)))
    hcat = _hyena_filters(feats, w1, hy_f_b1[0][None, :], hy_f_w2[0], hy_f_b2[0][None, :], hy_f_w3[0],
                          hy_f_b3[0][None, :], hy_f_freq[0][None, :], decay)
    f3, g3 = _dft_matrices(seq)
    hre, him = _filter_spectrum(f3, hcat)
    z = _long_conv(u, 0, u, 1, f3, g3, hre, him, 0, hy_bias[0, 0][None, :], F32)
    z = _long_conv(z, 0, u, 2, f3, g3, hre, him, 1, hy_bias[0, 1][None, :], BF16)

    w_out1 = mix_w_out[1].astype(BF16)
    x1, h2 = _merge(o_win, z, xcat, mod[1], w_out1, g[1, 1], g[1, 2], n_lat_tiles, n_lat_tiles)
    ffn1 = _ffn_weights(ffn_w_up[1], ffn_conv_w[1], ffn_conv_b[1], ffn_w_down[1])
    return _ffn(h2, x1, mod[1], *ffn1, g[1, 3], n_lat_tiles, n_lat_tiles)
```

```python
import functools
import math

import numpy as np
import jax
import jax.numpy as jnp
from jax import lax
from jax.experimental import pallas as pl
from jax.experimental.pallas import tpu as pltpu

D_MODEL = 1024
GRID_W = 64
MLA_HEADS = 8
MLA_NOPE = 64
MLA_ROPE = 32
MLA_V = 64
MLA_Q_RANK = 384
MLA_KV_RANK = 256
DIFF_HEADS = 4
DIFF_HEAD_DIM = 64
DIFF_V_DIM = 128
WIN_Q_HEADS = 8
WIN_KV_HEADS = 2
WIN_HEAD_DIM = 64
WINDOW = 128
HYENA_CH = 512
HYENA_ORDER = 2
HYENA_BANDS = 16
HYENA_DECAY_TARGET = 1e-2
HYENA_FAST_PCT = 0.3
HYENA_SLOW_PCT = 1.5
D_FF = 2816
ROPE_BASE = 10000.0
NORM_EPS = 1e-6
NEG_INF = -1e30

LANES = 128
HALO_ROWS = 16
ROW_TILE = 256
ATTN_Q_TILE = 512
LOG2E = math.log2(math.e)
FFN_ROW_TILE = 512
FFN_CHUNK = 256
FREQ_TILE = 256
VMEM_LIMIT = 48 * 1024 * 1024

BF16 = jnp.bfloat16
F32 = jnp.float32


def _params(*semantics):
    return pltpu.CompilerParams(dimension_semantics=semantics, vmem_limit_bytes=VMEM_LIMIT)


def _rms(x, g):
    return x * lax.rsqrt(jnp.mean(x * x, axis=-1, keepdims=True) + NORM_EPS) * g


def _dot(a, b):
    return jnp.dot(a, b, preferred_element_type=F32)


def _dot_nt(a, b):
    return lax.dot_general(a, b, (((1,), (1,)), ((), ())), preferred_element_type=F32)


def _lane_iota(shape):
    return lax.broadcasted_iota(jnp.int32, shape, len(shape) - 1)


def _rope_chunk(x, cos, sin_signed, half):
    first = (_lane_iota(x.shape) % (2 * half)) < half
    partner = jnp.where(first, pltpu.roll(x, LANES - half, 1), pltpu.roll(x, half, 1))
    return x * cos + partner * sin_signed


def _shift_rows(u, prev_row, next_row):
    rows = u.shape[0]
    r = lax.broadcasted_iota(jnp.int32, u.shape, 0)
    down = jnp.where(r == 0, prev_row, pltpu.roll(u, 1, 0))
    up = jnp.where(r == rows - 1, next_row, pltpu.roll(u, rows - 1, 0))
    return down, up


def _ada_kernel(c_ref, w_ref, b_ref, o_ref):
    c = c_ref[...]
    sc = c * jax.nn.sigmoid(c)
    o_ref[0] = _dot(sc.astype(BF16), w_ref[0].astype(BF16)) + b_ref[0]


def _ada_table(cond, ada_w, ada_b):
    depth, d, n = ada_w.shape
    rows = cond.shape[0]
    tn = 1536
    return pl.pallas_call(
        _ada_kernel,
        grid=(depth, n // tn),
        in_specs=[
            pl.BlockSpec((rows, d), lambda l, j: (0, 0)),
            pl.BlockSpec((1, d, tn), lambda l, j: (l, 0, j)),
            pl.BlockSpec((1, 1, tn), lambda l, j: (l, 0, j)),
        ],
        out_specs=pl.BlockSpec((1, rows, tn), lambda l, j: (l, 0, j)),
        out_shape=jax.ShapeDtypeStruct((depth, rows, n), F32),
        compiler_params=_params("arbitrary", "arbitrary"),
        name="ada_table",
    )(cond, ada_w, ada_b.reshape(depth, 1, n))


def _proj_even_kernel(x_ref, mod_ref, g_ref, win_ref, qg_ref, wuq_ref, kvg_ref, wukv_ref,
                      cm_ref, sm_ref, cd_ref, sd_ref,
                      qm_ref, km_ref, vm_ref, dq_ref, dk_ref, dv_ref):
    x = x_ref[0]
    mod = mod_ref[0]
    h = _rms(x, g_ref[...]) * (1.0 + mod[1:2]) + mod[0:1]
    p = _dot(h.astype(BF16), win_ref[...])
    o = 0
    cq = p[:, o:o + MLA_Q_RANK]; o += MLA_Q_RANK
    ckv = p[:, o:o + MLA_KV_RANK]; o += MLA_KV_RANK
    kr = p[:, o:o + LANES]; o += LANES
    dq = p[:, o:o + 512]; o += 512
    dk = p[:, o:o + 512]; o += 512
    dv = p[:, o:o + 512]
    q = _dot(_rms(cq, qg_ref[...]).astype(BF16), wuq_ref[...])
    kv = _dot(_rms(ckv, kvg_ref[...]).astype(BF16), wukv_ref[...])
    cm, sm, cd, sd = cm_ref[...], sm_ref[...], cd_ref[...], sd_ref[...]
    mla_scale = (MLA_NOPE + MLA_ROPE) ** -0.5 * LOG2E
    diff_scale = DIFF_HEAD_DIM ** -0.5 * LOG2E
    kr = _rope_chunk(kr, cm, sm, MLA_ROPE // 2)
    for hd in range(MLA_HEADS):
        c = slice(hd * LANES, (hd + 1) * LANES)
        qm_ref[0, :, c] = (_rope_chunk(q[:, c], cm, sm, MLA_ROPE // 2) * mla_scale).astype(BF16)
        km_ref[0, :, c] = (kv[:, c] + kr).astype(BF16)
    vm_ref[0] = kv[:, MLA_HEADS * LANES:].astype(BF16)
    for hd in range(DIFF_HEADS):
        c = slice(hd * LANES, (hd + 1) * LANES)
        dq_ref[0, :, c] = (_rope_chunk(dq[:, c], cd, sd, DIFF_HEAD_DIM // 2) * diff_scale).astype(BF16)
        dk_ref[0, :, c] = _rope_chunk(dk[:, c], cd, sd, DIFF_HEAD_DIM // 2).astype(BF16)
    dv_ref[0] = dv.astype(BF16)


def _proj_even(xcat, mod, g, w_in, qg, w_uq, kvg, w_ukv, cm, sm, cd, sd, n_lat_tiles):
    b, rows, d = xcat.shape
    nt = rows // ROW_TILE
    ctx_row = mod.shape[0] - 1
    const = lambda *shape: pl.BlockSpec(shape, lambda bi, i: (0,) * len(shape))
    tab = pl.BlockSpec((ROW_TILE, LANES), lambda bi, i: (i, 0))
    out = lambda w: pl.BlockSpec((1, ROW_TILE, w), lambda bi, i: (bi, i, 0))
    shp = lambda w: jax.ShapeDtypeStruct((b, rows, w), BF16)
    return pl.pallas_call(
        _proj_even_kernel,
        grid=(b, nt),
        in_specs=[
            pl.BlockSpec((1, ROW_TILE, d), lambda bi, i: (bi, i, 0)),
            pl.BlockSpec((1, 6, d), lambda bi, i: (jnp.where(i < n_lat_tiles, bi, ctx_row), 0, 0)),
            const(1, d), const(*w_in.shape), const(1, MLA_Q_RANK), const(*w_uq.shape),
            const(1, MLA_KV_RANK), const(*w_ukv.shape), tab, tab, tab, tab,
        ],
        out_specs=[out(1024), out(1024), out(1024), out(512), out(512), out(512)],
        out_shape=[shp(1024), shp(1024), shp(1024), shp(512), shp(512), shp(512)],
        compiler_params=_params("parallel", "arbitrary"),
        name="proj_even",
    )(xcat, mod, g, w_in, qg, w_uq, kvg, w_ukv, cm, sm, cd, sd)


def _softmax_pv(s, v):
    m = jnp.max(s, axis=-1, keepdims=True)
    p = jnp.exp2(s - m)
    l = jnp.sum(p, axis=-1, keepdims=True)
    return _dot(p.astype(BF16), v) / l


def _mla_kernel(q_ref, k_ref, v_ref, o_ref):
    acc = None
    for hd in range(2):
        c = slice(hd * LANES, (hd + 1) * LANES)
        o = _softmax_pv(_dot_nt(q_ref[0, :, c], k_ref[0, :, c]), v_ref[0, :, c])
        acc = o if acc is None else acc + o
    o_ref[0] = acc.astype(o_ref.dtype)


def _mla_attention(qm, km, vm, tq, q_off, n_q, k_off, n_k):
    b = qm.shape[0]
    pairs = MLA_HEADS // 2
    qo, ko = q_off // tq, k_off // n_k
    return pl.pallas_call(
        _mla_kernel,
        grid=(b, pairs, n_q // tq),
        in_specs=[
            pl.BlockSpec((1, tq, 2 * LANES), lambda bi, p, i: (bi, qo + i, p)),
            pl.BlockSpec((1, n_k, 2 * LANES), lambda bi, p, i: (bi, ko, p)),
            pl.BlockSpec((1, n_k, 2 * LANES), lambda bi, p, i: (bi, ko, p)),
        ],
        out_specs=pl.BlockSpec((1, tq, LANES), lambda bi, p, i: (bi, i, p)),
        out_shape=jax.ShapeDtypeStruct((b, n_q, pairs * LANES), BF16),
        compiler_params=_params("parallel", "arbitrary", "arbitrary"),
        name="mla_attention",
    )(qm, km, vm)


def _diff_kernel(q_ref, k_ref, v_ref, lam_ref, g_ref, o_ref, *, lambda_init):
    lv = lam_ref[...]
    lam = (jnp.exp(jnp.sum(lv[0:1] * lv[1:2], axis=-1, keepdims=True))
           - jnp.exp(jnp.sum(lv[2:3] * lv[3:4], axis=-1, keepdims=True)) + lambda_init)
    q = q_ref[0]
    k = k_ref[0]
    v = v_ref[0]
    first = _lane_iota(q.shape) < DIFF_HEAD_DIM
    zero = jnp.zeros_like(q)
    o1 = _softmax_pv(_dot_nt(jnp.where(first, q, zero), k), v)
    o2 = _softmax_pv(_dot_nt(jnp.where(first, zero, q), k), v)
    o = o1 - lam * o2
    o_ref[0] = (_rms(o, g_ref[...]) * (1.0 - lambda_init)).astype(o_ref.dtype)


def _diff_attention(dq, dk, dv, diff_lambda, subln_g, lambda_init, tq, q_off, n_q, k_off, n_k):
    b = dq.shape[0]
    qo, ko = q_off // tq, k_off // n_k
    return pl.pallas_call(
        functools.partial(_diff_kernel, lambda_init=lambda_init),
        grid=(b, DIFF_HEADS, n_q // tq),
        in_specs=[
            pl.BlockSpec((1, tq, LANES), lambda bi, h, i: (bi, qo + i, h)),
            pl.BlockSpec((1, n_k, LANES), lambda bi, h, i: (bi, ko, h)),
            pl.BlockSpec((1, n_k, LANES), lambda bi, h, i: (bi, ko, h)),
            pl.BlockSpec((4, DIFF_HEAD_DIM), lambda bi, h, i: (0, 0)),
            pl.BlockSpec((1, DIFF_V_DIM), lambda bi, h, i: (0, 0)),
        ],
        out_specs=pl.BlockSpec((1, tq, LANES), lambda bi, h, i: (bi, i, h)),
        out_shape=jax.ShapeDtypeStruct((b, n_q, DIFF_HEADS * LANES), BF16),
        compiler_params=_params("parallel", "arbitrary", "arbitrary"),
        name="diff_attention",
    )(dq, dk, dv, diff_lambda, subln_g)


def _merge_kernel(a_ref, b_ref, x_ref, mod_ref, w_ref, g1_ref, g2_ref, x1_ref, h2_ref):
    half = a_ref.shape[-1]
    mod = mod_ref[0]
    y = _dot(a_ref[0], w_ref[0:half, :]) + _dot(b_ref[0], w_ref[half:, :])
    x1 = x_ref[0] + mod[2:3] * _rms(y, g1_ref[...])
    x1_ref[0] = x1
    h2_ref[0] = (_rms(x1, g2_ref[...]) * (1.0 + mod[4:5]) + mod[3:4]).astype(BF16)


def _merge(a, bb, x, mod, w_out, g1, g2, tm, x_off, ctx_mod):
    b, rows, half = a.shape
    d = x.shape[-1]
    nt = rows // tm
    xo = x_off // tm
    ctx_row = mod.shape[0] - 1
    tile = lambda w: pl.BlockSpec((1, tm, w), lambda bi, i: (bi, i, 0))
    const = lambda *shape: pl.BlockSpec(shape, lambda bi, i: (0,) * len(shape))
    return pl.pallas_call(
        _merge_kernel,
        grid=(b, nt),
        in_specs=[
            tile(half), tile(half), pl.BlockSpec((1, tm, d), lambda bi, i: (bi, xo + i, 0)),
            pl.BlockSpec((1, 6, d), lambda bi, i: (ctx_row if ctx_mod else bi, 0, 0)),
            const(*w_out.shape), const(1, d), const(1, d),
        ],
        out_specs=[tile(d), tile(d)],
        out_shape=[jax.ShapeDtypeStruct((b, rows, d), F32), jax.ShapeDtypeStruct((b, rows, d), BF16)],
        compiler_params=_params("parallel", "arbitrary"),
        name="merge",
    )(a, bb, x, mod, w_out, g1, g2)


def _ffn_kernel(h_ref, hp_ref, hn_ref, x_ref, mod_ref, wa_ref, wg_ref, cwa_ref, cwg_ref, wd_ref, g_ref,
                o_ref, lhs_ref, acc_ref):
    i = pl.program_id(1)
    tm = h_ref.shape[1]
    hp = hp_ref[0]
    hn = hn_ref[0]
    lhs_ref[0:HALO_ROWS] = jnp.where(i == 0, jnp.zeros_like(hp), hp)
    lhs_ref[HALO_ROWS:HALO_ROWS + tm] = h_ref[0]
    lhs_ref[HALO_ROWS + tm:] = jnp.where(i == pl.num_programs(1) - 1, jnp.zeros_like(hn), hn)
    rows = tm + 2 * HALO_ROWS
    mid = slice(HALO_ROWS, HALO_ROWS + tm)

    def conv_up(w, cw):
        u = _dot(lhs_ref[...], w)
        down = pltpu.roll(u, 1, 0)[mid]
        up = pltpu.roll(u, rows - 1, 0)[mid]
        return down * cw[0:1] + u[mid] * cw[1:2] + up * cw[2:3] + cw[3:4]

    for j in range(wa_ref.shape[0]):
        a = conv_up(wa_ref[j], cwa_ref[j])
        g = conv_up(wg_ref[j], cwg_ref[j])
        act = (g * jax.nn.sigmoid(g)) * a
        y = _dot(act.astype(BF16), wd_ref[j])
        if j == 0:
            acc_ref[...] = y
        else:
            acc_ref[...] += y
    mod = mod_ref[0]
    o_ref[0] = x_ref[0] + mod[5:6] * _rms(acc_ref[...], g_ref[...])


def _ffn(h2, x1, mod, wa, wg, cwa, cwg, wd, g3, tm, row_offset, n_rows, ctx_mod):
    b, total_rows, d = h2.shape
    ctx_row = mod.shape[0] - 1
    nt = n_rows // tm
    off = row_offset // tm
    per = tm // HALO_ROWS
    last_halo = total_rows // HALO_ROWS - 1
    const = lambda *shape: pl.BlockSpec(shape, lambda bi, i: (0,) * len(shape), pipeline_mode=pl.Buffered(1))
    tile = pl.BlockSpec((1, tm, d), lambda bi, i: (bi, off + i, 0))
    return pl.pallas_call(
        _ffn_kernel,
        grid=(b, nt),
        in_specs=[
            tile,
            pl.BlockSpec((1, HALO_ROWS, d), lambda bi, i: (bi, jnp.maximum((off + i) * per - 1, 0), 0)),
            pl.BlockSpec((1, HALO_ROWS, d), lambda bi, i: (bi, jnp.minimum((off + i + 1) * per, last_halo), 0)),
            tile,
            pl.BlockSpec((1, 6, d), lambda bi, i: (ctx_row if ctx_mod else bi, 0, 0)),
            const(*wa.shape), const(*wg.shape), const(*cwa.shape), const(*cwg.shape), const(*wd.shape),
            const(1, d),
        ],
        out_specs=pl.BlockSpec((1, tm, d), lambda bi, i: (bi, i, 0)),
        out_shape=jax.ShapeDtypeStruct((b, n_rows, d), F32),
        scratch_shapes=[pltpu.VMEM((tm + 2 * HALO_ROWS, d), BF16), pltpu.VMEM((tm, d), F32)],
        compiler_params=_params("parallel", "arbitrary"),
        name="conv_ffn",
    )(h2, h2, h2, x1, mod, wa, wg, cwa, cwg, wd, g3)


def _proj_odd_kernel(x_ref, xp_ref, xn_ref, mod_ref, g_ref, wqkv_ref, wu_ref, cw_ref, cd_ref, sd_ref,
                     q_ref, k_ref, v_ref, u_ref, *, n_tiles):
    i = pl.program_id(1)
    mod = mod_ref[0]

    def normed(x):
        return (_rms(x, g_ref[...]) * (1.0 + mod[1:2]) + mod[0:1]).astype(BF16)

    h = normed(x_ref[0])
    p = _dot(h, wqkv_ref[...])
    cd, sd = cd_ref[...], sd_ref[...]
    scale = WIN_HEAD_DIM ** -0.5 * LOG2E
    for c in range(4):
        cs = slice(c * LANES, (c + 1) * LANES)
        q_ref[0, :, cs] = (_rope_chunk(p[:, cs], cd, sd, WIN_HEAD_DIM // 2) * scale).astype(BF16)
    for c in range(2):
        cs = slice(512 + c * LANES, 512 + (c + 1) * LANES)
        k_ref[0, :, c * LANES:(c + 1) * LANES] = _rope_chunk(p[:, cs], cd, sd, WIN_HEAD_DIM // 2).astype(BF16)
    v_ref[0] = p[:, 768:1024].astype(BF16)
    prev_ok = jnp.where(i == 0, 0.0, 1.0)
    next_ok = jnp.where(i == n_tiles - 1, 0.0, 1.0)
    wu = wu_ref[...]
    u = _dot(h, wu)
    up = _dot(normed(xp_ref[0]), wu)[HALO_ROWS - 1:HALO_ROWS] * prev_ok
    un = _dot(normed(xn_ref[0]), wu)[0:1] * next_ok
    down, upw = _shift_rows(u, up, un)
    cw = cw_ref[...]
    u_ref[0] = down * cw[0:1] + u * cw[1:2] + upw * cw[2:3] + cw[3:4]


def _proj_odd(xcat, mod, g, wqkv, wu, cw, cd, sd, n_tiles):
    b, _, d = xcat.shape
    per = ROW_TILE // HALO_ROWS
    last_halo = n_tiles * per - 1
    rows = n_tiles * ROW_TILE
    const = lambda *shape: pl.BlockSpec(shape, lambda bi, i: (0,) * len(shape))
    tab = pl.BlockSpec((ROW_TILE, LANES), lambda bi, i: (i, 0))
    out = lambda w: pl.BlockSpec((1, ROW_TILE, w), lambda bi, i: (bi, i, 0))
    return pl.pallas_call(
        functools.partial(_proj_odd_kernel, n_tiles=n_tiles),
        grid=(b, n_tiles),
        in_specs=[
            pl.BlockSpec((1, ROW_TILE, d), lambda bi, i: (bi, i, 0)),
            pl.BlockSpec((1, HALO_ROWS, d), lambda bi, i: (bi, jnp.maximum(i * per - 1, 0), 0)),
            pl.BlockSpec((1, HALO_ROWS, d), lambda bi, i: (bi, jnp.minimum((i + 1) * per, last_halo), 0)),
            pl.BlockSpec((1, 6, d), lambda bi, i: (bi, 0, 0)),
            const(1, d), const(*wqkv.shape), const(*wu.shape), const(*cw.shape), tab, tab,
        ],
        out_specs=[out(512), out(256), out(256), out(3 * HYENA_CH)],
        out_shape=[jax.ShapeDtypeStruct((b, rows, 512), BF16), jax.ShapeDtypeStruct((b, rows, 256), BF16),
                   jax.ShapeDtypeStruct((b, rows, 256), BF16), jax.ShapeDtypeStruct((b, rows, 3 * HYENA_CH), F32)],
        compiler_params=_params("parallel", "arbitrary"),
        name="proj_odd",
    )(xcat, xcat, xcat, mod, g, wqkv, wu, cw, cd, sd)


def _proj_ctx_kv_kernel(x_ref, mod_ref, g_ref, w_ref, k_ref, v_ref):
    mod = mod_ref[0]
    h = (_rms(x_ref[0], g_ref[...]) * (1.0 + mod[1:2]) + mod[0:1]).astype(BF16)
    p = _dot(h, w_ref[...])
    k_ref[0] = p[:, :2 * LANES].astype(BF16)
    v_ref[0] = p[:, 2 * LANES:].astype(BF16)


def _proj_ctx_kv(xcat, mod, g, wkv, ctx_tile, n_ctx):
    b, _, d = xcat.shape
    ctx_row = mod.shape[0] - 1
    out = pl.BlockSpec((1, n_ctx, 2 * LANES), lambda bi: (bi, 0, 0))
    return pl.pallas_call(
        _proj_ctx_kv_kernel,
        grid=(b,),
        in_specs=[
            pl.BlockSpec((1, n_ctx, d), lambda bi: (bi, ctx_tile, 0)),
            pl.BlockSpec((1, 6, d), lambda bi: (ctx_row, 0, 0)),
            pl.BlockSpec((1, d), lambda bi: (0, 0)),
            pl.BlockSpec(wkv.shape, lambda bi: (0, 0)),
        ],
        out_specs=[out, out],
        out_shape=[jax.ShapeDtypeStruct((b, n_ctx, 2 * LANES), BF16)] * 2,
        compiler_params=_params("parallel"),
        name="proj_ctx_kv",
    )(xcat, mod, g, wkv)


def _window_kernel(sink_ref, q_ref, kc_ref, vc_ref, kp_ref, km_ref, kn_ref, vp_ref, vm_ref, vn_ref, o_ref,
                   *, n_lat):
    i = pl.program_id(1)
    tq = q_ref.shape[1]
    n_ctx = kc_ref.shape[1]
    nk = n_ctx + tq + 2 * WINDOW
    col = lax.broadcasted_iota(jnp.int32, (tq, nk), 1)
    row = lax.broadcasted_iota(jnp.int32, (tq, nk), 0)
    k_pos = i * tq - WINDOW + (col - n_ctx)
    q_pos = i * tq + row
    visible = (col < n_ctx) | ((jnp.abs(q_pos - k_pos) <= WINDOW) & (k_pos >= 0) & (k_pos < n_lat))
    for hk in range(WIN_KV_HEADS):
        c = slice(hk * LANES, (hk + 1) * LANES)
        k_all = jnp.concatenate([kc_ref[0, :, c], kp_ref[0, :, c], km_ref[0, :, c], kn_ref[0, :, c]], axis=0)
        v_all = jnp.concatenate([vc_ref[0, :, c], vp_ref[0, :, c], vm_ref[0, :, c], vn_ref[0, :, c]], axis=0)
        first_v = _lane_iota(v_all.shape) < WIN_HEAD_DIM
        v_half = (jnp.where(first_v, v_all, jnp.zeros_like(v_all)), jnp.where(first_v, jnp.zeros_like(v_all), v_all))
        for pair in range(2):
            chunk = hk * 2 + pair
            q = q_ref[0, :, chunk * LANES:(chunk + 1) * LANES]
            first_q = _lane_iota(q.shape) < WIN_HEAD_DIM
            out = None
            for half in range(2):
                qh = jnp.where(first_q, q, jnp.zeros_like(q)) if half == 0 else jnp.where(first_q, jnp.zeros_like(q), q)
                s = jnp.where(visible, _dot_nt(qh, k_all), NEG_INF)
                sink = sink_ref[chunk * 2 + half] * LOG2E
                m = jnp.maximum(jnp.max(s, axis=-1, keepdims=True), sink)
                p = jnp.exp2(s - m)
                l = jnp.sum(p, axis=-1, keepdims=True) + jnp.exp2(sink - m)
                o = _dot(p.astype(BF16), v_half[half]) / l
                out = o if out is None else out + o
            o_ref[0, :, chunk * LANES:(chunk + 1) * LANES] = out.astype(o_ref.dtype)


def _window_attention(q, kd, vd, kc, vc, sink):
    b, n_lat, _ = q.shape
    tq = ROW_TILE
    n_ctx = kc.shape[1]
    per = tq // WINDOW
    last = n_lat // WINDOW - 1
    main = pl.BlockSpec((1, tq, 2 * LANES), lambda bi, i: (bi, i, 0))
    prev = pl.BlockSpec((1, WINDOW, 2 * LANES), lambda bi, i: (bi, jnp.maximum(i * per - 1, 0), 0))
    nxt = pl.BlockSpec((1, WINDOW, 2 * LANES), lambda bi, i: (bi, jnp.minimum((i + 1) * per, last), 0))
    ctx = pl.BlockSpec((1, n_ctx, 2 * LANES), lambda bi, i: (bi, 0, 0))
    return pl.pallas_call(
        functools.partial(_window_kernel, n_lat=n_lat),
        grid=(b, n_lat // tq),
        in_specs=[
            pl.BlockSpec(memory_space=pltpu.SMEM),
            pl.BlockSpec((1, tq, 4 * LANES), lambda bi, i: (bi, i, 0)),
            ctx, ctx, prev, main, nxt, prev, main, nxt,
        ],
        out_specs=pl.BlockSpec((1, tq, 4 * LANES), lambda bi, i: (bi, i, 0)),
        out_shape=jax.ShapeDtypeStruct((b, n_lat, 4 * LANES), BF16),
        compiler_params=_params("parallel", "arbitrary"),
        name="window_attention",
    )(sink, q, kc, vc, kd, kd, kd, vd, vd, vd)


def _hyena_filter_kernel(feat_ref, w1_ref, b1_ref, w2_ref, b2_ref, w3_ref, b3_ref, freq_ref, decay_ref, h_ref,
                         *, n_forward):
    j = pl.program_id(0)
    hp = lax.Precision.HIGHEST
    freq = freq_ref[...]
    hid = jnp.sin(freq * (jnp.dot(feat_ref[...], w1_ref[...], precision=hp, preferred_element_type=F32) + b1_ref[...]))
    hid = jnp.sin(freq * (jnp.dot(hid, w2_ref[...], precision=hp, preferred_element_type=F32) + b2_ref[...]))
    h = jnp.dot(hid, w3_ref[...], precision=hp, preferred_element_type=F32) + b3_ref[...]
    h = h * decay_ref[...]
    row = lax.broadcasted_iota(jnp.int32, h.shape, 0)
    h_ref[...] = jnp.where((row == 0) & (j >= n_forward), 0.0, h).astype(h_ref.dtype)


def _hyena_filters(feats, w1, b1, w2, b2, w3, b3, freq, decay):
    length = feats.shape[0]
    n = w3.shape[1]
    ch = decay.shape[1]
    full = lambda a: pl.BlockSpec(a.shape, lambda j: (0, 0))
    return pl.pallas_call(
        functools.partial(_hyena_filter_kernel, n_forward=HYENA_ORDER),
        grid=(n // ch,),
        in_specs=[full(feats), full(w1), full(b1), full(w2), full(b2),
                  pl.BlockSpec((w3.shape[0], ch), lambda j: (0, j)), pl.BlockSpec((1, ch), lambda j: (0, j)),
                  full(freq), full(decay)],
        out_specs=pl.BlockSpec((length, ch), lambda j: (0, j)),
        out_shape=jax.ShapeDtypeStruct((length, n), BF16),
        compiler_params=_params("arbitrary"),
        name="hyena_filters",
    )(feats, w1, b1, w2, b2, w3, b3, freq, decay)


def _spectrum_kernel(f_ref, h_ref, re_ref, im_ref):
    j = pl.program_id(0)
    tk = re_ref.shape[0]
    n = re_ref.shape[1]
    a = _dot(f_ref[0], h_ref[...])
    re_ref[...] = a[:tk, :n] + a[:tk, n:]
    row = lax.broadcasted_iota(jnp.int32, (tk, n), 0)
    sign = jnp.where((row == 0) & (j == 0), 1.0, -1.0)
    im_ref[...] = a[tk:, :n] + sign * a[tk:, n:]


def _filter_spectrum(f3, hcat):
    nt, tk2, length = f3.shape
    n = hcat.shape[1] // 2
    tk = tk2 // 2
    out = pl.BlockSpec((tk, n), lambda j: (j, 0))
    return pl.pallas_call(
        _spectrum_kernel,
        grid=(nt,),
        in_specs=[pl.BlockSpec((1, tk2, length), lambda j: (j, 0, 0)),
                  pl.BlockSpec(hcat.shape, lambda j: (0, 0))],
        out_specs=[out, out],
        out_shape=[jax.ShapeDtypeStruct((nt * tk, n), F32)] * 2,
        compiler_params=_params("arbitrary"),
        name="filter_spectrum",
    )(f3, hcat)


def _long_conv_kernel(z_ref, x_ref, f_ref, g_ref, hre_ref, him_ref, bias_ref, o_ref, zb_ref, acc_ref):
    j = pl.program_id(1)
    tk = hre_ref.shape[0]

    @pl.when(j == 0)
    def _():
        zb_ref[...] = z_ref[0].astype(BF16)
        acc_ref[...] = jnp.zeros_like(acc_ref)

    a = _dot(f_ref[0], zb_ref[...])
    a_re, a_im = a[:tk], a[tk:]
    h_re, h_im = hre_ref[...], him_ref[...]
    row = lax.broadcasted_iota(jnp.int32, a_re.shape, 0)
    packed = (row == 0) & (j == 0)
    ii = a_im * h_im
    p_re = a_re * h_re - jnp.where(packed, 0.0, ii)
    p_im = jnp.where(packed, ii, a_re * h_im + a_im * h_re)
    p = jnp.concatenate([p_re, p_im], axis=0).astype(BF16)
    acc_ref[...] += _dot(g_ref[0], p)

    @pl.when(j == pl.num_programs(1) - 1)
    def _():
        o_ref[0] = (x_ref[0] * (acc_ref[...] + bias_ref[...] * z_ref[0])).astype(o_ref.dtype)


def _long_conv(z_src, z_col, x_src, x_col, f3, g3, hre, him, h_col, bias, out_dtype):
    b, length, _ = z_src.shape
    nt, tk2, _ = f3.shape
    tk = tk2 // 2
    ch = HYENA_CH
    return pl.pallas_call(
        _long_conv_kernel,
        grid=(b, nt),
        in_specs=[
            pl.BlockSpec((1, length, ch), lambda bi, j: (bi, 0, z_col)),
            pl.BlockSpec((1, length, ch), lambda bi, j: (bi, 0, x_col)),
            pl.BlockSpec((1, tk2, length), lambda bi, j: (j, 0, 0)),
            pl.BlockSpec((1, length, tk2), lambda bi, j: (j, 0, 0)),
            pl.BlockSpec((tk, ch), lambda bi, j: (j, h_col)),
            pl.BlockSpec((tk, ch), lambda bi, j: (j, h_col)),
            pl.BlockSpec((1, ch), lambda bi, j: (0, 0)),
        ],
        out_specs=pl.BlockSpec((1, length, ch), lambda bi, j: (bi, 0, 0)),
        out_shape=jax.ShapeDtypeStruct((b, length, ch), out_dtype),
        scratch_shapes=[pltpu.VMEM((length, ch), BF16), pltpu.VMEM((length, ch), F32)],
        compiler_params=_params("parallel", "arbitrary"),
        name="long_conv",
    )(z_src, x_src, f3, g3, hre, him, bias)


def _axial_rope(length, rot_dim):
    rows = length // GRID_W
    row = jnp.repeat(jnp.arange(rows), GRID_W).astype(F32)
    col = jnp.tile(jnp.arange(GRID_W), rows).astype(F32)
    quarter = rot_dim // 4
    inv = ROPE_BASE ** (-jnp.arange(quarter, dtype=F32) / quarter)
    ang = jnp.concatenate([row[:, None] * inv, col[:, None] * inv], axis=-1)
    return jnp.cos(ang), jnp.sin(ang)


def _rope_tables(length, n_ctx):
    cos_m, sin_m = _axial_rope(length, MLA_ROPE)
    cos_d, sin_d = _axial_rope(length, DIFF_HEAD_DIM)
    ones = lambda w: jnp.ones((length, w), F32)
    zeros = lambda w: jnp.zeros((length, w), F32)
    cm = jnp.concatenate([ones(MLA_NOPE), cos_m, cos_m, ones(LANES - MLA_NOPE - MLA_ROPE)], axis=1)
    sm = jnp.concatenate([zeros(MLA_NOPE), -sin_m, sin_m, zeros(LANES - MLA_NOPE - MLA_ROPE)], axis=1)
    cd = jnp.concatenate([cos_d] * 4, axis=1)
    sd = jnp.concatenate([-sin_d, sin_d] * 2, axis=1)
    pad = lambda t, v: jnp.concatenate([t, jnp.full((n_ctx, LANES), v, F32)], axis=0)
    return pad(cm, 1.0), pad(sm, 0.0), pad(cd, 1.0), pad(sd, 0.0)


def _dft_matrices(length):
    n = 2 * length
    k = np.arange(length, dtype=np.int64)
    ang = 2.0 * np.pi * ((k[:, None] * k[None, :]) % n).astype(np.float64) / n
    alt = np.where(k % 2 == 0, 1.0, -1.0)
    f_re = np.cos(ang)
    f_im = -np.sin(ang)
    f_im[0, :] = alt
    g_re = (2.0 / n) * np.cos(ang.T)
    g_im = -(2.0 / n) * np.sin(ang.T)
    g_re[:, 0] = 1.0 / n
    g_im[:, 0] = alt / n
    nt = length // FREQ_TILE
    f3 = np.concatenate([f_re.reshape(nt, FREQ_TILE, length), f_im.reshape(nt, FREQ_TILE, length)], axis=1)
    g3 = np.concatenate([g_re.reshape(length, nt, FREQ_TILE), g_im.reshape(length, nt, FREQ_TILE)], axis=2)
    g3 = np.transpose(g3, (1, 0, 2))
    return jnp.asarray(f3, dtype=BF16), jnp.asarray(g3, dtype=BF16)


def _hyena_features(length):
    t = jnp.arange(length, dtype=F32)
    tn = t / max(length - 1, 1)
    bands = jnp.linspace(1e-4, HYENA_BANDS - 1, HYENA_BANDS, dtype=F32)
    ang = 2.0 * math.pi * bands[None, :] * t[:, None] / length
    feats = jnp.concatenate([tn[:, None], jnp.cos(ang), -jnp.sin(ang)], axis=-1)
    min_decay = math.log(HYENA_DECAY_TARGET) / HYENA_SLOW_PCT
    max_decay = math.log(HYENA_DECAY_TARGET) / HYENA_FAST_PCT
    deltas = jnp.abs(jnp.linspace(min_decay, max_decay, HYENA_CH, dtype=F32))
    decay = jnp.exp(-tn[:, None] * deltas[None, :])
    return feats, decay


def _pad_cols(w, left, right):
    return jnp.pad(w, ((0, 0),) * (w.ndim - 1) + ((left, right),))


def _even_weights(w_in, w_uq, w_ukv):
    d = w_in.shape[0]
    o = MLA_Q_RANK + MLA_KV_RANK
    w_kr = _pad_cols(w_in[:, o:o + MLA_ROPE], MLA_NOPE, LANES - MLA_NOPE - MLA_ROPE)
    w_in_ext = jnp.concatenate([w_in[:, :o], w_kr, w_in[:, o + MLA_ROPE:]], axis=1)
    uq = _pad_cols(w_uq.reshape(MLA_Q_RANK, MLA_HEADS, MLA_NOPE + MLA_ROPE), 0, LANES - MLA_NOPE - MLA_ROPE)
    ukv = w_ukv.reshape(MLA_KV_RANK, MLA_HEADS, MLA_NOPE + MLA_V)
    uk = _pad_cols(ukv[:, :, :MLA_NOPE], 0, LANES - MLA_NOPE)
    uv = ukv[:, :, MLA_NOPE:].reshape(MLA_KV_RANK, MLA_HEADS // 2, 2, MLA_V)
    uv = jnp.stack([_pad_cols(uv[:, :, 0], 0, LANES - MLA_V), _pad_cols(uv[:, :, 1], LANES - MLA_V, 0)], axis=2)
    w_ukv_ext = jnp.concatenate([uk.reshape(MLA_KV_RANK, -1), uv.reshape(MLA_KV_RANK, -1)], axis=1)
    return w_in_ext.astype(BF16), uq.reshape(MLA_Q_RANK, -1).astype(BF16), w_ukv_ext.astype(BF16)


def _odd_weights(w_in):
    d = w_in.shape[0]
    nq = WIN_Q_HEADS * WIN_HEAD_DIM
    nkv = WIN_KV_HEADS * WIN_HEAD_DIM
    dup = lambda w: jnp.concatenate([w.reshape(d, WIN_KV_HEADS, WIN_HEAD_DIM)] * 2, axis=-1).reshape(d, -1)
    wk = dup(w_in[:, nq:nq + nkv])
    wv = dup(w_in[:, nq + nkv:nq + 2 * nkv])
    wqkv = jnp.concatenate([w_in[:, :nq], wk, wv], axis=1).astype(BF16)
    wkv = jnp.concatenate([wk, wv], axis=1).astype(BF16)
    wu = w_in[:, nq + 2 * nkv:].astype(BF16)
    return wqkv, wkv, wu


def _ffn_weights(w_up, conv_w, conv_b, w_down):
    d = w_up.shape[0]
    nc = D_FF // FFN_CHUNK
    chunks = lambda w: jnp.transpose(w.reshape(w.shape[0], nc, FFN_CHUNK), (1, 0, 2))
    cw = jnp.concatenate([conv_w, conv_b[None, :]], axis=0)
    return (chunks(w_up[:, :D_FF]).astype(BF16), chunks(w_up[:, D_FF:]).astype(BF16),
            chunks(cw[:, :D_FF]), chunks(cw[:, D_FF:]),
            w_down.reshape(nc, FFN_CHUNK, d).astype(BF16))


def kernel(x, c, ctx, c_ctx, ada_w, ada_b, norm_g, mix_w_out, ffn_w_up, ffn_conv_w, ffn_conv_b, ffn_w_down, even_w_in, mla_q_norm_g, mla_w_uq, mla_kv_norm_g, mla_w_ukv, diff_lambda, diff_subln_g, odd_w_in, win_sink, hy_conv_w, hy_conv_b, hy_f_w1, hy_f_b1, hy_f_w2, hy_f_b2, hy_f_w3, hy_f_b3, hy_f_freq, hy_bias):
    b, seq, d = x.shape
    n_ctx = ctx.shape[1]
    depth = ada_w.shape[0]
    assert depth == 2 and d == D_MODEL and seq % ROW_TILE == 0 and n_ctx == ROW_TILE
    n_lat_tiles = seq // ROW_TILE
    n_tiles = n_lat_tiles + 1

    pad_rows = (-(b + 1)) % 8
    cond = jnp.concatenate([c, c_ctx[None, :], jnp.zeros((pad_rows, d), F32)], axis=0)
    mod = _ada_table(cond, ada_w, ada_b).reshape(depth, b + 1 + pad_rows, 6, d)[:, :b + 1]
    g = norm_g.reshape(depth, 4, 1, d)
    cm, sm, cd, sd = _rope_tables(seq, n_ctx)
    xcat = jnp.concatenate([x, ctx], axis=1)

    lambda_init = 0.8 - 0.6 * math.exp(-0.3 * 0)
    w_in_e, w_uq_e, w_ukv_e = _even_weights(even_w_in[0], mla_w_uq[0], mla_w_ukv[0])
    qm, km, vm, dq, dk, dv = _proj_even(
        xcat, mod[0], g[0, 0], w_in_e, mla_q_norm_g[0][None, :], w_uq_e, mla_kv_norm_g[0][None, :], w_ukv_e,
        cm, sm, cd, sd, n_lat_tiles)
    rows = seq + n_ctx
    lam_args = (diff_lambda[0], diff_subln_g[0][None, :], lambda_init)
    w_out0 = mix_w_out[0].astype(BF16)
    ffn0 = _ffn_weights(ffn_w_up[0], ffn_conv_w[0], ffn_conv_b[0], ffn_w_down[0])
    o_mla = _mla_attention(qm, km, vm, ATTN_Q_TILE, 0, seq, 0, rows)
    o_diff = _diff_attention(dq, dk, dv, *lam_args, ATTN_Q_TILE, 0, seq, 0, rows)
    x1, h2 = _merge(o_mla, o_diff, xcat, mod[0], w_out0, g[0, 1], g[0, 2], ROW_TILE, 0, False)
    x_lat = _ffn(h2, x1, mod[0], *ffn0, g[0, 3], FFN_ROW_TILE, 0, seq, False)
    o_mla = _mla_attention(qm, km, vm, n_ctx, seq, n_ctx, seq, n_ctx)
    o_diff = _diff_attention(dq, dk, dv, *lam_args, n_ctx, seq, n_ctx, seq, n_ctx)
    x1, h2 = _merge(o_mla, o_diff, xcat, mod[0], w_out0, g[0, 1], g[0, 2], n_ctx, seq, True)
    x_ctx = _ffn(h2, x1, mod[0], *ffn0, g[0, 3], n_ctx, 0, n_ctx, True)

    wqkv, wkv, wu = _odd_weights(odd_w_in[0])
    cw_hy = jnp.concatenate([hy_conv_w[0], hy_conv_b[0][None, :]], axis=0)
    q, kd, vd, u = _proj_odd(x_lat, mod[1], g[1, 0], wqkv, wu, cw_hy, cd, sd, n_lat_tiles)
    kc, vc = _proj_ctx_kv(x_ctx, mod[1], g[1, 0], wkv, 0, n_ctx)
    o_win = _window_attention(q, kd, vd, kc, vc, win_sink[0])

    feats, decay = _hyena_features(seq)
    emb = feats.shape[1]
    feats = _pad_cols(feats, 0, LANES - emb)
    w1 = jnp.pad(hy_f_w1[0], ((0, LANES - emb), (0, 0)))
    hcat = _hyena_filters(feats, w1, hy_f_b1[0][None, :], hy_f_w2[0], hy_f_b2[0][None, :], hy_f_w3[0],
                          hy_f_b3[0][None, :], hy_f_freq[0][None, :], decay)
    f3, g3 = _dft_matrices(seq)
    hre, him = _filter_spectrum(f3, hcat)
    z = _long_conv(u, 0, u, 1, f3, g3, hre, him, 0, hy_bias[0, 0][None, :], F32)
    z = _long_conv(z, 0, u, 2, f3, g3, hre, him, 1, hy_bias[0, 1][None, :], BF16)

    w_out1 = mix_w_out[1].astype(BF16)
    x1, h2 = _merge(o_win, z, x_lat, mod[1], w_out1, g[1, 1], g[1, 2], ROW_TILE, 0, False)
    ffn1 = _ffn_weights(ffn_w_up[1], ffn_conv_w[1], ffn_conv_b[1], ffn_w_down[1])
    return _ffn(h2, x1, mod[1], *ffn1, g[1, 3], FFN_ROW_TILE, 0, seq, False)
```

```python
import functools
import math

import numpy as np
import jax
import jax.numpy as jnp
from jax import lax
from jax.experimental import pallas as pl
from jax.experimental.pallas import tpu as pltpu

D_MODEL = 1024
GRID_W = 64
MLA_HEADS = 8
MLA_NOPE = 64
MLA_ROPE = 32
MLA_V = 64
MLA_Q_RANK = 384
MLA_KV_RANK = 256
DIFF_HEADS = 4
DIFF_HEAD_DIM = 64
DIFF_V_DIM = 128
WIN_Q_HEADS = 8
WIN_KV_HEADS = 2
WIN_HEAD_DIM = 64
WINDOW = 128
HYENA_CH = 512
HYENA_ORDER = 2
HYENA_BANDS = 16
HYENA_DECAY_TARGET = 1e-2
HYENA_FAST_PCT = 0.3
HYENA_SLOW_PCT = 1.5
D_FF = 2816
ROPE_BASE = 10000.0
NORM_EPS = 1e-6
NEG_INF = -1e30

LANES = 128
HALO_ROWS = 16
ROW_TILE = 256
ATTN_Q_TILE = 1024
ATTN_SUB_TILE = 256
LOG2E = math.log2(math.e)
FFN_ROW_TILE = 512
FFN_CHUNK = 256
FREQ_TILE = 256
VMEM_LIMIT = 48 * 1024 * 1024

BF16 = jnp.bfloat16
F32 = jnp.float32


def _params(*semantics):
    return pltpu.CompilerParams(dimension_semantics=semantics, vmem_limit_bytes=VMEM_LIMIT)


def _rms(x, g):
    return x * lax.rsqrt(jnp.mean(x * x, axis=-1, keepdims=True) + NORM_EPS) * g


def _dot(a, b):
    return jnp.dot(a, b, preferred_element_type=F32)


def _dot_nt(a, b):
    return lax.dot_general(a, b, (((1,), (1,)), ((), ())), preferred_element_type=F32)


def _lane_iota(shape):
    return lax.broadcasted_iota(jnp.int32, shape, len(shape) - 1)


def _rope_chunk(x, cos, sin_signed, half):
    first = (_lane_iota(x.shape) % (2 * half)) < half
    partner = jnp.where(first, pltpu.roll(x, LANES - half, 1), pltpu.roll(x, half, 1))
    return x * cos + partner * sin_signed


def _shift_rows(u, prev_row, next_row):
    rows = u.shape[0]
    r = lax.broadcasted_iota(jnp.int32, u.shape, 0)
    down = jnp.where(r == 0, prev_row, pltpu.roll(u, 1, 0))
    up = jnp.where(r == rows - 1, next_row, pltpu.roll(u, rows - 1, 0))
    return down, up


def _ada_kernel(c_ref, w_ref, b_ref, o_ref):
    c = c_ref[...]
    sc = c * jax.nn.sigmoid(c)
    o_ref[0] = _dot(sc.astype(BF16), w_ref[0].astype(BF16)) + b_ref[0]


def _ada_table(cond, ada_w, ada_b):
    depth, d, n = ada_w.shape
    rows = cond.shape[0]
    tn = 1536
    return pl.pallas_call(
        _ada_kernel,
        grid=(depth, n // tn),
        in_specs=[
            pl.BlockSpec((rows, d), lambda l, j: (0, 0)),
            pl.BlockSpec((1, d, tn), lambda l, j: (l, 0, j)),
            pl.BlockSpec((1, 1, tn), lambda l, j: (l, 0, j)),
        ],
        out_specs=pl.BlockSpec((1, rows, tn), lambda l, j: (l, 0, j)),
        out_shape=jax.ShapeDtypeStruct((depth, rows, n), F32),
        compiler_params=_params("arbitrary", "arbitrary"),
        name="ada_table",
    )(cond, ada_w, ada_b.reshape(depth, 1, n))


def _proj_even_kernel(x_ref, mod_ref, g_ref, win_ref, qg_ref, wuq_ref, kvg_ref, wukv_ref,
                      cm_ref, sm_ref, cd_ref, sd_ref,
                      qm_ref, km_ref, vm_ref, dq_ref, dk_ref, dv_ref):
    x = x_ref[0]
    mod = mod_ref[0]
    h = _rms(x, g_ref[...]) * (1.0 + mod[1:2]) + mod[0:1]
    p = _dot(h.astype(BF16), win_ref[...])
    o = 0
    cq = p[:, o:o + MLA_Q_RANK]; o += MLA_Q_RANK
    ckv = p[:, o:o + MLA_KV_RANK]; o += MLA_KV_RANK
    kr = p[:, o:o + LANES]; o += LANES
    dq = p[:, o:o + 512]; o += 512
    dk = p[:, o:o + 512]; o += 512
    dv = p[:, o:o + 512]
    q = _dot(_rms(cq, qg_ref[...]).astype(BF16), wuq_ref[...])
    kv = _dot(_rms(ckv, kvg_ref[...]).astype(BF16), wukv_ref[...])
    cm, sm, cd, sd = cm_ref[...], sm_ref[...], cd_ref[...], sd_ref[...]
    mla_scale = (MLA_NOPE + MLA_ROPE) ** -0.5 * LOG2E
    diff_scale = DIFF_HEAD_DIM ** -0.5 * LOG2E
    kr = _rope_chunk(kr, cm, sm, MLA_ROPE // 2)
    for hd in range(MLA_HEADS):
        c = slice(hd * LANES, (hd + 1) * LANES)
        qm_ref[0, :, c] = (_rope_chunk(q[:, c], cm, sm, MLA_ROPE // 2) * mla_scale).astype(BF16)
        km_ref[0, c, :] = (kv[:, c] + kr).T.astype(BF16)
    vm_ref[0] = kv[:, MLA_HEADS * LANES:].astype(BF16)
    for hd in range(DIFF_HEADS):
        c = slice(hd * LANES, (hd + 1) * LANES)
        dq_ref[0, :, c] = (_rope_chunk(dq[:, c], cd, sd, DIFF_HEAD_DIM // 2) * diff_scale).astype(BF16)
        dk_ref[0, c, :] = _rope_chunk(dk[:, c], cd, sd, DIFF_HEAD_DIM // 2).T.astype(BF16)
    dv_ref[0] = dv.astype(BF16)


def _proj_even(xcat, mod, g, w_in, qg, w_uq, kvg, w_ukv, cm, sm, cd, sd, n_lat_tiles):
    b, rows, d = xcat.shape
    nt = rows // ROW_TILE
    ctx_row = mod.shape[0] - 1
    const = lambda *shape: pl.BlockSpec(shape, lambda bi, i: (0,) * len(shape))
    tab = pl.BlockSpec((ROW_TILE, LANES), lambda bi, i: (i, 0))
    out = lambda w: pl.BlockSpec((1, ROW_TILE, w), lambda bi, i: (bi, i, 0))
    shp = lambda w: jax.ShapeDtypeStruct((b, rows, w), BF16)
    out_t = lambda w: pl.BlockSpec((1, w, ROW_TILE), lambda bi, i: (bi, 0, i))
    shp_t = lambda w: jax.ShapeDtypeStruct((b, w, rows), BF16)
    return pl.pallas_call(
        _proj_even_kernel,
        grid=(b, nt),
        in_specs=[
            pl.BlockSpec((1, ROW_TILE, d), lambda bi, i: (bi, i, 0)),
            pl.BlockSpec((1, 6, d), lambda bi, i: (jnp.where(i < n_lat_tiles, bi, ctx_row), 0, 0)),
            const(1, d), const(*w_in.shape), const(1, MLA_Q_RANK), const(*w_uq.shape),
            const(1, MLA_KV_RANK), const(*w_ukv.shape), tab, tab, tab, tab,
        ],
        out_specs=[out(1024), out_t(1024), out(1024), out(512), out_t(512), out(512)],
        out_shape=[shp(1024), shp_t(1024), shp(1024), shp(512), shp_t(512), shp(512)],
        compiler_params=_params("parallel", "arbitrary"),
        name="proj_even",
    )(xcat, mod, g, w_in, qg, w_uq, kvg, w_ukv, cm, sm, cd, sd)


def _attend(q, kt, v):
    s = _dot(q, kt)
    m = jnp.max(s, axis=-1, keepdims=True)
    p = jnp.exp2(s - m)
    l = jnp.sum(p, axis=-1, keepdims=True)
    return _dot(p.astype(BF16), v) / l


def _mla_kernel(q_ref, k_ref, v_ref, o_ref):
    sub = min(q_ref.shape[1], ATTN_SUB_TILE)
    for r in range(q_ref.shape[1] // sub):
        rs = slice(r * sub, (r + 1) * sub)
        acc = None
        for hd in range(2):
            c = slice(hd * LANES, (hd + 1) * LANES)
            o = _attend(q_ref[0, rs, c], k_ref[0, c, :], v_ref[0, :, c])
            acc = o if acc is None else acc + o
        o_ref[0, rs, :] = acc.astype(o_ref.dtype)


def _mla_attention(qm, km, vm, tq, q_off, n_q, k_off, n_k):
    b = qm.shape[0]
    pairs = MLA_HEADS // 2
    qo, ko = q_off // tq, k_off // n_k
    return pl.pallas_call(
        _mla_kernel,
        grid=(b, pairs, n_q // tq),
        in_specs=[
            pl.BlockSpec((1, tq, 2 * LANES), lambda bi, p, i: (bi, qo + i, p)),
            pl.BlockSpec((1, 2 * LANES, n_k), lambda bi, p, i: (bi, p, ko)),
            pl.BlockSpec((1, n_k, 2 * LANES), lambda bi, p, i: (bi, ko, p)),
        ],
        out_specs=pl.BlockSpec((1, tq, LANES), lambda bi, p, i: (bi, i, p)),
        out_shape=jax.ShapeDtypeStruct((b, n_q, pairs * LANES), BF16),
        compiler_params=_params("parallel", "arbitrary", "arbitrary"),
        name="mla_attention",
    )(qm, km, vm)


def _diff_kernel(q_ref, k_ref, v_ref, lam_ref, g_ref, o_ref, *, lambda_init):
    lv = lam_ref[...]
    lam = (jnp.exp(jnp.sum(lv[0:1] * lv[1:2], axis=-1, keepdims=True))
           - jnp.exp(jnp.sum(lv[2:3] * lv[3:4], axis=-1, keepdims=True)) + lambda_init)
    kt = k_ref[0]
    v = v_ref[0]
    sub = min(q_ref.shape[1], ATTN_SUB_TILE)
    for r in range(q_ref.shape[1] // sub):
        rs = slice(r * sub, (r + 1) * sub)
        q = q_ref[0, rs, :]
        first = _lane_iota(q.shape) < DIFF_HEAD_DIM
        zero = jnp.zeros_like(q)
        o1 = _attend(jnp.where(first, q, zero), kt, v)
        o2 = _attend(jnp.where(first, zero, q), kt, v)
        o = o1 - lam * o2
        o_ref[0, rs, :] = (_rms(o, g_ref[...]) * (1.0 - lambda_init)).astype(o_ref.dtype)


def _diff_attention(dq, dk, dv, diff_lambda, subln_g, lambda_init, tq, q_off, n_q, k_off, n_k):
    b = dq.shape[0]
    qo, ko = q_off // tq, k_off // n_k
    return pl.pallas_call(
        functools.partial(_diff_kernel, lambda_init=lambda_init),
        grid=(b, DIFF_HEADS, n_q // tq),
        in_specs=[
            pl.BlockSpec((1, tq, LANES), lambda bi, h, i: (bi, qo + i, h)),
            pl.BlockSpec((1, LANES, n_k), lambda bi, h, i: (bi, h, ko)),
            pl.BlockSpec((1, n_k, LANES), lambda bi, h, i: (bi, ko, h)),
            pl.BlockSpec((4, DIFF_HEAD_DIM), lambda bi, h, i: (0, 0)),
            pl.BlockSpec((1, DIFF_V_DIM), lambda bi, h, i: (0, 0)),
        ],
        out_specs=pl.BlockSpec((1, tq, LANES), lambda bi, h, i: (bi, i, h)),
        out_shape=jax.ShapeDtypeStruct((b, n_q, DIFF_HEADS * LANES), BF16),
        compiler_params=_params("parallel", "arbitrary", "arbitrary"),
        name="diff_attention",
    )(dq, dk, dv, diff_lambda, subln_g)


def _merge_kernel(a_ref, b_ref, x_ref, mod_ref, w_ref, g1_ref, g2_ref, x1_ref, h2_ref):
    half = a_ref.shape[-1]
    mod = mod_ref[0]
    y = _dot(a_ref[0], w_ref[0:half, :]) + _dot(b_ref[0], w_ref[half:, :])
    x1 = x_ref[0] + mod[2:3] * _rms(y, g1_ref[...])
    x1_ref[0] = x1
    h2_ref[0] = (_rms(x1, g2_ref[...]) * (1.0 + mod[4:5]) + mod[3:4]).astype(BF16)


def _merge(a, bb, x, mod, w_out, g1, g2, tm, x_off, ctx_mod):
    b, rows, half = a.shape
    d = x.shape[-1]
    nt = rows // tm
    xo = x_off // tm
    ctx_row = mod.shape[0] - 1
    tile = lambda w: pl.BlockSpec((1, tm, w), lambda bi, i: (bi, i, 0))
    const = lambda *shape: pl.BlockSpec(shape, lambda bi, i: (0,) * len(shape))
    return pl.pallas_call(
        _merge_kernel,
        grid=(b, nt),
        in_specs=[
            tile(half), tile(half), pl.BlockSpec((1, tm, d), lambda bi, i: (bi, xo + i, 0)),
            pl.BlockSpec((1, 6, d), lambda bi, i: (ctx_row if ctx_mod else bi, 0, 0)),
            const(*w_out.shape), const(1, d), const(1, d),
        ],
        out_specs=[tile(d), tile(d)],
        out_shape=[jax.ShapeDtypeStruct((b, rows, d), F32), jax.ShapeDtypeStruct((b, rows, d), BF16)],
        compiler_params=_params("parallel", "arbitrary"),
        name="merge",
    )(a, bb, x, mod, w_out, g1, g2)


def _ffn_kernel(h_ref, hp_ref, hn_ref, x_ref, mod_ref, wa_ref, wg_ref, cwa_ref, cwg_ref, wd_ref, g_ref,
                o_ref, lhs_ref, ua0_ref, ug0_ref, ua1_ref, ug1_ref, acc_ref):
    i = pl.program_id(1)
    tm = h_ref.shape[1]
    n_chunks = wa_ref.shape[0]
    hp = hp_ref[0]
    hn = hn_ref[0]
    lhs_ref[0:HALO_ROWS] = jnp.where(i == 0, jnp.zeros_like(hp), hp)
    lhs_ref[HALO_ROWS:HALO_ROWS + tm] = h_ref[0]
    lhs_ref[HALO_ROWS + tm:] = jnp.where(i == pl.num_programs(1) - 1, jnp.zeros_like(hn), hn)
    rows = tm + 2 * HALO_ROWS
    mid = slice(HALO_ROWS, HALO_ROWS + tm)
    slots = ((ua0_ref, ug0_ref), (ua1_ref, ug1_ref))

    def up(j, slot):
        lhs = lhs_ref[...]
        slots[slot][0][...] = _dot(lhs, wa_ref[j])
        slots[slot][1][...] = _dot(lhs, wg_ref[j])

    def conv(u, cw):
        down = pltpu.roll(u, 1, 0)[mid]
        above = pltpu.roll(u, rows - 1, 0)[mid]
        return down * cw[0:1] + u[mid] * cw[1:2] + above * cw[2:3] + cw[3:4]

    def down(j, slot, first=False):
        a = conv(slots[slot][0][...], cwa_ref[j])
        g = conv(slots[slot][1][...], cwg_ref[j])
        act = (g * jax.nn.sigmoid(g)) * a
        y = _dot(act.astype(BF16), wd_ref[j])
        if first:
            acc_ref[...] = y
        else:
            acc_ref[...] += y

    up(0, 0)
    up(1, 1)
    down(0, 0, first=True)
    for j in range(2, n_chunks):
        up(j, j % 2)
        down(j - 1, (j - 1) % 2)
    down(n_chunks - 1, (n_chunks - 1) % 2)
    mod = mod_ref[0]
    o_ref[0] = x_ref[0] + mod[5:6] * _rms(acc_ref[...], g_ref[...])


def _ffn(h2, x1, mod, wa, wg, cwa, cwg, wd, g3, tm, row_offset, n_rows, ctx_mod):
    b, total_rows, d = h2.shape
    ctx_row = mod.shape[0] - 1
    nt = n_rows // tm
    off = row_offset // tm
    per = tm // HALO_ROWS
    last_halo = total_rows // HALO_ROWS - 1
    const = lambda *shape: pl.BlockSpec(shape, lambda bi, i: (0,) * len(shape), pipeline_mode=pl.Buffered(1))
    tile = pl.BlockSpec((1, tm, d), lambda bi, i: (bi, off + i, 0))
    return pl.pallas_call(
        _ffn_kernel,
        grid=(b, nt),
        in_specs=[
            tile,
            pl.BlockSpec((1, HALO_ROWS, d), lambda bi, i: (bi, jnp.maximum((off + i) * per - 1, 0), 0)),
            pl.BlockSpec((1, HALO_ROWS, d), lambda bi, i: (bi, jnp.minimum((off + i + 1) * per, last_halo), 0)),
            tile,
            pl.BlockSpec((1, 6, d), lambda bi, i: (ctx_row if ctx_mod else bi, 0, 0)),
            const(*wa.shape), const(*wg.shape), const(*cwa.shape), const(*cwg.shape), const(*wd.shape),
            const(1, d),
        ],
        out_specs=pl.BlockSpec((1, tm, d), lambda bi, i: (bi, i, 0)),
        out_shape=jax.ShapeDtypeStruct((b, n_rows, d), F32),
        scratch_shapes=[pltpu.VMEM((tm + 2 * HALO_ROWS, d), BF16)]
        + [pltpu.VMEM((tm + 2 * HALO_ROWS, FFN_CHUNK), F32)] * 4 + [pltpu.VMEM((tm, d), F32)],
        compiler_params=_params("parallel", "arbitrary"),
        name="conv_ffn",
    )(h2, h2, h2, x1, mod, wa, wg, cwa, cwg, wd, g3)


def _proj_odd_kernel(x_ref, xp_ref, xn_ref, mod_ref, g_ref, wqkv_ref, wu_ref, cw_ref, cd_ref, sd_ref,
                     q_ref, k_ref, v_ref, u_ref, *, n_tiles):
    i = pl.program_id(1)
    mod = mod_ref[0]

    def normed(x):
        return (_rms(x, g_ref[...]) * (1.0 + mod[1:2]) + mod[0:1]).astype(BF16)

    h = normed(x_ref[0])
    p = _dot(h, wqkv_ref[...])
    cd, sd = cd_ref[...], sd_ref[...]
    scale = WIN_HEAD_DIM ** -0.5 * LOG2E
    for c in range(4):
        cs = slice(c * LANES, (c + 1) * LANES)
        q_ref[0, :, cs] = (_rope_chunk(p[:, cs], cd, sd, WIN_HEAD_DIM // 2) * scale).astype(BF16)
    for c in range(2):
        cs = slice(512 + c * LANES, 512 + (c + 1) * LANES)
        k_ref[0, :, c * LANES:(c + 1) * LANES] = _rope_chunk(p[:, cs], cd, sd, WIN_HEAD_DIM // 2).astype(BF16)
    v_ref[0] = p[:, 768:1024].astype(BF16)
    prev_ok = jnp.where(i == 0, 0.0, 1.0)
    next_ok = jnp.where(i == n_tiles - 1, 0.0, 1.0)
    wu = wu_ref[...]
    u = _dot(h, wu)
    up = _dot(normed(xp_ref[0]), wu)[HALO_ROWS - 1:HALO_ROWS] * prev_ok
    un = _dot(normed(xn_ref[0]), wu)[0:1] * next_ok
    down, upw = _shift_rows(u, up, un)
    cw = cw_ref[...]
    u_ref[0] = down * cw[0:1] + u * cw[1:2] + upw * cw[2:3] + cw[3:4]


def _proj_odd(xcat, mod, g, wqkv, wu, cw, cd, sd, n_tiles):
    b, _, d = xcat.shape
    per = ROW_TILE // HALO_ROWS
    last_halo = n_tiles * per - 1
    rows = n_tiles * ROW_TILE
    const = lambda *shape: pl.BlockSpec(shape, lambda bi, i: (0,) * len(shape))
    tab = pl.BlockSpec((ROW_TILE, LANES), lambda bi, i: (i, 0))
    out = lambda w: pl.BlockSpec((1, ROW_TILE, w), lambda bi, i: (bi, i, 0))
    return pl.pallas_call(
        functools.partial(_proj_odd_kernel, n_tiles=n_tiles),
        grid=(b, n_tiles),
        in_specs=[
            pl.BlockSpec((1, ROW_TILE, d), lambda bi, i: (bi, i, 0)),
            pl.BlockSpec((1, HALO_ROWS, d), lambda bi, i: (bi, jnp.maximum(i * per - 1, 0), 0)),
            pl.BlockSpec((1, HALO_ROWS, d), lambda bi, i: (bi, jnp.minimum((i + 1) * per, last_halo), 0)),
            pl.BlockSpec((1, 6, d), lambda bi, i: (bi, 0, 0)),
            const(1, d), const(*wqkv.shape), const(*wu.shape), const(*cw.shape), tab, tab,
        ],
        out_specs=[out(512), out(256), out(256), out(3 * HYENA_CH)],
        out_shape=[jax.ShapeDtypeStruct((b, rows, 512), BF16), jax.ShapeDtypeStruct((b, rows, 256), BF16),
                   jax.ShapeDtypeStruct((b, rows, 256), BF16), jax.ShapeDtypeStruct((b, rows, 3 * HYENA_CH), F32)],
        compiler_params=_params("parallel", "arbitrary"),
        name="proj_odd",
    )(xcat, xcat, xcat, mod, g, wqkv, wu, cw, cd, sd)


def _proj_ctx_kv_kernel(x_ref, mod_ref, g_ref, w_ref, k_ref, v_ref):
    mod = mod_ref[0]
    h = (_rms(x_ref[0], g_ref[...]) * (1.0 + mod[1:2]) + mod[0:1]).astype(BF16)
    p = _dot(h, w_ref[...])
    k_ref[0] = p[:, :2 * LANES].astype(BF16)
    v_ref[0] = p[:, 2 * LANES:].astype(BF16)


def _proj_ctx_kv(xcat, mod, g, wkv, ctx_tile, n_ctx):
    b, _, d = xcat.shape
    ctx_row = mod.shape[0] - 1
    out = pl.BlockSpec((1, n_ctx, 2 * LANES), lambda bi: (bi, 0, 0))
    return pl.pallas_call(
        _proj_ctx_kv_kernel,
        grid=(b,),
        in_specs=[
            pl.BlockSpec((1, n_ctx, d), lambda bi: (bi, ctx_tile, 0)),
            pl.BlockSpec((1, 6, d), lambda bi: (ctx_row, 0, 0)),
            pl.BlockSpec((1, d), lambda bi: (0, 0)),
            pl.BlockSpec(wkv.shape, lambda bi: (0, 0)),
        ],
        out_specs=[out, out],
        out_shape=[jax.ShapeDtypeStruct((b, n_ctx, 2 * LANES), BF16)] * 2,
        compiler_params=_params("parallel"),
        name="proj_ctx_kv",
    )(xcat, mod, g, wkv)


def _window_kernel(sink_ref, q_ref, kc_ref, vc_ref, kp_ref, km_ref, kn_ref, vp_ref, vm_ref, vn_ref, o_ref,
                   *, n_lat):
    i = pl.program_id(1)
    tq = q_ref.shape[1]
    n_ctx = kc_ref.shape[1]
    nk = n_ctx + tq + 2 * WINDOW
    col = lax.broadcasted_iota(jnp.int32, (tq, nk), 1)
    row = lax.broadcasted_iota(jnp.int32, (tq, nk), 0)
    k_pos = i * tq - WINDOW + (col - n_ctx)
    q_pos = i * tq + row
    visible = (col < n_ctx) | ((jnp.abs(q_pos - k_pos) <= WINDOW) & (k_pos >= 0) & (k_pos < n_lat))
    for hk in range(WIN_KV_HEADS):
        c = slice(hk * LANES, (hk + 1) * LANES)
        k_all = jnp.concatenate([kc_ref[0, :, c], kp_ref[0, :, c], km_ref[0, :, c], kn_ref[0, :, c]], axis=0)
        v_all = jnp.concatenate([vc_ref[0, :, c], vp_ref[0, :, c], vm_ref[0, :, c], vn_ref[0, :, c]], axis=0)
        first_v = _lane_iota(v_all.shape) < WIN_HEAD_DIM
        v_half = (jnp.where(first_v, v_all, jnp.zeros_like(v_all)), jnp.where(first_v, jnp.zeros_like(v_all), v_all))
        for pair in range(2):
            chunk = hk * 2 + pair
            q = q_ref[0, :, chunk * LANES:(chunk + 1) * LANES]
            first_q = _lane_iota(q.shape) < WIN_HEAD_DIM
            out = None
            for half in range(2):
                qh = jnp.where(first_q, q, jnp.zeros_like(q)) if half == 0 else jnp.where(first_q, jnp.zeros_like(q), q)
                s = jnp.where(visible, _dot_nt(qh, k_all), NEG_INF)
                sink = sink_ref[chunk * 2 + half] * LOG2E
                m = jnp.maximum(jnp.max(s, axis=-1, keepdims=True), sink)
                p = jnp.exp2(s - m)
                l = jnp.sum(p, axis=-1, keepdims=True) + jnp.exp2(sink - m)
                o = _dot(p.astype(BF16), v_half[half]) / l
                out = o if out is None else out + o
            o_ref[0, :, chunk * LANES:(chunk + 1) * LANES] = out.astype(o_ref.dtype)


def _window_attention(q, kd, vd, kc, vc, sink):
    b, n_lat, _ = q.shape
    tq = ROW_TILE
    n_ctx = kc.shape[1]
    per = tq // WINDOW
    last = n_lat // WINDOW - 1
    main = pl.BlockSpec((1, tq, 2 * LANES), lambda bi, i: (bi, i, 0))
    prev = pl.BlockSpec((1, WINDOW, 2 * LANES), lambda bi, i: (bi, jnp.maximum(i * per - 1, 0), 0))
    nxt = pl.BlockSpec((1, WINDOW, 2 * LANES), lambda bi, i: (bi, jnp.minimum((i + 1) * per, last), 0))
    ctx = pl.BlockSpec((1, n_ctx, 2 * LANES), lambda bi, i: (bi, 0, 0))
    return pl.pallas_call(
        functools.partial(_window_kernel, n_lat=n_lat),
        grid=(b, n_lat // tq),
        in_specs=[
            pl.BlockSpec(memory_space=pltpu.SMEM),
            pl.BlockSpec((1, tq, 4 * LANES), lambda bi, i: (bi, i, 0)),
            ctx, ctx, prev, main, nxt, prev, main, nxt,
        ],
        out_specs=pl.BlockSpec((1, tq, 4 * LANES), lambda bi, i: (bi, i, 0)),
        out_shape=jax.ShapeDtypeStruct((b, n_lat, 4 * LANES), BF16),
        compiler_params=_params("parallel", "arbitrary"),
        name="window_attention",
    )(sink, q, kc, vc, kd, kd, kd, vd, vd, vd)


def _hyena_filter_kernel(feat_ref, w1_ref, b1_ref, w2_ref, b2_ref, w3_ref, b3_ref, freq_ref, decay_ref, h_ref,
                         *, n_forward):
    j = pl.program_id(0)
    hp = lax.Precision.HIGHEST
    freq = freq_ref[...]
    hid = jnp.sin(freq * (jnp.dot(feat_ref[...], w1_ref[...], precision=hp, preferred_element_type=F32) + b1_ref[...]))
    hid = jnp.sin(freq * (jnp.dot(hid, w2_ref[...], precision=hp, preferred_element_type=F32) + b2_ref[...]))
    h = jnp.dot(hid, w3_ref[...], precision=hp, preferred_element_type=F32) + b3_ref[...]
    h = h * decay_ref[...]
    row = lax.broadcasted_iota(jnp.int32, h.shape, 0)
    h_ref[...] = jnp.where((row == 0) & (j >= n_forward), 0.0, h).astype(h_ref.dtype)


def _hyena_filters(feats, w1, b1, w2, b2, w3, b3, freq, decay):
    length = feats.shape[0]
    n = w3.shape[1]
    ch = decay.shape[1]
    full = lambda a: pl.BlockSpec(a.shape, lambda j: (0, 0))
    return pl.pallas_call(
        functools.partial(_hyena_filter_kernel, n_forward=HYENA_ORDER),
        grid=(n // ch,),
        in_specs=[full(feats), full(w1), full(b1), full(w2), full(b2),
                  pl.BlockSpec((w3.shape[0], ch), lambda j: (0, j)), pl.BlockSpec((1, ch), lambda j: (0, j)),
                  full(freq), full(decay)],
        out_specs=pl.BlockSpec((length, ch), lambda j: (0, j)),
        out_shape=jax.ShapeDtypeStruct((length, n), BF16),
        compiler_params=_params("arbitrary"),
        name="hyena_filters",
    )(feats, w1, b1, w2, b2, w3, b3, freq, decay)


def _spectrum_kernel(f_ref, h_ref, re_ref, im_ref):
    j = pl.program_id(0)
    tk = re_ref.shape[0]
    n = re_ref.shape[1]
    a = _dot(f_ref[0], h_ref[...])
    re_ref[...] = a[:tk, :n] + a[:tk, n:]
    row = lax.broadcasted_iota(jnp.int32, (tk, n), 0)
    sign = jnp.where((row == 0) & (j == 0), 1.0, -1.0)
    im_ref[...] = a[tk:, :n] + sign * a[tk:, n:]


def _filter_spectrum(f3, hcat):
    nt, tk2, length = f3.shape
    n = hcat.shape[1] // 2
    tk = tk2 // 2
    out = pl.BlockSpec((tk, n), lambda j: (j, 0))
    return pl.pallas_call(
        _spectrum_kernel,
        grid=(nt,),
        in_specs=[pl.BlockSpec((1, tk2, length), lambda j: (j, 0, 0)),
                  pl.BlockSpec(hcat.shape, lambda j: (0, 0))],
        out_specs=[out, out],
        out_shape=[jax.ShapeDtypeStruct((nt * tk, n), F32)] * 2,
        compiler_params=_params("arbitrary"),
        name="filter_spectrum",
    )(f3, hcat)


def _long_conv_kernel(z_ref, x_ref, f_ref, g_ref, hre_ref, him_ref, bias_ref, o_ref, zb_ref, acc_ref):
    j = pl.program_id(1)
    tk = hre_ref.shape[0]

    @pl.when(j == 0)
    def _():
        zb_ref[...] = z_ref[0].astype(BF16)
        acc_ref[...] = jnp.zeros_like(acc_ref)

    a = _dot(f_ref[0], zb_ref[...])
    a_re, a_im = a[:tk], a[tk:]
    h_re, h_im = hre_ref[...], him_ref[...]
    row = lax.broadcasted_iota(jnp.int32, a_re.shape, 0)
    packed = (row == 0) & (j == 0)
    ii = a_im * h_im
    p_re = a_re * h_re - jnp.where(packed, 0.0, ii)
    p_im = jnp.where(packed, ii, a_re * h_im + a_im * h_re)
    p = jnp.concatenate([p_re, p_im], axis=0).astype(BF16)
    acc_ref[...] += _dot(g_ref[0], p)

    @pl.when(j == pl.num_programs(1) - 1)
    def _():
        o_ref[0] = (x_ref[0] * (acc_ref[...] + bias_ref[...] * z_ref[0])).astype(o_ref.dtype)


def _long_conv(z_src, z_col, x_src, x_col, f3, g3, hre, him, h_col, bias, out_dtype):
    b, length, _ = z_src.shape
    nt, tk2, _ = f3.shape
    tk = tk2 // 2
    ch = HYENA_CH
    return pl.pallas_call(
        _long_conv_kernel,
        grid=(b, nt),
        in_specs=[
            pl.BlockSpec((1, length, ch), lambda bi, j: (bi, 0, z_col)),
            pl.BlockSpec((1, length, ch), lambda bi, j: (bi, 0, x_col)),
            pl.BlockSpec((1, tk2, length), lambda bi, j: (j, 0, 0)),
            pl.BlockSpec((1, length, tk2), lambda bi, j: (j, 0, 0)),
            pl.BlockSpec((tk, ch), lambda bi, j: (j, h_col)),
            pl.BlockSpec((tk, ch), lambda bi, j: (j, h_col)),
            pl.BlockSpec((1, ch), lambda bi, j: (0, 0)),
        ],
        out_specs=pl.BlockSpec((1, length, ch), lambda bi, j: (bi, 0, 0)),
        out_shape=jax.ShapeDtypeStruct((b, length, ch), out_dtype),
        scratch_shapes=[pltpu.VMEM((length, ch), BF16), pltpu.VMEM((length, ch), F32)],
        compiler_params=_params("parallel", "arbitrary"),
        name="long_conv",
    )(z_src, x_src, f3, g3, hre, him, bias)


def _axial_rope(length, rot_dim):
    rows = length // GRID_W
    row = jnp.repeat(jnp.arange(rows), GRID_W).astype(F32)
    col = jnp.tile(jnp.arange(GRID_W), rows).astype(F32)
    quarter = rot_dim // 4
    inv = ROPE_BASE ** (-jnp.arange(quarter, dtype=F32) / quarter)
    ang = jnp.concatenate([row[:, None] * inv, col[:, None] * inv], axis=-1)
    return jnp.cos(ang), jnp.sin(ang)


def _rope_tables(length, n_ctx):
    cos_m, sin_m = _axial_rope(length, MLA_ROPE)
    cos_d, sin_d = _axial_rope(length, DIFF_HEAD_DIM)
    ones = lambda w: jnp.ones((length, w), F32)
    zeros = lambda w: jnp.zeros((length, w), F32)
    cm = jnp.concatenate([ones(MLA_NOPE), cos_m, cos_m, ones(LANES - MLA_NOPE - MLA_ROPE)], axis=1)
    sm = jnp.concatenate([zeros(MLA_NOPE), -sin_m, sin_m, zeros(LANES - MLA_NOPE - MLA_ROPE)], axis=1)
    cd = jnp.concatenate([cos_d] * 4, axis=1)
    sd = jnp.concatenate([-sin_d, sin_d] * 2, axis=1)
    pad = lambda t, v: jnp.concatenate([t, jnp.full((n_ctx, LANES), v, F32)], axis=0)
    return pad(cm, 1.0), pad(sm, 0.0), pad(cd, 1.0), pad(sd, 0.0)


def _dft_matrices(length):
    n = 2 * length
    k = np.arange(length, dtype=np.int64)
    ang = 2.0 * np.pi * ((k[:, None] * k[None, :]) % n).astype(np.float64) / n
    alt = np.where(k % 2 == 0, 1.0, -1.0)
    f_re = np.cos(ang)
    f_im = -np.sin(ang)
    f_im[0, :] = alt
    g_re = (2.0 / n) * np.cos(ang.T)
    g_im = -(2.0 / n) * np.sin(ang.T)
    g_re[:, 0] = 1.0 / n
    g_im[:, 0] = alt / n
    nt = length // FREQ_TILE
    f3 = np.concatenate([f_re.reshape(nt, FREQ_TILE, length), f_im.reshape(nt, FREQ_TILE, length)], axis=1)
    g3 = np.concatenate([g_re.reshape(length, nt, FREQ_TILE), g_im.reshape(length, nt, FREQ_TILE)], axis=2)
    g3 = np.transpose(g3, (1, 0, 2))
    return jnp.asarray(f3, dtype=BF16), jnp.asarray(g3, dtype=BF16)


def _hyena_features(length):
    t = jnp.arange(length, dtype=F32)
    tn = t / max(length - 1, 1)
    bands = jnp.linspace(1e-4, HYENA_BANDS - 1, HYENA_BANDS, dtype=F32)
    ang = 2.0 * math.pi * bands[None, :] * t[:, None] / length
    feats = jnp.concatenate([tn[:, None], jnp.cos(ang), -jnp.sin(ang)], axis=-1)
    min_decay = math.log(HYENA_DECAY_TARGET) / HYENA_SLOW_PCT
    max_decay = math.log(HYENA_DECAY_TARGET) / HYENA_FAST_PCT
    deltas = jnp.abs(jnp.linspace(min_decay, max_decay, HYENA_CH, dtype=F32))
    decay = jnp.exp(-tn[:, None] * deltas[None, :])
    return feats, decay


def _pad_cols(w, left, right):
    return jnp.pad(w, ((0, 0),) * (w.ndim - 1) + ((left, right),))


def _even_weights(w_in, w_uq, w_ukv):
    d = w_in.shape[0]
    o = MLA_Q_RANK + MLA_KV_RANK
    w_kr = _pad_cols(w_in[:, o:o + MLA_ROPE], MLA_NOPE, LANES - MLA_NOPE - MLA_ROPE)
    w_in_ext = jnp.concatenate([w_in[:, :o], w_kr, w_in[:, o + MLA_ROPE:]], axis=1)
    uq = _pad_cols(w_uq.reshape(MLA_Q_RANK, MLA_HEADS, MLA_NOPE + MLA_ROPE), 0, LANES - MLA_NOPE - MLA_ROPE)
    ukv = w_ukv.reshape(MLA_KV_RANK, MLA_HEADS, MLA_NOPE + MLA_V)
    uk = _pad_cols(ukv[:, :, :MLA_NOPE], 0, LANES - MLA_NOPE)
    uv = ukv[:, :, MLA_NOPE:].reshape(MLA_KV_RANK, MLA_HEADS // 2, 2, MLA_V)
    uv = jnp.stack([_pad_cols(uv[:, :, 0], 0, LANES - MLA_V), _pad_cols(uv[:, :, 1], LANES - MLA_V, 0)], axis=2)
    w_ukv_ext = jnp.concatenate([uk.reshape(MLA_KV_RANK, -1), uv.reshape(MLA_KV_RANK, -1)], axis=1)
    return w_in_ext.astype(BF16), uq.reshape(MLA_Q_RANK, -1).astype(BF16), w_ukv_ext.astype(BF16)


def _odd_weights(w_in):
    d = w_in.shape[0]
    nq = WIN_Q_HEADS * WIN_HEAD_DIM
    nkv = WIN_KV_HEADS * WIN_HEAD_DIM
    dup = lambda w: jnp.concatenate([w.reshape(d, WIN_KV_HEADS, WIN_HEAD_DIM)] * 2, axis=-1).reshape(d, -1)
    wk = dup(w_in[:, nq:nq + nkv])
    wv = dup(w_in[:, nq + nkv:nq + 2 * nkv])
    wqkv = jnp.concatenate([w_in[:, :nq], wk, wv], axis=1).astype(BF16)
    wkv = jnp.concatenate([wk, wv], axis=1).astype(BF16)
    wu = w_in[:, nq + 2 * nkv:].astype(BF16)
    return wqkv, wkv, wu


def _ffn_weights(w_up, conv_w, conv_b, w_down):
    d = w_up.shape[0]
    nc = D_FF // FFN_CHUNK
    chunks = lambda w: jnp.transpose(w.reshape(w.shape[0], nc, FFN_CHUNK), (1, 0, 2))
    cw = jnp.concatenate([conv_w, conv_b[None, :]], axis=0)
    return (chunks(w_up[:, :D_FF]).astype(BF16), chunks(w_up[:, D_FF:]).astype(BF16),
            chunks(cw[:, :D_FF]), chunks(cw[:, D_FF:]),
            w_down.reshape(nc, FFN_CHUNK, d).astype(BF16))


def kernel(x, c, ctx, c_ctx, ada_w, ada_b, norm_g, mix_w_out, ffn_w_up, ffn_conv_w, ffn_conv_b, ffn_w_down, even_w_in, mla_q_norm_g, mla_w_uq, mla_kv_norm_g, mla_w_ukv, diff_lambda, diff_subln_g, odd_w_in, win_sink, hy_conv_w, hy_conv_b, hy_f_w1, hy_f_b1, hy_f_w2, hy_f_b2, hy_f_w3, hy_f_b3, hy_f_freq, hy_bias):
    b, seq, d = x.shape
    n_ctx = ctx.shape[1]
    depth = ada_w.shape[0]
    assert depth == 2 and d == D_MODEL and seq % ROW_TILE == 0 and n_ctx == ROW_TILE
    n_lat_tiles = seq // ROW_TILE
    n_tiles = n_lat_tiles + 1

    pad_rows = (-(b + 1)) % 8
    cond = jnp.concatenate([c, c_ctx[None, :], jnp.zeros((pad_rows, d), F32)], axis=0)
    mod = _ada_table(cond, ada_w, ada_b).reshape(depth, b + 1 + pad_rows, 6, d)[:, :b + 1]
    g = norm_g.reshape(depth, 4, 1, d)
    cm, sm, cd, sd = _rope_tables(seq, n_ctx)
    xcat = jnp.concatenate([x, ctx], axis=1)

    lambda_init = 0.8 - 0.6 * math.exp(-0.3 * 0)
    w_in_e, w_uq_e, w_ukv_e = _even_weights(even_w_in[0], mla_w_uq[0], mla_w_ukv[0])
    qm, km, vm, dq, dk, dv = _proj_even(
        xcat, mod[0], g[0, 0], w_in_e, mla_q_norm_g[0][None, :], w_uq_e, mla_kv_norm_g[0][None, :], w_ukv_e,
        cm, sm, cd, sd, n_lat_tiles)
    rows = seq + n_ctx
    lam_args = (diff_lambda[0], diff_subln_g[0][None, :], lambda_init)
    w_out0 = mix_w_out[0].astype(BF16)
    ffn0 = _ffn_weights(ffn_w_up[0], ffn_conv_w[0], ffn_conv_b[0], ffn_w_down[0])
    o_mla = _mla_attention(qm, km, vm, ATTN_Q_TILE, 0, seq, 0, rows)
    o_diff = _diff_attention(dq, dk, dv, *lam_args, ATTN_Q_TILE, 0, seq, 0, rows)
    x1, h2 = _merge(o_mla, o_diff, xcat, mod[0], w_out0, g[0, 1], g[0, 2], ROW_TILE, 0, False)
    x_lat = _ffn(h2, x1, mod[0], *ffn0, g[0, 3], FFN_ROW_TILE, 0, seq, False)
    o_mla = _mla_attention(qm, km, vm, n_ctx, seq, n_ctx, seq, n_ctx)
    o_diff = _diff_attention(dq, dk, dv, *lam_args, n_ctx, seq, n_ctx, seq, n_ctx)
    x1, h2 = _merge(o_mla, o_diff, xcat, mod[0], w_out0, g[0, 1], g[0, 2], n_ctx, seq, True)
    x_ctx = _ffn(h2, x1, mod[0], *ffn0, g[0, 3], n_ctx, 0, n_ctx, True)

    wqkv, wkv, wu = _odd_weights(odd_w_in[0])
    cw_hy = jnp.concatenate([hy_conv_w[0], hy_conv_b[0][None, :]], axis=0)
    q, kd, vd, u = _proj_odd(x_lat, mod[1], g[1, 0], wqkv, wu, cw_hy, cd, sd, n_lat_tiles)
    kc, vc = _proj_ctx_kv(x_ctx, mod[1], g[1, 0], wkv, 0, n_ctx)
    o_win = _window_attention(q, kd, vd, kc, vc, win_sink[0])

    feats, decay = _hyena_features(seq)
    emb = feats.shape[1]
    feats = _pad_cols(feats, 0, LANES - emb)
    w1 = jnp.pad(hy_f_w1[0], ((0, LANES - emb), (0, 0)))
    hcat = _hyena_filters(feats, w1, hy_f_b1[0][None, :], hy_f_w2[0], hy_f_b2[0][None, :], hy_f_w3[0],
                          hy_f_b3[0][None, :], hy_f_freq[0][None, :], decay)
    f3, g3 = _dft_matrices(seq)
    hre, him = _filter_spectrum(f3, hcat)
    z = _long_conv(u, 0, u, 1, f3, g3, hre, him, 0, hy_bias[0, 0][None, :], F32)
    z = _long_conv(z, 0, u, 2, f3, g3, hre, him, 1, hy_bias[0, 1][None, :], BF16)

    w_out1 = mix_w_out[1].astype(BF16)
    x1, h2 = _merge(o_win, z, x_lat, mod[1], w_out1, g[1, 1], g[1, 2], ROW_TILE, 0, False)
    ffn1 = _ffn_weights(ffn_w_up[1], ffn_conv_w[1], ffn_conv_b[1], ffn_w_down[1])
    return _ffn(h2, x1, mod[1], *ffn1, g[1, 3], FFN_ROW_TILE, 0, seq, False)
```

```python
import functools
import math

import numpy as np
import jax
import jax.numpy as jnp
from jax import lax
from jax.experimental import pallas as pl
from jax.experimental.pallas import tpu as pltpu

D_MODEL = 1024
GRID_W = 64
MLA_HEADS = 8
MLA_NOPE = 64
MLA_ROPE = 32
MLA_V = 64
MLA_Q_RANK = 384
MLA_KV_RANK = 256
DIFF_HEADS = 4
DIFF_HEAD_DIM = 64
DIFF_V_DIM = 128
WIN_Q_HEADS = 8
WIN_KV_HEADS = 2
WIN_HEAD_DIM = 64
WINDOW = 128
HYENA_CH = 512
HYENA_ORDER = 2
HYENA_BANDS = 16
HYENA_DECAY_TARGET = 1e-2
HYENA_FAST_PCT = 0.3
HYENA_SLOW_PCT = 1.5
D_FF = 2816
ROPE_BASE = 10000.0
NORM_EPS = 1e-6
NEG_INF = -1e30

LANES = 128
HALO_ROWS = 16
ROW_TILE = 256
ATTN_Q_TILE = 1024
ATTN_SUB_TILE = 256
LOG2E = math.log2(math.e)
FFN_ROW_TILE = 512
FFN_CHUNK = 256
FFN_DOWN_GROUP = 4
FREQ_TILE = 256
VMEM_LIMIT = 48 * 1024 * 1024

BF16 = jnp.bfloat16
F32 = jnp.float32


def _params(*semantics):
    return pltpu.CompilerParams(dimension_semantics=semantics, vmem_limit_bytes=VMEM_LIMIT)


def _rms(x, g):
    return x * lax.rsqrt(jnp.mean(x * x, axis=-1, keepdims=True) + NORM_EPS) * g


def _dot(a, b):
    return jnp.dot(a, b, preferred_element_type=F32)


def _dot_nt(a, b):
    return lax.dot_general(a, b, (((1,), (1,)), ((), ())), preferred_element_type=F32)


def _lane_iota(shape):
    return lax.broadcasted_iota(jnp.int32, shape, len(shape) - 1)


def _rope_chunk(x, cos, sin_signed, half):
    first = (_lane_iota(x.shape) % (2 * half)) < half
    partner = jnp.where(first, pltpu.roll(x, LANES - half, 1), pltpu.roll(x, half, 1))
    return x * cos + partner * sin_signed


def _shift_rows(u, prev_row, next_row):
    rows = u.shape[0]
    r = lax.broadcasted_iota(jnp.int32, u.shape, 0)
    down = jnp.where(r == 0, prev_row, pltpu.roll(u, 1, 0))
    up = jnp.where(r == rows - 1, next_row, pltpu.roll(u, rows - 1, 0))
    return down, up


def _ada_kernel(c_ref, w_ref, b_ref, o_ref):
    c = c_ref[...]
    sc = c * jax.nn.sigmoid(c)
    o_ref[0] = _dot(sc.astype(BF16), w_ref[0].astype(BF16)) + b_ref[0]


def _ada_table(cond, ada_w, ada_b):
    depth, d, n = ada_w.shape
    rows = cond.shape[0]
    tn = 1536
    return pl.pallas_call(
        _ada_kernel,
        grid=(depth, n // tn),
        in_specs=[
            pl.BlockSpec((rows, d), lambda l, j: (0, 0)),
            pl.BlockSpec((1, d, tn), lambda l, j: (l, 0, j)),
            pl.BlockSpec((1, 1, tn), lambda l, j: (l, 0, j)),
        ],
        out_specs=pl.BlockSpec((1, rows, tn), lambda l, j: (l, 0, j)),
        out_shape=jax.ShapeDtypeStruct((depth, rows, n), F32),
        compiler_params=_params("arbitrary", "arbitrary"),
        name="ada_table",
    )(cond, ada_w, ada_b.reshape(depth, 1, n))


def _proj_even_kernel(x_ref, c_ref, mod_ref, g_ref, win_ref, qg_ref, wuq_ref, kvg_ref, wukv_ref,
                      cm_ref, sm_ref, cd_ref, sd_ref,
                      qm_ref, km_ref, vm_ref, dq_ref, dk_ref, dv_ref, *, n_lat_tiles):
    x = jnp.where(pl.program_id(1) < n_lat_tiles, x_ref[0], c_ref[0])
    mod = mod_ref[0]
    h = _rms(x, g_ref[...]) * (1.0 + mod[1:2]) + mod[0:1]
    p = _dot(h.astype(BF16), win_ref[...])
    o = 0
    cq = p[:, o:o + MLA_Q_RANK]; o += MLA_Q_RANK
    ckv = p[:, o:o + MLA_KV_RANK]; o += MLA_KV_RANK
    kr = p[:, o:o + LANES]; o += LANES
    dq = p[:, o:o + 512]; o += 512
    dk = p[:, o:o + 512]; o += 512
    dv = p[:, o:o + 512]
    q = _dot(_rms(cq, qg_ref[...]).astype(BF16), wuq_ref[...])
    kv = _dot(_rms(ckv, kvg_ref[...]).astype(BF16), wukv_ref[...])
    cm, sm, cd, sd = cm_ref[...], sm_ref[...], cd_ref[...], sd_ref[...]
    mla_scale = (MLA_NOPE + MLA_ROPE) ** -0.5 * LOG2E
    diff_scale = DIFF_HEAD_DIM ** -0.5 * LOG2E
    kr = _rope_chunk(kr, cm, sm, MLA_ROPE // 2)
    for hd in range(MLA_HEADS):
        c = slice(hd * LANES, (hd + 1) * LANES)
        qm_ref[0, :, c] = (_rope_chunk(q[:, c], cm, sm, MLA_ROPE // 2) * mla_scale).astype(BF16)
        km_ref[0, c, :] = (kv[:, c] + kr).T.astype(BF16)
    vm_ref[0] = kv[:, MLA_HEADS * LANES:].astype(BF16)
    for hd in range(DIFF_HEADS):
        c = slice(hd * LANES, (hd + 1) * LANES)
        dq_ref[0, :, c] = (_rope_chunk(dq[:, c], cd, sd, DIFF_HEAD_DIM // 2) * diff_scale).astype(BF16)
        dk_ref[0, c, :] = _rope_chunk(dk[:, c], cd, sd, DIFF_HEAD_DIM // 2).T.astype(BF16)
    dv_ref[0] = dv.astype(BF16)


def _proj_even(x, ctx, mod, g, w_in, qg, w_uq, kvg, w_ukv, cm, sm, cd, sd):
    b, seq, d = x.shape
    assert ctx.shape[1] == ROW_TILE
    n_lat_tiles = seq // ROW_TILE
    nt = n_lat_tiles + 1
    rows = nt * ROW_TILE
    ctx_row = mod.shape[0] - 1
    const = lambda *shape: pl.BlockSpec(shape, lambda bi, i: (0,) * len(shape))
    tab = pl.BlockSpec((ROW_TILE, LANES), lambda bi, i: (i, 0))
    out = lambda w: pl.BlockSpec((1, ROW_TILE, w), lambda bi, i: (bi, i, 0))
    shp = lambda w: jax.ShapeDtypeStruct((b, rows, w), BF16)
    out_t = lambda w: pl.BlockSpec((1, w, ROW_TILE), lambda bi, i: (bi, 0, i))
    shp_t = lambda w: jax.ShapeDtypeStruct((b, w, rows), BF16)
    return pl.pallas_call(
        functools.partial(_proj_even_kernel, n_lat_tiles=n_lat_tiles),
        grid=(b, nt),
        in_specs=[
            pl.BlockSpec((1, ROW_TILE, d), lambda bi, i: (bi, jnp.minimum(i, n_lat_tiles - 1), 0)),
            pl.BlockSpec((1, ROW_TILE, d), lambda bi, i: (bi, 0, 0)),
            pl.BlockSpec((1, 6, d), lambda bi, i: (jnp.where(i < n_lat_tiles, bi, ctx_row), 0, 0)),
            const(1, d), const(*w_in.shape), const(1, MLA_Q_RANK), const(*w_uq.shape),
            const(1, MLA_KV_RANK), const(*w_ukv.shape), tab, tab, tab, tab,
        ],
        out_specs=[out(1024), out_t(1024), out(1024), out(512), out_t(512), out(512)],
        out_shape=[shp(1024), shp_t(1024), shp(1024), shp(512), shp_t(512), shp(512)],
        compiler_params=_params("parallel", "arbitrary"),
        name="proj_even",
    )(x, ctx, mod, g, w_in, qg, w_uq, kvg, w_ukv, cm, sm, cd, sd)


def _attend(q, kt, v):
    s = _dot(q, kt)
    m = jnp.max(s, axis=-1, keepdims=True)
    p = jnp.exp2(s - m)
    l = jnp.sum(p, axis=-1, keepdims=True)
    return _dot(p.astype(BF16), v) / l


def _mla_kernel(q_ref, k_ref, v_ref, o_ref):
    sub = min(q_ref.shape[1], ATTN_SUB_TILE)
    for r in range(q_ref.shape[1] // sub):
        rs = slice(r * sub, (r + 1) * sub)
        acc = None
        for hd in range(2):
            c = slice(hd * LANES, (hd + 1) * LANES)
            o = _attend(q_ref[0, rs, c], k_ref[0, c, :], v_ref[0, :, c])
            acc = o if acc is None else acc + o
        o_ref[0, rs, :] = acc.astype(o_ref.dtype)


def _mla_attention(qm, km, vm, tq, q_off, n_q, k_off, n_k):
    b = qm.shape[0]
    pairs = MLA_HEADS // 2
    qo, ko = q_off // tq, k_off // n_k
    return pl.pallas_call(
        _mla_kernel,
        grid=(b, pairs, n_q // tq),
        in_specs=[
            pl.BlockSpec((1, tq, 2 * LANES), lambda bi, p, i: (bi, qo + i, p)),
            pl.BlockSpec((1, 2 * LANES, n_k), lambda bi, p, i: (bi, p, ko)),
            pl.BlockSpec((1, n_k, 2 * LANES), lambda bi, p, i: (bi, ko, p)),
        ],
        out_specs=pl.BlockSpec((1, tq, LANES), lambda bi, p, i: (bi, i, p)),
        out_shape=jax.ShapeDtypeStruct((b, n_q, pairs * LANES), BF16),
        compiler_params=_params("parallel", "arbitrary", "arbitrary"),
        name="mla_attention",
    )(qm, km, vm)


def _diff_kernel(q_ref, k_ref, v_ref, lam_ref, g_ref, o_ref, *, lambda_init):
    lv = lam_ref[...]
    lam = (jnp.exp(jnp.sum(lv[0:1] * lv[1:2], axis=-1, keepdims=True))
           - jnp.exp(jnp.sum(lv[2:3] * lv[3:4], axis=-1, keepdims=True)) + lambda_init)
    kt = k_ref[0]
    v = v_ref[0]
    sub = min(q_ref.shape[1], ATTN_SUB_TILE)
    for r in range(q_ref.shape[1] // sub):
        rs = slice(r * sub, (r + 1) * sub)
        q = q_ref[0, rs, :]
        first = _lane_iota(q.shape) < DIFF_HEAD_DIM
        zero = jnp.zeros_like(q)
        o1 = _attend(jnp.where(first, q, zero), kt, v)
        o2 = _attend(jnp.where(first, zero, q), kt, v)
        o = o1 - lam * o2
        o_ref[0, rs, :] = (_rms(o, g_ref[...]) * (1.0 - lambda_init)).astype(o_ref.dtype)


def _diff_attention(dq, dk, dv, diff_lambda, subln_g, lambda_init, tq, q_off, n_q, k_off, n_k):
    b = dq.shape[0]
    qo, ko = q_off // tq, k_off // n_k
    return pl.pallas_call(
        functools.partial(_diff_kernel, lambda_init=lambda_init),
        grid=(b, DIFF_HEADS, n_q // tq),
        in_specs=[
            pl.BlockSpec((1, tq, LANES), lambda bi, h, i: (bi, qo + i, h)),
            pl.BlockSpec((1, LANES, n_k), lambda bi, h, i: (bi, h, ko)),
            pl.BlockSpec((1, n_k, LANES), lambda bi, h, i: (bi, ko, h)),
            pl.BlockSpec((4, DIFF_HEAD_DIM), lambda bi, h, i: (0, 0)),
            pl.BlockSpec((1, DIFF_V_DIM), lambda bi, h, i: (0, 0)),
        ],
        out_specs=pl.BlockSpec((1, tq, LANES), lambda bi, h, i: (bi, i, h)),
        out_shape=jax.ShapeDtypeStruct((b, n_q, DIFF_HEADS * LANES), BF16),
        compiler_params=_params("parallel", "arbitrary", "arbitrary"),
        name="diff_attention",
    )(dq, dk, dv, diff_lambda, subln_g)


def _merge_kernel(a_ref, b_ref, x_ref, mod_ref, w_ref, g1_ref, g2_ref, x1_ref, h2_ref):
    half = a_ref.shape[-1]
    mod = mod_ref[0]
    y = _dot(a_ref[0], w_ref[0:half, :]) + _dot(b_ref[0], w_ref[half:, :])
    x1 = x_ref[0] + mod[2:3] * _rms(y, g1_ref[...])
    x1_ref[0] = x1
    h2_ref[0] = (_rms(x1, g2_ref[...]) * (1.0 + mod[4:5]) + mod[3:4]).astype(BF16)


def _merge(a, bb, x, mod, w_out, g1, g2, tm, x_off, ctx_mod):
    b, rows, half = a.shape
    d = x.shape[-1]
    nt = rows // tm
    xo = x_off // tm
    ctx_row = mod.shape[0] - 1
    tile = lambda w: pl.BlockSpec((1, tm, w), lambda bi, i: (bi, i, 0))
    const = lambda *shape: pl.BlockSpec(shape, lambda bi, i: (0,) * len(shape))
    return pl.pallas_call(
        _merge_kernel,
        grid=(b, nt),
        in_specs=[
            tile(half), tile(half), pl.BlockSpec((1, tm, d), lambda bi, i: (bi, xo + i, 0)),
            pl.BlockSpec((1, 6, d), lambda bi, i: (ctx_row if ctx_mod else bi, 0, 0)),
            const(*w_out.shape), const(1, d), const(1, d),
        ],
        out_specs=[tile(d), tile(d)],
        out_shape=[jax.ShapeDtypeStruct((b, rows, d), F32), jax.ShapeDtypeStruct((b, rows, d), BF16)],
        compiler_params=_params("parallel", "arbitrary"),
        name="merge",
    )(a, bb, x, mod, w_out, g1, g2)


def _ffn_kernel(h_ref, hp_ref, hn_ref, x_ref, mod_ref, wa_ref, wg_ref, cwa_ref, cwg_ref, wd_ref, g_ref,
                o_ref, lhs_ref, ua0_ref, ug0_ref, ua1_ref, ug1_ref, act0_ref, act1_ref, acc_ref):
    i = pl.program_id(1)
    tm = h_ref.shape[1]
    n_chunks = wa_ref.shape[0]
    hp = hp_ref[0]
    hn = hn_ref[0]
    lhs_ref[0:HALO_ROWS] = jnp.where(i == 0, jnp.zeros_like(hp), hp)
    lhs_ref[HALO_ROWS:HALO_ROWS + tm] = h_ref[0]
    lhs_ref[HALO_ROWS + tm:] = jnp.where(i == pl.num_programs(1) - 1, jnp.zeros_like(hn), hn)
    rows = tm + 2 * HALO_ROWS
    mid = slice(HALO_ROWS, HALO_ROWS + tm)
    slots = ((ua0_ref, ug0_ref), (ua1_ref, ug1_ref))

    def up(j, slot):
        lhs = lhs_ref[...]
        slots[slot][0][...] = _dot(lhs, wa_ref[j])
        slots[slot][1][...] = _dot(lhs, wg_ref[j])

    def conv(u, cw):
        down = pltpu.roll(u, 1, 0)[mid]
        above = pltpu.roll(u, rows - 1, 0)[mid]
        return down * cw[0:1] + u[mid] * cw[1:2] + above * cw[2:3] + cw[3:4]

    acts = (act0_ref, act1_ref)
    width = wa_ref.shape[2]

    def activate(j):
        a = conv(slots[j % 2][0][...], cwa_ref[j])
        g = conv(slots[j % 2][1][...], cwg_ref[j])
        k = j % FFN_DOWN_GROUP
        acts[(j // FFN_DOWN_GROUP) % 2][:, k * width:(k + 1) * width] = ((g * jax.nn.sigmoid(g)) * a).astype(BF16)

    def down(j_last):
        j0 = j_last - j_last % FFN_DOWN_GROUP
        k = (j_last - j0 + 1) * width
        return _dot(acts[(j_last // FFN_DOWN_GROUP) % 2][:, :k], wd_ref[j0 * width:(j_last + 1) * width, :])

    up(0, 0)
    up(1, 1)
    y = None
    for j in range(n_chunks):
        activate(j)
        if j + 2 < n_chunks:
            up(j + 2, j % 2)
        if j % FFN_DOWN_GROUP == FFN_DOWN_GROUP - 1 or j == n_chunks - 1:
            y = down(j)
            if j == n_chunks - 1:
                break
            if j < FFN_DOWN_GROUP:
                acc_ref[...] = y
            else:
                acc_ref[...] += y
    total = y if n_chunks <= FFN_DOWN_GROUP else acc_ref[...] + y
    mod = mod_ref[0]
    o_ref[0] = x_ref[0] + mod[5:6] * _rms(total, g_ref[...])


def _ffn(h2, x1, mod, wa, wg, cwa, cwg, wd, g3, tm, row_offset, n_rows, ctx_mod):
    b, total_rows, d = h2.shape
    ctx_row = mod.shape[0] - 1
    nt = n_rows // tm
    off = row_offset // tm
    per = tm // HALO_ROWS
    last_halo = total_rows // HALO_ROWS - 1
    const = lambda *shape: pl.BlockSpec(shape, lambda bi, i: (0,) * len(shape), pipeline_mode=pl.Buffered(1))
    tile = pl.BlockSpec((1, tm, d), lambda bi, i: (bi, off + i, 0))
    return pl.pallas_call(
        _ffn_kernel,
        grid=(b, nt),
        in_specs=[
            tile,
            pl.BlockSpec((1, HALO_ROWS, d), lambda bi, i: (bi, jnp.maximum((off + i) * per - 1, 0), 0)),
            pl.BlockSpec((1, HALO_ROWS, d), lambda bi, i: (bi, jnp.minimum((off + i + 1) * per, last_halo), 0)),
            tile,
            pl.BlockSpec((1, 6, d), lambda bi, i: (ctx_row if ctx_mod else bi, 0, 0)),
            const(*wa.shape), const(*wg.shape), const(*cwa.shape), const(*cwg.shape), const(*wd.shape),
            const(1, d),
        ],
        out_specs=pl.BlockSpec((1, tm, d), lambda bi, i: (bi, i, 0)),
        out_shape=jax.ShapeDtypeStruct((b, n_rows, d), F32),
        scratch_shapes=[pltpu.VMEM((tm + 2 * HALO_ROWS, d), BF16)]
        + [pltpu.VMEM((tm + 2 * HALO_ROWS, FFN_CHUNK), F32)] * 4
        + [pltpu.VMEM((tm, FFN_DOWN_GROUP * FFN_CHUNK), BF16)] * 2 + [pltpu.VMEM((tm, d), F32)],
        compiler_params=_params("parallel", "arbitrary"),
        name="conv_ffn",
    )(h2, h2, h2, x1, mod, wa, wg, cwa, cwg, wd, g3)


def _proj_odd_kernel(x_ref, xp_ref, xn_ref, mod_ref, g_ref, wqkv_ref, wu_ref, cw_ref, cd_ref, sd_ref,
                     q_ref, k_ref, v_ref, u_ref, *, n_tiles):
    i = pl.program_id(1)
    mod = mod_ref[0]

    def normed(x):
        return (_rms(x, g_ref[...]) * (1.0 + mod[1:2]) + mod[0:1]).astype(BF16)

    h = normed(x_ref[0])
    p = _dot(h, wqkv_ref[...])
    cd, sd = cd_ref[...], sd_ref[...]
    scale = WIN_HEAD_DIM ** -0.5 * LOG2E
    for c in range(4):
        cs = slice(c * LANES, (c + 1) * LANES)
        q_ref[0, :, cs] = (_rope_chunk(p[:, cs], cd, sd, WIN_HEAD_DIM // 2) * scale).astype(BF16)
    for c in range(2):
        cs = slice(512 + c * LANES, 512 + (c + 1) * LANES)
        k_ref[0, c * LANES:(c + 1) * LANES, :] = _rope_chunk(p[:, cs], cd, sd, WIN_HEAD_DIM // 2).T.astype(BF16)
    v_ref[0] = p[:, 768:1024].astype(BF16)
    prev_ok = jnp.where(i == 0, 0.0, 1.0)
    next_ok = jnp.where(i == n_tiles - 1, 0.0, 1.0)
    wu = wu_ref[...]
    u = _dot(h, wu)
    up = _dot(normed(xp_ref[0]), wu)[HALO_ROWS - 1:HALO_ROWS] * prev_ok
    un = _dot(normed(xn_ref[0]), wu)[0:1] * next_ok
    down, upw = _shift_rows(u, up, un)
    cw = cw_ref[...]
    u_ref[0] = down * cw[0:1] + u * cw[1:2] + upw * cw[2:3] + cw[3:4]


def _proj_odd(xcat, mod, g, wqkv, wu, cw, cd, sd, n_tiles):
    b, _, d = xcat.shape
    per = ROW_TILE // HALO_ROWS
    last_halo = n_tiles * per - 1
    rows = n_tiles * ROW_TILE
    const = lambda *shape: pl.BlockSpec(shape, lambda bi, i: (0,) * len(shape))
    tab = pl.BlockSpec((ROW_TILE, LANES), lambda bi, i: (i, 0))
    out = lambda w: pl.BlockSpec((1, ROW_TILE, w), lambda bi, i: (bi, i, 0))
    return pl.pallas_call(
        functools.partial(_proj_odd_kernel, n_tiles=n_tiles),
        grid=(b, n_tiles),
        in_specs=[
            pl.BlockSpec((1, ROW_TILE, d), lambda bi, i: (bi, i, 0)),
            pl.BlockSpec((1, HALO_ROWS, d), lambda bi, i: (bi, jnp.maximum(i * per - 1, 0), 0)),
            pl.BlockSpec((1, HALO_ROWS, d), lambda bi, i: (bi, jnp.minimum((i + 1) * per, last_halo), 0)),
            pl.BlockSpec((1, 6, d), lambda bi, i: (bi, 0, 0)),
            const(1, d), const(*wqkv.shape), const(*wu.shape), const(*cw.shape), tab, tab,
        ],
        out_specs=[out(512), pl.BlockSpec((1, 2 * LANES, ROW_TILE), lambda bi, i: (bi, 0, i)), out(256),
                   out(3 * HYENA_CH)],
        out_shape=[jax.ShapeDtypeStruct((b, rows, 512), BF16), jax.ShapeDtypeStruct((b, 2 * LANES, rows), BF16),
                   jax.ShapeDtypeStruct((b, rows, 256), BF16), jax.ShapeDtypeStruct((b, rows, 3 * HYENA_CH), F32)],
        compiler_params=_params("parallel", "arbitrary"),
        name="proj_odd",
    )(xcat, xcat, xcat, mod, g, wqkv, wu, cw, cd, sd)


def _proj_ctx_kv_kernel(x_ref, mod_ref, g_ref, w_ref, k_ref, v_ref):
    mod = mod_ref[0]
    h = (_rms(x_ref[0], g_ref[...]) * (1.0 + mod[1:2]) + mod[0:1]).astype(BF16)
    p = _dot(h, w_ref[...])
    k_ref[0] = p[:, :2 * LANES].T.astype(BF16)
    v_ref[0] = p[:, 2 * LANES:].astype(BF16)


def _proj_ctx_kv(xcat, mod, g, wkv, ctx_tile, n_ctx):
    b, _, d = xcat.shape
    ctx_row = mod.shape[0] - 1
    out = pl.BlockSpec((1, n_ctx, 2 * LANES), lambda bi: (bi, 0, 0))
    return pl.pallas_call(
        _proj_ctx_kv_kernel,
        grid=(b,),
        in_specs=[
            pl.BlockSpec((1, n_ctx, d), lambda bi: (bi, ctx_tile, 0)),
            pl.BlockSpec((1, 6, d), lambda bi: (ctx_row, 0, 0)),
            pl.BlockSpec((1, d), lambda bi: (0, 0)),
            pl.BlockSpec(wkv.shape, lambda bi: (0, 0)),
        ],
        out_specs=[pl.BlockSpec((1, 2 * LANES, n_ctx), lambda bi: (bi, 0, 0)), out],
        out_shape=[jax.ShapeDtypeStruct((b, 2 * LANES, n_ctx), BF16), jax.ShapeDtypeStruct((b, n_ctx, 2 * LANES), BF16)],
        compiler_params=_params("parallel"),
        name="proj_ctx_kv",
    )(xcat, mod, g, wkv)


def _window_kernel(sink_ref, q_ref, kc_ref, vc_ref, kp_ref, km_ref, kn_ref, vp_ref, vm_ref, vn_ref, o_ref,
                   *, n_lat):
    i = pl.program_id(1)
    tq = q_ref.shape[1]
    n_ctx = vc_ref.shape[1]
    nk = n_ctx + tq + 2 * WINDOW
    col = lax.broadcasted_iota(jnp.int32, (tq, nk), 1)
    row = lax.broadcasted_iota(jnp.int32, (tq, nk), 0)
    k_pos = i * tq - WINDOW + (col - n_ctx)
    q_pos = i * tq + row
    visible = (col < n_ctx) | ((jnp.abs(q_pos - k_pos) <= WINDOW) & (k_pos >= 0) & (k_pos < n_lat))
    bias = jnp.where(visible, 0.0, NEG_INF)
    group = WIN_Q_HEADS // WIN_KV_HEADS
    bias = jnp.concatenate([bias, bias], axis=0)
    second_head = lax.broadcasted_iota(jnp.int32, (2 * tq, 1), 0) >= tq
    first = _lane_iota((tq, LANES)) < WIN_HEAD_DIM
    for hk in range(WIN_KV_HEADS):
        c = slice(hk * LANES, (hk + 1) * LANES)
        kt = jnp.concatenate([kc_ref[0, c, :], kp_ref[0, c, :], km_ref[0, c, :], kn_ref[0, c, :]], axis=1)
        v = jnp.concatenate([vc_ref[0, :, c], vp_ref[0, :, c], vm_ref[0, :, c], vn_ref[0, :, c]], axis=0)
        for pair in range(group // 2):
            chunk = hk * (group // 2) + pair
            q = q_ref[0, :, chunk * LANES:(chunk + 1) * LANES]
            zero = jnp.zeros_like(q)
            s = _dot(jnp.concatenate([jnp.where(first, q, zero), jnp.where(first, zero, q)], axis=0), kt) + bias
            sink = jnp.where(second_head, sink_ref[2 * chunk + 1], sink_ref[2 * chunk]) * LOG2E
            m = jnp.maximum(jnp.max(s, axis=-1, keepdims=True), sink)
            p = jnp.exp2(s - m)
            l = jnp.sum(p, axis=-1, keepdims=True) + jnp.exp2(sink - m)
            o = _dot(p.astype(BF16), v) / l
            o_ref[0, :, chunk * LANES:(chunk + 1) * LANES] = jnp.where(first, o[:tq], o[tq:]).astype(o_ref.dtype)


def _window_attention(q, kdt, vd, kct, vc, sink):
    b, n_lat, _ = q.shape
    tq = ROW_TILE
    n_ctx = vc.shape[1]
    per = tq // WINDOW
    last = n_lat // WINDOW - 1
    prev_i = lambda i: jnp.maximum(i * per - 1, 0)
    next_i = lambda i: jnp.minimum((i + 1) * per, last)
    w = 2 * LANES
    return pl.pallas_call(
        functools.partial(_window_kernel, n_lat=n_lat),
        grid=(b, n_lat // tq),
        in_specs=[
            pl.BlockSpec(memory_space=pltpu.SMEM),
            pl.BlockSpec((1, tq, 4 * LANES), lambda bi, i: (bi, i, 0)),
            pl.BlockSpec((1, w, n_ctx), lambda bi, i: (bi, 0, 0)),
            pl.BlockSpec((1, n_ctx, w), lambda bi, i: (bi, 0, 0)),
            pl.BlockSpec((1, w, WINDOW), lambda bi, i: (bi, 0, prev_i(i))),
            pl.BlockSpec((1, w, tq), lambda bi, i: (bi, 0, i)),
            pl.BlockSpec((1, w, WINDOW), lambda bi, i: (bi, 0, next_i(i))),
            pl.BlockSpec((1, WINDOW, w), lambda bi, i: (bi, prev_i(i), 0)),
            pl.BlockSpec((1, tq, w), lambda bi, i: (bi, i, 0)),
            pl.BlockSpec((1, WINDOW, w), lambda bi, i: (bi, next_i(i), 0)),
        ],
        out_specs=pl.BlockSpec((1, tq, 4 * LANES), lambda bi, i: (bi, i, 0)),
        out_shape=jax.ShapeDtypeStruct((b, n_lat, 4 * LANES), BF16),
        compiler_params=_params("parallel", "arbitrary"),
        name="window_attention",
    )(sink, q, kct, vc, kdt, kdt, kdt, vd, vd, vd)


def _hyena_filter_kernel(feat_ref, w1_ref, b1_ref, w2_ref, b2_ref, w3_ref, b3_ref, freq_ref, decay_ref, h_ref,
                         *, n_forward):
    j = pl.program_id(0)
    hp = lax.Precision.HIGHEST
    freq = freq_ref[...]
    hid = jnp.sin(freq * (jnp.dot(feat_ref[...], w1_ref[...], precision=hp, preferred_element_type=F32) + b1_ref[...]))
    hid = jnp.sin(freq * (jnp.dot(hid, w2_ref[...], precision=hp, preferred_element_type=F32) + b2_ref[...]))
    h = jnp.dot(hid, w3_ref[...], precision=hp, preferred_element_type=F32) + b3_ref[...]
    h = h * decay_ref[...]
    row = lax.broadcasted_iota(jnp.int32, h.shape, 0)
    h_ref[...] = jnp.where((row == 0) & (j >= n_forward), 0.0, h).astype(h_ref.dtype)


def _hyena_filters(feats, w1, b1, w2, b2, w3, b3, freq, decay):
    length = feats.shape[0]
    n = w3.shape[1]
    ch = decay.shape[1]
    full = lambda a: pl.BlockSpec(a.shape, lambda j: (0, 0))
    return pl.pallas_call(
        functools.partial(_hyena_filter_kernel, n_forward=HYENA_ORDER),
        grid=(n // ch,),
        in_specs=[full(feats), full(w1), full(b1), full(w2), full(b2),
                  pl.BlockSpec((w3.shape[0], ch), lambda j: (0, j)), pl.BlockSpec((1, ch), lambda j: (0, j)),
                  full(freq), full(decay)],
        out_specs=pl.BlockSpec((length, ch), lambda j: (0, j)),
        out_shape=jax.ShapeDtypeStruct((length, n), BF16),
        compiler_params=_params("arbitrary"),
        name="hyena_filters",
    )(feats, w1, b1, w2, b2, w3, b3, freq, decay)


def _spectrum_kernel(f_ref, h_ref, re_ref, im_ref):
    j = pl.program_id(0)
    tk = re_ref.shape[0]
    n = re_ref.shape[1]
    a = _dot(f_ref[0], h_ref[...])
    re_ref[...] = a[:tk, :n] + a[:tk, n:]
    row = lax.broadcasted_iota(jnp.int32, (tk, n), 0)
    sign = jnp.where((row == 0) & (j == 0), 1.0, -1.0)
    im_ref[...] = a[tk:, :n] + sign * a[tk:, n:]


def _filter_spectrum(f3, hcat):
    nt, tk2, length = f3.shape
    n = hcat.shape[1] // 2
    tk = tk2 // 2
    out = pl.BlockSpec((tk, n), lambda j: (j, 0))
    return pl.pallas_call(
        _spectrum_kernel,
        grid=(nt,),
        in_specs=[pl.BlockSpec((1, tk2, length), lambda j: (j, 0, 0)),
                  pl.BlockSpec(hcat.shape, lambda j: (0, 0))],
        out_specs=[out, out],
        out_shape=[jax.ShapeDtypeStruct((nt * tk, n), F32)] * 2,
        compiler_params=_params("arbitrary"),
        name="filter_spectrum",
    )(f3, hcat)


def _long_conv_kernel(z_ref, x_ref, f_ref, g_ref, hre_ref, him_ref, bias_ref, o_ref, zb_ref, acc_ref):
    j = pl.program_id(1)
    tk = hre_ref.shape[0]

    @pl.when(j == 0)
    def _():
        zb_ref[...] = z_ref[0].astype(BF16)
        acc_ref[...] = jnp.zeros_like(acc_ref)

    a = _dot(f_ref[0], zb_ref[...])
    a_re, a_im = a[:tk], a[tk:]
    h_re, h_im = hre_ref[...], him_ref[...]
    row = lax.broadcasted_iota(jnp.int32, a_re.shape, 0)
    packed = (row == 0) & (j == 0)
    ii = a_im * h_im
    p_re = a_re * h_re - jnp.where(packed, 0.0, ii)
    p_im = jnp.where(packed, ii, a_re * h_im + a_im * h_re)
    p = jnp.concatenate([p_re, p_im], axis=0).astype(BF16)
    acc_ref[...] += _dot(g_ref[0], p)

    @pl.when(j == pl.num_programs(1) - 1)
    def _():
        o_ref[0] = (x_ref[0] * (acc_ref[...] + bias_ref[...] * z_ref[0])).astype(o_ref.dtype)


def _long_conv(z_src, z_col, x_src, x_col, f3, g3, hre, him, h_col, bias, out_dtype):
    b, length, _ = z_src.shape
    nt, tk2, _ = f3.shape
    tk = tk2 // 2
    ch = HYENA_CH
    return pl.pallas_call(
        _long_conv_kernel,
        grid=(b, nt),
        in_specs=[
            pl.BlockSpec((1, length, ch), lambda bi, j: (bi, 0, z_col)),
            pl.BlockSpec((1, length, ch), lambda bi, j: (bi, 0, x_col)),
            pl.BlockSpec((1, tk2, length), lambda bi, j: (j, 0, 0)),
            pl.BlockSpec((1, length, tk2), lambda bi, j: (j, 0, 0)),
            pl.BlockSpec((tk, ch), lambda bi, j: (j, h_col)),
            pl.BlockSpec((tk, ch), lambda bi, j: (j, h_col)),
            pl.BlockSpec((1, ch), lambda bi, j: (0, 0)),
        ],
        out_specs=pl.BlockSpec((1, length, ch), lambda bi, j: (bi, 0, 0)),
        out_shape=jax.ShapeDtypeStruct((b, length, ch), out_dtype),
        scratch_shapes=[pltpu.VMEM((length, ch), BF16), pltpu.VMEM((length, ch), F32)],
        compiler_params=_params("parallel", "arbitrary"),
        name="long_conv",
    )(z_src, x_src, f3, g3, hre, him, bias)


def _axial_rope(length, rot_dim):
    rows = length // GRID_W
    row = jnp.repeat(jnp.arange(rows), GRID_W).astype(F32)
    col = jnp.tile(jnp.arange(GRID_W), rows).astype(F32)
    quarter = rot_dim // 4
    inv = ROPE_BASE ** (-jnp.arange(quarter, dtype=F32) / quarter)
    ang = jnp.concatenate([row[:, None] * inv, col[:, None] * inv], axis=-1)
    return jnp.cos(ang), jnp.sin(ang)


def _rope_tables(length, n_ctx):
    cos_m, sin_m = _axial_rope(length, MLA_ROPE)
    cos_d, sin_d = _axial_rope(length, DIFF_HEAD_DIM)
    ones = lambda w: jnp.ones((length, w), F32)
    zeros = lambda w: jnp.zeros((length, w), F32)
    cm = jnp.concatenate([ones(MLA_NOPE), cos_m, cos_m, ones(LANES - MLA_NOPE - MLA_ROPE)], axis=1)
    sm = jnp.concatenate([zeros(MLA_NOPE), -sin_m, sin_m, zeros(LANES - MLA_NOPE - MLA_ROPE)], axis=1)
    cd = jnp.concatenate([cos_d] * 4, axis=1)
    sd = jnp.concatenate([-sin_d, sin_d] * 2, axis=1)
    pad = lambda t, v: jnp.concatenate([t, jnp.full((n_ctx, LANES), v, F32)], axis=0)
    return pad(cm, 1.0), pad(sm, 0.0), pad(cd, 1.0), pad(sd, 0.0)


def _dft_matrices(length):
    n = 2 * length
    k = np.arange(length, dtype=np.int64)
    ang = 2.0 * np.pi * ((k[:, None] * k[None, :]) % n).astype(np.float64) / n
    alt = np.where(k % 2 == 0, 1.0, -1.0)
    f_re = np.cos(ang)
    f_im = -np.sin(ang)
    f_im[0, :] = alt
    g_re = (2.0 / n) * np.cos(ang.T)
    g_im = -(2.0 / n) * np.sin(ang.T)
    g_re[:, 0] = 1.0 / n
    g_im[:, 0] = alt / n
    nt = length // FREQ_TILE
    f3 = np.concatenate([f_re.reshape(nt, FREQ_TILE, length), f_im.reshape(nt, FREQ_TILE, length)], axis=1)
    g3 = np.concatenate([g_re.reshape(length, nt, FREQ_TILE), g_im.reshape(length, nt, FREQ_TILE)], axis=2)
    g3 = np.transpose(g3, (1, 0, 2))
    return jnp.asarray(f3, dtype=BF16), jnp.asarray(g3, dtype=BF16)


def _hyena_features(length):
    t = jnp.arange(length, dtype=F32)
    tn = t / max(length - 1, 1)
    bands = jnp.linspace(1e-4, HYENA_BANDS - 1, HYENA_BANDS, dtype=F32)
    ang = 2.0 * math.pi * bands[None, :] * t[:, None] / length
    feats = jnp.concatenate([tn[:, None], jnp.cos(ang), -jnp.sin(ang)], axis=-1)
    min_decay = math.log(HYENA_DECAY_TARGET) / HYENA_SLOW_PCT
    max_decay = math.log(HYENA_DECAY_TARGET) / HYENA_FAST_PCT
    deltas = jnp.abs(jnp.linspace(min_decay, max_decay, HYENA_CH, dtype=F32))
    decay = jnp.exp(-tn[:, None] * deltas[None, :])
    return feats, decay


def _pad_cols(w, left, right):
    return jnp.pad(w, ((0, 0),) * (w.ndim - 1) + ((left, right),))


def _even_weights(w_in, w_uq, w_ukv):
    d = w_in.shape[0]
    o = MLA_Q_RANK + MLA_KV_RANK
    w_kr = _pad_cols(w_in[:, o:o + MLA_ROPE], MLA_NOPE, LANES - MLA_NOPE - MLA_ROPE)
    w_in_ext = jnp.concatenate([w_in[:, :o], w_kr, w_in[:, o + MLA_ROPE:]], axis=1)
    uq = _pad_cols(w_uq.reshape(MLA_Q_RANK, MLA_HEADS, MLA_NOPE + MLA_ROPE), 0, LANES - MLA_NOPE - MLA_ROPE)
    ukv = w_ukv.reshape(MLA_KV_RANK, MLA_HEADS, MLA_NOPE + MLA_V)
    uk = _pad_cols(ukv[:, :, :MLA_NOPE], 0, LANES - MLA_NOPE)
    uv = ukv[:, :, MLA_NOPE:].reshape(MLA_KV_RANK, MLA_HEADS // 2, 2, MLA_V)
    uv = jnp.stack([_pad_cols(uv[:, :, 0], 0, LANES - MLA_V), _pad_cols(uv[:, :, 1], LANES - MLA_V, 0)], axis=2)
    w_ukv_ext = jnp.concatenate([uk.reshape(MLA_KV_RANK, -1), uv.reshape(MLA_KV_RANK, -1)], axis=1)
    return w_in_ext.astype(BF16), uq.reshape(MLA_Q_RANK, -1).astype(BF16), w_ukv_ext.astype(BF16)


def _odd_weights(w_in):
    d = w_in.shape[0]
    nq = WIN_Q_HEADS * WIN_HEAD_DIM
    nkv = WIN_KV_HEADS * WIN_HEAD_DIM
    dup = lambda w: jnp.concatenate([w.reshape(d, WIN_KV_HEADS, WIN_HEAD_DIM)] * 2, axis=-1).reshape(d, -1)
    wk = dup(w_in[:, nq:nq + nkv])
    wv = dup(w_in[:, nq + nkv:nq + 2 * nkv])
    wqkv = jnp.concatenate([w_in[:, :nq], wk, wv], axis=1).astype(BF16)
    wkv = jnp.concatenate([wk, wv], axis=1).astype(BF16)
    wu = w_in[:, nq + 2 * nkv:].astype(BF16)
    return wqkv, wkv, wu


def _ffn_weights(w_up, conv_w, conv_b, w_down):
    d = w_up.shape[0]
    nc = D_FF // FFN_CHUNK
    chunks = lambda w: jnp.transpose(w.reshape(w.shape[0], nc, FFN_CHUNK), (1, 0, 2))
    cw = jnp.concatenate([conv_w, conv_b[None, :]], axis=0)
    return (chunks(w_up[:, :D_FF]).astype(BF16), chunks(w_up[:, D_FF:]).astype(BF16),
            chunks(cw[:, :D_FF]), chunks(cw[:, D_FF:]),
            w_down.astype(BF16))


def kernel(x, c, ctx, c_ctx, ada_w, ada_b, norm_g, mix_w_out, ffn_w_up, ffn_conv_w, ffn_conv_b, ffn_w_down, even_w_in, mla_q_norm_g, mla_w_uq, mla_kv_norm_g, mla_w_ukv, diff_lambda, diff_subln_g, odd_w_in, win_sink, hy_conv_w, hy_conv_b, hy_f_w1, hy_f_b1, hy_f_w2, hy_f_b2, hy_f_w3, hy_f_b3, hy_f_freq, hy_bias):
    b, seq, d = x.shape
    n_ctx = ctx.shape[1]
    depth = ada_w.shape[0]
    assert depth == 2 and d == D_MODEL and seq % ROW_TILE == 0 and n_ctx == ROW_TILE
    n_lat_tiles = seq // ROW_TILE
    n_tiles = n_lat_tiles + 1

    pad_rows = (-(b + 1)) % 8
    cond = jnp.concatenate([c, c_ctx[None, :], jnp.zeros((pad_rows, d), F32)], axis=0)
    mod = _ada_table(cond, ada_w, ada_b).reshape(depth, b + 1 + pad_rows, 6, d)[:, :b + 1]
    g = norm_g.reshape(depth, 4, 1, d)
    cm, sm, cd, sd = _rope_tables(seq, n_ctx)

    lambda_init = 0.8 - 0.6 * math.exp(-0.3 * 0)
    w_in_e, w_uq_e, w_ukv_e = _even_weights(even_w_in[0], mla_w_uq[0], mla_w_ukv[0])
    qm, km, vm, dq, dk, dv = _proj_even(
        x, ctx, mod[0], g[0, 0], w_in_e, mla_q_norm_g[0][None, :], w_uq_e, mla_kv_norm_g[0][None, :], w_ukv_e,
        cm, sm, cd, sd)
    rows = seq + n_ctx
    lam_args = (diff_lambda[0], diff_subln_g[0][None, :], lambda_init)
    w_out0 = mix_w_out[0].astype(BF16)
    ffn0 = _ffn_weights(ffn_w_up[0], ffn_conv_w[0], ffn_conv_b[0], ffn_w_down[0])
    o_mla = _mla_attention(qm, km, vm, ATTN_Q_TILE, 0, seq, 0, rows)
    o_diff = _diff_attention(dq, dk, dv, *lam_args, ATTN_Q_TILE, 0, seq, 0, rows)
    x1, h2 = _merge(o_mla, o_diff, x, mod[0], w_out0, g[0, 1], g[0, 2], ROW_TILE, 0, False)
    x_lat = _ffn(h2, x1, mod[0], *ffn0, g[0, 3], FFN_ROW_TILE, 0, seq, False)
    o_mla = _mla_attention(qm, km, vm, n_ctx, seq, n_ctx, seq, n_ctx)
    o_diff = _diff_attention(dq, dk, dv, *lam_args, n_ctx, seq, n_ctx, seq, n_ctx)
    x1, h2 = _merge(o_mla, o_diff, ctx, mod[0], w_out0, g[0, 1], g[0, 2], n_ctx, 0, True)
    x_ctx = _ffn(h2, x1, mod[0], *ffn0, g[0, 3], n_ctx, 0, n_ctx, True)

    wqkv, wkv, wu = _odd_weights(odd_w_in[0])
    cw_hy = jnp.concatenate([hy_conv_w[0], hy_conv_b[0][None, :]], axis=0)
    q, kd, vd, u = _proj_odd(x_lat, mod[1], g[1, 0], wqkv, wu, cw_hy, cd, sd, n_lat_tiles)
    kc, vc = _proj_ctx_kv(x_ctx, mod[1], g[1, 0], wkv, 0, n_ctx)
    o_win = _window_attention(q, kd, vd, kc, vc, win_sink[0])

    feats, decay = _hyena_features(seq)
    emb = feats.shape[1]
    feats = _pad_cols(feats, 0, LANES - emb)
    w1 = jnp.pad(hy_f_w1[0], ((0, LANES - emb), (0, 0)))
    hcat = _hyena_filters(feats, w1, hy_f_b1[0][None, :], hy_f_w2[0], hy_f_b2[0][None, :], hy_f_w3[0],
                          hy_f_b3[0][None, :], hy_f_freq[0][None, :], decay)
    f3, g3 = _dft_matrices(seq)
    hre, him = _filter_spectrum(f3, hcat)
    z = _long_conv(u, 0, u, 1, f3, g3, hre, him, 0, hy_bias[0, 0][None, :], F32)
    z = _long_conv(z, 0, u, 2, f3, g3, hre, him, 1, hy_bias[0, 1][None, :], BF16)

    w_out1 = mix_w_out[1].astype(BF16)
    x1, h2 = _merge(o_win, z, x_lat, mod[1], w_out1, g[1, 1], g[1, 2], ROW_TILE, 0, False)
    ffn1 = _ffn_weights(ffn_w_up[1], ffn_conv_w[1], ffn_conv_b[1], ffn_w_down[1])
    return _ffn(h2, x1, mod[1], *ffn1, g[1, 3], FFN_ROW_TILE, 0, seq, False)
```

```python
import functools
import math

import numpy as np
import jax
import jax.numpy as jnp
from jax import lax
from jax.experimental import pallas as pl
from jax.experimental.pallas import tpu as pltpu

D_MODEL = 1024
GRID_W = 64
MLA_HEADS = 8
MLA_NOPE = 64
MLA_ROPE = 32
MLA_V = 64
MLA_Q_RANK = 384
MLA_KV_RANK = 256
DIFF_HEADS = 4
DIFF_HEAD_DIM = 64
DIFF_V_DIM = 128
WIN_Q_HEADS = 8
WIN_KV_HEADS = 2
WIN_HEAD_DIM = 64
WINDOW = 128
HYENA_CH = 512
HYENA_ORDER = 2
HYENA_BANDS = 16
HYENA_DECAY_TARGET = 1e-2
HYENA_FAST_PCT = 0.3
HYENA_SLOW_PCT = 1.5
D_FF = 2816
ROPE_BASE = 10000.0
NORM_EPS = 1e-6
NEG_INF = -1e30

LANES = 128
HALO_ROWS = 16
ROW_TILE = 256
ATTN_Q_TILE = 1024
ATTN_SUB_TILE = 256
ATTN_KEY_CHUNK = 256
LOG2E = math.log2(math.e)
FFN_ROW_TILE = 512
PROJ_ROW_TILE = 512
FFN_CHUNK = 256
FFN_DOWN_GROUP = 4
FREQ_TILE = 256
VMEM_LIMIT = 48 * 1024 * 1024

BF16 = jnp.bfloat16
F32 = jnp.float32


def _params(*semantics):
    return pltpu.CompilerParams(dimension_semantics=semantics, vmem_limit_bytes=VMEM_LIMIT)


def _rms(x, g):
    return x * lax.rsqrt(jnp.mean(x * x, axis=-1, keepdims=True) + NORM_EPS) * g


def _dot(a, b):
    return jnp.dot(a, b, preferred_element_type=F32)


def _dot_nt(a, b):
    return lax.dot_general(a, b, (((1,), (1,)), ((), ())), preferred_element_type=F32)


def _lane_iota(shape):
    return lax.broadcasted_iota(jnp.int32, shape, len(shape) - 1)


def _rope_chunk(x, cos, sin_signed, half):
    first = (_lane_iota(x.shape) % (2 * half)) < half
    partner = jnp.where(first, pltpu.roll(x, LANES - half, 1), pltpu.roll(x, half, 1))
    return x * cos + partner * sin_signed


def _shift_rows(u, prev_row, next_row):
    rows = u.shape[0]
    r = lax.broadcasted_iota(jnp.int32, u.shape, 0)
    down = jnp.where(r == 0, prev_row, pltpu.roll(u, 1, 0))
    up = jnp.where(r == rows - 1, next_row, pltpu.roll(u, rows - 1, 0))
    return down, up


def _ada_kernel(c_ref, w_ref, b_ref, o_ref):
    c = c_ref[...]
    sc = c * jax.nn.sigmoid(c)
    o_ref[0] = _dot(sc.astype(BF16), w_ref[0].astype(BF16)) + b_ref[0]


def _ada_table(cond, ada_w, ada_b):
    depth, d, n = ada_w.shape
    rows = cond.shape[0]
    tn = 1536
    return pl.pallas_call(
        _ada_kernel,
        grid=(depth, n // tn),
        in_specs=[
            pl.BlockSpec((rows, d), lambda l, j: (0, 0)),
            pl.BlockSpec((1, d, tn), lambda l, j: (l, 0, j)),
            pl.BlockSpec((1, 1, tn), lambda l, j: (l, 0, j)),
        ],
        out_specs=pl.BlockSpec((1, rows, tn), lambda l, j: (l, 0, j)),
        out_shape=jax.ShapeDtypeStruct((depth, rows, n), F32),
        compiler_params=_params("arbitrary", "arbitrary"),
        name="ada_table",
    )(cond, ada_w, ada_b.reshape(depth, 1, n))


def _proj_even_kernel(x_ref, mod_ref, g_ref, win_ref, qg_ref, wuq_ref, kvg_ref, wukv_ref,
                      cm_ref, sm_ref, cd_ref, sd_ref, *rest):
    qm_ref, km_ref, vm_ref, dq_ref, dk_ref, dv_ref = rest[-6:]
    x = x_ref[0]
    mod = mod_ref[0]
    h = _rms(x, g_ref[...]) * (1.0 + mod[1:2]) + mod[0:1]
    p = _dot(h.astype(BF16), win_ref[...])
    o = 0
    cq = p[:, o:o + MLA_Q_RANK]; o += MLA_Q_RANK
    ckv = p[:, o:o + MLA_KV_RANK]; o += MLA_KV_RANK
    kr = p[:, o:o + LANES]; o += LANES
    dq = p[:, o:o + 512]; o += 512
    dk = p[:, o:o + 512]; o += 512
    dv = p[:, o:o + 512]
    q = _dot(_rms(cq, qg_ref[...]).astype(BF16), wuq_ref[...])
    kv = _dot(_rms(ckv, kvg_ref[...]).astype(BF16), wukv_ref[...])
    cm, sm, cd, sd = cm_ref[...], sm_ref[...], cd_ref[...], sd_ref[...]
    mla_scale = (MLA_NOPE + MLA_ROPE) ** -0.5 * LOG2E
    diff_scale = DIFF_HEAD_DIM ** -0.5 * LOG2E
    kr = _rope_chunk(kr, cm, sm, MLA_ROPE // 2)
    for hd in range(MLA_HEADS):
        c = slice(hd * LANES, (hd + 1) * LANES)
        qm_ref[0, c, :] = (_rope_chunk(q[:, c], cm, sm, MLA_ROPE // 2) * mla_scale).astype(BF16).T
        km_ref[0, :, c] = (kv[:, c] + kr).astype(BF16)
        vm_ref[0, c, :] = kv[:, MLA_HEADS * LANES + hd * LANES:MLA_HEADS * LANES + (hd + 1) * LANES].astype(BF16).T
    for hd in range(DIFF_HEADS):
        c = slice(hd * LANES, (hd + 1) * LANES)
        dq_ref[0, c, :] = (_rope_chunk(dq[:, c], cd, sd, DIFF_HEAD_DIM // 2) * diff_scale).astype(BF16).T
        dk_ref[0, :, c] = _rope_chunk(dk[:, c], cd, sd, DIFF_HEAD_DIM // 2).astype(BF16)
        dv_ref[0, c, :] = dv[:, c].astype(BF16).T


def _proj_even(x, mod, weights, tables, rows, tm, row_off, ctx_mod, prior=None):
    b, n, d = x.shape
    off = row_off // tm
    ctx_row = mod.shape[0] - 1
    const = lambda a: pl.BlockSpec(a.shape, lambda bi, i: (0,) * a.ndim)
    tab = pl.BlockSpec((tm, LANES), lambda bi, i: (off + i, 0))
    out = lambda w: pl.BlockSpec((1, tm, w), lambda bi, i: (bi, off + i, 0))
    shp = lambda w: jax.ShapeDtypeStruct((b, rows, w), BF16)
    out_t = lambda w: pl.BlockSpec((1, w, tm), lambda bi, i: (bi, 0, off + i))
    shp_t = lambda w: jax.ShapeDtypeStruct((b, w, rows), BF16)
    n_in = 2 + len(weights) + len(tables)
    prior = () if prior is None else tuple(prior)
    return pl.pallas_call(
        _proj_even_kernel,
        grid=(b, n // tm),
        in_specs=[
            pl.BlockSpec((1, tm, d), lambda bi, i: (bi, i, 0)),
            pl.BlockSpec((1, 6, d), lambda bi, i: (ctx_row if ctx_mod else bi, 0, 0)),
            *[const(w) for w in weights], *[tab for _ in tables],
            *[pl.BlockSpec(memory_space=pl.ANY) for _ in prior],
        ],
        out_specs=[out_t(1024), out(1024), out_t(1024), out_t(512), out(512), out_t(512)],
        out_shape=[shp_t(1024), shp(1024), shp_t(1024), shp_t(512), shp(512), shp_t(512)],
        input_output_aliases={n_in + k: k for k in range(len(prior))},
        compiler_params=_params("parallel", "arbitrary"),
        name="proj_even",
    )(x, mod, *weights, *tables, *prior)


def _attend_t(qt, k, vt):
    return _softmax_pv_t(_logits_t(qt, k), vt)


def _logits_t(qt, k):
    nk = k.shape[0]
    step = min(nk, ATTN_KEY_CHUNK)
    return [_dot(k[a:a + step], qt) for a in range(0, nk, step)]


def _softmax_pv_t(s, vt):
    step = s[0].shape[0]
    m = functools.reduce(jnp.maximum, [jnp.max(si, axis=0, keepdims=True) for si in s])
    p = [jnp.exp2(si - m) for si in s]
    l = sum(jnp.sum(pi, axis=0, keepdims=True) for pi in p)
    o = sum(_dot(vt[:, j * step:(j + 1) * step], pi.astype(BF16)) for j, pi in enumerate(p))
    return o / l


def _mla_kernel(qt_ref, k_ref, vt_ref, o_ref):
    sub = min(qt_ref.shape[2], ATTN_SUB_TILE)
    items = [(r, hd) for r in range(qt_ref.shape[2] // sub) for hd in range(2)]

    def logits(item):
        r, hd = item
        c = slice(hd * LANES, (hd + 1) * LANES)
        return _logits_t(qt_ref[0, c, r * sub:(r + 1) * sub], k_ref[0, :, c])

    s_next = logits(items[0])
    acc = None
    for j, (r, hd) in enumerate(items):
        s = s_next
        if j + 1 < len(items):
            s_next = logits(items[j + 1])
        o = _softmax_pv_t(s, vt_ref[0, hd * LANES:(hd + 1) * LANES, :])
        acc = o if hd == 0 else acc + o
        if hd == 1:
            o_ref[0, r * sub:(r + 1) * sub, :] = acc.T.astype(o_ref.dtype)


def _mla_attention(qmt, km, vmt, tq, q_off, n_q, k_off, n_k):
    b = km.shape[0]
    pairs = MLA_HEADS // 2
    qo, ko = q_off // tq, k_off // n_k
    return pl.pallas_call(
        _mla_kernel,
        grid=(b, pairs, n_q // tq),
        in_specs=[
            pl.BlockSpec((1, 2 * LANES, tq), lambda bi, p, i: (bi, p, qo + i)),
            pl.BlockSpec((1, n_k, 2 * LANES), lambda bi, p, i: (bi, ko, p)),
            pl.BlockSpec((1, 2 * LANES, n_k), lambda bi, p, i: (bi, p, ko)),
        ],
        out_specs=pl.BlockSpec((1, tq, LANES), lambda bi, p, i: (bi, i, p)),
        out_shape=jax.ShapeDtypeStruct((b, n_q, pairs * LANES), BF16),
        compiler_params=_params("parallel", "arbitrary", "arbitrary"),
        name="mla_attention",
    )(qmt, km, vmt)


def _diff_kernel(qt_ref, k_ref, vt_ref, lam_ref, g_ref, o_ref, *, lambda_init):
    lv = lam_ref[...]
    lam = (jnp.exp(jnp.sum(lv[0:1] * lv[1:2], axis=-1, keepdims=True))
           - jnp.exp(jnp.sum(lv[2:3] * lv[3:4], axis=-1, keepdims=True)) + lambda_init)
    vt = vt_ref[0]
    sub = min(qt_ref.shape[2], ATTN_SUB_TILE)
    items = [(r, which) for r in range(qt_ref.shape[2] // sub) for which in range(2)]

    def logits(item):
        r, which = item
        qt = qt_ref[0, :, r * sub:(r + 1) * sub]
        first = lax.broadcasted_iota(jnp.int32, qt.shape, 0) < DIFF_HEAD_DIM
        zero = jnp.zeros_like(qt)
        return _logits_t(jnp.where(first, qt, zero) if which == 0 else jnp.where(first, zero, qt), k_ref[0])

    s_next = logits(items[0])
    o1 = None
    for j, (r, which) in enumerate(items):
        s = s_next
        if j + 1 < len(items):
            s_next = logits(items[j + 1])
        o = _softmax_pv_t(s, vt)
        if which == 0:
            o1 = o
        else:
            o = (o1 - lam * o).T
            o_ref[0, r * sub:(r + 1) * sub, :] = (_rms(o, g_ref[...]) * (1.0 - lambda_init)).astype(o_ref.dtype)


def _diff_attention(dq, dk, dv, diff_lambda, subln_g, lambda_init, tq, q_off, n_q, k_off, n_k):
    b = dq.shape[0]
    qo, ko = q_off // tq, k_off // n_k
    return pl.pallas_call(
        functools.partial(_diff_kernel, lambda_init=lambda_init),
        grid=(b, DIFF_HEADS, n_q // tq),
        in_specs=[
            pl.BlockSpec((1, LANES, tq), lambda bi, h, i: (bi, h, qo + i)),
            pl.BlockSpec((1, n_k, LANES), lambda bi, h, i: (bi, ko, h)),
            pl.BlockSpec((1, LANES, n_k), lambda bi, h, i: (bi, h, ko)),
            pl.BlockSpec((4, DIFF_HEAD_DIM), lambda bi, h, i: (0, 0)),
            pl.BlockSpec((1, DIFF_V_DIM), lambda bi, h, i: (0, 0)),
        ],
        out_specs=pl.BlockSpec((1, tq, LANES), lambda bi, h, i: (bi, i, h)),
        out_shape=jax.ShapeDtypeStruct((b, n_q, DIFF_HEADS * LANES), BF16),
        compiler_params=_params("parallel", "arbitrary", "arbitrary"),
        name="diff_attention",
    )(dq, dk, dv, diff_lambda, subln_g)


def _merge_kernel(a_ref, b_ref, x_ref, mod_ref, w_ref, g1_ref, g2_ref, x1_ref, h2_ref):
    half = a_ref.shape[-1]
    mod = mod_ref[0]
    y = _dot(a_ref[0], w_ref[0:half, :]) + _dot(b_ref[0], w_ref[half:, :])
    x1 = x_ref[0] + mod[2:3] * _rms(y, g1_ref[...])
    x1_ref[0] = x1
    h2_ref[0] = (_rms(x1, g2_ref[...]) * (1.0 + mod[4:5]) + mod[3:4]).astype(BF16)


def _merge(a, bb, x, mod, w_out, g1, g2, tm, x_off, ctx_mod):
    b, rows, half = a.shape
    d = x.shape[-1]
    nt = rows // tm
    xo = x_off // tm
    ctx_row = mod.shape[0] - 1
    tile = lambda w: pl.BlockSpec((1, tm, w), lambda bi, i: (bi, i, 0))
    const = lambda *shape: pl.BlockSpec(shape, lambda bi, i: (0,) * len(shape))
    return pl.pallas_call(
        _merge_kernel,
        grid=(b, nt),
        in_specs=[
            tile(half), tile(half), pl.BlockSpec((1, tm, d), lambda bi, i: (bi, xo + i, 0)),
            pl.BlockSpec((1, 6, d), lambda bi, i: (ctx_row if ctx_mod else bi, 0, 0)),
            const(*w_out.shape), const(1, d), const(1, d),
        ],
        out_specs=[tile(d), tile(d)],
        out_shape=[jax.ShapeDtypeStruct((b, rows, d), F32), jax.ShapeDtypeStruct((b, rows, d), BF16)],
        compiler_params=_params("parallel", "arbitrary"),
        name="merge",
    )(a, bb, x, mod, w_out, g1, g2)


def _ffn_kernel(h_ref, hp_ref, hn_ref, x_ref, mod_ref, wa_ref, wg_ref, cwa_ref, cwg_ref, wd_ref, g_ref,
                o_ref, lhs_ref, ua0_ref, ug0_ref, ua1_ref, ug1_ref, act0_ref, act1_ref, acc_ref):
    i = pl.program_id(1)
    tm = h_ref.shape[1]
    n_chunks = wa_ref.shape[0]
    hp = hp_ref[0]
    hn = hn_ref[0]
    lhs_ref[0:HALO_ROWS] = jnp.where(i == 0, jnp.zeros_like(hp), hp)
    lhs_ref[HALO_ROWS:HALO_ROWS + tm] = h_ref[0]
    lhs_ref[HALO_ROWS + tm:] = jnp.where(i == pl.num_programs(1) - 1, jnp.zeros_like(hn), hn)
    rows = tm + 2 * HALO_ROWS
    mid = slice(HALO_ROWS, HALO_ROWS + tm)
    slots = ((ua0_ref, ug0_ref), (ua1_ref, ug1_ref))

    def up(j, slot):
        lhs = lhs_ref[...]
        slots[slot][0][...] = _dot(lhs, wa_ref[j])
        slots[slot][1][...] = _dot(lhs, wg_ref[j])

    def conv(u, cw):
        down = pltpu.roll(u, 1, 0)[mid]
        above = pltpu.roll(u, rows - 1, 0)[mid]
        return down * cw[0:1] + u[mid] * cw[1:2] + above * cw[2:3] + cw[3:4]

    acts = (act0_ref, act1_ref)
    width = wa_ref.shape[2]

    def activate(j):
        a = conv(slots[j % 2][0][...], cwa_ref[j])
        g = conv(slots[j % 2][1][...], cwg_ref[j])
        k = j % FFN_DOWN_GROUP
        acts[(j // FFN_DOWN_GROUP) % 2][:, k * width:(k + 1) * width] = ((g * jax.nn.sigmoid(g)) * a).astype(BF16)

    def down(j_last):
        j0 = j_last - j_last % FFN_DOWN_GROUP
        k = (j_last - j0 + 1) * width
        return _dot(acts[(j_last // FFN_DOWN_GROUP) % 2][:, :k], wd_ref[j0 * width:(j_last + 1) * width, :])

    up(0, 0)
    up(1, 1)
    y = None
    for j in range(n_chunks):
        activate(j)
        if j + 2 < n_chunks:
            up(j + 2, j % 2)
        if j % FFN_DOWN_GROUP == FFN_DOWN_GROUP - 1 or j == n_chunks - 1:
            y = down(j)
            if j == n_chunks - 1:
                break
            if j < FFN_DOWN_GROUP:
                acc_ref[...] = y
            else:
                acc_ref[...] += y
    total = y if n_chunks <= FFN_DOWN_GROUP else acc_ref[...] + y
    mod = mod_ref[0]
    o_ref[0] = x_ref[0] + mod[5:6] * _rms(total, g_ref[...])


def _ffn(h2, x1, mod, wa, wg, cwa, cwg, wd, g3, tm, row_offset, n_rows, ctx_mod):
    b, total_rows, d = h2.shape
    ctx_row = mod.shape[0] - 1
    nt = n_rows // tm
    off = row_offset // tm
    per = tm // HALO_ROWS
    last_halo = total_rows // HALO_ROWS - 1
    const = lambda *shape: pl.BlockSpec(shape, lambda bi, i: (0,) * len(shape), pipeline_mode=pl.Buffered(1))
    tile = pl.BlockSpec((1, tm, d), lambda bi, i: (bi, off + i, 0))
    return pl.pallas_call(
        _ffn_kernel,
        grid=(b, nt),
        in_specs=[
            tile,
            pl.BlockSpec((1, HALO_ROWS, d), lambda bi, i: (bi, jnp.maximum((off + i) * per - 1, 0), 0)),
            pl.BlockSpec((1, HALO_ROWS, d), lambda bi, i: (bi, jnp.minimum((off + i + 1) * per, last_halo), 0)),
            tile,
            pl.BlockSpec((1, 6, d), lambda bi, i: (ctx_row if ctx_mod else bi, 0, 0)),
            const(*wa.shape), const(*wg.shape), const(*cwa.shape), const(*cwg.shape), const(*wd.shape),
            const(1, d),
        ],
        out_specs=pl.BlockSpec((1, tm, d), lambda bi, i: (bi, i, 0)),
        out_shape=jax.ShapeDtypeStruct((b, n_rows, d), F32),
        scratch_shapes=[pltpu.VMEM((tm + 2 * HALO_ROWS, d), BF16)]
        + [pltpu.VMEM((tm + 2 * HALO_ROWS, FFN_CHUNK), F32)] * 4
        + [pltpu.VMEM((tm, FFN_DOWN_GROUP * FFN_CHUNK), BF16)] * 2 + [pltpu.VMEM((tm, d), F32)],
        compiler_params=_params("parallel", "arbitrary"),
        name="conv_ffn",
    )(h2, h2, h2, x1, mod, wa, wg, cwa, cwg, wd, g3)


def _proj_odd_kernel(x_ref, xp_ref, xn_ref, mod_ref, g_ref, wqkv_ref, wu_ref, cw_ref, cd_ref, sd_ref,
                     q_ref, k_ref, v_ref, u_ref, *, n_tiles):
    i = pl.program_id(1)
    mod = mod_ref[0]

    def normed(x):
        return (_rms(x, g_ref[...]) * (1.0 + mod[1:2]) + mod[0:1]).astype(BF16)

    h = normed(x_ref[0])
    p = _dot(h, wqkv_ref[...])
    cd, sd = cd_ref[...], sd_ref[...]
    scale = WIN_HEAD_DIM ** -0.5 * LOG2E
    for c in range(4):
        cs = slice(c * LANES, (c + 1) * LANES)
        q_ref[0, :, cs] = (_rope_chunk(p[:, cs], cd, sd, WIN_HEAD_DIM // 2) * scale).astype(BF16)
    for c in range(2):
        cs = slice(512 + c * LANES, 512 + (c + 1) * LANES)
        k_ref[0, c * LANES:(c + 1) * LANES, :] = _rope_chunk(p[:, cs], cd, sd, WIN_HEAD_DIM // 2).astype(BF16).T
    v_ref[0] = p[:, 768:1024].astype(BF16)
    prev_ok = jnp.where(i == 0, 0.0, 1.0)
    next_ok = jnp.where(i == n_tiles - 1, 0.0, 1.0)
    wu = wu_ref[...]
    u = _dot(h, wu)
    up = _dot(normed(xp_ref[0]), wu)[HALO_ROWS - 1:HALO_ROWS] * prev_ok
    un = _dot(normed(xn_ref[0]), wu)[0:1] * next_ok
    down, upw = _shift_rows(u, up, un)
    cw = cw_ref[...]
    u_ref[0] = down * cw[0:1] + u * cw[1:2] + upw * cw[2:3] + cw[3:4]


def _proj_odd(xcat, mod, g, wqkv, wu, cw, cd, sd, tm):
    b, rows, d = xcat.shape
    n_tiles = rows // tm
    per = tm // HALO_ROWS
    last_halo = n_tiles * per - 1
    const = lambda *shape: pl.BlockSpec(shape, lambda bi, i: (0,) * len(shape))
    tab = pl.BlockSpec((tm, LANES), lambda bi, i: (i, 0))
    out = lambda w: pl.BlockSpec((1, tm, w), lambda bi, i: (bi, i, 0))
    return pl.pallas_call(
        functools.partial(_proj_odd_kernel, n_tiles=n_tiles),
        grid=(b, n_tiles),
        in_specs=[
            pl.BlockSpec((1, tm, d), lambda bi, i: (bi, i, 0)),
            pl.BlockSpec((1, HALO_ROWS, d), lambda bi, i: (bi, jnp.maximum(i * per - 1, 0), 0)),
            pl.BlockSpec((1, HALO_ROWS, d), lambda bi, i: (bi, jnp.minimum((i + 1) * per, last_halo), 0)),
            pl.BlockSpec((1, 6, d), lambda bi, i: (bi, 0, 0)),
            const(1, d), const(*wqkv.shape), const(*wu.shape), const(*cw.shape), tab, tab,
        ],
        out_specs=[out(512), pl.BlockSpec((1, 2 * LANES, tm), lambda bi, i: (bi, 0, i)), out(256),
                   out(3 * HYENA_CH)],
        out_shape=[jax.ShapeDtypeStruct((b, rows, 512), BF16), jax.ShapeDtypeStruct((b, 2 * LANES, rows), BF16),
                   jax.ShapeDtypeStruct((b, rows, 256), BF16), jax.ShapeDtypeStruct((b, rows, 3 * HYENA_CH), F32)],
        compiler_params=_params("parallel", "arbitrary"),
        name="proj_odd",
    )(xcat, xcat, xcat, mod, g, wqkv, wu, cw, cd, sd)


def _proj_ctx_kv_kernel(x_ref, mod_ref, g_ref, w_ref, k_ref, v_ref):
    mod = mod_ref[0]
    h = (_rms(x_ref[0], g_ref[...]) * (1.0 + mod[1:2]) + mod[0:1]).astype(BF16)
    p = _dot(h, w_ref[...])
    k_ref[0] = p[:, :2 * LANES].astype(BF16).T
    v_ref[0] = p[:, 2 * LANES:].astype(BF16)


def _proj_ctx_kv(xcat, mod, g, wkv, ctx_tile, n_ctx):
    b, _, d = xcat.shape
    ctx_row = mod.shape[0] - 1
    out = pl.BlockSpec((1, n_ctx, 2 * LANES), lambda bi: (bi, 0, 0))
    return pl.pallas_call(
        _proj_ctx_kv_kernel,
        grid=(b,),
        in_specs=[
            pl.BlockSpec((1, n_ctx, d), lambda bi: (bi, ctx_tile, 0)),
            pl.BlockSpec((1, 6, d), lambda bi: (ctx_row, 0, 0)),
            pl.BlockSpec((1, d), lambda bi: (0, 0)),
            pl.BlockSpec(wkv.shape, lambda bi: (0, 0)),
        ],
        out_specs=[pl.BlockSpec((1, 2 * LANES, n_ctx), lambda bi: (bi, 0, 0)), out],
        out_shape=[jax.ShapeDtypeStruct((b, 2 * LANES, n_ctx), BF16), jax.ShapeDtypeStruct((b, n_ctx, 2 * LANES), BF16)],
        compiler_params=_params("parallel"),
        name="proj_ctx_kv",
    )(xcat, mod, g, wkv)


def _window_kernel(sink_ref, q_ref, kc_ref, vc_ref, kp_ref, km_ref, kn_ref, vp_ref, vm_ref, vn_ref, o_ref,
                   *, n_lat):
    i = pl.program_id(1)
    tq = q_ref.shape[1]
    n_ctx = vc_ref.shape[1]
    nk = n_ctx + tq + 2 * WINDOW
    col = lax.broadcasted_iota(jnp.int32, (tq, nk), 1)
    row = lax.broadcasted_iota(jnp.int32, (tq, nk), 0)
    k_pos = i * tq - WINDOW + (col - n_ctx)
    q_pos = i * tq + row
    visible = (col < n_ctx) | ((jnp.abs(q_pos - k_pos) <= WINDOW) & (k_pos >= 0) & (k_pos < n_lat))
    bias = jnp.where(visible, 0.0, NEG_INF)
    group = WIN_Q_HEADS // WIN_KV_HEADS
    bias = jnp.concatenate([bias, bias], axis=0)
    second_head = lax.broadcasted_iota(jnp.int32, (2 * tq, 1), 0) >= tq
    first = _lane_iota((tq, LANES)) < WIN_HEAD_DIM
    for hk in range(WIN_KV_HEADS):
        c = slice(hk * LANES, (hk + 1) * LANES)
        kt = jnp.concatenate([kc_ref[0, c, :], kp_ref[0, c, :], km_ref[0, c, :], kn_ref[0, c, :]], axis=1)
        v = jnp.concatenate([vc_ref[0, :, c], vp_ref[0, :, c], vm_ref[0, :, c], vn_ref[0, :, c]], axis=0)
        for pair in range(group // 2):
            chunk = hk * (group // 2) + pair
            q = q_ref[0, :, chunk * LANES:(chunk + 1) * LANES]
            zero = jnp.zeros_like(q)
            s = _dot(jnp.concatenate([jnp.where(first, q, zero), jnp.where(first, zero, q)], axis=0), kt) + bias
            sink = jnp.where(second_head, sink_ref[2 * chunk + 1], sink_ref[2 * chunk]) * LOG2E
            m = jnp.maximum(jnp.max(s, axis=-1, keepdims=True), sink)
            p = jnp.exp2(s - m)
            l = jnp.sum(p, axis=-1, keepdims=True) + jnp.exp2(sink - m)
            o = _dot(p.astype(BF16), v) / l
            o_ref[0, :, chunk * LANES:(chunk + 1) * LANES] = jnp.where(first, o[:tq], o[tq:]).astype(o_ref.dtype)


def _window_attention(q, kdt, vd, kct, vc, sink):
    b, n_lat, _ = q.shape
    tq = ROW_TILE
    n_ctx = vc.shape[1]
    per = tq // WINDOW
    last = n_lat // WINDOW - 1
    prev_i = lambda i: jnp.maximum(i * per - 1, 0)
    next_i = lambda i: jnp.minimum((i + 1) * per, last)
    w = 2 * LANES
    return pl.pallas_call(
        functools.partial(_window_kernel, n_lat=n_lat),
        grid=(b, n_lat // tq),
        in_specs=[
            pl.BlockSpec(memory_space=pltpu.SMEM),
            pl.BlockSpec((1, tq, 4 * LANES), lambda bi, i: (bi, i, 0)),
            pl.BlockSpec((1, w, n_ctx), lambda bi, i: (bi, 0, 0)),
            pl.BlockSpec((1, n_ctx, w), lambda bi, i: (bi, 0, 0)),
            pl.BlockSpec((1, w, WINDOW), lambda bi, i: (bi, 0, prev_i(i))),
            pl.BlockSpec((1, w, tq), lambda bi, i: (bi, 0, i)),
            pl.BlockSpec((1, w, WINDOW), lambda bi, i: (bi, 0, next_i(i))),
            pl.BlockSpec((1, WINDOW, w), lambda bi, i: (bi, prev_i(i), 0)),
            pl.BlockSpec((1, tq, w), lambda bi, i: (bi, i, 0)),
            pl.BlockSpec((1, WINDOW, w), lambda bi, i: (bi, next_i(i), 0)),
        ],
        out_specs=pl.BlockSpec((1, tq, 4 * LANES), lambda bi, i: (bi, i, 0)),
        out_shape=jax.ShapeDtypeStruct((b, n_lat, 4 * LANES), BF16),
        compiler_params=_params("parallel", "arbitrary"),
        name="window_attention",
    )(sink, q, kct, vc, kdt, kdt, kdt, vd, vd, vd)


def _hyena_filter_kernel(feat_ref, w1_ref, b1_ref, w2_ref, b2_ref, w3_ref, b3_ref, freq_ref, decay_ref, h_ref,
                         *, n_forward):
    j = pl.program_id(0)
    hp = lax.Precision.HIGHEST
    freq = freq_ref[...]
    hid = jnp.sin(freq * (jnp.dot(feat_ref[...], w1_ref[...], precision=hp, preferred_element_type=F32) + b1_ref[...]))
    hid = jnp.sin(freq * (jnp.dot(hid, w2_ref[...], precision=hp, preferred_element_type=F32) + b2_ref[...]))
    h = jnp.dot(hid, w3_ref[...], precision=hp, preferred_element_type=F32) + b3_ref[...]
    h = h * decay_ref[...]
    row = lax.broadcasted_iota(jnp.int32, h.shape, 0)
    h_ref[...] = jnp.where((row == 0) & (j >= n_forward), 0.0, h).astype(h_ref.dtype)


def _hyena_filters(feats, w1, b1, w2, b2, w3, b3, freq, decay):
    length = feats.shape[0]
    n = w3.shape[1]
    ch = decay.shape[1]
    full = lambda a: pl.BlockSpec(a.shape, lambda j: (0, 0))
    return pl.pallas_call(
        functools.partial(_hyena_filter_kernel, n_forward=HYENA_ORDER),
        grid=(n // ch,),
        in_specs=[full(feats), full(w1), full(b1), full(w2), full(b2),
                  pl.BlockSpec((w3.shape[0], ch), lambda j: (0, j)), pl.BlockSpec((1, ch), lambda j: (0, j)),
                  full(freq), full(decay)],
        out_specs=pl.BlockSpec((length, ch), lambda j: (0, j)),
        out_shape=jax.ShapeDtypeStruct((length, n), BF16),
        compiler_params=_params("arbitrary"),
        name="hyena_filters",
    )(feats, w1, b1, w2, b2, w3, b3, freq, decay)


def _spectrum_kernel(f_ref, h_ref, re_ref, im_ref):
    j = pl.program_id(0)
    tk = re_ref.shape[0]
    n = re_ref.shape[1]
    a = _dot(f_ref[0], h_ref[...])
    re_ref[...] = a[:tk, :n] + a[:tk, n:]
    row = lax.broadcasted_iota(jnp.int32, (tk, n), 0)
    sign = jnp.where((row == 0) & (j == 0), 1.0, -1.0)
    im_ref[...] = a[tk:, :n] + sign * a[tk:, n:]


def _filter_spectrum(f3, hcat):
    nt, tk2, length = f3.shape
    n = hcat.shape[1] // 2
    tk = tk2 // 2
    out = pl.BlockSpec((tk, n), lambda j: (j, 0))
    return pl.pallas_call(
        _spectrum_kernel,
        grid=(nt,),
        in_specs=[pl.BlockSpec((1, tk2, length), lambda j: (j, 0, 0)),
                  pl.BlockSpec(hcat.shape, lambda j: (0, 0))],
        out_specs=[out, out],
        out_shape=[jax.ShapeDtypeStruct((nt * tk, n), F32)] * 2,
        compiler_params=_params("arbitrary"),
        name="filter_spectrum",
    )(f3, hcat)


def _long_conv_kernel(z_ref, x_ref, f_ref, g_ref, hre_ref, him_ref, bias_ref, o_ref, zb_ref, acc_ref):
    j = pl.program_id(1)
    tk = hre_ref.shape[0]

    @pl.when(j == 0)
    def _():
        zb_ref[...] = z_ref[0].astype(BF16)
        acc_ref[...] = jnp.zeros_like(acc_ref)

    a = _dot(f_ref[0], zb_ref[...])
    a_re, a_im = a[:tk], a[tk:]
    h_re, h_im = hre_ref[...], him_ref[...]
    row = lax.broadcasted_iota(jnp.int32, a_re.shape, 0)
    packed = (row == 0) & (j == 0)
    ii = a_im * h_im
    p_re = a_re * h_re - jnp.where(packed, 0.0, ii)
    p_im = jnp.where(packed, ii, a_re * h_im + a_im * h_re)
    p = jnp.concatenate([p_re, p_im], axis=0).astype(BF16)
    acc_ref[...] += _dot(g_ref[0], p)

    @pl.when(j == pl.num_programs(1) - 1)
    def _():
        o_ref[0] = (x_ref[0] * (acc_ref[...] + bias_ref[...] * z_ref[0])).astype(o_ref.dtype)


def _long_conv(z_src, z_col, x_src, x_col, f3, g3, hre, him, h_col, bias, out_dtype):
    b, length, _ = z_src.shape
    nt, tk2, _ = f3.shape
    tk = tk2 // 2
    ch = HYENA_CH
    return pl.pallas_call(
        _long_conv_kernel,
        grid=(b, nt),
        in_specs=[
            pl.BlockSpec((1, length, ch), lambda bi, j: (bi, 0, z_col)),
            pl.BlockSpec((1, length, ch), lambda bi, j: (bi, 0, x_col)),
            pl.BlockSpec((1, tk2, length), lambda bi, j: (j, 0, 0)),
            pl.BlockSpec((1, length, tk2), lambda bi, j: (j, 0, 0)),
            pl.BlockSpec((tk, ch), lambda bi, j: (j, h_col)),
            pl.BlockSpec((tk, ch), lambda bi, j: (j, h_col)),
            pl.BlockSpec((1, ch), lambda bi, j: (0, 0)),
        ],
        out_specs=pl.BlockSpec((1, length, ch), lambda bi, j: (bi, 0, 0)),
        out_shape=jax.ShapeDtypeStruct((b, length, ch), out_dtype),
        scratch_shapes=[pltpu.VMEM((length, ch), BF16), pltpu.VMEM((length, ch), F32)],
        compiler_params=_params("parallel", "arbitrary"),
        name="long_conv",
    )(z_src, x_src, f3, g3, hre, him, bias)


def _axial_rope(length, rot_dim):
    rows = length // GRID_W
    row = jnp.repeat(jnp.arange(rows), GRID_W).astype(F32)
    col = jnp.tile(jnp.arange(GRID_W), rows).astype(F32)
    quarter = rot_dim // 4
    inv = ROPE_BASE ** (-jnp.arange(quarter, dtype=F32) / quarter)
    ang = jnp.concatenate([row[:, None] * inv, col[:, None] * inv], axis=-1)
    return jnp.cos(ang), jnp.sin(ang)


def _rope_tables(length, n_ctx):
    cos_m, sin_m = _axial_rope(length, MLA_ROPE)
    cos_d, sin_d = _axial_rope(length, DIFF_HEAD_DIM)
    ones = lambda w: jnp.ones((length, w), F32)
    zeros = lambda w: jnp.zeros((length, w), F32)
    cm = jnp.concatenate([ones(MLA_NOPE), cos_m, cos_m, ones(LANES - MLA_NOPE - MLA_ROPE)], axis=1)
    sm = jnp.concatenate([zeros(MLA_NOPE), -sin_m, sin_m, zeros(LANES - MLA_NOPE - MLA_ROPE)], axis=1)
    cd = jnp.concatenate([cos_d] * 4, axis=1)
    sd = jnp.concatenate([-sin_d, sin_d] * 2, axis=1)
    pad = lambda t, v: jnp.concatenate([t, jnp.full((n_ctx, LANES), v, F32)], axis=0)
    return pad(cm, 1.0), pad(sm, 0.0), pad(cd, 1.0), pad(sd, 0.0)


def _dft_matrices(length):
    n = 2 * length
    k = np.arange(length, dtype=np.int64)
    ang = 2.0 * np.pi * ((k[:, None] * k[None, :]) % n).astype(np.float64) / n
    alt = np.where(k % 2 == 0, 1.0, -1.0)
    f_re = np.cos(ang)
    f_im = -np.sin(ang)
    f_im[0, :] = alt
    g_re = (2.0 / n) * np.cos(ang.T)
    g_im = -(2.0 / n) * np.sin(ang.T)
    g_re[:, 0] = 1.0 / n
    g_im[:, 0] = alt / n
    nt = length // FREQ_TILE
    f3 = np.concatenate([f_re.reshape(nt, FREQ_TILE, length), f_im.reshape(nt, FREQ_TILE, length)], axis=1)
    g3 = np.concatenate([g_re.reshape(length, nt, FREQ_TILE), g_im.reshape(length, nt, FREQ_TILE)], axis=2)
    g3 = np.transpose(g3, (1, 0, 2))
    return jnp.asarray(f3, dtype=BF16), jnp.asarray(g3, dtype=BF16)


def _hyena_features(length):
    t = jnp.arange(length, dtype=F32)
    tn = t / max(length - 1, 1)
    bands = jnp.linspace(1e-4, HYENA_BANDS - 1, HYENA_BANDS, dtype=F32)
    ang = 2.0 * math.pi * bands[None, :] * t[:, None] / length
    feats = jnp.concatenate([tn[:, None], jnp.cos(ang), -jnp.sin(ang)], axis=-1)
    min_decay = math.log(HYENA_DECAY_TARGET) / HYENA_SLOW_PCT
    max_decay = math.log(HYENA_DECAY_TARGET) / HYENA_FAST_PCT
    deltas = jnp.abs(jnp.linspace(min_decay, max_decay, HYENA_CH, dtype=F32))
    decay = jnp.exp(-tn[:, None] * deltas[None, :])
    return feats, decay


def _pad_cols(w, left, right):
    return jnp.pad(w, ((0, 0),) * (w.ndim - 1) + ((left, right),))


def _even_weights(w_in, w_uq, w_ukv):
    d = w_in.shape[0]
    o = MLA_Q_RANK + MLA_KV_RANK
    w_kr = _pad_cols(w_in[:, o:o + MLA_ROPE], MLA_NOPE, LANES - MLA_NOPE - MLA_ROPE)
    w_in_ext = jnp.concatenate([w_in[:, :o], w_kr, w_in[:, o + MLA_ROPE:]], axis=1)
    uq = _pad_cols(w_uq.reshape(MLA_Q_RANK, MLA_HEADS, MLA_NOPE + MLA_ROPE), 0, LANES - MLA_NOPE - MLA_ROPE)
    ukv = w_ukv.reshape(MLA_KV_RANK, MLA_HEADS, MLA_NOPE + MLA_V)
    uk = _pad_cols(ukv[:, :, :MLA_NOPE], 0, LANES - MLA_NOPE)
    uv = ukv[:, :, MLA_NOPE:].reshape(MLA_KV_RANK, MLA_HEADS // 2, 2, MLA_V)
    uv = jnp.stack([_pad_cols(uv[:, :, 0], 0, LANES - MLA_V), _pad_cols(uv[:, :, 1], LANES - MLA_V, 0)], axis=2)
    w_ukv_ext = jnp.concatenate([uk.reshape(MLA_KV_RANK, -1), uv.reshape(MLA_KV_RANK, -1)], axis=1)
    return w_in_ext.astype(BF16), uq.reshape(MLA_Q_RANK, -1).astype(BF16), w_ukv_ext.astype(BF16)


def _odd_weights(w_in):
    d = w_in.shape[0]
    nq = WIN_Q_HEADS * WIN_HEAD_DIM
    nkv = WIN_KV_HEADS * WIN_HEAD_DIM
    dup = lambda w: jnp.concatenate([w.reshape(d, WIN_KV_HEADS, WIN_HEAD_DIM)] * 2, axis=-1).reshape(d, -1)
    wk = dup(w_in[:, nq:nq + nkv])
    wv = dup(w_in[:, nq + nkv:nq + 2 * nkv])
    wqkv = jnp.concatenate([w_in[:, :nq], wk, wv], axis=1).astype(BF16)
    wkv = jnp.concatenate([wk, wv], axis=1).astype(BF16)
    wu = w_in[:, nq + 2 * nkv:].astype(BF16)
    return wqkv, wkv, wu


def _ffn_weights(w_up, conv_w, conv_b, w_down):
    d = w_up.shape[0]
    nc = D_FF // FFN_CHUNK
    chunks = lambda w: jnp.transpose(w.reshape(w.shape[0], nc, FFN_CHUNK), (1, 0, 2))
    cw = jnp.concatenate([conv_w, conv_b[None, :]], axis=0)
    return (chunks(w_up[:, :D_FF]).astype(BF16), chunks(w_up[:, D_FF:]).astype(BF16),
            chunks(cw[:, :D_FF]), chunks(cw[:, D_FF:]),
            w_down.astype(BF16))


def kernel(x, c, ctx, c_ctx, ada_w, ada_b, norm_g, mix_w_out, ffn_w_up, ffn_conv_w, ffn_conv_b, ffn_w_down, even_w_in, mla_q_norm_g, mla_w_uq, mla_kv_norm_g, mla_w_ukv, diff_lambda, diff_subln_g, odd_w_in, win_sink, hy_conv_w, hy_conv_b, hy_f_w1, hy_f_b1, hy_f_w2, hy_f_b2, hy_f_w3, hy_f_b3, hy_f_freq, hy_bias):
    b, seq, d = x.shape
    n_ctx = ctx.shape[1]
    depth = ada_w.shape[0]
    assert depth == 2 and d == D_MODEL and seq % ROW_TILE == 0 and n_ctx == ROW_TILE
    n_lat_tiles = seq // ROW_TILE
    n_tiles = n_lat_tiles + 1

    pad_rows = (-(b + 1)) % 8
    cond = jnp.concatenate([c, c_ctx[None, :], jnp.zeros((pad_rows, d), F32)], axis=0)
    mod = _ada_table(cond, ada_w, ada_b).reshape(depth, b + 1 + pad_rows, 6, d)[:, :b + 1]
    g = norm_g.reshape(depth, 4, 1, d)
    cm, sm, cd, sd = _rope_tables(seq, n_ctx)

    lambda_init = 0.8 - 0.6 * math.exp(-0.3 * 0)
    w_in_e, w_uq_e, w_ukv_e = _even_weights(even_w_in[0], mla_w_uq[0], mla_w_ukv[0])
    rows = seq + n_ctx
    even_w = (g[0, 0], w_in_e, mla_q_norm_g[0][None, :], w_uq_e, mla_kv_norm_g[0][None, :], w_ukv_e)
    proj = _proj_even(x, mod[0], even_w, (cm, sm, cd, sd), rows, PROJ_ROW_TILE, 0, False)
    qm, km, vm, dq, dk, dv = _proj_even(ctx, mod[0], even_w, (cm, sm, cd, sd), rows, n_ctx, seq, True, prior=proj)
    lam_args = (diff_lambda[0], diff_subln_g[0][None, :], lambda_init)
    w_out0 = mix_w_out[0].astype(BF16)
    ffn0 = _ffn_weights(ffn_w_up[0], ffn_conv_w[0], ffn_conv_b[0], ffn_w_down[0])
    o_mla = _mla_attention(qm, km, vm, ATTN_Q_TILE, 0, seq, 0, rows)
    o_diff = _diff_attention(dq, dk, dv, *lam_args, ATTN_Q_TILE, 0, seq, 0, rows)
    x1, h2 = _merge(o_mla, o_diff, x, mod[0], w_out0, g[0, 1], g[0, 2], ROW_TILE, 0, False)
    x_lat = _ffn(h2, x1, mod[0], *ffn0, g[0, 3], FFN_ROW_TILE, 0, seq, False)
    o_mla = _mla_attention(qm, km, vm, n_ctx, seq, n_ctx, seq, n_ctx)
    o_diff = _diff_attention(dq, dk, dv, *lam_args, n_ctx, seq, n_ctx, seq, n_ctx)
    x1, h2 = _merge(o_mla, o_diff, ctx, mod[0], w_out0, g[0, 1], g[0, 2], n_ctx, 0, True)
    x_ctx = _ffn(h2, x1, mod[0], *ffn0, g[0, 3], n_ctx, 0, n_ctx, True)

    wqkv, wkv, wu = _odd_weights(odd_w_in[0])
    cw_hy = jnp.concatenate([hy_conv_w[0], hy_conv_b[0][None, :]], axis=0)
    q, kd, vd, u = _proj_odd(x_lat, mod[1], g[1, 0], wqkv, wu, cw_hy, cd, sd, PROJ_ROW_TILE)
    kc, vc = _proj_ctx_kv(x_ctx, mod[1], g[1, 0], wkv, 0, n_ctx)
    o_win = _window_attention(q, kd, vd, kc, vc, win_sink[0])

    feats, decay = _hyena_features(seq)
    emb = feats.shape[1]
    feats = _pad_cols(feats, 0, LANES - emb)
    w1 = jnp.pad(hy_f_w1[0], ((0, LANES - emb), (0, 0)))
    hcat = _hyena_filters(feats, w1, hy_f_b1[0][None, :], hy_f_w2[0], hy_f_b2[0][None, :], hy_f_w3[0],
                          hy_f_b3[0][None, :], hy_f_freq[0][None, :], decay)
    f3, g3 = _dft_matrices(seq)
    hre, him = _filter_spectrum(f3, hcat)
    z = _long_conv(u, 0, u, 1, f3, g3, hre, him, 0, hy_bias[0, 0][None, :], F32)
    z = _long_conv(z, 0, u, 2, f3, g3, hre, him, 1, hy_bias[0, 1][None, :], BF16)

    w_out1 = mix_w_out[1].astype(BF16)
    x1, h2 = _merge(o_win, z, x_lat, mod[1], w_out1, g[1, 1], g[1, 2], ROW_TILE, 0, False)
    ffn1 = _ffn_weights(ffn_w_up[1], ffn_conv_w[1], ffn_conv_b[1], ffn_w_down[1])
    return _ffn(h2, x1, mod[1], *ffn1, g[1, 3], FFN_ROW_TILE, 0, seq, False)
```

```python
import functools
import math

import numpy as np
import jax
import jax.numpy as jnp
from jax import lax
from jax.experimental import pallas as pl
from jax.experimental.pallas import tpu as pltpu

D_MODEL = 1024
GRID_W = 64
MLA_HEADS = 8
MLA_NOPE = 64
MLA_ROPE = 32
MLA_V = 64
MLA_Q_RANK = 384
MLA_KV_RANK = 256
DIFF_HEADS = 4
DIFF_HEAD_DIM = 64
DIFF_V_DIM = 128
WIN_Q_HEADS = 8
WIN_KV_HEADS = 2
WIN_HEAD_DIM = 64
WINDOW = 128
HYENA_CH = 512
HYENA_ORDER = 2
HYENA_BANDS = 16
HYENA_DECAY_TARGET = 1e-2
HYENA_FAST_PCT = 0.3
HYENA_SLOW_PCT = 1.5
D_FF = 2816
ROPE_BASE = 10000.0
NORM_EPS = 1e-6
NEG_INF = -1e30

LANES = 128
HALO_ROWS = 16
ROW_TILE = 256
ATTN_Q_TILE = 1024
ATTN_SUB_TILE = 256
ATTN_KEY_CHUNK = 256
LOG2E = math.log2(math.e)
FFN_ROW_TILE = 512
PROJ_ROW_TILE = 512
FFN_CHUNK = 256
FFN_DOWN_GROUP = 4
FREQ_TILE = 256
CONV_SAMPLES = 2
VMEM_LIMIT = 48 * 1024 * 1024

BF16 = jnp.bfloat16
F32 = jnp.float32


def _params(*semantics):
    return pltpu.CompilerParams(dimension_semantics=semantics, vmem_limit_bytes=VMEM_LIMIT)


def _rms(x, g):
    return x * lax.rsqrt(jnp.mean(x * x, axis=-1, keepdims=True) + NORM_EPS) * g


def _dot(a, b):
    return jnp.dot(a, b, preferred_element_type=F32)


def _dot_nt(a, b):
    return lax.dot_general(a, b, (((1,), (1,)), ((), ())), preferred_element_type=F32)


def _lane_iota(shape):
    return lax.broadcasted_iota(jnp.int32, shape, len(shape) - 1)


def _rope_chunk(x, cos, sin_signed, half):
    first = (_lane_iota(x.shape) % (2 * half)) < half
    partner = jnp.where(first, pltpu.roll(x, LANES - half, 1), pltpu.roll(x, half, 1))
    return x * cos + partner * sin_signed


def _shift_rows(u, prev_row, next_row):
    rows = u.shape[0]
    r = lax.broadcasted_iota(jnp.int32, u.shape, 0)
    down = jnp.where(r == 0, prev_row, pltpu.roll(u, 1, 0))
    up = jnp.where(r == rows - 1, next_row, pltpu.roll(u, rows - 1, 0))
    return down, up


def _ada_kernel(c_ref, w_ref, b_ref, o_ref):
    c = c_ref[...]
    sc = c * jax.nn.sigmoid(c)
    o_ref[0] = _dot(sc.astype(BF16), w_ref[0].astype(BF16)) + b_ref[0]


def _ada_table(cond, ada_w, ada_b):
    depth, d, n = ada_w.shape
    rows = cond.shape[0]
    tn = 1536
    return pl.pallas_call(
        _ada_kernel,
        grid=(depth, n // tn),
        in_specs=[
            pl.BlockSpec((rows, d), lambda l, j: (0, 0)),
            pl.BlockSpec((1, d, tn), lambda l, j: (l, 0, j)),
            pl.BlockSpec((1, 1, tn), lambda l, j: (l, 0, j)),
        ],
        out_specs=pl.BlockSpec((1, rows, tn), lambda l, j: (l, 0, j)),
        out_shape=jax.ShapeDtypeStruct((depth, rows, n), F32),
        compiler_params=_params("arbitrary", "arbitrary"),
        name="ada_table",
    )(cond, ada_w, ada_b.reshape(depth, 1, n))


def _proj_even_kernel(x_ref, mod_ref, g_ref, win_ref, qg_ref, wuq_ref, kvg_ref, wukv_ref,
                      cm_ref, sm_ref, cd_ref, sd_ref, *rest):
    qm_ref, km_ref, vm_ref, dq_ref, dk_ref, dv_ref = rest[-6:]
    x = x_ref[0]
    mod = mod_ref[0]
    h = _rms(x, g_ref[...]) * (1.0 + mod[1:2]) + mod[0:1]
    p = _dot(h.astype(BF16), win_ref[...])
    o = 0
    cq = p[:, o:o + MLA_Q_RANK]; o += MLA_Q_RANK
    ckv = p[:, o:o + MLA_KV_RANK]; o += MLA_KV_RANK
    kr = p[:, o:o + LANES]; o += LANES
    dq = p[:, o:o + 512]; o += 512
    dk = p[:, o:o + 512]; o += 512
    dv = p[:, o:o + 512]
    q = _dot(_rms(cq, qg_ref[...]).astype(BF16), wuq_ref[...])
    kv = _dot(_rms(ckv, kvg_ref[...]).astype(BF16), wukv_ref[...])
    cm, sm, cd, sd = cm_ref[...], sm_ref[...], cd_ref[...], sd_ref[...]
    mla_scale = (MLA_NOPE + MLA_ROPE) ** -0.5 * LOG2E
    diff_scale = DIFF_HEAD_DIM ** -0.5 * LOG2E
    kr = _rope_chunk(kr, cm, sm, MLA_ROPE // 2)
    for hd in range(MLA_HEADS):
        c = slice(hd * LANES, (hd + 1) * LANES)
        qm_ref[0, c, :] = (_rope_chunk(q[:, c], cm, sm, MLA_ROPE // 2) * mla_scale).astype(BF16).T
        km_ref[0, :, c] = (kv[:, c] + kr).astype(BF16)
        vm_ref[0, c, :] = kv[:, MLA_HEADS * LANES + hd * LANES:MLA_HEADS * LANES + (hd + 1) * LANES].astype(BF16).T
    for hd in range(DIFF_HEADS):
        c = slice(hd * LANES, (hd + 1) * LANES)
        dq_ref[0, c, :] = (_rope_chunk(dq[:, c], cd, sd, DIFF_HEAD_DIM // 2) * diff_scale).astype(BF16).T
        dk_ref[0, :, c] = _rope_chunk(dk[:, c], cd, sd, DIFF_HEAD_DIM // 2).astype(BF16)
        dv_ref[0, c, :] = dv[:, c].astype(BF16).T


def _proj_even(x, mod, weights, tables, rows, tm, row_off, ctx_mod, prior=None):
    b, n, d = x.shape
    off = row_off // tm
    ctx_row = mod.shape[0] - 1
    const = lambda a: pl.BlockSpec(a.shape, lambda bi, i: (0,) * a.ndim)
    tab = pl.BlockSpec((tm, LANES), lambda bi, i: (off + i, 0))
    out = lambda w: pl.BlockSpec((1, tm, w), lambda bi, i: (bi, off + i, 0))
    shp = lambda w: jax.ShapeDtypeStruct((b, rows, w), BF16)
    out_t = lambda w: pl.BlockSpec((1, w, tm), lambda bi, i: (bi, 0, off + i))
    shp_t = lambda w: jax.ShapeDtypeStruct((b, w, rows), BF16)
    n_in = 2 + len(weights) + len(tables)
    prior = () if prior is None else tuple(prior)
    return pl.pallas_call(
        _proj_even_kernel,
        grid=(b, n // tm),
        in_specs=[
            pl.BlockSpec((1, tm, d), lambda bi, i: (bi, i, 0)),
            pl.BlockSpec((1, 6, d), lambda bi, i: (ctx_row if ctx_mod else bi, 0, 0)),
            *[const(w) for w in weights], *[tab for _ in tables],
            *[pl.BlockSpec(memory_space=pl.ANY) for _ in prior],
        ],
        out_specs=[out_t(1024), out(1024), out_t(1024), out_t(512), out(512), out_t(512)],
        out_shape=[shp_t(1024), shp(1024), shp_t(1024), shp_t(512), shp(512), shp_t(512)],
        input_output_aliases={n_in + k: k for k in range(len(prior))},
        compiler_params=_params("parallel", "arbitrary"),
        name="proj_even",
    )(x, mod, *weights, *tables, *prior)


def _attend_t(qt, k, vt):
    return _softmax_pv_t(_logits_t(qt, k), vt)


def _logits_t(qt, k):
    nk = k.shape[0]
    step = min(nk, ATTN_KEY_CHUNK)
    return [_dot(k[a:a + step], qt) for a in range(0, nk, step)]


def _softmax_pv_t(s, vt):
    step = s[0].shape[0]
    m = functools.reduce(jnp.maximum, [jnp.max(si, axis=0, keepdims=True) for si in s])
    p = [jnp.exp2(si - m) for si in s]
    l = sum(jnp.sum(pi, axis=0, keepdims=True) for pi in p)
    o = sum(_dot(vt[:, j * step:(j + 1) * step], pi.astype(BF16)) for j, pi in enumerate(p))
    return o / l


def _mla_kernel(qt_ref, k_ref, vt_ref, o_ref):
    sub = min(qt_ref.shape[2], ATTN_SUB_TILE)
    items = [(r, hd) for r in range(qt_ref.shape[2] // sub) for hd in range(2)]

    def logits(item):
        r, hd = item
        c = slice(hd * LANES, (hd + 1) * LANES)
        return _logits_t(qt_ref[0, c, r * sub:(r + 1) * sub], k_ref[0, :, c])

    s_next = logits(items[0])
    acc = None
    for j, (r, hd) in enumerate(items):
        s = s_next
        if j + 1 < len(items):
            s_next = logits(items[j + 1])
        o = _softmax_pv_t(s, vt_ref[0, hd * LANES:(hd + 1) * LANES, :])
        acc = o if hd == 0 else acc + o
        if hd == 1:
            o_ref[0, r * sub:(r + 1) * sub, :] = acc.T.astype(o_ref.dtype)


def _mla_attention(qmt, km, vmt, tq, q_off, n_q, k_off, n_k):
    b = km.shape[0]
    pairs = MLA_HEADS // 2
    qo, ko = q_off // tq, k_off // n_k
    return pl.pallas_call(
        _mla_kernel,
        grid=(b, pairs, n_q // tq),
        in_specs=[
            pl.BlockSpec((1, 2 * LANES, tq), lambda bi, p, i: (bi, p, qo + i)),
            pl.BlockSpec((1, n_k, 2 * LANES), lambda bi, p, i: (bi, ko, p)),
            pl.BlockSpec((1, 2 * LANES, n_k), lambda bi, p, i: (bi, p, ko)),
        ],
        out_specs=pl.BlockSpec((1, tq, LANES), lambda bi, p, i: (bi, i, p)),
        out_shape=jax.ShapeDtypeStruct((b, n_q, pairs * LANES), BF16),
        compiler_params=_params("parallel", "arbitrary", "arbitrary"),
        name="mla_attention",
    )(qmt, km, vmt)


def _diff_kernel(qt_ref, k_ref, vt_ref, lam_ref, g_ref, o_ref, *, lambda_init):
    lv = lam_ref[...]
    lam = (jnp.exp(jnp.sum(lv[0:1] * lv[1:2], axis=-1, keepdims=True))
           - jnp.exp(jnp.sum(lv[2:3] * lv[3:4], axis=-1, keepdims=True)) + lambda_init)
    vt = vt_ref[0]
    sub = min(qt_ref.shape[2], ATTN_SUB_TILE)
    items = [(r, which) for r in range(qt_ref.shape[2] // sub) for which in range(2)]

    def logits(item):
        r, which = item
        qt = qt_ref[0, :, r * sub:(r + 1) * sub]
        first = lax.broadcasted_iota(jnp.int32, qt.shape, 0) < DIFF_HEAD_DIM
        zero = jnp.zeros_like(qt)
        return _logits_t(jnp.where(first, qt, zero) if which == 0 else jnp.where(first, zero, qt), k_ref[0])

    s_next = logits(items[0])
    o1 = None
    for j, (r, which) in enumerate(items):
        s = s_next
        if j + 1 < len(items):
            s_next = logits(items[j + 1])
        o = _softmax_pv_t(s, vt)
        if which == 0:
            o1 = o
        else:
            o = (o1 - lam * o).T
            o_ref[0, r * sub:(r + 1) * sub, :] = (_rms(o, g_ref[...]) * (1.0 - lambda_init)).astype(o_ref.dtype)


def _diff_attention(dq, dk, dv, diff_lambda, subln_g, lambda_init, tq, q_off, n_q, k_off, n_k):
    b = dq.shape[0]
    qo, ko = q_off // tq, k_off // n_k
    return pl.pallas_call(
        functools.partial(_diff_kernel, lambda_init=lambda_init),
        grid=(b, DIFF_HEADS, n_q // tq),
        in_specs=[
            pl.BlockSpec((1, LANES, tq), lambda bi, h, i: (bi, h, qo + i)),
            pl.BlockSpec((1, n_k, LANES), lambda bi, h, i: (bi, ko, h)),
            pl.BlockSpec((1, LANES, n_k), lambda bi, h, i: (bi, h, ko)),
            pl.BlockSpec((4, DIFF_HEAD_DIM), lambda bi, h, i: (0, 0)),
            pl.BlockSpec((1, DIFF_V_DIM), lambda bi, h, i: (0, 0)),
        ],
        out_specs=pl.BlockSpec((1, tq, LANES), lambda bi, h, i: (bi, i, h)),
        out_shape=jax.ShapeDtypeStruct((b, n_q, DIFF_HEADS * LANES), BF16),
        compiler_params=_params("parallel", "arbitrary", "arbitrary"),
        name="diff_attention",
    )(dq, dk, dv, diff_lambda, subln_g)


def _merge_kernel(a_ref, b_ref, x_ref, mod_ref, w_ref, g1_ref, g2_ref, x1_ref, h2_ref):
    half = a_ref.shape[-1]
    mod = mod_ref[0]
    y = _dot(a_ref[0], w_ref[0:half, :]) + _dot(b_ref[0], w_ref[half:, :])
    x1 = x_ref[0] + mod[2:3] * _rms(y, g1_ref[...])
    x1_ref[0] = x1
    h2_ref[0] = (_rms(x1, g2_ref[...]) * (1.0 + mod[4:5]) + mod[3:4]).astype(BF16)


def _merge(a, bb, x, mod, w_out, g1, g2, tm, x_off, ctx_mod):
    b, rows, half = a.shape
    d = x.shape[-1]
    nt = rows // tm
    xo = x_off // tm
    ctx_row = mod.shape[0] - 1
    tile = lambda w: pl.BlockSpec((1, tm, w), lambda bi, i: (bi, i, 0))
    const = lambda *shape: pl.BlockSpec(shape, lambda bi, i: (0,) * len(shape))
    return pl.pallas_call(
        _merge_kernel,
        grid=(b, nt),
        in_specs=[
            tile(half), tile(half), pl.BlockSpec((1, tm, d), lambda bi, i: (bi, xo + i, 0)),
            pl.BlockSpec((1, 6, d), lambda bi, i: (ctx_row if ctx_mod else bi, 0, 0)),
            const(*w_out.shape), const(1, d), const(1, d),
        ],
        out_specs=[tile(d), tile(d)],
        out_shape=[jax.ShapeDtypeStruct((b, rows, d), F32), jax.ShapeDtypeStruct((b, rows, d), BF16)],
        compiler_params=_params("parallel", "arbitrary"),
        name="merge",
    )(a, bb, x, mod, w_out, g1, g2)


def _ffn_kernel(h_ref, hp_ref, hn_ref, x_ref, mod_ref, wa_ref, wg_ref, cwa_ref, cwg_ref, wd_ref, g_ref,
                o_ref, lhs_ref, ua0_ref, ug0_ref, ua1_ref, ug1_ref, act0_ref, act1_ref, acc_ref):
    i = pl.program_id(1)
    tm = h_ref.shape[1]
    n_chunks = wa_ref.shape[0]
    hp = hp_ref[0]
    hn = hn_ref[0]
    lhs_ref[0:HALO_ROWS] = jnp.where(i == 0, jnp.zeros_like(hp), hp)
    lhs_ref[HALO_ROWS:HALO_ROWS + tm] = h_ref[0]
    lhs_ref[HALO_ROWS + tm:] = jnp.where(i == pl.num_programs(1) - 1, jnp.zeros_like(hn), hn)
    rows = tm + 2 * HALO_ROWS
    mid = slice(HALO_ROWS, HALO_ROWS + tm)
    slots = ((ua0_ref, ug0_ref), (ua1_ref, ug1_ref))

    def up(j, slot):
        lhs = lhs_ref[...]
        slots[slot][0][...] = _dot(lhs, wa_ref[j])
        slots[slot][1][...] = _dot(lhs, wg_ref[j])

    def conv(u, cw):
        down = pltpu.roll(u, 1, 0)[mid]
        above = pltpu.roll(u, rows - 1, 0)[mid]
        return down * cw[0:1] + u[mid] * cw[1:2] + above * cw[2:3] + cw[3:4]

    acts = (act0_ref, act1_ref)
    width = wa_ref.shape[2]

    def activate(j):
        a = conv(slots[j % 2][0][...], cwa_ref[j])
        g = conv(slots[j % 2][1][...], cwg_ref[j])
        k = j % FFN_DOWN_GROUP
        acts[(j // FFN_DOWN_GROUP) % 2][:, k * width:(k + 1) * width] = ((g * jax.nn.sigmoid(g)) * a).astype(BF16)

    def down(j_last):
        j0 = j_last - j_last % FFN_DOWN_GROUP
        k = (j_last - j0 + 1) * width
        return _dot(acts[(j_last // FFN_DOWN_GROUP) % 2][:, :k], wd_ref[j0 * width:(j_last + 1) * width, :])

    up(0, 0)
    up(1, 1)
    y = None
    for j in range(n_chunks):
        activate(j)
        if j + 2 < n_chunks:
            up(j + 2, j % 2)
        if j % FFN_DOWN_GROUP == FFN_DOWN_GROUP - 1 or j == n_chunks - 1:
            y = down(j)
            if j == n_chunks - 1:
                break
            if j < FFN_DOWN_GROUP:
                acc_ref[...] = y
            else:
                acc_ref[...] += y
    total = y if n_chunks <= FFN_DOWN_GROUP else acc_ref[...] + y
    mod = mod_ref[0]
    o_ref[0] = x_ref[0] + mod[5:6] * _rms(total, g_ref[...])


def _ffn(h2, x1, mod, wa, wg, cwa, cwg, wd, g3, tm, row_offset, n_rows, ctx_mod):
    b, total_rows, d = h2.shape
    ctx_row = mod.shape[0] - 1
    nt = n_rows // tm
    off = row_offset // tm
    per = tm // HALO_ROWS
    last_halo = total_rows // HALO_ROWS - 1
    const = lambda *shape: pl.BlockSpec(shape, lambda bi, i: (0,) * len(shape), pipeline_mode=pl.Buffered(1))
    tile = pl.BlockSpec((1, tm, d), lambda bi, i: (bi, off + i, 0))
    return pl.pallas_call(
        _ffn_kernel,
        grid=(b, nt),
        in_specs=[
            tile,
            pl.BlockSpec((1, HALO_ROWS, d), lambda bi, i: (bi, jnp.maximum((off + i) * per - 1, 0), 0)),
            pl.BlockSpec((1, HALO_ROWS, d), lambda bi, i: (bi, jnp.minimum((off + i + 1) * per, last_halo), 0)),
            tile,
            pl.BlockSpec((1, 6, d), lambda bi, i: (ctx_row if ctx_mod else bi, 0, 0)),
            const(*wa.shape), const(*wg.shape), const(*cwa.shape), const(*cwg.shape), const(*wd.shape),
            const(1, d),
        ],
        out_specs=pl.BlockSpec((1, tm, d), lambda bi, i: (bi, i, 0)),
        out_shape=jax.ShapeDtypeStruct((b, n_rows, d), F32),
        scratch_shapes=[pltpu.VMEM((tm + 2 * HALO_ROWS, d), BF16)]
        + [pltpu.VMEM((tm + 2 * HALO_ROWS, FFN_CHUNK), F32)] * 4
        + [pltpu.VMEM((tm, FFN_DOWN_GROUP * FFN_CHUNK), BF16)] * 2 + [pltpu.VMEM((tm, d), F32)],
        compiler_params=_params("parallel", "arbitrary"),
        name="conv_ffn",
    )(h2, h2, h2, x1, mod, wa, wg, cwa, cwg, wd, g3)


def _proj_odd_kernel(x_ref, xp_ref, xn_ref, mod_ref, g_ref, wqkv_ref, wu_ref, cw_ref, cd_ref, sd_ref,
                     q_ref, k_ref, v_ref, u_ref, *, n_tiles):
    i = pl.program_id(1)
    mod = mod_ref[0]

    def normed(x):
        return (_rms(x, g_ref[...]) * (1.0 + mod[1:2]) + mod[0:1]).astype(BF16)

    h = normed(x_ref[0])
    p = _dot(h, wqkv_ref[...])
    cd, sd = cd_ref[...], sd_ref[...]
    scale = WIN_HEAD_DIM ** -0.5 * LOG2E
    for c in range(4):
        cs = slice(c * LANES, (c + 1) * LANES)
        q_ref[0, :, cs] = (_rope_chunk(p[:, cs], cd, sd, WIN_HEAD_DIM // 2) * scale).astype(BF16)
    for c in range(2):
        cs = slice(512 + c * LANES, 512 + (c + 1) * LANES)
        k_ref[0, c * LANES:(c + 1) * LANES, :] = _rope_chunk(p[:, cs], cd, sd, WIN_HEAD_DIM // 2).astype(BF16).T
    v_ref[0] = p[:, 768:1024].astype(BF16)
    prev_ok = jnp.where(i == 0, 0.0, 1.0)
    next_ok = jnp.where(i == n_tiles - 1, 0.0, 1.0)
    wu = wu_ref[...]
    u = _dot(h, wu)
    up = _dot(normed(xp_ref[0]), wu)[HALO_ROWS - 1:HALO_ROWS] * prev_ok
    un = _dot(normed(xn_ref[0]), wu)[0:1] * next_ok
    down, upw = _shift_rows(u, up, un)
    cw = cw_ref[...]
    u_ref[0] = (down * cw[0:1] + u * cw[1:2] + upw * cw[2:3] + cw[3:4]).astype(u_ref.dtype)


def _proj_odd(xcat, mod, g, wqkv, wu, cw, cd, sd, tm):
    b, rows, d = xcat.shape
    n_tiles = rows // tm
    per = tm // HALO_ROWS
    last_halo = n_tiles * per - 1
    const = lambda *shape: pl.BlockSpec(shape, lambda bi, i: (0,) * len(shape))
    tab = pl.BlockSpec((tm, LANES), lambda bi, i: (i, 0))
    out = lambda w: pl.BlockSpec((1, tm, w), lambda bi, i: (bi, i, 0))
    return pl.pallas_call(
        functools.partial(_proj_odd_kernel, n_tiles=n_tiles),
        grid=(b, n_tiles),
        in_specs=[
            pl.BlockSpec((1, tm, d), lambda bi, i: (bi, i, 0)),
            pl.BlockSpec((1, HALO_ROWS, d), lambda bi, i: (bi, jnp.maximum(i * per - 1, 0), 0)),
            pl.BlockSpec((1, HALO_ROWS, d), lambda bi, i: (bi, jnp.minimum((i + 1) * per, last_halo), 0)),
            pl.BlockSpec((1, 6, d), lambda bi, i: (bi, 0, 0)),
            const(1, d), const(*wqkv.shape), const(*wu.shape), const(*cw.shape), tab, tab,
        ],
        out_specs=[out(512), pl.BlockSpec((1, 2 * LANES, tm), lambda bi, i: (bi, 0, i)), out(256),
                   out(3 * HYENA_CH)],
        out_shape=[jax.ShapeDtypeStruct((b, rows, 512), BF16), jax.ShapeDtypeStruct((b, 2 * LANES, rows), BF16),
                   jax.ShapeDtypeStruct((b, rows, 256), BF16), jax.ShapeDtypeStruct((b, rows, 3 * HYENA_CH), BF16)],
        compiler_params=_params("parallel", "arbitrary"),
        name="proj_odd",
    )(xcat, xcat, xcat, mod, g, wqkv, wu, cw, cd, sd)


def _proj_ctx_kv_kernel(x_ref, mod_ref, g_ref, w_ref, k_ref, v_ref):
    mod = mod_ref[0]
    h = (_rms(x_ref[0], g_ref[...]) * (1.0 + mod[1:2]) + mod[0:1]).astype(BF16)
    p = _dot(h, w_ref[...])
    k_ref[0] = p[:, :2 * LANES].astype(BF16).T
    v_ref[0] = p[:, 2 * LANES:].astype(BF16)


def _proj_ctx_kv(xcat, mod, g, wkv, ctx_tile, n_ctx):
    b, _, d = xcat.shape
    ctx_row = mod.shape[0] - 1
    out = pl.BlockSpec((1, n_ctx, 2 * LANES), lambda bi: (bi, 0, 0))
    return pl.pallas_call(
        _proj_ctx_kv_kernel,
        grid=(b,),
        in_specs=[
            pl.BlockSpec((1, n_ctx, d), lambda bi: (bi, ctx_tile, 0)),
            pl.BlockSpec((1, 6, d), lambda bi: (ctx_row, 0, 0)),
            pl.BlockSpec((1, d), lambda bi: (0, 0)),
            pl.BlockSpec(wkv.shape, lambda bi: (0, 0)),
        ],
        out_specs=[pl.BlockSpec((1, 2 * LANES, n_ctx), lambda bi: (bi, 0, 0)), out],
        out_shape=[jax.ShapeDtypeStruct((b, 2 * LANES, n_ctx), BF16), jax.ShapeDtypeStruct((b, n_ctx, 2 * LANES), BF16)],
        compiler_params=_params("parallel"),
        name="proj_ctx_kv",
    )(xcat, mod, g, wkv)


def _window_kernel(sink_ref, q_ref, kc_ref, vc_ref, kp_ref, km_ref, kn_ref, vp_ref, vm_ref, vn_ref, o_ref,
                   *, n_lat):
    i = pl.program_id(1)
    tq = q_ref.shape[1]
    n_ctx = vc_ref.shape[1]
    nk = n_ctx + tq + 2 * WINDOW
    col = lax.broadcasted_iota(jnp.int32, (tq, nk), 1)
    row = lax.broadcasted_iota(jnp.int32, (tq, nk), 0)
    k_pos = i * tq - WINDOW + (col - n_ctx)
    q_pos = i * tq + row
    visible = (col < n_ctx) | ((jnp.abs(q_pos - k_pos) <= WINDOW) & (k_pos >= 0) & (k_pos < n_lat))
    bias = jnp.where(visible, 0.0, NEG_INF)
    group = WIN_Q_HEADS // WIN_KV_HEADS
    bias = jnp.concatenate([bias, bias], axis=0)
    second_head = lax.broadcasted_iota(jnp.int32, (2 * tq, 1), 0) >= tq
    first = _lane_iota((tq, LANES)) < WIN_HEAD_DIM
    kts, vs = [], []
    for hk in range(WIN_KV_HEADS):
        c = slice(hk * LANES, (hk + 1) * LANES)
        kts.append(jnp.concatenate([kc_ref[0, c, :], kp_ref[0, c, :], km_ref[0, c, :], kn_ref[0, c, :]], axis=1))
        vs.append(jnp.concatenate([vc_ref[0, :, c], vp_ref[0, :, c], vm_ref[0, :, c], vn_ref[0, :, c]], axis=0))

    def logits(chunk):
        q = q_ref[0, :, chunk * LANES:(chunk + 1) * LANES]
        zero = jnp.zeros_like(q)
        stacked = jnp.concatenate([jnp.where(first, q, zero), jnp.where(first, zero, q)], axis=0)
        return _dot(stacked, kts[chunk // (group // 2)])

    n_chunks = WIN_Q_HEADS // 2
    s_next = logits(0)
    for chunk in range(n_chunks):
        s = s_next + bias
        if chunk + 1 < n_chunks:
            s_next = logits(chunk + 1)
        sink = jnp.where(second_head, sink_ref[2 * chunk + 1], sink_ref[2 * chunk]) * LOG2E
        m = jnp.maximum(jnp.max(s, axis=-1, keepdims=True), sink)
        p = jnp.exp2(s - m)
        l = jnp.sum(p, axis=-1, keepdims=True) + jnp.exp2(sink - m)
        o = _dot(p.astype(BF16), vs[chunk // (group // 2)]) / l
        o_ref[0, :, chunk * LANES:(chunk + 1) * LANES] = jnp.where(first, o[:tq], o[tq:]).astype(o_ref.dtype)


def _window_attention(q, kdt, vd, kct, vc, sink):
    b, n_lat, _ = q.shape
    tq = ROW_TILE
    n_ctx = vc.shape[1]
    per = tq // WINDOW
    last = n_lat // WINDOW - 1
    prev_i = lambda i: jnp.maximum(i * per - 1, 0)
    next_i = lambda i: jnp.minimum((i + 1) * per, last)
    w = 2 * LANES
    return pl.pallas_call(
        functools.partial(_window_kernel, n_lat=n_lat),
        grid=(b, n_lat // tq),
        in_specs=[
            pl.BlockSpec(memory_space=pltpu.SMEM),
            pl.BlockSpec((1, tq, 4 * LANES), lambda bi, i: (bi, i, 0)),
            pl.BlockSpec((1, w, n_ctx), lambda bi, i: (bi, 0, 0)),
            pl.BlockSpec((1, n_ctx, w), lambda bi, i: (bi, 0, 0)),
            pl.BlockSpec((1, w, WINDOW), lambda bi, i: (bi, 0, prev_i(i))),
            pl.BlockSpec((1, w, tq), lambda bi, i: (bi, 0, i)),
            pl.BlockSpec((1, w, WINDOW), lambda bi, i: (bi, 0, next_i(i))),
            pl.BlockSpec((1, WINDOW, w), lambda bi, i: (bi, prev_i(i), 0)),
            pl.BlockSpec((1, tq, w), lambda bi, i: (bi, i, 0)),
            pl.BlockSpec((1, WINDOW, w), lambda bi, i: (bi, next_i(i), 0)),
        ],
        out_specs=pl.BlockSpec((1, tq, 4 * LANES), lambda bi, i: (bi, i, 0)),
        out_shape=jax.ShapeDtypeStruct((b, n_lat, 4 * LANES), BF16),
        compiler_params=_params("parallel", "arbitrary"),
        name="window_attention",
    )(sink, q, kct, vc, kdt, kdt, kdt, vd, vd, vd)


def _hyena_filter_kernel(feat_ref, w1_ref, b1_ref, w2_ref, b2_ref, w3_ref, b3_ref, freq_ref, decay_ref, h_ref,
                         *, n_forward):
    j = pl.program_id(0)
    hp = lax.Precision.HIGHEST
    freq = freq_ref[...]
    hid = jnp.sin(freq * (jnp.dot(feat_ref[...], w1_ref[...], precision=hp, preferred_element_type=F32) + b1_ref[...]))
    hid = jnp.sin(freq * (jnp.dot(hid, w2_ref[...], precision=hp, preferred_element_type=F32) + b2_ref[...]))
    h = jnp.dot(hid, w3_ref[...], precision=hp, preferred_element_type=F32) + b3_ref[...]
    h = h * decay_ref[...]
    row = lax.broadcasted_iota(jnp.int32, h.shape, 0)
    h_ref[...] = jnp.where((row == 0) & (j >= n_forward), 0.0, h).astype(h_ref.dtype)


def _hyena_filters(feats, w1, b1, w2, b2, w3, b3, freq, decay):
    length = feats.shape[0]
    n = w3.shape[1]
    ch = decay.shape[1]
    full = lambda a: pl.BlockSpec(a.shape, lambda j: (0, 0))
    return pl.pallas_call(
        functools.partial(_hyena_filter_kernel, n_forward=HYENA_ORDER),
        grid=(n // ch,),
        in_specs=[full(feats), full(w1), full(b1), full(w2), full(b2),
                  pl.BlockSpec((w3.shape[0], ch), lambda j: (0, j)), pl.BlockSpec((1, ch), lambda j: (0, j)),
                  full(freq), full(decay)],
        out_specs=pl.BlockSpec((length, ch), lambda j: (0, j)),
        out_shape=jax.ShapeDtypeStruct((length, n), BF16),
        compiler_params=_params("arbitrary"),
        name="hyena_filters",
    )(feats, w1, b1, w2, b2, w3, b3, freq, decay)


def _spectrum_kernel(f_ref, h_ref, re_ref, im_ref):
    j = pl.program_id(0)
    tk = re_ref.shape[0]
    n = re_ref.shape[1]
    a = _dot(f_ref[0], h_ref[...])
    re_ref[...] = a[:tk, :n] + a[:tk, n:]
    row = lax.broadcasted_iota(jnp.int32, (tk, n), 0)
    sign = jnp.where((row == 0) & (j == 0), 1.0, -1.0)
    im_ref[...] = a[tk:, :n] + sign * a[tk:, n:]


def _filter_spectrum(f3, hcat):
    nt, tk2, length = f3.shape
    n = hcat.shape[1] // 2
    tk = tk2 // 2
    out = pl.BlockSpec((tk, n), lambda j: (j, 0))
    return pl.pallas_call(
        _spectrum_kernel,
        grid=(nt,),
        in_specs=[pl.BlockSpec((1, tk2, length), lambda j: (j, 0, 0)),
                  pl.BlockSpec(hcat.shape, lambda j: (0, 0))],
        out_specs=[out, out],
        out_shape=[jax.ShapeDtypeStruct((nt * tk, n), F32)] * 2,
        compiler_params=_params("arbitrary"),
        name="filter_spectrum",
    )(f3, hcat)


def _long_conv_kernel(z_ref, x_ref, f_ref, g_ref, hre_ref, him_ref, bias_ref, o_ref, acc_ref):
    j = pl.program_id(1)
    tk = hre_ref.shape[0]
    row = lax.broadcasted_iota(jnp.int32, hre_ref.shape, 0)
    packed = (row == 0) & (j == 0)
    bias = bias_ref[...]
    h_re = hre_ref[...] + bias
    h_im = him_ref[...] + jnp.where(packed, bias, 0.0)

    @pl.when(j == 0)
    def _():
        acc_ref[...] = jnp.zeros_like(acc_ref)

    spectra = [_dot(f_ref[0], z_ref[s]) for s in range(z_ref.shape[0])]
    for s, a in enumerate(spectra):
        a_re, a_im = a[:tk], a[tk:]
        ii = a_im * h_im
        p_re = a_re * h_re - jnp.where(packed, 0.0, ii)
        p_im = jnp.where(packed, ii, a_re * h_im + a_im * h_re)
        acc_ref[s] += _dot(g_ref[0], jnp.concatenate([p_re, p_im], axis=0).astype(BF16))

    @pl.when(j == pl.num_programs(1) - 1)
    def _():
        o_ref[...] = (x_ref[...].astype(F32) * acc_ref[...]).astype(o_ref.dtype)


def _long_conv(z_src, z_col, x_src, x_col, f3, g3, hre, him, h_col, bias):
    b, length, _ = z_src.shape
    nt, tk2, _ = f3.shape
    tk = tk2 // 2
    ch = HYENA_CH
    nb = CONV_SAMPLES
    return pl.pallas_call(
        _long_conv_kernel,
        grid=(b // nb, nt),
        in_specs=[
            pl.BlockSpec((nb, length, ch), lambda bi, j: (bi, 0, z_col)),
            pl.BlockSpec((nb, length, ch), lambda bi, j: (bi, 0, x_col)),
            pl.BlockSpec((1, tk2, length), lambda bi, j: (j, 0, 0)),
            pl.BlockSpec((1, length, tk2), lambda bi, j: (j, 0, 0)),
            pl.BlockSpec((tk, ch), lambda bi, j: (j, h_col)),
            pl.BlockSpec((tk, ch), lambda bi, j: (j, h_col)),
            pl.BlockSpec((1, ch), lambda bi, j: (0, 0)),
        ],
        out_specs=pl.BlockSpec((nb, length, ch), lambda bi, j: (bi, 0, 0)),
        out_shape=jax.ShapeDtypeStruct((b, length, ch), BF16),
        scratch_shapes=[pltpu.VMEM((nb, length, ch), F32)],
        compiler_params=_params("parallel", "arbitrary"),
        name="long_conv",
    )(z_src, x_src, f3, g3, hre, him, bias)


def _axial_rope(length, rot_dim):
    rows = length // GRID_W
    row = jnp.repeat(jnp.arange(rows), GRID_W).astype(F32)
    col = jnp.tile(jnp.arange(GRID_W), rows).astype(F32)
    quarter = rot_dim // 4
    inv = ROPE_BASE ** (-jnp.arange(quarter, dtype=F32) / quarter)
    ang = jnp.concatenate([row[:, None] * inv, col[:, None] * inv], axis=-1)
    return jnp.cos(ang), jnp.sin(ang)


def _rope_tables(length, n_ctx):
    cos_m, sin_m = _axial_rope(length, MLA_ROPE)
    cos_d, sin_d = _axial_rope(length, DIFF_HEAD_DIM)
    ones = lambda w: jnp.ones((length, w), F32)
    zeros = lambda w: jnp.zeros((length, w), F32)
    cm = jnp.concatenate([ones(MLA_NOPE), cos_m, cos_m, ones(LANES - MLA_NOPE - MLA_ROPE)], axis=1)
    sm = jnp.concatenate([zeros(MLA_NOPE), -sin_m, sin_m, zeros(LANES - MLA_NOPE - MLA_ROPE)], axis=1)
    cd = jnp.concatenate([cos_d] * 4, axis=1)
    sd = jnp.concatenate([-sin_d, sin_d] * 2, axis=1)
    pad = lambda t, v: jnp.concatenate([t, jnp.full((n_ctx, LANES), v, F32)], axis=0)
    return pad(cm, 1.0), pad(sm, 0.0), pad(cd, 1.0), pad(sd, 0.0)


def _dft_matrices(length):
    n = 2 * length
    k = np.arange(length, dtype=np.int64)
    ang = 2.0 * np.pi * ((k[:, None] * k[None, :]) % n).astype(np.float64) / n
    alt = np.where(k % 2 == 0, 1.0, -1.0)
    f_re = np.cos(ang)
    f_im = -np.sin(ang)
    f_im[0, :] = alt
    g_re = (2.0 / n) * np.cos(ang.T)
    g_im = -(2.0 / n) * np.sin(ang.T)
    g_re[:, 0] = 1.0 / n
    g_im[:, 0] = alt / n
    nt = length // FREQ_TILE
    f3 = np.concatenate([f_re.reshape(nt, FREQ_TILE, length), f_im.reshape(nt, FREQ_TILE, length)], axis=1)
    g3 = np.concatenate([g_re.reshape(length, nt, FREQ_TILE), g_im.reshape(length, nt, FREQ_TILE)], axis=2)
    g3 = np.transpose(g3, (1, 0, 2))
    return jnp.asarray(f3, dtype=BF16), jnp.asarray(g3, dtype=BF16)


def _hyena_features(length):
    t = jnp.arange(length, dtype=F32)
    tn = t / max(length - 1, 1)
    bands = jnp.linspace(1e-4, HYENA_BANDS - 1, HYENA_BANDS, dtype=F32)
    ang = 2.0 * math.pi * bands[None, :] * t[:, None] / length
    feats = jnp.concatenate([tn[:, None], jnp.cos(ang), -jnp.sin(ang)], axis=-1)
    min_decay = math.log(HYENA_DECAY_TARGET) / HYENA_SLOW_PCT
    max_decay = math.log(HYENA_DECAY_TARGET) / HYENA_FAST_PCT
    deltas = jnp.abs(jnp.linspace(min_decay, max_decay, HYENA_CH, dtype=F32))
    decay = jnp.exp(-tn[:, None] * deltas[None, :])
    return feats, decay


def _pad_cols(w, left, right):
    return jnp.pad(w, ((0, 0),) * (w.ndim - 1) + ((left, right),))


def _even_weights(w_in, w_uq, w_ukv):
    d = w_in.shape[0]
    o = MLA_Q_RANK + MLA_KV_RANK
    w_kr = _pad_cols(w_in[:, o:o + MLA_ROPE], MLA_NOPE, LANES - MLA_NOPE - MLA_ROPE)
    w_in_ext = jnp.concatenate([w_in[:, :o], w_kr, w_in[:, o + MLA_ROPE:]], axis=1)
    uq = _pad_cols(w_uq.reshape(MLA_Q_RANK, MLA_HEADS, MLA_NOPE + MLA_ROPE), 0, LANES - MLA_NOPE - MLA_ROPE)
    ukv = w_ukv.reshape(MLA_KV_RANK, MLA_HEADS, MLA_NOPE + MLA_V)
    uk = _pad_cols(ukv[:, :, :MLA_NOPE], 0, LANES - MLA_NOPE)
    uv = ukv[:, :, MLA_NOPE:].reshape(MLA_KV_RANK, MLA_HEADS // 2, 2, MLA_V)
    uv = jnp.stack([_pad_cols(uv[:, :, 0], 0, LANES - MLA_V), _pad_cols(uv[:, :, 1], LANES - MLA_V, 0)], axis=2)
    w_ukv_ext = jnp.concatenate([uk.reshape(MLA_KV_RANK, -1), uv.reshape(MLA_KV_RANK, -1)], axis=1)
    return w_in_ext.astype(BF16), uq.reshape(MLA_Q_RANK, -1).astype(BF16), w_ukv_ext.astype(BF16)


def _odd_weights(w_in):
    d = w_in.shape[0]
    nq = WIN_Q_HEADS * WIN_HEAD_DIM
    nkv = WIN_KV_HEADS * WIN_HEAD_DIM
    dup = lambda w: jnp.concatenate([w.reshape(d, WIN_KV_HEADS, WIN_HEAD_DIM)] * 2, axis=-1).reshape(d, -1)
    wk = dup(w_in[:, nq:nq + nkv])
    wv = dup(w_in[:, nq + nkv:nq + 2 * nkv])
    wqkv = jnp.concatenate([w_in[:, :nq], wk, wv], axis=1).astype(BF16)
    wkv = jnp.concatenate([wk, wv], axis=1).astype(BF16)
    wu = w_in[:, nq + 2 * nkv:].astype(BF16)
    return wqkv, wkv, wu


def _ffn_weights(w_up, conv_w, conv_b, w_down):
    d = w_up.shape[0]
    nc = D_FF // FFN_CHUNK
    chunks = lambda w: jnp.transpose(w.reshape(w.shape[0], nc, FFN_CHUNK), (1, 0, 2))
    cw = jnp.concatenate([conv_w, conv_b[None, :]], axis=0)
    return (chunks(w_up[:, :D_FF]).astype(BF16), chunks(w_up[:, D_FF:]).astype(BF16),
            chunks(cw[:, :D_FF]), chunks(cw[:, D_FF:]),
            w_down.astype(BF16))


def kernel(x, c, ctx, c_ctx, ada_w, ada_b, norm_g, mix_w_out, ffn_w_up, ffn_conv_w, ffn_conv_b, ffn_w_down, even_w_in, mla_q_norm_g, mla_w_uq, mla_kv_norm_g, mla_w_ukv, diff_lambda, diff_subln_g, odd_w_in, win_sink, hy_conv_w, hy_conv_b, hy_f_w1, hy_f_b1, hy_f_w2, hy_f_b2, hy_f_w3, hy_f_b3, hy_f_freq, hy_bias):
    b, seq, d = x.shape
    n_ctx = ctx.shape[1]
    depth = ada_w.shape[0]
    assert depth == 2 and d == D_MODEL and seq % ROW_TILE == 0 and n_ctx == ROW_TILE
    n_lat_tiles = seq // ROW_TILE
    n_tiles = n_lat_tiles + 1

    pad_rows = (-(b + 1)) % 8
    cond = jnp.concatenate([c, c_ctx[None, :], jnp.zeros((pad_rows, d), F32)], axis=0)
    mod = _ada_table(cond, ada_w, ada_b).reshape(depth, b + 1 + pad_rows, 6, d)[:, :b + 1]
    g = norm_g.reshape(depth, 4, 1, d)
    cm, sm, cd, sd = _rope_tables(seq, n_ctx)

    lambda_init = 0.8 - 0.6 * math.exp(-0.3 * 0)
    w_in_e, w_uq_e, w_ukv_e = _even_weights(even_w_in[0], mla_w_uq[0], mla_w_ukv[0])
    rows = seq + n_ctx
    even_w = (g[0, 0], w_in_e, mla_q_norm_g[0][None, :], w_uq_e, mla_kv_norm_g[0][None, :], w_ukv_e)
    proj = _proj_even(x, mod[0], even_w, (cm, sm, cd, sd), rows, PROJ_ROW_TILE, 0, False)
    qm, km, vm, dq, dk, dv = _proj_even(ctx, mod[0], even_w, (cm, sm, cd, sd), rows, n_ctx, seq, True, prior=proj)
    lam_args = (diff_lambda[0], diff_subln_g[0][None, :], lambda_init)
    w_out0 = mix_w_out[0].astype(BF16)
    ffn0 = _ffn_weights(ffn_w_up[0], ffn_conv_w[0], ffn_conv_b[0], ffn_w_down[0])
    o_mla = _mla_attention(qm, km, vm, ATTN_Q_TILE, 0, seq, 0, rows)
    o_diff = _diff_attention(dq, dk, dv, *lam_args, ATTN_Q_TILE, 0, seq, 0, rows)
    x1, h2 = _merge(o_mla, o_diff, x, mod[0], w_out0, g[0, 1], g[0, 2], ROW_TILE, 0, False)
    x_lat = _ffn(h2, x1, mod[0], *ffn0, g[0, 3], FFN_ROW_TILE, 0, seq, False)
    o_mla = _mla_attention(qm, km, vm, n_ctx, seq, n_ctx, seq, n_ctx)
    o_diff = _diff_attention(dq, dk, dv, *lam_args, n_ctx, seq, n_ctx, seq, n_ctx)
    x1, h2 = _merge(o_mla, o_diff, ctx, mod[0], w_out0, g[0, 1], g[0, 2], n_ctx, 0, True)
    x_ctx = _ffn(h2, x1, mod[0], *ffn0, g[0, 3], n_ctx, 0, n_ctx, True)

    wqkv, wkv, wu = _odd_weights(odd_w_in[0])
    cw_hy = jnp.concatenate([hy_conv_w[0], hy_conv_b[0][None, :]], axis=0)
    q, kd, vd, u = _proj_odd(x_lat, mod[1], g[1, 0], wqkv, wu, cw_hy, cd, sd, PROJ_ROW_TILE)
    kc, vc = _proj_ctx_kv(x_ctx, mod[1], g[1, 0], wkv, 0, n_ctx)
    o_win = _window_attention(q, kd, vd, kc, vc, win_sink[0])

    feats, decay = _hyena_features(seq)
    emb = feats.shape[1]
    feats = _pad_cols(feats, 0, LANES - emb)
    w1 = jnp.pad(hy_f_w1[0], ((0, LANES - emb), (0, 0)))
    hcat = _hyena_filters(feats, w1, hy_f_b1[0][None, :], hy_f_w2[0], hy_f_b2[0][None, :], hy_f_w3[0],
                          hy_f_b3[0][None, :], hy_f_freq[0][None, :], decay)
    f3, g3 = _dft_matrices(seq)
    hre, him = _filter_spectrum(f3, hcat)
    z = _long_conv(u, 0, u, 1, f3, g3, hre, him, 0, hy_bias[0, 0][None, :])
    z = _long_conv(z, 0, u, 2, f3, g3, hre, him, 1, hy_bias[0, 1][None, :])

    w_out1 = mix_w_out[1].astype(BF16)
    x1, h2 = _merge(o_win, z, x_lat, mod[1], w_out1, g[1, 1], g[1, 2], ROW_TILE, 0, False)
    ffn1 = _ffn_weights(ffn_w_up[1], ffn_conv_w[1], ffn_conv_b[1], ffn_w_down[1])
    return _ffn(h2, x1, mod[1], *ffn1, g[1, 3], FFN_ROW_TILE, 0, seq, False)
```

```python
import functools
import math

import numpy as np
import jax
import jax.numpy as jnp
from jax import lax
from jax.experimental import pallas as pl
from jax.experimental.pallas import tpu as pltpu

D_MODEL = 1024
GRID_W = 64
MLA_HEADS = 8
MLA_NOPE = 64
MLA_ROPE = 32
MLA_V = 64
MLA_Q_RANK = 384
MLA_KV_RANK = 256
DIFF_HEADS = 4
DIFF_HEAD_DIM = 64
DIFF_V_DIM = 128
WIN_Q_HEADS = 8
WIN_KV_HEADS = 2
WIN_HEAD_DIM = 64
WINDOW = 128
HYENA_CH = 512
HYENA_ORDER = 2
HYENA_BANDS = 16
HYENA_DECAY_TARGET = 1e-2
HYENA_FAST_PCT = 0.3
HYENA_SLOW_PCT = 1.5
D_FF = 2816
ROPE_BASE = 10000.0
NORM_EPS = 1e-6
NEG_INF = -1e30

LANES = 128
HALO_ROWS = 16
ROW_TILE = 256
ATTN_Q_TILE = 1024
ATTN_SUB_TILE = 256
ATTN_KEY_CHUNK = 256
LOG2E = math.log2(math.e)
FFN_ROW_TILE = 512
PROJ_ROW_TILE = 512
MERGE_ROW_TILE = 1024
FFN_CHUNK = 256
FFN_DOWN_GROUP = 4
FREQ_TILE = 256
CONV_SAMPLES = 2
VMEM_LIMIT = 48 * 1024 * 1024

BF16 = jnp.bfloat16
F32 = jnp.float32


def _params(*semantics):
    return pltpu.CompilerParams(dimension_semantics=semantics, vmem_limit_bytes=VMEM_LIMIT)


def _rms(x, g):
    return x * lax.rsqrt(jnp.mean(x * x, axis=-1, keepdims=True) + NORM_EPS) * g


def _dot(a, b):
    return jnp.dot(a, b, preferred_element_type=F32)


def _dot_nt(a, b):
    return lax.dot_general(a, b, (((1,), (1,)), ((), ())), preferred_element_type=F32)


def _lane_iota(shape):
    return lax.broadcasted_iota(jnp.int32, shape, len(shape) - 1)


def _rope_chunk(x, cos, sin_signed, half):
    first = (_lane_iota(x.shape) % (2 * half)) < half
    partner = jnp.where(first, pltpu.roll(x, LANES - half, 1), pltpu.roll(x, half, 1))
    return x * cos + partner * sin_signed


def _shift_rows(u, prev_row, next_row):
    rows = u.shape[0]
    r = lax.broadcasted_iota(jnp.int32, u.shape, 0)
    down = jnp.where(r == 0, prev_row, pltpu.roll(u, 1, 0))
    up = jnp.where(r == rows - 1, next_row, pltpu.roll(u, rows - 1, 0))
    return down, up


def _ada_kernel(c_ref, w_ref, b_ref, o_ref):
    c = c_ref[...]
    sc = c * jax.nn.sigmoid(c)
    o_ref[0] = _dot(sc.astype(BF16), w_ref[0].astype(BF16)) + b_ref[0]


def _ada_table(cond, ada_w, ada_b):
    depth, d, n = ada_w.shape
    rows = cond.shape[0]
    tn = 1536
    return pl.pallas_call(
        _ada_kernel,
        grid=(depth, n // tn),
        in_specs=[
            pl.BlockSpec((rows, d), lambda l, j: (0, 0)),
            pl.BlockSpec((1, d, tn), lambda l, j: (l, 0, j)),
            pl.BlockSpec((1, 1, tn), lambda l, j: (l, 0, j)),
        ],
        out_specs=pl.BlockSpec((1, rows, tn), lambda l, j: (l, 0, j)),
        out_shape=jax.ShapeDtypeStruct((depth, rows, n), F32),
        compiler_params=_params("arbitrary", "arbitrary"),
        name="ada_table",
    )(cond, ada_w, ada_b.reshape(depth, 1, n))


def _proj_even_kernel(x_ref, mod_ref, g_ref, win_ref, qg_ref, wuq_ref, kvg_ref, wukv_ref,
                      cm_ref, sm_ref, cd_ref, sd_ref, *rest):
    qm_ref, km_ref, vm_ref, dq_ref, dk_ref, dv_ref = rest[-6:]
    x = x_ref[0]
    mod = mod_ref[0]
    h = _rms(x, g_ref[...]) * (1.0 + mod[1:2]) + mod[0:1]
    p = _dot(h.astype(BF16), win_ref[...])
    o = 0
    cq = p[:, o:o + MLA_Q_RANK]; o += MLA_Q_RANK
    ckv = p[:, o:o + MLA_KV_RANK]; o += MLA_KV_RANK
    kr = p[:, o:o + LANES]; o += LANES
    dq = p[:, o:o + 512]; o += 512
    dk = p[:, o:o + 512]; o += 512
    dv = p[:, o:o + 512]
    q = _dot(_rms(cq, qg_ref[...]).astype(BF16), wuq_ref[...])
    kv = _dot(_rms(ckv, kvg_ref[...]).astype(BF16), wukv_ref[...])
    cm, sm, cd, sd = cm_ref[...], sm_ref[...], cd_ref[...], sd_ref[...]
    mla_scale = (MLA_NOPE + MLA_ROPE) ** -0.5 * LOG2E
    diff_scale = DIFF_HEAD_DIM ** -0.5 * LOG2E
    kr = _rope_chunk(kr, cm, sm, MLA_ROPE // 2)
    for hd in range(MLA_HEADS):
        c = slice(hd * LANES, (hd + 1) * LANES)
        qm_ref[0, c, :] = (_rope_chunk(q[:, c], cm, sm, MLA_ROPE // 2) * mla_scale).astype(BF16).T
        km_ref[0, :, c] = (kv[:, c] + kr).astype(BF16)
        vm_ref[0, c, :] = kv[:, MLA_HEADS * LANES + hd * LANES:MLA_HEADS * LANES + (hd + 1) * LANES].astype(BF16).T
    for hd in range(DIFF_HEADS):
        c = slice(hd * LANES, (hd + 1) * LANES)
        dq_ref[0, c, :] = (_rope_chunk(dq[:, c], cd, sd, DIFF_HEAD_DIM // 2) * diff_scale).astype(BF16).T
        dk_ref[0, :, c] = _rope_chunk(dk[:, c], cd, sd, DIFF_HEAD_DIM // 2).astype(BF16)
        dv_ref[0, c, :] = dv[:, c].astype(BF16).T


def _proj_even(x, mod, weights, tables, rows, tm, row_off, ctx_mod, prior=None):
    b, n, d = x.shape
    off = row_off // tm
    ctx_row = mod.shape[0] - 1
    const = lambda a: pl.BlockSpec(a.shape, lambda bi, i: (0,) * a.ndim)
    tab = pl.BlockSpec((tm, LANES), lambda bi, i: (off + i, 0))
    out = lambda w: pl.BlockSpec((1, tm, w), lambda bi, i: (bi, off + i, 0))
    shp = lambda w: jax.ShapeDtypeStruct((b, rows, w), BF16)
    out_t = lambda w: pl.BlockSpec((1, w, tm), lambda bi, i: (bi, 0, off + i))
    shp_t = lambda w: jax.ShapeDtypeStruct((b, w, rows), BF16)
    n_in = 2 + len(weights) + len(tables)
    prior = () if prior is None else tuple(prior)
    return pl.pallas_call(
        _proj_even_kernel,
        grid=(b, n // tm),
        in_specs=[
            pl.BlockSpec((1, tm, d), lambda bi, i: (bi, i, 0)),
            pl.BlockSpec((1, 6, d), lambda bi, i: (ctx_row if ctx_mod else bi, 0, 0)),
            *[const(w) for w in weights], *[tab for _ in tables],
            *[pl.BlockSpec(memory_space=pl.ANY) for _ in prior],
        ],
        out_specs=[out_t(1024), out(1024), out_t(1024), out_t(512), out(512), out_t(512)],
        out_shape=[shp_t(1024), shp(1024), shp_t(1024), shp_t(512), shp(512), shp_t(512)],
        input_output_aliases={n_in + k: k for k in range(len(prior))},
        compiler_params=_params("parallel", "arbitrary"),
        name="proj_even",
    )(x, mod, *weights, *tables, *prior)


def _attend_t(qt, k, vt):
    return _softmax_pv_t(_logits_t(qt, k), vt)


def _logits_t(qt, k):
    nk = k.shape[0]
    step = min(nk, ATTN_KEY_CHUNK)
    return [_dot(k[a:a + step], qt) for a in range(0, nk, step)]


def _softmax_pv_t(s, vt):
    step = s[0].shape[0]
    m = functools.reduce(jnp.maximum, [jnp.max(si, axis=0, keepdims=True) for si in s])
    p = [jnp.exp2(si - m) for si in s]
    l = sum(jnp.sum(pi, axis=0, keepdims=True) for pi in p)
    o = sum(_dot(vt[:, j * step:(j + 1) * step], pi.astype(BF16)) for j, pi in enumerate(p))
    return o / l


def _mla_kernel(qt_ref, k_ref, vt_ref, o_ref):
    sub = min(qt_ref.shape[2], ATTN_SUB_TILE)
    items = [(r, hd) for r in range(qt_ref.shape[2] // sub) for hd in range(2)]

    def logits(item):
        r, hd = item
        c = slice(hd * LANES, (hd + 1) * LANES)
        return _logits_t(qt_ref[0, c, r * sub:(r + 1) * sub], k_ref[0, :, c])

    s_next = logits(items[0])
    acc = None
    for j, (r, hd) in enumerate(items):
        s = s_next
        if j + 1 < len(items):
            s_next = logits(items[j + 1])
        o = _softmax_pv_t(s, vt_ref[0, hd * LANES:(hd + 1) * LANES, :])
        acc = o if hd == 0 else acc + o
        if hd == 1:
            o_ref[0, r * sub:(r + 1) * sub, :] = acc.T.astype(o_ref.dtype)


def _mla_attention(qmt, km, vmt, tq, q_off, n_q, k_off, n_k):
    b = km.shape[0]
    pairs = MLA_HEADS // 2
    qo, ko = q_off // tq, k_off // n_k
    return pl.pallas_call(
        _mla_kernel,
        grid=(b, pairs, n_q // tq),
        in_specs=[
            pl.BlockSpec((1, 2 * LANES, tq), lambda bi, p, i: (bi, p, qo + i)),
            pl.BlockSpec((1, n_k, 2 * LANES), lambda bi, p, i: (bi, ko, p)),
            pl.BlockSpec((1, 2 * LANES, n_k), lambda bi, p, i: (bi, p, ko)),
        ],
        out_specs=pl.BlockSpec((1, tq, LANES), lambda bi, p, i: (bi, i, p)),
        out_shape=jax.ShapeDtypeStruct((b, n_q, pairs * LANES), BF16),
        compiler_params=_params("parallel", "arbitrary", "arbitrary"),
        name="mla_attention",
    )(qmt, km, vmt)


def _diff_kernel(qt_ref, k_ref, vt_ref, lam_ref, g_ref, o_ref, *, lambda_init):
    lv = lam_ref[...]
    lam = (jnp.exp(jnp.sum(lv[0:1] * lv[1:2], axis=-1, keepdims=True))
           - jnp.exp(jnp.sum(lv[2:3] * lv[3:4], axis=-1, keepdims=True)) + lambda_init)
    vt = vt_ref[0]
    sub = min(qt_ref.shape[2], ATTN_SUB_TILE)
    items = [(r, which) for r in range(qt_ref.shape[2] // sub) for which in range(2)]

    def logits(item):
        r, which = item
        qt = qt_ref[0, :, r * sub:(r + 1) * sub]
        first = lax.broadcasted_iota(jnp.int32, qt.shape, 0) < DIFF_HEAD_DIM
        zero = jnp.zeros_like(qt)
        return _logits_t(jnp.where(first, qt, zero) if which == 0 else jnp.where(first, zero, qt), k_ref[0])

    s_next = logits(items[0])
    o1 = None
    for j, (r, which) in enumerate(items):
        s = s_next
        if j + 1 < len(items):
            s_next = logits(items[j + 1])
        o = _softmax_pv_t(s, vt)
        if which == 0:
            o1 = o
        else:
            o = (o1 - lam * o).T
            o_ref[0, r * sub:(r + 1) * sub, :] = (_rms(o, g_ref[...]) * (1.0 - lambda_init)).astype(o_ref.dtype)


def _diff_attention(dq, dk, dv, diff_lambda, subln_g, lambda_init, tq, q_off, n_q, k_off, n_k):
    b = dq.shape[0]
    qo, ko = q_off // tq, k_off // n_k
    return pl.pallas_call(
        functools.partial(_diff_kernel, lambda_init=lambda_init),
        grid=(b, DIFF_HEADS, n_q // tq),
        in_specs=[
            pl.BlockSpec((1, LANES, tq), lambda bi, h, i: (bi, h, qo + i)),
            pl.BlockSpec((1, n_k, LANES), lambda bi, h, i: (bi, ko, h)),
            pl.BlockSpec((1, LANES, n_k), lambda bi, h, i: (bi, h, ko)),
            pl.BlockSpec((4, DIFF_HEAD_DIM), lambda bi, h, i: (0, 0)),
            pl.BlockSpec((1, DIFF_V_DIM), lambda bi, h, i: (0, 0)),
        ],
        out_specs=pl.BlockSpec((1, tq, LANES), lambda bi, h, i: (bi, i, h)),
        out_shape=jax.ShapeDtypeStruct((b, n_q, DIFF_HEADS * LANES), BF16),
        compiler_params=_params("parallel", "arbitrary", "arbitrary"),
        name="diff_attention",
    )(dq, dk, dv, diff_lambda, subln_g)


def _merge_kernel(a_ref, b_ref, x_ref, mod_ref, w_ref, g1_ref, g2_ref, x1_ref, h2_ref):
    half = a_ref.shape[-1]
    mod = mod_ref[0]
    y = _dot(a_ref[0], w_ref[0:half, :]) + _dot(b_ref[0], w_ref[half:, :])
    x1 = x_ref[0] + mod[2:3] * _rms(y, g1_ref[...])
    x1_ref[0] = x1
    h2_ref[0] = (_rms(x1, g2_ref[...]) * (1.0 + mod[4:5]) + mod[3:4]).astype(BF16)


def _merge(a, bb, x, mod, w_out, g1, g2, tm, x_off, ctx_mod):
    b, rows, half = a.shape
    d = x.shape[-1]
    nt = rows // tm
    xo = x_off // tm
    ctx_row = mod.shape[0] - 1
    tile = lambda w: pl.BlockSpec((1, tm, w), lambda bi, i: (bi, i, 0))
    const = lambda *shape: pl.BlockSpec(shape, lambda bi, i: (0,) * len(shape))
    return pl.pallas_call(
        _merge_kernel,
        grid=(b, nt),
        in_specs=[
            tile(half), tile(half), pl.BlockSpec((1, tm, d), lambda bi, i: (bi, xo + i, 0)),
            pl.BlockSpec((1, 6, d), lambda bi, i: (ctx_row if ctx_mod else bi, 0, 0)),
            const(*w_out.shape), const(1, d), const(1, d),
        ],
        out_specs=[tile(d), tile(d)],
        out_shape=[jax.ShapeDtypeStruct((b, rows, d), F32), jax.ShapeDtypeStruct((b, rows, d), BF16)],
        compiler_params=_params("parallel", "arbitrary"),
        name="merge",
    )(a, bb, x, mod, w_out, g1, g2)


def _ffn_kernel(h_ref, hp_ref, hn_ref, x_ref, mod_ref, wa_ref, wg_ref, cwa_ref, cwg_ref, wd_ref, g_ref,
                o_ref, lhs_ref, ua0_ref, ug0_ref, ua1_ref, ug1_ref, act0_ref, act1_ref, acc_ref):
    i = pl.program_id(1)
    tm = h_ref.shape[1]
    n_chunks = wa_ref.shape[0]
    hp = hp_ref[0]
    hn = hn_ref[0]
    lhs_ref[0:HALO_ROWS] = jnp.where(i == 0, jnp.zeros_like(hp), hp)
    lhs_ref[HALO_ROWS:HALO_ROWS + tm] = h_ref[0]
    lhs_ref[HALO_ROWS + tm:] = jnp.where(i == pl.num_programs(1) - 1, jnp.zeros_like(hn), hn)
    rows = tm + 2 * HALO_ROWS
    mid = slice(HALO_ROWS, HALO_ROWS + tm)
    slots = ((ua0_ref, ug0_ref), (ua1_ref, ug1_ref))

    def up(j, slot):
        lhs = lhs_ref[...]
        slots[slot][0][...] = _dot(lhs, wa_ref[j])
        slots[slot][1][...] = _dot(lhs, wg_ref[j])

    def conv(u, cw):
        down = pltpu.roll(u, 1, 0)[mid]
        above = pltpu.roll(u, rows - 1, 0)[mid]
        return down * cw[0:1] + u[mid] * cw[1:2] + above * cw[2:3] + cw[3:4]

    acts = (act0_ref, act1_ref)
    width = wa_ref.shape[2]

    def activate(j):
        a = conv(slots[j % 2][0][...], cwa_ref[j])
        g = conv(slots[j % 2][1][...], cwg_ref[j])
        k = j % FFN_DOWN_GROUP
        acts[(j // FFN_DOWN_GROUP) % 2][:, k * width:(k + 1) * width] = ((g * jax.nn.sigmoid(g)) * a).astype(BF16)

    def down(j_last):
        j0 = j_last - j_last % FFN_DOWN_GROUP
        k = (j_last - j0 + 1) * width
        return _dot(acts[(j_last // FFN_DOWN_GROUP) % 2][:, :k], wd_ref[j0 * width:(j_last + 1) * width, :])

    up(0, 0)
    up(1, 1)
    y = None
    for j in range(n_chunks):
        activate(j)
        if j + 2 < n_chunks:
            up(j + 2, j % 2)
        if j % FFN_DOWN_GROUP == FFN_DOWN_GROUP - 1 or j == n_chunks - 1:
            y = down(j)
            if j == n_chunks - 1:
                break
            if j < FFN_DOWN_GROUP:
                acc_ref[...] = y
            else:
                acc_ref[...] += y
    total = y if n_chunks <= FFN_DOWN_GROUP else acc_ref[...] + y
    mod = mod_ref[0]
    o_ref[0] = x_ref[0] + mod[5:6] * _rms(total, g_ref[...])


def _ffn(h2, x1, mod, wa, wg, cwa, cwg, wd, g3, tm, row_offset, n_rows, ctx_mod):
    b, total_rows, d = h2.shape
    ctx_row = mod.shape[0] - 1
    nt = n_rows // tm
    off = row_offset // tm
    per = tm // HALO_ROWS
    last_halo = total_rows // HALO_ROWS - 1
    const = lambda *shape: pl.BlockSpec(shape, lambda bi, i: (0,) * len(shape), pipeline_mode=pl.Buffered(1))
    tile = pl.BlockSpec((1, tm, d), lambda bi, i: (bi, off + i, 0))
    return pl.pallas_call(
        _ffn_kernel,
        grid=(b, nt),
        in_specs=[
            tile,
            pl.BlockSpec((1, HALO_ROWS, d), lambda bi, i: (bi, jnp.maximum((off + i) * per - 1, 0), 0)),
            pl.BlockSpec((1, HALO_ROWS, d), lambda bi, i: (bi, jnp.minimum((off + i + 1) * per, last_halo), 0)),
            tile,
            pl.BlockSpec((1, 6, d), lambda bi, i: (ctx_row if ctx_mod else bi, 0, 0)),
            const(*wa.shape), const(*wg.shape), const(*cwa.shape), const(*cwg.shape), const(*wd.shape),
            const(1, d),
        ],
        out_specs=pl.BlockSpec((1, tm, d), lambda bi, i: (bi, i, 0)),
        out_shape=jax.ShapeDtypeStruct((b, n_rows, d), F32),
        scratch_shapes=[pltpu.VMEM((tm + 2 * HALO_ROWS, d), BF16)]
        + [pltpu.VMEM((tm + 2 * HALO_ROWS, FFN_CHUNK), F32)] * 4
        + [pltpu.VMEM((tm, FFN_DOWN_GROUP * FFN_CHUNK), BF16)] * 2 + [pltpu.VMEM((tm, d), F32)],
        compiler_params=_params("parallel", "arbitrary"),
        name="conv_ffn",
    )(h2, h2, h2, x1, mod, wa, wg, cwa, cwg, wd, g3)


def _mix_ffn_kernel(a_ref, ap_ref, an_ref, b_ref, bp_ref, bn_ref, x_ref, xp_ref, xn_ref, mod_ref,
                    wo_ref, g1_ref, g2_ref, wa_ref, wg_ref, cwa_ref, cwg_ref, wd_ref, g3_ref,
                    o_ref, lhs_ref, x1_ref, ua0_ref, ug0_ref, ua1_ref, ug1_ref, act0_ref, act1_ref, acc_ref):
    i = pl.program_id(1)
    tm = a_ref.shape[1]
    half = a_ref.shape[2]
    n_chunks = wa_ref.shape[0]
    rows = tm + 2 * HALO_ROWS
    mid = slice(HALO_ROWS, HALO_ROWS + tm)
    mod = mod_ref[0]

    ext = lambda p_ref, m_ref, n_ref: jnp.concatenate([p_ref[0], m_ref[0], n_ref[0]], axis=0)
    y = _dot(ext(ap_ref, a_ref, an_ref), wo_ref[0:half, :]) + _dot(ext(bp_ref, b_ref, bn_ref), wo_ref[half:, :])
    x1 = ext(xp_ref, x_ref, xn_ref) + mod[2:3] * _rms(y, g1_ref[...])
    h2 = _rms(x1, g2_ref[...]) * (1.0 + mod[4:5]) + mod[3:4]
    r = lax.broadcasted_iota(jnp.int32, h2.shape, 0)
    pad = ((r < HALO_ROWS) & (i == 0)) | ((r >= HALO_ROWS + tm) & (i == pl.num_programs(1) - 1))
    lhs_ref[...] = jnp.where(pad, 0.0, h2).astype(BF16)
    x1_ref[...] = x1[mid]
    slots = ((ua0_ref, ug0_ref), (ua1_ref, ug1_ref))

    def up(j, slot):
        lhs = lhs_ref[...]
        slots[slot][0][...] = _dot(lhs, wa_ref[j])
        slots[slot][1][...] = _dot(lhs, wg_ref[j])

    def conv(u, cw):
        down = pltpu.roll(u, 1, 0)[mid]
        above = pltpu.roll(u, rows - 1, 0)[mid]
        return down * cw[0:1] + u[mid] * cw[1:2] + above * cw[2:3] + cw[3:4]

    acts = (act0_ref, act1_ref)
    width = wa_ref.shape[2]

    def activate(j):
        a = conv(slots[j % 2][0][...], cwa_ref[j])
        g = conv(slots[j % 2][1][...], cwg_ref[j])
        k = j % FFN_DOWN_GROUP
        acts[(j // FFN_DOWN_GROUP) % 2][:, k * width:(k + 1) * width] = ((g * jax.nn.sigmoid(g)) * a).astype(BF16)

    def down(j_last):
        j0 = j_last - j_last % FFN_DOWN_GROUP
        k = (j_last - j0 + 1) * width
        return _dot(acts[(j_last // FFN_DOWN_GROUP) % 2][:, :k], wd_ref[j0 * width:(j_last + 1) * width, :])

    up(0, 0)
    up(1, 1)
    y = None
    for j in range(n_chunks):
        activate(j)
        if j + 2 < n_chunks:
            up(j + 2, j % 2)
        if j % FFN_DOWN_GROUP == FFN_DOWN_GROUP - 1 or j == n_chunks - 1:
            y = down(j)
            if j == n_chunks - 1:
                break
            if j < FFN_DOWN_GROUP:
                acc_ref[...] = y
            else:
                acc_ref[...] += y
    total = y if n_chunks <= FFN_DOWN_GROUP else acc_ref[...] + y
    o_ref[0] = x1_ref[...] + mod[5:6] * _rms(total, g3_ref[...])


def _mix_ffn(a, bb, x, mod, w_out, g1, g2, wa, wg, cwa, cwg, wd, g3, tm, ctx_mod):
    b, n_rows, half = a.shape
    d = x.shape[-1]
    ctx_row = mod.shape[0] - 1
    per = tm // HALO_ROWS
    last_halo = n_rows // HALO_ROWS - 1
    const = lambda *shape: pl.BlockSpec(shape, lambda bi, i: (0,) * len(shape), pipeline_mode=pl.Buffered(1))
    tile = lambda w: pl.BlockSpec((1, tm, w), lambda bi, i: (bi, i, 0))
    prev = lambda w: pl.BlockSpec((1, HALO_ROWS, w), lambda bi, i: (bi, jnp.maximum(i * per - 1, 0), 0))
    nxt = lambda w: pl.BlockSpec((1, HALO_ROWS, w), lambda bi, i: (bi, jnp.minimum((i + 1) * per, last_halo), 0))
    rows = tm + 2 * HALO_ROWS
    return pl.pallas_call(
        _mix_ffn_kernel,
        grid=(b, n_rows // tm),
        in_specs=[
            tile(half), prev(half), nxt(half), tile(half), prev(half), nxt(half), tile(d), prev(d), nxt(d),
            pl.BlockSpec((1, 6, d), lambda bi, i: (ctx_row if ctx_mod else bi, 0, 0)),
            const(*w_out.shape), const(1, d), const(1, d),
            const(*wa.shape), const(*wg.shape), const(*cwa.shape), const(*cwg.shape), const(*wd.shape),
            const(1, d),
        ],
        out_specs=tile(d),
        out_shape=jax.ShapeDtypeStruct((b, n_rows, d), F32),
        scratch_shapes=[pltpu.VMEM((rows, d), BF16), pltpu.VMEM((tm, d), F32)]
        + [pltpu.VMEM((rows, FFN_CHUNK), F32)] * 4
        + [pltpu.VMEM((tm, FFN_DOWN_GROUP * FFN_CHUNK), BF16)] * 2 + [pltpu.VMEM((tm, d), F32)],
        compiler_params=_params("parallel", "arbitrary"),
        name="mix_ffn",
    )(a, a, a, bb, bb, bb, x, x, x, mod, w_out, g1, g2, wa, wg, cwa, cwg, wd, g3)


def _proj_odd_kernel(x_ref, xp_ref, xn_ref, mod_ref, g_ref, wqkv_ref, wu_ref, cw_ref, cd_ref, sd_ref,
                     q_ref, k_ref, v_ref, u_ref, *, n_tiles):
    i = pl.program_id(1)
    mod = mod_ref[0]

    def normed(x):
        return (_rms(x, g_ref[...]) * (1.0 + mod[1:2]) + mod[0:1]).astype(BF16)

    h = normed(x_ref[0])
    p = _dot(h, wqkv_ref[...])
    cd, sd = cd_ref[...], sd_ref[...]
    scale = WIN_HEAD_DIM ** -0.5 * LOG2E
    for c in range(4):
        cs = slice(c * LANES, (c + 1) * LANES)
        q_ref[0, :, cs] = (_rope_chunk(p[:, cs], cd, sd, WIN_HEAD_DIM // 2) * scale).astype(BF16)
    for c in range(2):
        cs = slice(512 + c * LANES, 512 + (c + 1) * LANES)
        k_ref[0, c * LANES:(c + 1) * LANES, :] = _rope_chunk(p[:, cs], cd, sd, WIN_HEAD_DIM // 2).astype(BF16).T
    v_ref[0] = p[:, 768:1024].astype(BF16)
    prev_ok = jnp.where(i == 0, 0.0, 1.0)
    next_ok = jnp.where(i == n_tiles - 1, 0.0, 1.0)
    wu = wu_ref[...]
    u = _dot(h, wu)
    up = _dot(normed(xp_ref[0]), wu)[HALO_ROWS - 1:HALO_ROWS] * prev_ok
    un = _dot(normed(xn_ref[0]), wu)[0:1] * next_ok
    down, upw = _shift_rows(u, up, un)
    cw = cw_ref[...]
    u_ref[0] = (down * cw[0:1] + u * cw[1:2] + upw * cw[2:3] + cw[3:4]).astype(u_ref.dtype)


def _proj_odd(xcat, mod, g, wqkv, wu, cw, cd, sd, tm):
    b, rows, d = xcat.shape
    n_tiles = rows // tm
    per = tm // HALO_ROWS
    last_halo = n_tiles * per - 1
    const = lambda *shape: pl.BlockSpec(shape, lambda bi, i: (0,) * len(shape))
    tab = pl.BlockSpec((tm, LANES), lambda bi, i: (i, 0))
    out = lambda w: pl.BlockSpec((1, tm, w), lambda bi, i: (bi, i, 0))
    return pl.pallas_call(
        functools.partial(_proj_odd_kernel, n_tiles=n_tiles),
        grid=(b, n_tiles),
        in_specs=[
            pl.BlockSpec((1, tm, d), lambda bi, i: (bi, i, 0)),
            pl.BlockSpec((1, HALO_ROWS, d), lambda bi, i: (bi, jnp.maximum(i * per - 1, 0), 0)),
            pl.BlockSpec((1, HALO_ROWS, d), lambda bi, i: (bi, jnp.minimum((i + 1) * per, last_halo), 0)),
            pl.BlockSpec((1, 6, d), lambda bi, i: (bi, 0, 0)),
            const(1, d), const(*wqkv.shape), const(*wu.shape), const(*cw.shape), tab, tab,
        ],
        out_specs=[out(512), pl.BlockSpec((1, 2 * LANES, tm), lambda bi, i: (bi, 0, i)), out(256),
                   out(3 * HYENA_CH)],
        out_shape=[jax.ShapeDtypeStruct((b, rows, 512), BF16), jax.ShapeDtypeStruct((b, 2 * LANES, rows), BF16),
                   jax.ShapeDtypeStruct((b, rows, 256), BF16), jax.ShapeDtypeStruct((b, rows, 3 * HYENA_CH), BF16)],
        compiler_params=_params("parallel", "arbitrary"),
        name="proj_odd",
    )(xcat, xcat, xcat, mod, g, wqkv, wu, cw, cd, sd)


def _proj_ctx_kv_kernel(x_ref, mod_ref, g_ref, w_ref, k_ref, v_ref):
    mod = mod_ref[0]
    h = (_rms(x_ref[0], g_ref[...]) * (1.0 + mod[1:2]) + mod[0:1]).astype(BF16)
    p = _dot(h, w_ref[...])
    k_ref[0] = p[:, :2 * LANES].astype(BF16).T
    v_ref[0] = p[:, 2 * LANES:].astype(BF16)


def _proj_ctx_kv(xcat, mod, g, wkv, ctx_tile, n_ctx):
    b, _, d = xcat.shape
    ctx_row = mod.shape[0] - 1
    out = pl.BlockSpec((1, n_ctx, 2 * LANES), lambda bi: (bi, 0, 0))
    return pl.pallas_call(
        _proj_ctx_kv_kernel,
        grid=(b,),
        in_specs=[
            pl.BlockSpec((1, n_ctx, d), lambda bi: (bi, ctx_tile, 0)),
            pl.BlockSpec((1, 6, d), lambda bi: (ctx_row, 0, 0)),
            pl.BlockSpec((1, d), lambda bi: (0, 0)),
            pl.BlockSpec(wkv.shape, lambda bi: (0, 0)),
        ],
        out_specs=[pl.BlockSpec((1, 2 * LANES, n_ctx), lambda bi: (bi, 0, 0)), out],
        out_shape=[jax.ShapeDtypeStruct((b, 2 * LANES, n_ctx), BF16), jax.ShapeDtypeStruct((b, n_ctx, 2 * LANES), BF16)],
        compiler_params=_params("parallel"),
        name="proj_ctx_kv",
    )(xcat, mod, g, wkv)


def _window_kernel(sink_ref, q_ref, kc_ref, vc_ref, kp_ref, km_ref, kn_ref, vp_ref, vm_ref, vn_ref, o_ref,
                   *, n_lat):
    i = pl.program_id(1)
    tq = q_ref.shape[1]
    n_ctx = vc_ref.shape[1]
    nk = n_ctx + tq + 2 * WINDOW
    col = lax.broadcasted_iota(jnp.int32, (tq, nk), 1)
    row = lax.broadcasted_iota(jnp.int32, (tq, nk), 0)
    k_pos = i * tq - WINDOW + (col - n_ctx)
    q_pos = i * tq + row
    visible = (col < n_ctx) | ((jnp.abs(q_pos - k_pos) <= WINDOW) & (k_pos >= 0) & (k_pos < n_lat))
    bias = jnp.where(visible, 0.0, NEG_INF)
    group = WIN_Q_HEADS // WIN_KV_HEADS
    bias = jnp.concatenate([bias, bias], axis=0)
    second_head = lax.broadcasted_iota(jnp.int32, (2 * tq, 1), 0) >= tq
    first = _lane_iota((tq, LANES)) < WIN_HEAD_DIM
    kts, vs = [], []
    for hk in range(WIN_KV_HEADS):
        c = slice(hk * LANES, (hk + 1) * LANES)
        kts.append(jnp.concatenate([kc_ref[0, c, :], kp_ref[0, c, :], km_ref[0, c, :], kn_ref[0, c, :]], axis=1))
        vs.append(jnp.concatenate([vc_ref[0, :, c], vp_ref[0, :, c], vm_ref[0, :, c], vn_ref[0, :, c]], axis=0))

    def logits(chunk):
        q = q_ref[0, :, chunk * LANES:(chunk + 1) * LANES]
        zero = jnp.zeros_like(q)
        stacked = jnp.concatenate([jnp.where(first, q, zero), jnp.where(first, zero, q)], axis=0)
        return _dot(stacked, kts[chunk // (group // 2)])

    n_chunks = WIN_Q_HEADS // 2
    s_next = logits(0)
    for chunk in range(n_chunks):
        s = s_next + bias
        if chunk + 1 < n_chunks:
            s_next = logits(chunk + 1)
        sink = jnp.where(second_head, sink_ref[2 * chunk + 1], sink_ref[2 * chunk]) * LOG2E
        m = jnp.maximum(jnp.max(s, axis=-1, keepdims=True), sink)
        p = jnp.exp2(s - m)
        l = jnp.sum(p, axis=-1, keepdims=True) + jnp.exp2(sink - m)
        o = _dot(p.astype(BF16), vs[chunk // (group // 2)]) / l
        o_ref[0, :, chunk * LANES:(chunk + 1) * LANES] = jnp.where(first, o[:tq], o[tq:]).astype(o_ref.dtype)


def _window_attention(q, kdt, vd, kct, vc, sink):
    b, n_lat, _ = q.shape
    tq = ROW_TILE
    n_ctx = vc.shape[1]
    per = tq // WINDOW
    last = n_lat // WINDOW - 1
    prev_i = lambda i: jnp.maximum(i * per - 1, 0)
    next_i = lambda i: jnp.minimum((i + 1) * per, last)
    w = 2 * LANES
    return pl.pallas_call(
        functools.partial(_window_kernel, n_lat=n_lat),
        grid=(b, n_lat // tq),
        in_specs=[
            pl.BlockSpec(memory_space=pltpu.SMEM),
            pl.BlockSpec((1, tq, 4 * LANES), lambda bi, i: (bi, i, 0)),
            pl.BlockSpec((1, w, n_ctx), lambda bi, i: (bi, 0, 0)),
            pl.BlockSpec((1, n_ctx, w), lambda bi, i: (bi, 0, 0)),
            pl.BlockSpec((1, w, WINDOW), lambda bi, i: (bi, 0, prev_i(i))),
            pl.BlockSpec((1, w, tq), lambda bi, i: (bi, 0, i)),
            pl.BlockSpec((1, w, WINDOW), lambda bi, i: (bi, 0, next_i(i))),
            pl.BlockSpec((1, WINDOW, w), lambda bi, i: (bi, prev_i(i), 0)),
            pl.BlockSpec((1, tq, w), lambda bi, i: (bi, i, 0)),
            pl.BlockSpec((1, WINDOW, w), lambda bi, i: (bi, next_i(i), 0)),
        ],
        out_specs=pl.BlockSpec((1, tq, 4 * LANES), lambda bi, i: (bi, i, 0)),
        out_shape=jax.ShapeDtypeStruct((b, n_lat, 4 * LANES), BF16),
        compiler_params=_params("parallel", "arbitrary"),
        name="window_attention",
    )(sink, q, kct, vc, kdt, kdt, kdt, vd, vd, vd)


def _hyena_filter_kernel(feat_ref, w1_ref, b1_ref, w2_ref, b2_ref, w3_ref, b3_ref, freq_ref, decay_ref, h_ref,
                         *, n_forward):
    j = pl.program_id(0)
    hp = lax.Precision.HIGHEST
    freq = freq_ref[...]
    hid = jnp.sin(freq * (jnp.dot(feat_ref[...], w1_ref[...], precision=hp, preferred_element_type=F32) + b1_ref[...]))
    hid = jnp.sin(freq * (jnp.dot(hid, w2_ref[...], precision=hp, preferred_element_type=F32) + b2_ref[...]))
    h = jnp.dot(hid, w3_ref[...], precision=hp, preferred_element_type=F32) + b3_ref[...]
    h = h * decay_ref[...]
    row = lax.broadcasted_iota(jnp.int32, h.shape, 0)
    h_ref[...] = jnp.where((row == 0) & (j >= n_forward), 0.0, h).astype(h_ref.dtype)


def _hyena_filters(feats, w1, b1, w2, b2, w3, b3, freq, decay):
    length = feats.shape[0]
    n = w3.shape[1]
    ch = decay.shape[1]
    full = lambda a: pl.BlockSpec(a.shape, lambda j: (0, 0))
    return pl.pallas_call(
        functools.partial(_hyena_filter_kernel, n_forward=HYENA_ORDER),
        grid=(n // ch,),
        in_specs=[full(feats), full(w1), full(b1), full(w2), full(b2),
                  pl.BlockSpec((w3.shape[0], ch), lambda j: (0, j)), pl.BlockSpec((1, ch), lambda j: (0, j)),
                  full(freq), full(decay)],
        out_specs=pl.BlockSpec((length, ch), lambda j: (0, j)),
        out_shape=jax.ShapeDtypeStruct((length, n), BF16),
        compiler_params=_params("arbitrary"),
        name="hyena_filters",
    )(feats, w1, b1, w2, b2, w3, b3, freq, decay)


def _spectrum_kernel(f_ref, h_ref, re_ref, im_ref):
    j = pl.program_id(0)
    tk = re_ref.shape[0]
    n = re_ref.shape[1]
    a = _dot(f_ref[0], h_ref[...])
    re_ref[...] = a[:tk, :n] + a[:tk, n:]
    row = lax.broadcasted_iota(jnp.int32, (tk, n), 0)
    sign = jnp.where((row == 0) & (j == 0), 1.0, -1.0)
    im_ref[...] = a[tk:, :n] + sign * a[tk:, n:]


def _filter_spectrum(f3, hcat):
    nt, tk2, length = f3.shape
    n = hcat.shape[1] // 2
    tk = tk2 // 2
    out = pl.BlockSpec((tk, n), lambda j: (j, 0))
    return pl.pallas_call(
        _spectrum_kernel,
        grid=(nt,),
        in_specs=[pl.BlockSpec((1, tk2, length), lambda j: (j, 0, 0)),
                  pl.BlockSpec(hcat.shape, lambda j: (0, 0))],
        out_specs=[out, out],
        out_shape=[jax.ShapeDtypeStruct((nt * tk, n), F32)] * 2,
        compiler_params=_params("arbitrary"),
        name="filter_spectrum",
    )(f3, hcat)


def _long_conv_kernel(z_ref, x_ref, f_ref, g_ref, hre_ref, him_ref, bias_ref, o_ref, acc_ref):
    j = pl.program_id(1)
    tk = hre_ref.shape[0]
    row = lax.broadcasted_iota(jnp.int32, hre_ref.shape, 0)
    packed = (row == 0) & (j == 0)
    bias = bias_ref[...]
    h_re = hre_ref[...] + bias
    h_im = him_ref[...] + jnp.where(packed, bias, 0.0)

    @pl.when(j == 0)
    def _():
        acc_ref[...] = jnp.zeros_like(acc_ref)

    spectra = [_dot(f_ref[0], z_ref[s]) for s in range(z_ref.shape[0])]
    for s, a in enumerate(spectra):
        a_re, a_im = a[:tk], a[tk:]
        ii = a_im * h_im
        p_re = a_re * h_re - jnp.where(packed, 0.0, ii)
        p_im = jnp.where(packed, ii, a_re * h_im + a_im * h_re)
        acc_ref[s] += _dot(g_ref[0], jnp.concatenate([p_re, p_im], axis=0).astype(BF16))

    @pl.when(j == pl.num_programs(1) - 1)
    def _():
        o_ref[...] = (x_ref[...].astype(F32) * acc_ref[...]).astype(o_ref.dtype)


def _long_conv(z_src, z_col, x_src, x_col, f3, g3, hre, him, h_col, bias):
    b, length, _ = z_src.shape
    nt, tk2, _ = f3.shape
    tk = tk2 // 2
    ch = HYENA_CH
    nb = CONV_SAMPLES
    return pl.pallas_call(
        _long_conv_kernel,
        grid=(b // nb, nt),
        in_specs=[
            pl.BlockSpec((nb, length, ch), lambda bi, j: (bi, 0, z_col)),
            pl.BlockSpec((nb, length, ch), lambda bi, j: (bi, 0, x_col)),
            pl.BlockSpec((1, tk2, length), lambda bi, j: (j, 0, 0)),
            pl.BlockSpec((1, length, tk2), lambda bi, j: (j, 0, 0)),
            pl.BlockSpec((tk, ch), lambda bi, j: (j, h_col)),
            pl.BlockSpec((tk, ch), lambda bi, j: (j, h_col)),
            pl.BlockSpec((1, ch), lambda bi, j: (0, 0)),
        ],
        out_specs=pl.BlockSpec((nb, length, ch), lambda bi, j: (bi, 0, 0)),
        out_shape=jax.ShapeDtypeStruct((b, length, ch), BF16),
        scratch_shapes=[pltpu.VMEM((nb, length, ch), F32)],
        compiler_params=_params("parallel", "arbitrary"),
        name="long_conv",
    )(z_src, x_src, f3, g3, hre, him, bias)


def _axial_rope(length, rot_dim):
    rows = length // GRID_W
    row = jnp.repeat(jnp.arange(rows), GRID_W).astype(F32)
    col = jnp.tile(jnp.arange(GRID_W), rows).astype(F32)
    quarter = rot_dim // 4
    inv = ROPE_BASE ** (-jnp.arange(quarter, dtype=F32) / quarter)
    ang = jnp.concatenate([row[:, None] * inv, col[:, None] * inv], axis=-1)
    return jnp.cos(ang), jnp.sin(ang)


def _rope_tables(length, n_ctx):
    cos_m, sin_m = _axial_rope(length, MLA_ROPE)
    cos_d, sin_d = _axial_rope(length, DIFF_HEAD_DIM)
    ones = lambda w: jnp.ones((length, w), F32)
    zeros = lambda w: jnp.zeros((length, w), F32)
    cm = jnp.concatenate([ones(MLA_NOPE), cos_m, cos_m, ones(LANES - MLA_NOPE - MLA_ROPE)], axis=1)
    sm = jnp.concatenate([zeros(MLA_NOPE), -sin_m, sin_m, zeros(LANES - MLA_NOPE - MLA_ROPE)], axis=1)
    cd = jnp.concatenate([cos_d] * 4, axis=1)
    sd = jnp.concatenate([-sin_d, sin_d] * 2, axis=1)
    pad = lambda t, v: jnp.concatenate([t, jnp.full((n_ctx, LANES), v, F32)], axis=0)
    return pad(cm, 1.0), pad(sm, 0.0), pad(cd, 1.0), pad(sd, 0.0)


def _dft_matrices(length):
    n = 2 * length
    k = np.arange(length, dtype=np.int64)
    ang = 2.0 * np.pi * ((k[:, None] * k[None, :]) % n).astype(np.float64) / n
    alt = np.where(k % 2 == 0, 1.0, -1.0)
    f_re = np.cos(ang)
    f_im = -np.sin(ang)
    f_im[0, :] = alt
    g_re = (2.0 / n) * np.cos(ang.T)
    g_im = -(2.0 / n) * np.sin(ang.T)
    g_re[:, 0] = 1.0 / n
    g_im[:, 0] = alt / n
    nt = length // FREQ_TILE
    f3 = np.concatenate([f_re.reshape(nt, FREQ_TILE, length), f_im.reshape(nt, FREQ_TILE, length)], axis=1)
    g3 = np.concatenate([g_re.reshape(length, nt, FREQ_TILE), g_im.reshape(length, nt, FREQ_TILE)], axis=2)
    g3 = np.transpose(g3, (1, 0, 2))
    return jnp.asarray(f3, dtype=BF16), jnp.asarray(g3, dtype=BF16)


def _hyena_features(length):
    t = jnp.arange(length, dtype=F32)
    tn = t / max(length - 1, 1)
    bands = jnp.linspace(1e-4, HYENA_BANDS - 1, HYENA_BANDS, dtype=F32)
    ang = 2.0 * math.pi * bands[None, :] * t[:, None] / length
    feats = jnp.concatenate([tn[:, None], jnp.cos(ang), -jnp.sin(ang)], axis=-1)
    min_decay = math.log(HYENA_DECAY_TARGET) / HYENA_SLOW_PCT
    max_decay = math.log(HYENA_DECAY_TARGET) / HYENA_FAST_PCT
    deltas = jnp.abs(jnp.linspace(min_decay, max_decay, HYENA_CH, dtype=F32))
    decay = jnp.exp(-tn[:, None] * deltas[None, :])
    return feats, decay


def _pad_cols(w, left, right):
    return jnp.pad(w, ((0, 0),) * (w.ndim - 1) + ((left, right),))


def _even_weights(w_in, w_uq, w_ukv):
    d = w_in.shape[0]
    o = MLA_Q_RANK + MLA_KV_RANK
    w_kr = _pad_cols(w_in[:, o:o + MLA_ROPE], MLA_NOPE, LANES - MLA_NOPE - MLA_ROPE)
    w_in_ext = jnp.concatenate([w_in[:, :o], w_kr, w_in[:, o + MLA_ROPE:]], axis=1)
    uq = _pad_cols(w_uq.reshape(MLA_Q_RANK, MLA_HEADS, MLA_NOPE + MLA_ROPE), 0, LANES - MLA_NOPE - MLA_ROPE)
    ukv = w_ukv.reshape(MLA_KV_RANK, MLA_HEADS, MLA_NOPE + MLA_V)
    uk = _pad_cols(ukv[:, :, :MLA_NOPE], 0, LANES - MLA_NOPE)
    uv = ukv[:, :, MLA_NOPE:].reshape(MLA_KV_RANK, MLA_HEADS // 2, 2, MLA_V)
    uv = jnp.stack([_pad_cols(uv[:, :, 0], 0, LANES - MLA_V), _pad_cols(uv[:, :, 1], LANES - MLA_V, 0)], axis=2)
    w_ukv_ext = jnp.concatenate([uk.reshape(MLA_KV_RANK, -1), uv.reshape(MLA_KV_RANK, -1)], axis=1)
    return w_in_ext.astype(BF16), uq.reshape(MLA_Q_RANK, -1).astype(BF16), w_ukv_ext.astype(BF16)


def _odd_weights(w_in):
    d = w_in.shape[0]
    nq = WIN_Q_HEADS * WIN_HEAD_DIM
    nkv = WIN_KV_HEADS * WIN_HEAD_DIM
    dup = lambda w: jnp.concatenate([w.reshape(d, WIN_KV_HEADS, WIN_HEAD_DIM)] * 2, axis=-1).reshape(d, -1)
    wk = dup(w_in[:, nq:nq + nkv])
    wv = dup(w_in[:, nq + nkv:nq + 2 * nkv])
    wqkv = jnp.concatenate([w_in[:, :nq], wk, wv], axis=1).astype(BF16)
    wkv = jnp.concatenate([wk, wv], axis=1).astype(BF16)
    wu = w_in[:, nq + 2 * nkv:].astype(BF16)
    return wqkv, wkv, wu


def _ffn_weights(w_up, conv_w, conv_b, w_down):
    d = w_up.shape[0]
    nc = D_FF // FFN_CHUNK
    chunks = lambda w: jnp.transpose(w.reshape(w.shape[0], nc, FFN_CHUNK), (1, 0, 2))
    cw = jnp.concatenate([conv_w, conv_b[None, :]], axis=0)
    return (chunks(w_up[:, :D_FF]).astype(BF16), chunks(w_up[:, D_FF:]).astype(BF16),
            chunks(cw[:, :D_FF]), chunks(cw[:, D_FF:]),
            w_down.astype(BF16))


def kernel(x, c, ctx, c_ctx, ada_w, ada_b, norm_g, mix_w_out, ffn_w_up, ffn_conv_w, ffn_conv_b, ffn_w_down, even_w_in, mla_q_norm_g, mla_w_uq, mla_kv_norm_g, mla_w_ukv, diff_lambda, diff_subln_g, odd_w_in, win_sink, hy_conv_w, hy_conv_b, hy_f_w1, hy_f_b1, hy_f_w2, hy_f_b2, hy_f_w3, hy_f_b3, hy_f_freq, hy_bias):
    b, seq, d = x.shape
    n_ctx = ctx.shape[1]
    depth = ada_w.shape[0]
    assert depth == 2 and d == D_MODEL and seq % ROW_TILE == 0 and n_ctx == ROW_TILE
    n_lat_tiles = seq // ROW_TILE
    n_tiles = n_lat_tiles + 1

    pad_rows = (-(b + 1)) % 8
    cond = jnp.concatenate([c, c_ctx[None, :], jnp.zeros((pad_rows, d), F32)], axis=0)
    mod = _ada_table(cond, ada_w, ada_b).reshape(depth, b + 1 + pad_rows, 6, d)[:, :b + 1]
    g = norm_g.reshape(depth, 4, 1, d)
    cm, sm, cd, sd = _rope_tables(seq, n_ctx)

    lambda_init = 0.8 - 0.6 * math.exp(-0.3 * 0)
    w_in_e, w_uq_e, w_ukv_e = _even_weights(even_w_in[0], mla_w_uq[0], mla_w_ukv[0])
    rows = seq + n_ctx
    even_w = (g[0, 0], w_in_e, mla_q_norm_g[0][None, :], w_uq_e, mla_kv_norm_g[0][None, :], w_ukv_e)
    proj = _proj_even(x, mod[0], even_w, (cm, sm, cd, sd), rows, PROJ_ROW_TILE, 0, False)
    qm, km, vm, dq, dk, dv = _proj_even(ctx, mod[0], even_w, (cm, sm, cd, sd), rows, n_ctx, seq, True, prior=proj)
    lam_args = (diff_lambda[0], diff_subln_g[0][None, :], lambda_init)
    w_out0 = mix_w_out[0].astype(BF16)
    ffn0 = _ffn_weights(ffn_w_up[0], ffn_conv_w[0], ffn_conv_b[0], ffn_w_down[0])
    o_mla = _mla_attention(qm, km, vm, ATTN_Q_TILE, 0, seq, 0, rows)
    o_diff = _diff_attention(dq, dk, dv, *lam_args, ATTN_Q_TILE, 0, seq, 0, rows)
    x_lat = _mix_ffn(o_mla, o_diff, x, mod[0], w_out0, g[0, 1], g[0, 2], *ffn0, g[0, 3], FFN_ROW_TILE, False)
    o_mla = _mla_attention(qm, km, vm, n_ctx, seq, n_ctx, seq, n_ctx)
    o_diff = _diff_attention(dq, dk, dv, *lam_args, n_ctx, seq, n_ctx, seq, n_ctx)
    x_ctx = _mix_ffn(o_mla, o_diff, ctx, mod[0], w_out0, g[0, 1], g[0, 2], *ffn0, g[0, 3], n_ctx, True)

    wqkv, wkv, wu = _odd_weights(odd_w_in[0])
    cw_hy = jnp.concatenate([hy_conv_w[0], hy_conv_b[0][None, :]], axis=0)
    q, kd, vd, u = _proj_odd(x_lat, mod[1], g[1, 0], wqkv, wu, cw_hy, cd, sd, PROJ_ROW_TILE)
    kc, vc = _proj_ctx_kv(x_ctx, mod[1], g[1, 0], wkv, 0, n_ctx)
    o_win = _window_attention(q, kd, vd, kc, vc, win_sink[0])

    feats, decay = _hyena_features(seq)
    emb = feats.shape[1]
    feats = _pad_cols(feats, 0, LANES - emb)
    w1 = jnp.pad(hy_f_w1[0], ((0, LANES - emb), (0, 0)))
    hcat = _hyena_filters(feats, w1, hy_f_b1[0][None, :], hy_f_w2[0], hy_f_b2[0][None, :], hy_f_w3[0],
                          hy_f_b3[0][None, :], hy_f_freq[0][None, :], decay)
    f3, g3 = _dft_matrices(seq)
    hre, him = _filter_spectrum(f3, hcat)
    z = _long_conv(u, 0, u, 1, f3, g3, hre, him, 0, hy_bias[0, 0][None, :])
    z = _long_conv(z, 0, u, 2, f3, g3, hre, him, 1, hy_bias[0, 1][None, :])

    w_out1 = mix_w_out[1].astype(BF16)
    ffn1 = _ffn_weights(ffn_w_up[1], ffn_conv_w[1], ffn_conv_b[1], ffn_w_down[1])
    return _mix_ffn(o_win, z, x_lat, mod[1], w_out1, g[1, 1], g[1, 2], *ffn1, g[1, 3], FFN_ROW_TILE, False)
```

```python
import functools
import math

import numpy as np
import jax
import jax.numpy as jnp
from jax import lax
from jax.experimental import pallas as pl
from jax.experimental.pallas import tpu as pltpu

D_MODEL = 1024
GRID_W = 64
MLA_HEADS = 8
MLA_NOPE = 64
MLA_ROPE = 32
MLA_V = 64
MLA_Q_RANK = 384
MLA_KV_RANK = 256
DIFF_HEADS = 4
DIFF_HEAD_DIM = 64
DIFF_V_DIM = 128
WIN_Q_HEADS = 8
WIN_KV_HEADS = 2
WIN_HEAD_DIM = 64
WINDOW = 128
HYENA_CH = 512
HYENA_ORDER = 2
HYENA_BANDS = 16
HYENA_DECAY_TARGET = 1e-2
HYENA_FAST_PCT = 0.3
HYENA_SLOW_PCT = 1.5
D_FF = 2816
ROPE_BASE = 10000.0
NORM_EPS = 1e-6
NEG_INF = -1e30

LANES = 128
HALO_ROWS = 16
ROW_TILE = 256
ATTN_Q_TILE = 1024
ATTN_SUB_TILE = 256
ATTN_KEY_CHUNK = 256
LOG2E = math.log2(math.e)
FFN_ROW_TILE = 512
PROJ_ROW_TILE = 512
FFN_CHUNK = 256
FFN_DOWN_GROUP = 4
FREQ_TILE = 256
CONV_SAMPLES = 2
VMEM_LIMIT = 48 * 1024 * 1024

BF16 = jnp.bfloat16
F32 = jnp.float32


def _params(*semantics):
    return pltpu.CompilerParams(dimension_semantics=semantics, vmem_limit_bytes=VMEM_LIMIT)


def _rms(x, g):
    return x * lax.rsqrt(jnp.mean(x * x, axis=-1, keepdims=True) + NORM_EPS) * g


def _dot(a, b):
    return jnp.dot(a, b, preferred_element_type=F32)


def _lane_iota(shape):
    return lax.broadcasted_iota(jnp.int32, shape, len(shape) - 1)


def _rope_chunk(x, cos, sin_signed, half):
    first = (_lane_iota(x.shape) % (2 * half)) < half
    partner = jnp.where(first, pltpu.roll(x, LANES - half, 1), pltpu.roll(x, half, 1))
    return x * cos + partner * sin_signed


def _rope_rows(xt, cos_t, sin_signed_t, half):
    first = (lax.broadcasted_iota(jnp.int32, xt.shape, 0) % (2 * half)) < half
    partner = jnp.where(first, pltpu.roll(xt, LANES - half, 0), pltpu.roll(xt, half, 0))
    return xt * cos_t + partner * sin_signed_t


def _shift_rows(u, prev_row, next_row):
    rows = u.shape[0]
    r = lax.broadcasted_iota(jnp.int32, u.shape, 0)
    down = jnp.where(r == 0, prev_row, pltpu.roll(u, 1, 0))
    up = jnp.where(r == rows - 1, next_row, pltpu.roll(u, rows - 1, 0))
    return down, up


def _ada_kernel(c_ref, w_ref, b_ref, o_ref):
    c = c_ref[...]
    sc = c * jax.nn.sigmoid(c)
    o_ref[0] = _dot(sc.astype(BF16), w_ref[0].astype(BF16)) + b_ref[0]


def _ada_table(cond, ada_w, ada_b):
    depth, d, n = ada_w.shape
    rows = cond.shape[0]
    tn = 1536
    return pl.pallas_call(
        _ada_kernel,
        grid=(depth, n // tn),
        in_specs=[
            pl.BlockSpec((rows, d), lambda l, j: (0, 0)),
            pl.BlockSpec((1, d, tn), lambda l, j: (l, 0, j)),
            pl.BlockSpec((1, 1, tn), lambda l, j: (l, 0, j)),
        ],
        out_specs=pl.BlockSpec((1, rows, tn), lambda l, j: (l, 0, j)),
        out_shape=jax.ShapeDtypeStruct((depth, rows, n), F32),
        compiler_params=_params("arbitrary", "arbitrary"),
        name="ada_table",
    )(cond, ada_w, ada_b.reshape(depth, 1, n))


def _proj_even_kernel(x_ref, mod_ref, g_ref, win_ref, qg_ref, wuq_ref, kvg_ref, wukv_ref,
                      cm_ref, sm_ref, cd_ref, sd_ref, cmt_ref, smt_ref, cdt_ref, sdt_ref, *rest):
    qm_ref, km_ref, vm_ref, dq_ref, dk_ref, dv_ref = rest[-6:]
    x = x_ref[0]
    mod = mod_ref[0]
    h = _rms(x, g_ref[...]) * (1.0 + mod[1:2]) + mod[0:1]
    p = _dot(h.astype(BF16), win_ref[...])
    o = 0
    cq = p[:, o:o + MLA_Q_RANK]; o += MLA_Q_RANK
    ckv = p[:, o:o + MLA_KV_RANK]; o += MLA_KV_RANK
    kr = p[:, o:o + LANES]; o += LANES
    dq = p[:, o:o + 512]; o += 512
    dk = p[:, o:o + 512]; o += 512
    dv = p[:, o:o + 512]
    q = _dot(_rms(cq, qg_ref[...]).astype(BF16), wuq_ref[...])
    kv = _dot(_rms(ckv, kvg_ref[...]).astype(BF16), wukv_ref[...])
    cm, sm, cd, sd = cm_ref[...], sm_ref[...], cd_ref[...], sd_ref[...]
    cmt, smt, cdt, sdt = cmt_ref[...], smt_ref[...], cdt_ref[...], sdt_ref[...]
    mla_scale = (MLA_NOPE + MLA_ROPE) ** -0.5 * LOG2E
    diff_scale = DIFF_HEAD_DIM ** -0.5 * LOG2E
    kr = _rope_chunk(kr, cm, sm, MLA_ROPE // 2)
    for hd in range(MLA_HEADS):
        c = slice(hd * LANES, (hd + 1) * LANES)
        qm_ref[0, c, :] = (_rope_rows(q[:, c].T, cmt, smt, MLA_ROPE // 2) * mla_scale).astype(BF16)
        km_ref[0, :, c] = (kv[:, c] + kr).astype(BF16)
        vm_ref[0, c, :] = kv[:, MLA_HEADS * LANES + hd * LANES:MLA_HEADS * LANES + (hd + 1) * LANES].astype(BF16).T
    for hd in range(DIFF_HEADS):
        c = slice(hd * LANES, (hd + 1) * LANES)
        dq_ref[0, c, :] = (_rope_rows(dq[:, c].T, cdt, sdt, DIFF_HEAD_DIM // 2) * diff_scale).astype(BF16)
        dk_ref[0, :, c] = _rope_chunk(dk[:, c], cd, sd, DIFF_HEAD_DIM // 2).astype(BF16)
        dv_ref[0, c, :] = dv[:, c].astype(BF16).T


def _proj_even(x, mod, weights, tables, rows, tm, row_off, ctx_mod, prior=None):
    b, n, d = x.shape
    off = row_off // tm
    ctx_row = mod.shape[0] - 1
    const = lambda a: pl.BlockSpec(a.shape, lambda bi, i: (0,) * a.ndim)
    tab = pl.BlockSpec((tm, LANES), lambda bi, i: (off + i, 0))
    out = lambda w: pl.BlockSpec((1, tm, w), lambda bi, i: (bi, off + i, 0))
    shp = lambda w: jax.ShapeDtypeStruct((b, rows, w), BF16)
    out_t = lambda w: pl.BlockSpec((1, w, tm), lambda bi, i: (bi, 0, off + i))
    shp_t = lambda w: jax.ShapeDtypeStruct((b, w, rows), BF16)
    tab_t = pl.BlockSpec((LANES, tm), lambda bi, i: (0, off + i))
    tables_t = [t.T for t in tables]
    n_in = 2 + len(weights) + 2 * len(tables)
    prior = () if prior is None else tuple(prior)
    return pl.pallas_call(
        _proj_even_kernel,
        grid=(b, n // tm),
        in_specs=[
            pl.BlockSpec((1, tm, d), lambda bi, i: (bi, i, 0)),
            pl.BlockSpec((1, 6, d), lambda bi, i: (ctx_row if ctx_mod else bi, 0, 0)),
            *[const(w) for w in weights], *[tab for _ in tables], *[tab_t for _ in tables],
            *[pl.BlockSpec(memory_space=pl.ANY) for _ in prior],
        ],
        out_specs=[out_t(1024), out(1024), out_t(1024), out_t(512), out(512), out_t(512)],
        out_shape=[shp_t(1024), shp(1024), shp_t(1024), shp_t(512), shp(512), shp_t(512)],
        input_output_aliases={n_in + k: k for k in range(len(prior))},
        compiler_params=_params("parallel", "arbitrary"),
        name="proj_even",
    )(x, mod, *weights, *tables, *tables_t, *prior)


def _attend_t(qt, k, vt):
    return _softmax_pv_t(_logits_t(qt, k), vt)


def _logits_t(qt, k):
    nk = k.shape[0]
    step = min(nk, ATTN_KEY_CHUNK)
    return [_dot(k[a:a + step], qt) for a in range(0, nk, step)]


def _softmax_pv_t(s, vt):
    step = s[0].shape[0]
    m = functools.reduce(jnp.maximum, [jnp.max(si, axis=0, keepdims=True) for si in s])
    p = [jnp.exp2(si - m) for si in s]
    l = sum(jnp.sum(pi, axis=0, keepdims=True) for pi in p)
    o = sum(_dot(vt[:, j * step:(j + 1) * step], pi.astype(BF16)) for j, pi in enumerate(p))
    return o / l


def _mla_kernel(qt_ref, k_ref, vt_ref, o_ref):
    sub = min(qt_ref.shape[2], ATTN_SUB_TILE)
    items = [(r, hd) for r in range(qt_ref.shape[2] // sub) for hd in range(2)]

    def logits(item):
        r, hd = item
        c = slice(hd * LANES, (hd + 1) * LANES)
        return _logits_t(qt_ref[0, c, r * sub:(r + 1) * sub], k_ref[0, :, c])

    s_next = logits(items[0])
    acc = None
    for j, (r, hd) in enumerate(items):
        s = s_next
        if j + 1 < len(items):
            s_next = logits(items[j + 1])
        o = _softmax_pv_t(s, vt_ref[0, hd * LANES:(hd + 1) * LANES, :])
        acc = o if hd == 0 else acc + o
        if hd == 1:
            o_ref[0, r * sub:(r + 1) * sub, :] = acc.T.astype(o_ref.dtype)


def _mla_attention(qmt, km, vmt, tq, q_off, n_q, k_off, n_k):
    b = km.shape[0]
    pairs = MLA_HEADS // 2
    qo, ko = q_off // tq, k_off // n_k
    return pl.pallas_call(
        _mla_kernel,
        grid=(b, pairs, n_q // tq),
        in_specs=[
            pl.BlockSpec((1, 2 * LANES, tq), lambda bi, p, i: (bi, p, qo + i)),
            pl.BlockSpec((1, n_k, 2 * LANES), lambda bi, p, i: (bi, ko, p)),
            pl.BlockSpec((1, 2 * LANES, n_k), lambda bi, p, i: (bi, p, ko)),
        ],
        out_specs=pl.BlockSpec((1, tq, LANES), lambda bi, p, i: (bi, i, p)),
        out_shape=jax.ShapeDtypeStruct((b, n_q, pairs * LANES), BF16),
        compiler_params=_params("parallel", "arbitrary", "arbitrary"),
        name="mla_attention",
    )(qmt, km, vmt)


def _diff_kernel(qt_ref, k_ref, vt_ref, lam_ref, g_ref, o_ref, *, lambda_init):
    lv = lam_ref[...]
    lam = (jnp.exp(jnp.sum(lv[0:1] * lv[1:2], axis=-1, keepdims=True))
           - jnp.exp(jnp.sum(lv[2:3] * lv[3:4], axis=-1, keepdims=True)) + lambda_init)
    vt = vt_ref[0]
    sub = min(qt_ref.shape[2], ATTN_SUB_TILE)
    items = [(r, which) for r in range(qt_ref.shape[2] // sub) for which in range(2)]

    def logits(item):
        r, which = item
        qt = qt_ref[0, :, r * sub:(r + 1) * sub]
        first = lax.broadcasted_iota(jnp.int32, qt.shape, 0) < DIFF_HEAD_DIM
        zero = jnp.zeros_like(qt)
        return _logits_t(jnp.where(first, qt, zero) if which == 0 else jnp.where(first, zero, qt), k_ref[0])

    s_next = logits(items[0])
    o1 = None
    for j, (r, which) in enumerate(items):
        s = s_next
        if j + 1 < len(items):
            s_next = logits(items[j + 1])
        o = _softmax_pv_t(s, vt)
        if which == 0:
            o1 = o
        else:
            o = (o1 - lam * o).T
            o_ref[0, r * sub:(r + 1) * sub, :] = (_rms(o, g_ref[...]) * (1.0 - lambda_init)).astype(o_ref.dtype)


def _diff_attention(dq, dk, dv, diff_lambda, subln_g, lambda_init, tq, q_off, n_q, k_off, n_k):
    b = dq.shape[0]
    qo, ko = q_off // tq, k_off // n_k
    return pl.pallas_call(
        functools.partial(_diff_kernel, lambda_init=lambda_init),
        grid=(b, DIFF_HEADS, n_q // tq),
        in_specs=[
            pl.BlockSpec((1, LANES, tq), lambda bi, h, i: (bi, h, qo + i)),
            pl.BlockSpec((1, n_k, LANES), lambda bi, h, i: (bi, ko, h)),
            pl.BlockSpec((1, LANES, n_k), lambda bi, h, i: (bi, h, ko)),
            pl.BlockSpec((4, DIFF_HEAD_DIM), lambda bi, h, i: (0, 0)),
            pl.BlockSpec((1, DIFF_V_DIM), lambda bi, h, i: (0, 0)),
        ],
        out_specs=pl.BlockSpec((1, tq, LANES), lambda bi, h, i: (bi, i, h)),
        out_shape=jax.ShapeDtypeStruct((b, n_q, DIFF_HEADS * LANES), BF16),
        compiler_params=_params("parallel", "arbitrary", "arbitrary"),
        name="diff_attention",
    )(dq, dk, dv, diff_lambda, subln_g)


def _mix_ffn_kernel(a_ref, ap_ref, an_ref, b_ref, bp_ref, bn_ref, x_ref, xp_ref, xn_ref, mod_ref,
                    wo_ref, g1_ref, g2_ref, wa_ref, wg_ref, cwa_ref, cwg_ref, wd_ref, g3_ref,
                    o_ref, lhs_ref, x1_ref, ua0_ref, ug0_ref, ua1_ref, ug1_ref, act0_ref, act1_ref, acc_ref):
    i = pl.program_id(1)
    tm = a_ref.shape[1]
    half = a_ref.shape[2]
    n_chunks = wa_ref.shape[0]
    rows = tm + 2 * HALO_ROWS
    mid = slice(HALO_ROWS, HALO_ROWS + tm)
    mod = mod_ref[0]

    ext = lambda p_ref, m_ref, n_ref: jnp.concatenate([p_ref[0], m_ref[0], n_ref[0]], axis=0)
    y = _dot(ext(ap_ref, a_ref, an_ref), wo_ref[0:half, :]) + _dot(ext(bp_ref, b_ref, bn_ref), wo_ref[half:, :])
    x1 = ext(xp_ref, x_ref, xn_ref) + mod[2:3] * _rms(y, g1_ref[...])
    h2 = _rms(x1, g2_ref[...]) * (1.0 + mod[4:5]) + mod[3:4]
    r = lax.broadcasted_iota(jnp.int32, h2.shape, 0)
    pad = ((r < HALO_ROWS) & (i == 0)) | ((r >= HALO_ROWS + tm) & (i == pl.num_programs(1) - 1))
    lhs_ref[...] = jnp.where(pad, 0.0, h2).astype(BF16)
    x1_ref[...] = x1[mid]
    slots = ((ua0_ref, ug0_ref), (ua1_ref, ug1_ref))

    def up(j, slot):
        lhs = lhs_ref[...]
        slots[slot][0][...] = _dot(lhs, wa_ref[j])
        slots[slot][1][...] = _dot(lhs, wg_ref[j])

    def conv(u, cw):
        down = pltpu.roll(u, 1, 0)[mid]
        above = pltpu.roll(u, rows - 1, 0)[mid]
        return down * cw[0:1] + u[mid] * cw[1:2] + above * cw[2:3] + cw[3:4]

    acts = (act0_ref, act1_ref)
    width = wa_ref.shape[2]

    def activate(j):
        a = conv(slots[j % 2][0][...], cwa_ref[j])
        g = conv(slots[j % 2][1][...], cwg_ref[j])
        k = j % FFN_DOWN_GROUP
        acts[(j // FFN_DOWN_GROUP) % 2][:, k * width:(k + 1) * width] = ((g * jax.nn.sigmoid(g)) * a).astype(BF16)

    def down(j_last):
        j0 = j_last - j_last % FFN_DOWN_GROUP
        k = (j_last - j0 + 1) * width
        return _dot(acts[(j_last // FFN_DOWN_GROUP) % 2][:, :k], wd_ref[j0 * width:(j_last + 1) * width, :])

    up(0, 0)
    up(1, 1)
    y = None
    for j in range(n_chunks):
        activate(j)
        if j + 2 < n_chunks:
            up(j + 2, j % 2)
        if j % FFN_DOWN_GROUP == FFN_DOWN_GROUP - 1 or j == n_chunks - 1:
            y = down(j)
            if j == n_chunks - 1:
                break
            if j < FFN_DOWN_GROUP:
                acc_ref[...] = y
            else:
                acc_ref[...] += y
    total = y if n_chunks <= FFN_DOWN_GROUP else acc_ref[...] + y
    o_ref[0] = x1_ref[...] + mod[5:6] * _rms(total, g3_ref[...])


def _mix_ffn(a, bb, x, mod, w_out, g1, g2, wa, wg, cwa, cwg, wd, g3, tm, ctx_mod):
    b, n_rows, half = a.shape
    d = x.shape[-1]
    ctx_row = mod.shape[0] - 1
    per = tm // HALO_ROWS
    last_halo = n_rows // HALO_ROWS - 1
    const = lambda *shape: pl.BlockSpec(shape, lambda bi, i: (0,) * len(shape), pipeline_mode=pl.Buffered(1))
    tile = lambda w: pl.BlockSpec((1, tm, w), lambda bi, i: (bi, i, 0))
    prev = lambda w: pl.BlockSpec((1, HALO_ROWS, w), lambda bi, i: (bi, jnp.maximum(i * per - 1, 0), 0))
    nxt = lambda w: pl.BlockSpec((1, HALO_ROWS, w), lambda bi, i: (bi, jnp.minimum((i + 1) * per, last_halo), 0))
    rows = tm + 2 * HALO_ROWS
    return pl.pallas_call(
        _mix_ffn_kernel,
        grid=(b, n_rows // tm),
        in_specs=[
            tile(half), prev(half), nxt(half), tile(half), prev(half), nxt(half), tile(d), prev(d), nxt(d),
            pl.BlockSpec((1, 6, d), lambda bi, i: (ctx_row if ctx_mod else bi, 0, 0)),
            const(*w_out.shape), const(1, d), const(1, d),
            const(*wa.shape), const(*wg.shape), const(*cwa.shape), const(*cwg.shape), const(*wd.shape),
            const(1, d),
        ],
        out_specs=tile(d),
        out_shape=jax.ShapeDtypeStruct((b, n_rows, d), F32),
        scratch_shapes=[pltpu.VMEM((rows, d), BF16), pltpu.VMEM((tm, d), F32)]
        + [pltpu.VMEM((rows, FFN_CHUNK), F32)] * 4
        + [pltpu.VMEM((tm, FFN_DOWN_GROUP * FFN_CHUNK), BF16)] * 2 + [pltpu.VMEM((tm, d), F32)],
        compiler_params=_params("parallel", "arbitrary"),
        name="mix_ffn",
    )(a, a, a, bb, bb, bb, x, x, x, mod, w_out, g1, g2, wa, wg, cwa, cwg, wd, g3)


def _proj_odd_kernel(x_ref, xp_ref, xn_ref, mod_ref, g_ref, wqkv_ref, wu_ref, cw_ref, cd_ref, sd_ref,
                     cdt_ref, sdt_ref, q_ref, k_ref, v_ref, u_ref, *, n_tiles):
    i = pl.program_id(1)
    mod = mod_ref[0]

    def normed(x):
        return (_rms(x, g_ref[...]) * (1.0 + mod[1:2]) + mod[0:1]).astype(BF16)

    h = normed(x_ref[0])
    p = _dot(h, wqkv_ref[...])
    cd, sd, cdt, sdt = cd_ref[...], sd_ref[...], cdt_ref[...], sdt_ref[...]
    scale = WIN_HEAD_DIM ** -0.5 * LOG2E
    for c in range(4):
        cs = slice(c * LANES, (c + 1) * LANES)
        q_ref[0, cs, :] = (_rope_rows(p[:, cs].T, cdt, sdt, WIN_HEAD_DIM // 2) * scale).astype(BF16)
    for c in range(2):
        cs = slice(c * LANES, (c + 1) * LANES)
        k_ref[0, :, cs] = _rope_chunk(p[:, 512 + c * LANES:512 + (c + 1) * LANES], cd, sd, WIN_HEAD_DIM // 2).astype(BF16)
        v_ref[0, cs, :] = p[:, 768 + c * LANES:768 + (c + 1) * LANES].astype(BF16).T
    prev_ok = jnp.where(i == 0, 0.0, 1.0)
    next_ok = jnp.where(i == n_tiles - 1, 0.0, 1.0)
    wu = wu_ref[...]
    u = _dot(h, wu)
    up = _dot(normed(xp_ref[0]), wu)[HALO_ROWS - 1:HALO_ROWS] * prev_ok
    un = _dot(normed(xn_ref[0]), wu)[0:1] * next_ok
    down, upw = _shift_rows(u, up, un)
    cw = cw_ref[...]
    u_ref[0] = (down * cw[0:1] + u * cw[1:2] + upw * cw[2:3] + cw[3:4]).astype(u_ref.dtype)


def _proj_odd(xcat, mod, g, wqkv, wu, cw, cd, sd, tm):
    b, rows, d = xcat.shape
    n_tiles = rows // tm
    per = tm // HALO_ROWS
    last_halo = n_tiles * per - 1
    const = lambda *shape: pl.BlockSpec(shape, lambda bi, i: (0,) * len(shape))
    tab = pl.BlockSpec((tm, LANES), lambda bi, i: (i, 0))
    tab_t = pl.BlockSpec((LANES, tm), lambda bi, i: (0, i))
    out = lambda w: pl.BlockSpec((1, tm, w), lambda bi, i: (bi, i, 0))
    out_t = lambda w: pl.BlockSpec((1, w, tm), lambda bi, i: (bi, 0, i))
    return pl.pallas_call(
        functools.partial(_proj_odd_kernel, n_tiles=n_tiles),
        grid=(b, n_tiles),
        in_specs=[
            pl.BlockSpec((1, tm, d), lambda bi, i: (bi, i, 0)),
            pl.BlockSpec((1, HALO_ROWS, d), lambda bi, i: (bi, jnp.maximum(i * per - 1, 0), 0)),
            pl.BlockSpec((1, HALO_ROWS, d), lambda bi, i: (bi, jnp.minimum((i + 1) * per, last_halo), 0)),
            pl.BlockSpec((1, 6, d), lambda bi, i: (bi, 0, 0)),
            const(1, d), const(*wqkv.shape), const(*wu.shape), const(*cw.shape), tab, tab, tab_t, tab_t,
        ],
        out_specs=[out_t(512), out(256), out_t(256), out(3 * HYENA_CH)],
        out_shape=[jax.ShapeDtypeStruct((b, 512, rows), BF16), jax.ShapeDtypeStruct((b, rows, 256), BF16),
                   jax.ShapeDtypeStruct((b, 256, rows), BF16), jax.ShapeDtypeStruct((b, rows, 3 * HYENA_CH), BF16)],
        compiler_params=_params("parallel", "arbitrary"),
        name="proj_odd",
    )(xcat, xcat, xcat, mod, g, wqkv, wu, cw, cd, sd, cd.T, sd.T)


def _proj_ctx_kv_kernel(x_ref, mod_ref, g_ref, w_ref, k_ref, v_ref):
    mod = mod_ref[0]
    h = (_rms(x_ref[0], g_ref[...]) * (1.0 + mod[1:2]) + mod[0:1]).astype(BF16)
    p = _dot(h, w_ref[...])
    k_ref[0] = p[:, :2 * LANES].astype(BF16)
    v_ref[0] = p[:, 2 * LANES:].astype(BF16).T


def _proj_ctx_kv(xcat, mod, g, wkv, ctx_tile, n_ctx):
    b, _, d = xcat.shape
    ctx_row = mod.shape[0] - 1
    out = pl.BlockSpec((1, n_ctx, 2 * LANES), lambda bi: (bi, 0, 0))
    return pl.pallas_call(
        _proj_ctx_kv_kernel,
        grid=(b,),
        in_specs=[
            pl.BlockSpec((1, n_ctx, d), lambda bi: (bi, ctx_tile, 0)),
            pl.BlockSpec((1, 6, d), lambda bi: (ctx_row, 0, 0)),
            pl.BlockSpec((1, d), lambda bi: (0, 0)),
            pl.BlockSpec(wkv.shape, lambda bi: (0, 0)),
        ],
        out_specs=[out, pl.BlockSpec((1, 2 * LANES, n_ctx), lambda bi: (bi, 0, 0))],
        out_shape=[jax.ShapeDtypeStruct((b, n_ctx, 2 * LANES), BF16), jax.ShapeDtypeStruct((b, 2 * LANES, n_ctx), BF16)],
        compiler_params=_params("parallel"),
        name="proj_ctx_kv",
    )(xcat, mod, g, wkv)


def _window_kernel(sink_ref, q_ref, kc_ref, vc_ref, kp_ref, km_ref, kn_ref, vp_ref, vm_ref, vn_ref, o_ref,
                   *, n_lat):
    i = pl.program_id(1)
    tq = q_ref.shape[2]
    n_ctx = kc_ref.shape[1]
    nk = n_ctx + tq + 2 * WINDOW
    key = lax.broadcasted_iota(jnp.int32, (nk, tq), 0)
    qry = lax.broadcasted_iota(jnp.int32, (nk, tq), 1)
    k_pos = i * tq - WINDOW + (key - n_ctx)
    q_pos = i * tq + qry
    visible = (key < n_ctx) | ((jnp.abs(q_pos - k_pos) <= WINDOW) & (k_pos >= 0) & (k_pos < n_lat))
    bias = jnp.where(visible, 0.0, NEG_INF)
    bias = jnp.concatenate([bias, bias], axis=1)
    second_head = lax.broadcasted_iota(jnp.int32, (1, 2 * tq), 1) >= tq
    first = lax.broadcasted_iota(jnp.int32, (LANES, tq), 0) < WIN_HEAD_DIM
    group = WIN_Q_HEADS // WIN_KV_HEADS
    ks, vts = [], []
    for hk in range(WIN_KV_HEADS):
        c = slice(hk * LANES, (hk + 1) * LANES)
        ks.append(jnp.concatenate([kc_ref[0, :, c], kp_ref[0, :, c], km_ref[0, :, c], kn_ref[0, :, c]], axis=0))
        half = slice(hk * LANES, hk * LANES + WIN_HEAD_DIM)
        vts.append(jnp.concatenate([vc_ref[0, half, :], vp_ref[0, half, :], vm_ref[0, half, :], vn_ref[0, half, :]], axis=1))

    def logits(chunk):
        qt = q_ref[0, chunk * LANES:(chunk + 1) * LANES, :]
        zero = jnp.zeros_like(qt)
        stacked = jnp.concatenate([jnp.where(first, qt, zero), jnp.where(first, zero, qt)], axis=1)
        return _logits_t(stacked, ks[chunk // (group // 2)])

    n_chunks = WIN_Q_HEADS // 2
    s_next = logits(0)
    step = s_next[0].shape[0]
    for chunk in range(n_chunks):
        s = [sj + bias[j * step:(j + 1) * step] for j, sj in enumerate(s_next)]
        if chunk + 1 < n_chunks:
            s_next = logits(chunk + 1)
        sink =jnp.where(second_head, sink_ref[2 * chunk + 1], sink_ref[2 * chunk]) * LOG2E
        m = functools.reduce(jnp.maximum, [jnp.max(sj, axis=0, keepdims=True) for sj in s] + [sink])
        p = [jnp.exp2(sj - m) for sj in s]
        l = sum(jnp.sum(pj, axis=0, keepdims=True) for pj in p) + jnp.exp2(sink - m)
        vt = vts[chunk // (group // 2)]
        o = sum(_dot(vt[:, j * step:(j + 1) * step], pj.astype(BF16)) for j, pj in enumerate(p)) / l
        o = jnp.concatenate([o[:, :tq], o[:, tq:]], axis=0)
        o_ref[0, :, chunk * LANES:(chunk + 1) * LANES] = o.T.astype(o_ref.dtype)


def _window_attention(qt, kd, vdt, kc, vct, sink):
    b, n_lat, w = kd.shape
    tq = ROW_TILE
    n_ctx = kc.shape[1]
    per = tq // WINDOW
    last = n_lat // WINDOW - 1
    prev_i = lambda i: jnp.maximum(i * per - 1, 0)
    next_i = lambda i: jnp.minimum((i + 1) * per, last)
    return pl.pallas_call(
        functools.partial(_window_kernel, n_lat=n_lat),
        grid=(b, n_lat // tq),
        in_specs=[
            pl.BlockSpec(memory_space=pltpu.SMEM),
            pl.BlockSpec((1, 4 * LANES, tq), lambda bi, i: (bi, 0, i)),
            pl.BlockSpec((1, n_ctx, w), lambda bi, i: (bi, 0, 0)),
            pl.BlockSpec((1, w, n_ctx), lambda bi, i: (bi, 0, 0)),
            pl.BlockSpec((1, WINDOW, w), lambda bi, i: (bi, prev_i(i), 0)),
            pl.BlockSpec((1, tq, w), lambda bi, i: (bi, i, 0)),
            pl.BlockSpec((1, WINDOW, w), lambda bi, i: (bi, next_i(i), 0)),
            pl.BlockSpec((1, w, WINDOW), lambda bi, i: (bi, 0, prev_i(i))),
            pl.BlockSpec((1, w, tq), lambda bi, i: (bi, 0, i)),
            pl.BlockSpec((1, w, WINDOW), lambda bi, i: (bi, 0, next_i(i))),
        ],
        out_specs=pl.BlockSpec((1, tq, 4 * LANES), lambda bi, i: (bi, i, 0)),
        out_shape=jax.ShapeDtypeStruct((b, n_lat, 4 * LANES), BF16),
        compiler_params=_params("parallel", "arbitrary"),
        name="window_attention",
    )(sink, qt, kc, vct, kd, kd, kd, vdt, vdt, vdt)


def _hyena_filter_kernel(feat_ref, w1_ref, b1_ref, w2_ref, b2_ref, w3_ref, b3_ref, freq_ref, decay_ref, h_ref,
                         *, n_forward):
    j = pl.program_id(0)
    hp = lax.Precision.HIGHEST
    freq = freq_ref[...]
    hid = jnp.sin(freq * (jnp.dot(feat_ref[...], w1_ref[...], precision=hp, preferred_element_type=F32) + b1_ref[...]))
    hid = jnp.sin(freq * (jnp.dot(hid, w2_ref[...], precision=hp, preferred_element_type=F32) + b2_ref[...]))
    h = jnp.dot(hid, w3_ref[...], precision=hp, preferred_element_type=F32) + b3_ref[...]
    h = h * decay_ref[...]
    row = lax.broadcasted_iota(jnp.int32, h.shape, 0)
    h_ref[...] = jnp.where((row == 0) & (j >= n_forward), 0.0, h).astype(h_ref.dtype)


def _hyena_filters(feats, w1, b1, w2, b2, w3, b3, freq, decay):
    length = feats.shape[0]
    n = w3.shape[1]
    ch = decay.shape[1]
    full = lambda a: pl.BlockSpec(a.shape, lambda j: (0, 0))
    return pl.pallas_call(
        functools.partial(_hyena_filter_kernel, n_forward=HYENA_ORDER),
        grid=(n // ch,),
        in_specs=[full(feats), full(w1), full(b1), full(w2), full(b2),
                  pl.BlockSpec((w3.shape[0], ch), lambda j: (0, j)), pl.BlockSpec((1, ch), lambda j: (0, j)),
                  full(freq), full(decay)],
        out_specs=pl.BlockSpec((length, ch), lambda j: (0, j)),
        out_shape=jax.ShapeDtypeStruct((length, n), BF16),
        compiler_params=_params("arbitrary"),
        name="hyena_filters",
    )(feats, w1, b1, w2, b2, w3, b3, freq, decay)


def _spectrum_kernel(f_ref, h_ref, re_ref, im_ref):
    j = pl.program_id(0)
    tk = re_ref.shape[0]
    n = re_ref.shape[1]
    a = _dot(f_ref[0], h_ref[...])
    re_ref[...] = a[:tk, :n] + a[:tk, n:]
    row = lax.broadcasted_iota(jnp.int32, (tk, n), 0)
    sign = jnp.where((row == 0) & (j == 0), 1.0, -1.0)
    im_ref[...] = a[tk:, :n] + sign * a[tk:, n:]


def _filter_spectrum(f3, hcat):
    nt, tk2, length = f3.shape
    n = hcat.shape[1] // 2
    tk = tk2 // 2
    out = pl.BlockSpec((tk, n), lambda j: (j, 0))
    return pl.pallas_call(
        _spectrum_kernel,
        grid=(nt,),
        in_specs=[pl.BlockSpec((1, tk2, length), lambda j: (j, 0, 0)),
                  pl.BlockSpec(hcat.shape, lambda j: (0, 0))],
        out_specs=[out, out],
        out_shape=[jax.ShapeDtypeStruct((nt * tk, n), F32)] * 2,
        compiler_params=_params("arbitrary"),
        name="filter_spectrum",
    )(f3, hcat)


def _long_conv_kernel(z_ref, x_ref, f_ref, g_ref, hre_ref, him_ref, bias_ref, o_ref, acc_ref):
    j = pl.program_id(1)
    tk = hre_ref.shape[0]
    row = lax.broadcasted_iota(jnp.int32, hre_ref.shape, 0)
    packed = (row == 0) & (j == 0)
    bias = bias_ref[...]
    h_re = hre_ref[...] + bias
    h_im = him_ref[...] + jnp.where(packed, bias, 0.0)

    @pl.when(j == 0)
    def _():
        acc_ref[...] = jnp.zeros_like(acc_ref)

    spectra = [_dot(f_ref[0], z_ref[s]) for s in range(z_ref.shape[0])]
    for s, a in enumerate(spectra):
        a_re, a_im = a[:tk], a[tk:]
        ii = a_im * h_im
        p_re = a_re * h_re - jnp.where(packed, 0.0, ii)
        p_im = jnp.where(packed, ii, a_re * h_im + a_im * h_re)
        acc_ref[s] += _dot(g_ref[0], jnp.concatenate([p_re, p_im], axis=0).astype(BF16))

    @pl.when(j == pl.num_programs(1) - 1)
    def _():
        o_ref[...] = (x_ref[...].astype(F32) * acc_ref[...]).astype(o_ref.dtype)


def _long_conv(z_src, z_col, x_src, x_col, f3, g3, hre, him, h_col, bias):
    b, length, _ = z_src.shape
    nt, tk2, _ = f3.shape
    tk = tk2 // 2
    ch = HYENA_CH
    nb = CONV_SAMPLES
    return pl.pallas_call(
        _long_conv_kernel,
        grid=(b // nb, nt),
        in_specs=[
            pl.BlockSpec((nb, length, ch), lambda bi, j: (bi, 0, z_col)),
            pl.BlockSpec((nb, length, ch), lambda bi, j: (bi, 0, x_col)),
            pl.BlockSpec((1, tk2, length), lambda bi, j: (j, 0, 0)),
            pl.BlockSpec((1, length, tk2), lambda bi, j: (j, 0, 0)),
            pl.BlockSpec((tk, ch), lambda bi, j: (j, h_col)),
            pl.BlockSpec((tk, ch), lambda bi, j: (j, h_col)),
            pl.BlockSpec((1, ch), lambda bi, j: (0, 0)),
        ],
        out_specs=pl.BlockSpec((nb, length, ch), lambda bi, j: (bi, 0, 0)),
        out_shape=jax.ShapeDtypeStruct((b, length, ch), BF16),
        scratch_shapes=[pltpu.VMEM((nb, length, ch), F32)],
        compiler_params=_params("parallel", "arbitrary"),
        name="long_conv",
    )(z_src, x_src, f3, g3, hre, him, bias)


def _axial_rope(length, rot_dim):
    rows = length // GRID_W
    row = jnp.repeat(jnp.arange(rows), GRID_W).astype(F32)
    col = jnp.tile(jnp.arange(GRID_W), rows).astype(F32)
    quarter = rot_dim // 4
    inv = ROPE_BASE ** (-jnp.arange(quarter, dtype=F32) / quarter)
    ang = jnp.concatenate([row[:, None] * inv, col[:, None] * inv], axis=-1)
    return jnp.cos(ang), jnp.sin(ang)


def _rope_tables(length, n_ctx):
    cos_m, sin_m = _axial_rope(length, MLA_ROPE)
    cos_d, sin_d = _axial_rope(length, DIFF_HEAD_DIM)
    ones = lambda w: jnp.ones((length, w), F32)
    zeros = lambda w: jnp.zeros((length, w), F32)
    cm = jnp.concatenate([ones(MLA_NOPE), cos_m, cos_m, ones(LANES - MLA_NOPE - MLA_ROPE)], axis=1)
    sm = jnp.concatenate([zeros(MLA_NOPE), -sin_m, sin_m, zeros(LANES - MLA_NOPE - MLA_ROPE)], axis=1)
    cd = jnp.concatenate([cos_d] * 4, axis=1)
    sd = jnp.concatenate([-sin_d, sin_d] * 2, axis=1)
    pad = lambda t, v: jnp.concatenate([t, jnp.full((n_ctx, LANES), v, F32)], axis=0)
    return pad(cm, 1.0), pad(sm, 0.0), pad(cd, 1.0), pad(sd, 0.0)


def _dft_matrices(length):
    n = 2 * length
    k = np.arange(length, dtype=np.int64)
    ang = 2.0 * np.pi * ((k[:, None] * k[None, :]) % n).astype(np.float64) / n
    alt = np.where(k % 2 == 0, 1.0, -1.0)
    f_re = np.cos(ang)
    f_im = -np.sin(ang)
    f_im[0, :] = alt
    g_re = (2.0 / n) * np.cos(ang.T)
    g_im = -(2.0 / n) * np.sin(ang.T)
    g_re[:, 0] = 1.0 / n
    g_im[:, 0] = alt / n
    nt = length // FREQ_TILE
    f3 = np.concatenate([f_re.reshape(nt, FREQ_TILE, length), f_im.reshape(nt, FREQ_TILE, length)], axis=1)
    g3 = np.concatenate([g_re.reshape(length, nt, FREQ_TILE), g_im.reshape(length, nt, FREQ_TILE)], axis=2)
    g3 = np.transpose(g3, (1, 0, 2))
    return jnp.asarray(f3, dtype=BF16), jnp.asarray(g3, dtype=BF16)


def _hyena_features(length):
    t = jnp.arange(length, dtype=F32)
    tn = t / max(length - 1, 1)
    bands = jnp.linspace(1e-4, HYENA_BANDS - 1, HYENA_BANDS, dtype=F32)
    ang = 2.0 * math.pi * bands[None, :] * t[:, None] / length
    feats = jnp.concatenate([tn[:, None], jnp.cos(ang), -jnp.sin(ang)], axis=-1)
    min_decay = math.log(HYENA_DECAY_TARGET) / HYENA_SLOW_PCT
    max_decay = math.log(HYENA_DECAY_TARGET) / HYENA_FAST_PCT
    deltas = jnp.abs(jnp.linspace(min_decay, max_decay, HYENA_CH, dtype=F32))
    decay = jnp.exp(-tn[:, None] * deltas[None, :])
    return feats, decay


def _pad_cols(w, left, right):
    return jnp.pad(w, ((0, 0),) * (w.ndim - 1) + ((left, right),))


def _even_weights(w_in, w_uq, w_ukv):
    d = w_in.shape[0]
    o = MLA_Q_RANK + MLA_KV_RANK
    w_kr = _pad_cols(w_in[:, o:o + MLA_ROPE], MLA_NOPE, LANES - MLA_NOPE - MLA_ROPE)
    w_in_ext = jnp.concatenate([w_in[:, :o], w_kr, w_in[:, o + MLA_ROPE:]], axis=1)
    uq = _pad_cols(w_uq.reshape(MLA_Q_RANK, MLA_HEADS, MLA_NOPE + MLA_ROPE), 0, LANES - MLA_NOPE - MLA_ROPE)
    ukv = w_ukv.reshape(MLA_KV_RANK, MLA_HEADS, MLA_NOPE + MLA_V)
    uk = _pad_cols(ukv[:, :, :MLA_NOPE], 0, LANES - MLA_NOPE)
    uv = ukv[:, :, MLA_NOPE:].reshape(MLA_KV_RANK, MLA_HEADS // 2, 2, MLA_V)
    uv = jnp.stack([_pad_cols(uv[:, :, 0], 0, LANES - MLA_V), _pad_cols(uv[:, :, 1], LANES - MLA_V, 0)], axis=2)
    w_ukv_ext = jnp.concatenate([uk.reshape(MLA_KV_RANK, -1), uv.reshape(MLA_KV_RANK, -1)], axis=1)
    return w_in_ext.astype(BF16), uq.reshape(MLA_Q_RANK, -1).astype(BF16), w_ukv_ext.astype(BF16)


def _odd_weights(w_in):
    d = w_in.shape[0]
    nq = WIN_Q_HEADS * WIN_HEAD_DIM
    nkv = WIN_KV_HEADS * WIN_HEAD_DIM
    dup = lambda w: jnp.concatenate([w.reshape(d, WIN_KV_HEADS, WIN_HEAD_DIM)] * 2, axis=-1).reshape(d, -1)
    wk = dup(w_in[:, nq:nq + nkv])
    wv = dup(w_in[:, nq + nkv:nq + 2 * nkv])
    wqkv = jnp.concatenate([w_in[:, :nq], wk, wv], axis=1).astype(BF16)
    wkv = jnp.concatenate([wk, wv], axis=1).astype(BF16)
    wu = w_in[:, nq + 2 * nkv:].astype(BF16)
    return wqkv, wkv, wu


def _ffn_weights(w_up, conv_w, conv_b, w_down):
    d = w_up.shape[0]
    nc = D_FF // FFN_CHUNK
    chunks = lambda w: jnp.transpose(w.reshape(w.shape[0], nc, FFN_CHUNK), (1, 0, 2))
    cw = jnp.concatenate([conv_w, conv_b[None, :]], axis=0)
    return (chunks(w_up[:, :D_FF]).astype(BF16), chunks(w_up[:, D_FF:]).astype(BF16),
            chunks(cw[:, :D_FF]), chunks(cw[:, D_FF:]),
            w_down.astype(BF16))


def kernel(x, c, ctx, c_ctx, ada_w, ada_b, norm_g, mix_w_out, ffn_w_up, ffn_conv_w, ffn_conv_b, ffn_w_down, even_w_in, mla_q_norm_g, mla_w_uq, mla_kv_norm_g, mla_w_ukv, diff_lambda, diff_subln_g, odd_w_in, win_sink, hy_conv_w, hy_conv_b, hy_f_w1, hy_f_b1, hy_f_w2, hy_f_b2, hy_f_w3, hy_f_b3, hy_f_freq, hy_bias):
    b, seq, d = x.shape
    n_ctx = ctx.shape[1]
    depth = ada_w.shape[0]
    assert depth == 2 and d == D_MODEL and seq % ROW_TILE == 0 and n_ctx == ROW_TILE
    n_lat_tiles = seq // ROW_TILE
    n_tiles = n_lat_tiles + 1

    pad_rows = (-(b + 1)) % 8
    cond = jnp.concatenate([c, c_ctx[None, :], jnp.zeros((pad_rows, d), F32)], axis=0)
    mod = _ada_table(cond, ada_w, ada_b).reshape(depth, b + 1 + pad_rows, 6, d)[:, :b + 1]
    g = norm_g.reshape(depth, 4, 1, d)
    cm, sm, cd, sd = _rope_tables(seq, n_ctx)

    lambda_init = 0.8 - 0.6 * math.exp(-0.3 * 0)
    w_in_e, w_uq_e, w_ukv_e = _even_weights(even_w_in[0], mla_w_uq[0], mla_w_ukv[0])
    rows = seq + n_ctx
    even_w = (g[0, 0], w_in_e, mla_q_norm_g[0][None, :], w_uq_e, mla_kv_norm_g[0][None, :], w_ukv_e)
    proj = _proj_even(x, mod[0], even_w, (cm, sm, cd, sd), rows, PROJ_ROW_TILE, 0, False)
    qm, km, vm, dq, dk, dv = _proj_even(ctx, mod[0], even_w, (cm, sm, cd, sd), rows, n_ctx, seq, True, prior=proj)
    lam_args = (diff_lambda[0], diff_subln_g[0][None, :], lambda_init)
    w_out0 = mix_w_out[0].astype(BF16)
    ffn0 = _ffn_weights(ffn_w_up[0], ffn_conv_w[0], ffn_conv_b[0], ffn_w_down[0])
    o_mla = _mla_attention(qm, km, vm, ATTN_Q_TILE, 0, seq, 0, rows)
    o_diff = _diff_attention(dq, dk, dv, *lam_args, ATTN_Q_TILE, 0, seq, 0, rows)
    x_lat = _mix_ffn(o_mla, o_diff, x, mod[0], w_out0, g[0, 1], g[0, 2], *ffn0, g[0, 3], FFN_ROW_TILE, False)
    o_mla = _mla_attention(qm, km, vm, n_ctx, seq, n_ctx, seq, n_ctx)
    o_diff = _diff_attention(dq, dk, dv, *lam_args, n_ctx, seq, n_ctx, seq, n_ctx)
    x_ctx = _mix_ffn(o_mla, o_diff, ctx, mod[0], w_out0, g[0, 1], g[0, 2], *ffn0, g[0, 3], n_ctx, True)

    wqkv, wkv, wu = _odd_weights(odd_w_in[0])
    cw_hy = jnp.concatenate([hy_conv_w[0], hy_conv_b[0][None, :]], axis=0)
    q, kd, vd, u = _proj_odd(x_lat, mod[1], g[1, 0], wqkv, wu, cw_hy, cd, sd, PROJ_ROW_TILE)
    kc, vc = _proj_ctx_kv(x_ctx, mod[1], g[1, 0], wkv, 0, n_ctx)
    o_win = _window_attention(q, kd, vd, kc, vc, win_sink[0])

    feats, decay = _hyena_features(seq)
    emb = feats.shape[1]
    feats = _pad_cols(feats, 0, LANES - emb)
    w1 = jnp.pad(hy_f_w1[0], ((0, LANES - emb), (0, 0)))
    hcat = _hyena_filters(feats, w1, hy_f_b1[0][None, :], hy_f_w2[0], hy_f_b2[0][None, :], hy_f_w3[0],
                          hy_f_b3[0][None, :], hy_f_freq[0][None, :], decay)
    f3, g3 = _dft_matrices(seq)
    hre, him = _filter_spectrum(f3, hcat)
    z = _long_conv(u, 0, u, 1, f3, g3, hre, him, 0, hy_bias[0, 0][None, :])
    z = _long_conv(z, 0, u, 2, f3, g3, hre, him, 1, hy_bias[0, 1][None, :])

    w_out1 = mix_w_out[1].astype(BF16)
    ffn1 = _ffn_weights(ffn_w_up[1], ffn_conv_w[1], ffn_conv_b[1], ffn_w_down[1])
    return _mix_ffn(o_win, z, x_lat, mod[1], w_out1, g[1, 1], g[1, 2], *ffn1, g[1, 3], FFN_ROW_TILE, False)
```

```python
import functools
import math

import numpy as np
import jax
import jax.numpy as jnp
from jax import lax
from jax.experimental import pallas as pl
from jax.experimental.pallas import tpu as pltpu

D_MODEL = 1024
GRID_W = 64
MLA_HEADS = 8
MLA_NOPE = 64
MLA_ROPE = 32
MLA_V = 64
MLA_Q_RANK = 384
MLA_KV_RANK = 256
DIFF_HEADS = 4
DIFF_HEAD_DIM = 64
DIFF_V_DIM = 128
WIN_Q_HEADS = 8
WIN_KV_HEADS = 2
WIN_HEAD_DIM = 64
WINDOW = 128
HYENA_CH = 512
HYENA_ORDER = 2
HYENA_BANDS = 16
HYENA_DECAY_TARGET = 1e-2
HYENA_FAST_PCT = 0.3
HYENA_SLOW_PCT = 1.5
D_FF = 2816
ROPE_BASE = 10000.0
NORM_EPS = 1e-6
NEG_INF = -1e30

LANES = 128
HALO_ROWS = 16
ROW_TILE = 256
ATTN_Q_TILE = 1024
ATTN_SUB_TILE = 512
ATTN_KEY_CHUNK = 256
LOG2E = math.log2(math.e)
FFN_ROW_TILE = 512
PROJ_ROW_TILE = 512
FFN_CHUNK = 256
FFN_DOWN_GROUP = 4
FREQ_TILE = 256
CONV_SAMPLES = 2
VMEM_LIMIT = 48 * 1024 * 1024

BF16 = jnp.bfloat16
F32 = jnp.float32


def _params(*semantics):
    return pltpu.CompilerParams(dimension_semantics=semantics, vmem_limit_bytes=VMEM_LIMIT)


def _rms(x, g):
    return x * lax.rsqrt(jnp.mean(x * x, axis=-1, keepdims=True) + NORM_EPS) * g


def _dot(a, b):
    return jnp.dot(a, b, preferred_element_type=F32)


def _lane_iota(shape):
    return lax.broadcasted_iota(jnp.int32, shape, len(shape) - 1)


def _rope_chunk(x, cos, sin_signed, half):
    first = (_lane_iota(x.shape) % (2 * half)) < half
    partner = jnp.where(first, pltpu.roll(x, LANES - half, 1), pltpu.roll(x, half, 1))
    return x * cos + partner * sin_signed


def _rope_rows(xt, cos_t, sin_signed_t, half):
    first = (lax.broadcasted_iota(jnp.int32, xt.shape, 0) % (2 * half)) < half
    partner = jnp.where(first, pltpu.roll(xt, LANES - half, 0), pltpu.roll(xt, half, 0))
    return xt * cos_t + partner * sin_signed_t


def _shift_rows(u, prev_row, next_row):
    rows = u.shape[0]
    r = lax.broadcasted_iota(jnp.int32, u.shape, 0)
    down = jnp.where(r == 0, prev_row, pltpu.roll(u, 1, 0))
    up = jnp.where(r == rows - 1, next_row, pltpu.roll(u, rows - 1, 0))
    return down, up


def _ada_kernel(c_ref, w_ref, b_ref, o_ref):
    c = c_ref[...]
    sc = c * jax.nn.sigmoid(c)
    o_ref[0] = _dot(sc.astype(BF16), w_ref[0].astype(BF16)) + b_ref[0]


def _ada_table(cond, ada_w, ada_b):
    depth, d, n = ada_w.shape
    rows = cond.shape[0]
    tn = 1536
    return pl.pallas_call(
        _ada_kernel,
        grid=(depth, n // tn),
        in_specs=[
            pl.BlockSpec((rows, d), lambda l, j: (0, 0)),
            pl.BlockSpec((1, d, tn), lambda l, j: (l, 0, j)),
            pl.BlockSpec((1, 1, tn), lambda l, j: (l, 0, j)),
        ],
        out_specs=pl.BlockSpec((1, rows, tn), lambda l, j: (l, 0, j)),
        out_shape=jax.ShapeDtypeStruct((depth, rows, n), F32),
        compiler_params=_params("arbitrary", "arbitrary"),
        name="ada_table",
    )(cond, ada_w, ada_b.reshape(depth, 1, n))


def _proj_even_kernel(x_ref, mod_ref, g_ref, win_ref, qg_ref, wuq_ref, kvg_ref, wukv_ref,
                      cm_ref, sm_ref, cd_ref, sd_ref, cmt_ref, smt_ref, cdt_ref, sdt_ref, *rest):
    qm_ref, km_ref, vm_ref, dq_ref, dk_ref, dv_ref = rest[-6:]
    x = x_ref[0]
    mod = mod_ref[0]
    h = _rms(x, g_ref[...]) * (1.0 + mod[1:2]) + mod[0:1]
    p = _dot(h.astype(BF16), win_ref[...])
    o = 0
    cq = p[:, o:o + MLA_Q_RANK]; o += MLA_Q_RANK
    ckv = p[:, o:o + MLA_KV_RANK]; o += MLA_KV_RANK
    kr = p[:, o:o + LANES]; o += LANES
    dq = p[:, o:o + 512]; o += 512
    dk = p[:, o:o + 512]; o += 512
    dv = p[:, o:o + 512]
    q = _dot(_rms(cq, qg_ref[...]).astype(BF16), wuq_ref[...])
    kv = _dot(_rms(ckv, kvg_ref[...]).astype(BF16), wukv_ref[...])
    cm, sm, cd, sd = cm_ref[...], sm_ref[...], cd_ref[...], sd_ref[...]
    cmt, smt, cdt, sdt = cmt_ref[...], smt_ref[...], cdt_ref[...], sdt_ref[...]
    mla_scale = (MLA_NOPE + MLA_ROPE) ** -0.5 * LOG2E
    diff_scale = DIFF_HEAD_DIM ** -0.5 * LOG2E
    kr = _rope_chunk(kr, cm, sm, MLA_ROPE // 2)
    for hd in range(MLA_HEADS):
        c = slice(hd * LANES, (hd + 1) * LANES)
        qm_ref[0, c, :] = (_rope_rows(q[:, c].T, cmt, smt, MLA_ROPE // 2) * mla_scale).astype(BF16)
        km_ref[0, :, c] = (kv[:, c] + kr).astype(BF16)
        vm_ref[0, c, :] = kv[:, MLA_HEADS * LANES + hd * LANES:MLA_HEADS * LANES + (hd + 1) * LANES].astype(BF16).T
    for hd in range(DIFF_HEADS):
        c = slice(hd * LANES, (hd + 1) * LANES)
        dq_ref[0, c, :] = (_rope_rows(dq[:, c].T, cdt, sdt, DIFF_HEAD_DIM // 2) * diff_scale).astype(BF16)
        dk_ref[0, :, c] = _rope_chunk(dk[:, c], cd, sd, DIFF_HEAD_DIM // 2).astype(BF16)
        dv_ref[0, c, :] = dv[:, c].astype(BF16).T


def _proj_even(x, mod, weights, tables, rows, tm, row_off, ctx_mod, prior=None):
    b, n, d = x.shape
    off = row_off // tm
    ctx_row = mod.shape[0] - 1
    const = lambda a: pl.BlockSpec(a.shape, lambda bi, i: (0,) * a.ndim)
    tab = pl.BlockSpec((tm, LANES), lambda bi, i: (off + i, 0))
    out = lambda w: pl.BlockSpec((1, tm, w), lambda bi, i: (bi, off + i, 0))
    shp = lambda w: jax.ShapeDtypeStruct((b, rows, w), BF16)
    out_t = lambda w: pl.BlockSpec((1, w, tm), lambda bi, i: (bi, 0, off + i))
    shp_t = lambda w: jax.ShapeDtypeStruct((b, w, rows), BF16)
    tab_t = pl.BlockSpec((LANES, tm), lambda bi, i: (0, off + i))
    tables_t = [t.T for t in tables]
    n_in = 2 + len(weights) + 2 * len(tables)
    prior = () if prior is None else tuple(prior)
    return pl.pallas_call(
        _proj_even_kernel,
        grid=(b, n // tm),
        in_specs=[
            pl.BlockSpec((1, tm, d), lambda bi, i: (bi, i, 0)),
            pl.BlockSpec((1, 6, d), lambda bi, i: (ctx_row if ctx_mod else bi, 0, 0)),
            *[const(w) for w in weights], *[tab for _ in tables], *[tab_t for _ in tables],
            *[pl.BlockSpec(memory_space=pl.ANY) for _ in prior],
        ],
        out_specs=[out_t(1024), out(1024), out_t(1024), out_t(512), out(512), out_t(512)],
        out_shape=[shp_t(1024), shp(1024), shp_t(1024), shp_t(512), shp(512), shp_t(512)],
        input_output_aliases={n_in + k: k for k in range(len(prior))},
        compiler_params=_params("parallel", "arbitrary"),
        name="proj_even",
    )(x, mod, *weights, *tables, *tables_t, *prior)


def _attend_t(qt, k, vt):
    return _softmax_pv_t(_logits_t(qt, k), vt)


def _logits_t(qt, k):
    nk = k.shape[0]
    step = min(nk, ATTN_KEY_CHUNK)
    return [_dot(k[a:a + step], qt) for a in range(0, nk, step)]


def _softmax_pv_t(s, vt):
    step = s[0].shape[0]
    m = functools.reduce(jnp.maximum, [jnp.max(si, axis=0, keepdims=True) for si in s])
    p = [jnp.exp2(si - m) for si in s]
    l = sum(jnp.sum(pi, axis=0, keepdims=True) for pi in p)
    o = sum(_dot(vt[:, j * step:(j + 1) * step], pi.astype(BF16)) for j, pi in enumerate(p))
    return o / l


def _mla_kernel(qt_ref, k_ref, vt_ref, o_ref):
    sub = min(qt_ref.shape[2], ATTN_SUB_TILE)
    items = [(r, hd) for r in range(qt_ref.shape[2] // sub) for hd in range(2)]

    def logits(item):
        r, hd = item
        c = slice(hd * LANES, (hd + 1) * LANES)
        return _logits_t(qt_ref[0, c, r * sub:(r + 1) * sub], k_ref[0, :, c])

    s_next = logits(items[0])
    acc = None
    for j, (r, hd) in enumerate(items):
        s = s_next
        if j + 1 < len(items):
            s_next = logits(items[j + 1])
        o = _softmax_pv_t(s, vt_ref[0, hd * LANES:(hd + 1) * LANES, :])
        acc = o if hd == 0 else acc + o
        if hd == 1:
            o_ref[0, r * sub:(r + 1) * sub, :] = acc.T.astype(o_ref.dtype)


def _mla_attention(qmt, km, vmt, tq, q_off, n_q, k_off, n_k):
    b = km.shape[0]
    pairs = MLA_HEADS // 2
    qo, ko = q_off // tq, k_off // n_k
    return pl.pallas_call(
        _mla_kernel,
        grid=(b, pairs, n_q // tq),
        in_specs=[
            pl.BlockSpec((1, 2 * LANES, tq), lambda bi, p, i: (bi, p, qo + i)),
            pl.BlockSpec((1, n_k, 2 * LANES), lambda bi, p, i: (bi, ko, p)),
            pl.BlockSpec((1, 2 * LANES, n_k), lambda bi, p, i: (bi, p, ko)),
        ],
        out_specs=pl.BlockSpec((1, tq, LANES), lambda bi, p, i: (bi, i, p)),
        out_shape=jax.ShapeDtypeStruct((b, n_q, pairs * LANES), BF16),
        compiler_params=_params("parallel", "arbitrary", "arbitrary"),
        name="mla_attention",
    )(qmt, km, vmt)


def _diff_kernel(qt_ref, k_ref, vt_ref, lam_ref, g_ref, o_ref, *, lambda_init):
    lv = lam_ref[...]
    lam = (jnp.exp(jnp.sum(lv[0:1] * lv[1:2], axis=-1, keepdims=True))
           - jnp.exp(jnp.sum(lv[2:3] * lv[3:4], axis=-1, keepdims=True)) + lambda_init)
    vt = vt_ref[0]
    sub = min(qt_ref.shape[2], ATTN_SUB_TILE)
    items = [(r, which) for r in range(qt_ref.shape[2] // sub) for which in range(2)]

    def logits(item):
        r, which = item
        qt = qt_ref[0, :, r * sub:(r + 1) * sub]
        first = lax.broadcasted_iota(jnp.int32, qt.shape, 0) < DIFF_HEAD_DIM
        zero = jnp.zeros_like(qt)
        return _logits_t(jnp.where(first, qt, zero) if which == 0 else jnp.where(first, zero, qt), k_ref[0])

    s_next = logits(items[0])
    o1 = None
    for j, (r, which) in enumerate(items):
        s = s_next
        if j + 1 < len(items):
            s_next = logits(items[j + 1])
        o = _softmax_pv_t(s, vt)
        if which == 0:
            o1 = o
        else:
            o = (o1 - lam * o).T
            o_ref[0, r * sub:(r + 1) * sub, :] = (_rms(o, g_ref[...]) * (1.0 - lambda_init)).astype(o_ref.dtype)


def _diff_attention(dq, dk, dv, diff_lambda, subln_g, lambda_init, tq, q_off, n_q, k_off, n_k):
    b = dq.shape[0]
    qo, ko = q_off // tq, k_off // n_k
    return pl.pallas_call(
        functools.partial(_diff_kernel, lambda_init=lambda_init),
        grid=(b, DIFF_HEADS, n_q // tq),
        in_specs=[
            pl.BlockSpec((1, LANES, tq), lambda bi, h, i: (bi, h, qo + i)),
            pl.BlockSpec((1, n_k, LANES), lambda bi, h, i: (bi, ko, h)),
            pl.BlockSpec((1, LANES, n_k), lambda bi, h, i: (bi, h, ko)),
            pl.BlockSpec((4, DIFF_HEAD_DIM), lambda bi, h, i: (0, 0)),
            pl.BlockSpec((1, DIFF_V_DIM), lambda bi, h, i: (0, 0)),
        ],
        out_specs=pl.BlockSpec((1, tq, LANES), lambda bi, h, i: (bi, i, h)),
        out_shape=jax.ShapeDtypeStruct((b, n_q, DIFF_HEADS * LANES), BF16),
        compiler_params=_params("parallel", "arbitrary", "arbitrary"),
        name="diff_attention",
    )(dq, dk, dv, diff_lambda, subln_g)


def _mix_ffn_kernel(a_ref, ap_ref, an_ref, b_ref, bp_ref, bn_ref, x_ref, xp_ref, xn_ref, mod_ref,
                    wo_ref, g1_ref, g2_ref, wa_ref, wg_ref, cwa_ref, cwg_ref, wd_ref, g3_ref,
                    o_ref, lhs_ref, x1_ref, ua0_ref, ug0_ref, ua1_ref, ug1_ref, act0_ref, act1_ref, acc_ref):
    i = pl.program_id(1)
    tm = a_ref.shape[1]
    half = a_ref.shape[2]
    n_chunks = wa_ref.shape[0]
    rows = tm + 2 * HALO_ROWS
    mid = slice(HALO_ROWS, HALO_ROWS + tm)
    mod = mod_ref[0]

    ext = lambda p_ref, m_ref, n_ref: jnp.concatenate([p_ref[0], m_ref[0], n_ref[0]], axis=0)
    y = _dot(ext(ap_ref, a_ref, an_ref), wo_ref[0:half, :]) + _dot(ext(bp_ref, b_ref, bn_ref), wo_ref[half:, :])
    x1 = ext(xp_ref, x_ref, xn_ref) + mod[2:3] * _rms(y, g1_ref[...])
    h2 = _rms(x1, g2_ref[...]) * (1.0 + mod[4:5]) + mod[3:4]
    r = lax.broadcasted_iota(jnp.int32, h2.shape, 0)
    pad = ((r < HALO_ROWS) & (i == 0)) | ((r >= HALO_ROWS + tm) & (i == pl.num_programs(1) - 1))
    lhs_ref[...] = jnp.where(pad, 0.0, h2).astype(BF16)
    x1_ref[...] = x1[mid]
    slots = ((ua0_ref, ug0_ref), (ua1_ref, ug1_ref))

    def up(j, slot):
        lhs = lhs_ref[...]
        slots[slot][0][...] = _dot(lhs, wa_ref[j])
        slots[slot][1][...] = _dot(lhs, wg_ref[j])

    def conv(u, cw):
        down = pltpu.roll(u, 1, 0)[mid]
        above = pltpu.roll(u, rows - 1, 0)[mid]
        return down * cw[0:1] + u[mid] * cw[1:2] + above * cw[2:3] + cw[3:4]

    acts = (act0_ref, act1_ref)
    width = wa_ref.shape[2]

    def activate(j):
        a = conv(slots[j % 2][0][...], cwa_ref[j])
        g = conv(slots[j % 2][1][...], cwg_ref[j])
        k = j % FFN_DOWN_GROUP
        acts[(j // FFN_DOWN_GROUP) % 2][:, k * width:(k + 1) * width] = ((g * jax.nn.sigmoid(g)) * a).astype(BF16)

    def down(j_last):
        j0 = j_last - j_last % FFN_DOWN_GROUP
        k = (j_last - j0 + 1) * width
        return _dot(acts[(j_last // FFN_DOWN_GROUP) % 2][:, :k], wd_ref[j0 * width:(j_last + 1) * width, :])

    up(0, 0)
    up(1, 1)
    y = None
    for j in range(n_chunks):
        activate(j)
        if j + 2 < n_chunks:
            up(j + 2, j % 2)
        if j % FFN_DOWN_GROUP == FFN_DOWN_GROUP - 1 or j == n_chunks - 1:
            y = down(j)
            if j == n_chunks - 1:
                break
            if j < FFN_DOWN_GROUP:
                acc_ref[...] = y
            else:
                acc_ref[...] += y
    total = y if n_chunks <= FFN_DOWN_GROUP else acc_ref[...] + y
    o_ref[0] = x1_ref[...] + mod[5:6] * _rms(total, g3_ref[...])


def _mix_ffn(a, bb, x, mod, w_out, g1, g2, wa, wg, cwa, cwg, wd, g3, tm, ctx_mod):
    b, n_rows, half = a.shape
    d = x.shape[-1]
    ctx_row = mod.shape[0] - 1
    per = tm // HALO_ROWS
    last_halo = n_rows // HALO_ROWS - 1
    const = lambda *shape: pl.BlockSpec(shape, lambda bi, i: (0,) * len(shape), pipeline_mode=pl.Buffered(1))
    tile = lambda w: pl.BlockSpec((1, tm, w), lambda bi, i: (bi, i, 0))
    prev = lambda w: pl.BlockSpec((1, HALO_ROWS, w), lambda bi, i: (bi, jnp.maximum(i * per - 1, 0), 0))
    nxt = lambda w: pl.BlockSpec((1, HALO_ROWS, w), lambda bi, i: (bi, jnp.minimum((i + 1) * per, last_halo), 0))
    rows = tm + 2 * HALO_ROWS
    return pl.pallas_call(
        _mix_ffn_kernel,
        grid=(b, n_rows // tm),
        in_specs=[
            tile(half), prev(half), nxt(half), tile(half), prev(half), nxt(half), tile(d), prev(d), nxt(d),
            pl.BlockSpec((1, 6, d), lambda bi, i: (ctx_row if ctx_mod else bi, 0, 0)),
            const(*w_out.shape), const(1, d), const(1, d),
            const(*wa.shape), const(*wg.shape), const(*cwa.shape), const(*cwg.shape), const(*wd.shape),
            const(1, d),
        ],
        out_specs=tile(d),
        out_shape=jax.ShapeDtypeStruct((b, n_rows, d), F32),
        scratch_shapes=[pltpu.VMEM((rows, d), BF16), pltpu.VMEM((tm, d), F32)]
        + [pltpu.VMEM((rows, FFN_CHUNK), F32)] * 4
        + [pltpu.VMEM((tm, FFN_DOWN_GROUP * FFN_CHUNK), BF16)] * 2 + [pltpu.VMEM((tm, d), F32)],
        compiler_params=_params("parallel", "arbitrary"),
        name="mix_ffn",
    )(a, a, a, bb, bb, bb, x, x, x, mod, w_out, g1, g2, wa, wg, cwa, cwg, wd, g3)


def _proj_odd_kernel(x_ref, xp_ref, xn_ref, mod_ref, g_ref, wqkv_ref, wu_ref, cw_ref, cd_ref, sd_ref,
                     q_ref, k_ref, v_ref, u_ref, *, n_tiles):
    i = pl.program_id(1)
    mod = mod_ref[0]

    def normed(x):
        return (_rms(x, g_ref[...]) * (1.0 + mod[1:2]) + mod[0:1]).astype(BF16)

    h = normed(x_ref[0])
    p = _dot(h, wqkv_ref[...])
    cd, sd = cd_ref[...], sd_ref[...]
    scale = WIN_HEAD_DIM ** -0.5 * LOG2E
    for c in range(4):
        cs = slice(c * LANES, (c + 1) * LANES)
        q_ref[0, :, cs] = (_rope_chunk(p[:, cs], cd, sd, WIN_HEAD_DIM // 2) * scale).astype(BF16)
    for c in range(2):
        cs = slice(512 + c * LANES, 512 + (c + 1) * LANES)
        k_ref[0, c * LANES:(c + 1) * LANES, :] = _rope_chunk(p[:, cs], cd, sd, WIN_HEAD_DIM // 2).astype(BF16).T
    v_ref[0] = p[:, 768:1024].astype(BF16)
    prev_ok = jnp.where(i == 0, 0.0, 1.0)
    next_ok = jnp.where(i == n_tiles - 1, 0.0, 1.0)
    wu = wu_ref[...]
    u = _dot(h, wu)
    up = _dot(normed(xp_ref[0]), wu)[HALO_ROWS - 1:HALO_ROWS] * prev_ok
    un = _dot(normed(xn_ref[0]), wu)[0:1] * next_ok
    down, upw = _shift_rows(u, up, un)
    cw = cw_ref[...]
    u_ref[0] = (down * cw[0:1] + u * cw[1:2] + upw * cw[2:3] + cw[3:4]).astype(u_ref.dtype)


def _proj_odd(xcat, mod, g, wqkv, wu, cw, cd, sd, tm):
    b, rows, d = xcat.shape
    n_tiles = rows // tm
    per = tm // HALO_ROWS
    last_halo = n_tiles * per - 1
    const = lambda *shape: pl.BlockSpec(shape, lambda bi, i: (0,) * len(shape))
    tab = pl.BlockSpec((tm, LANES), lambda bi, i: (i, 0))
    out = lambda w: pl.BlockSpec((1, tm, w), lambda bi, i: (bi, i, 0))
    return pl.pallas_call(
        functools.partial(_proj_odd_kernel, n_tiles=n_tiles),
        grid=(b, n_tiles),
        in_specs=[
            pl.BlockSpec((1, tm, d), lambda bi, i: (bi, i, 0)),
            pl.BlockSpec((1, HALO_ROWS, d), lambda bi, i: (bi, jnp.maximum(i * per - 1, 0), 0)),
            pl.BlockSpec((1, HALO_ROWS, d), lambda bi, i: (bi, jnp.minimum((i + 1) * per, last_halo), 0)),
            pl.BlockSpec((1, 6, d), lambda bi, i: (bi, 0, 0)),
            const(1, d), const(*wqkv.shape), const(*wu.shape), const(*cw.shape), tab, tab,
        ],
        out_specs=[out(512), pl.BlockSpec((1, 2 * LANES, tm), lambda bi, i: (bi, 0, i)), out(256),
                   out(3 * HYENA_CH)],
        out_shape=[jax.ShapeDtypeStruct((b, rows, 512), BF16), jax.ShapeDtypeStruct((b, 2 * LANES, rows), BF16),
                   jax.ShapeDtypeStruct((b, rows, 256), BF16), jax.ShapeDtypeStruct((b, rows, 3 * HYENA_CH), BF16)],
        compiler_params=_params("parallel", "arbitrary"),
        name="proj_odd",
    )(xcat, xcat, xcat, mod, g, wqkv, wu, cw, cd, sd)


def _proj_ctx_kv_kernel(x_ref, mod_ref, g_ref, w_ref, k_ref, v_ref):
    mod = mod_ref[0]
    h = (_rms(x_ref[0], g_ref[...]) * (1.0 + mod[1:2]) + mod[0:1]).astype(BF16)
    p = _dot(h, w_ref[...])
    k_ref[0] = p[:, :2 * LANES].astype(BF16).T
    v_ref[0] = p[:, 2 * LANES:].astype(BF16)


def _proj_ctx_kv(xcat, mod, g, wkv, ctx_tile, n_ctx):
    b, _, d = xcat.shape
    ctx_row = mod.shape[0] - 1
    out = pl.BlockSpec((1, n_ctx, 2 * LANES), lambda bi: (bi, 0, 0))
    return pl.pallas_call(
        _proj_ctx_kv_kernel,
        grid=(b,),
        in_specs=[
            pl.BlockSpec((1, n_ctx, d), lambda bi: (bi, ctx_tile, 0)),
            pl.BlockSpec((1, 6, d), lambda bi: (ctx_row, 0, 0)),
            pl.BlockSpec((1, d), lambda bi: (0, 0)),
            pl.BlockSpec(wkv.shape, lambda bi: (0, 0)),
        ],
        out_specs=[pl.BlockSpec((1, 2 * LANES, n_ctx), lambda bi: (bi, 0, 0)), out],
        out_shape=[jax.ShapeDtypeStruct((b, 2 * LANES, n_ctx), BF16), jax.ShapeDtypeStruct((b, n_ctx, 2 * LANES), BF16)],
        compiler_params=_params("parallel"),
        name="proj_ctx_kv",
    )(xcat, mod, g, wkv)


def _window_kernel(sink_ref, q_ref, kc_ref, vc_ref, kp_ref, km_ref, kn_ref, vp_ref, vm_ref, vn_ref, o_ref,
                   *, n_lat):
    i = pl.program_id(1)
    tq = q_ref.shape[1]
    n_ctx = vc_ref.shape[1]
    nk = n_ctx + tq + 2 * WINDOW
    col = lax.broadcasted_iota(jnp.int32, (tq, nk), 1)
    row = lax.broadcasted_iota(jnp.int32, (tq, nk), 0)
    k_pos = i * tq - WINDOW + (col - n_ctx)
    q_pos = i * tq + row
    visible = (col < n_ctx) | ((jnp.abs(q_pos - k_pos) <= WINDOW) & (k_pos >= 0) & (k_pos < n_lat))
    bias = jnp.where(visible, 0.0, NEG_INF)
    group = WIN_Q_HEADS // WIN_KV_HEADS
    bias = jnp.concatenate([bias, bias], axis=0)
    second_head = lax.broadcasted_iota(jnp.int32, (2 * tq, 1), 0) >= tq
    first = _lane_iota((tq, LANES)) < WIN_HEAD_DIM
    kts, vs = [], []
    for hk in range(WIN_KV_HEADS):
        c = slice(hk * LANES, (hk + 1) * LANES)
        kts.append(jnp.concatenate([kc_ref[0, c, :], kp_ref[0, c, :], km_ref[0, c, :], kn_ref[0, c, :]], axis=1))
        vs.append(jnp.concatenate([vc_ref[0, :, c], vp_ref[0, :, c], vm_ref[0, :, c], vn_ref[0, :, c]], axis=0))

    def logits(chunk):
        q = q_ref[0, :, chunk * LANES:(chunk + 1) * LANES]
        zero = jnp.zeros_like(q)
        stacked = jnp.concatenate([jnp.where(first, q, zero), jnp.where(first, zero, q)], axis=0)
        return _dot(stacked, kts[chunk // (group // 2)])

    n_chunks = WIN_Q_HEADS // 2
    s_next = logits(0)
    for chunk in range(n_chunks):
        s = s_next + bias
        if chunk + 1 < n_chunks:
            s_next = logits(chunk + 1)
        sink = jnp.where(second_head, sink_ref[2 * chunk + 1], sink_ref[2 * chunk]) * LOG2E
        m = jnp.maximum(jnp.max(s, axis=-1, keepdims=True), sink)
        p = jnp.exp2(s - m)
        l = jnp.sum(p, axis=-1, keepdims=True) + jnp.exp2(sink - m)
        o = _dot(p.astype(BF16), vs[chunk // (group // 2)]) / l
        o_ref[0, :, chunk * LANES:(chunk + 1) * LANES] = jnp.where(first, o[:tq], o[tq:]).astype(o_ref.dtype)


def _window_attention(q, kdt, vd, kct, vc, sink):
    b, n_lat, _ = q.shape
    tq = ROW_TILE
    n_ctx = vc.shape[1]
    per = tq // WINDOW
    last = n_lat // WINDOW - 1
    prev_i = lambda i: jnp.maximum(i * per - 1, 0)
    next_i = lambda i: jnp.minimum((i + 1) * per, last)
    w = 2 * LANES
    return pl.pallas_call(
        functools.partial(_window_kernel, n_lat=n_lat),
        grid=(b, n_lat // tq),
        in_specs=[
            pl.BlockSpec(memory_space=pltpu.SMEM),
            pl.BlockSpec((1, tq, 4 * LANES), lambda bi, i: (bi, i, 0)),
            pl.BlockSpec((1, w, n_ctx), lambda bi, i: (bi, 0, 0)),
            pl.BlockSpec((1, n_ctx, w), lambda bi, i: (bi, 0, 0)),
            pl.BlockSpec((1, w, WINDOW), lambda bi, i: (bi, 0, prev_i(i))),
            pl.BlockSpec((1, w, tq), lambda bi, i: (bi, 0, i)),
            pl.BlockSpec((1, w, WINDOW), lambda bi, i: (bi, 0, next_i(i))),
            pl.BlockSpec((1, WINDOW, w), lambda bi, i: (bi, prev_i(i), 0)),
            pl.BlockSpec((1, tq, w), lambda bi, i: (bi, i, 0)),
            pl.BlockSpec((1, WINDOW, w), lambda bi, i: (bi, next_i(i), 0)),
        ],
        out_specs=pl.BlockSpec((1, tq, 4 * LANES), lambda bi, i: (bi, i, 0)),
        out_shape=jax.ShapeDtypeStruct((b, n_lat, 4 * LANES), BF16),
        compiler_params=_params("parallel", "arbitrary"),
        name="window_attention",
    )(sink, q, kct, vc, kdt, kdt, kdt, vd, vd, vd)


def _hyena_filter_kernel(feat_ref, w1_ref, b1_ref, w2_ref, b2_ref, w3_ref, b3_ref, freq_ref, decay_ref, h_ref,
                         *, n_forward):
    j = pl.program_id(0)
    hp = lax.Precision.HIGHEST
    freq = freq_ref[...]
    hid = jnp.sin(freq * (jnp.dot(feat_ref[...], w1_ref[...], precision=hp, preferred_element_type=F32) + b1_ref[...]))
    hid = jnp.sin(freq * (jnp.dot(hid, w2_ref[...], precision=hp, preferred_element_type=F32) + b2_ref[...]))
    h = jnp.dot(hid, w3_ref[...], precision=hp, preferred_element_type=F32) + b3_ref[...]
    h = h * decay_ref[...]
    row = lax.broadcasted_iota(jnp.int32, h.shape, 0)
    h_ref[...] = jnp.where((row == 0) & (j >= n_forward), 0.0, h).astype(h_ref.dtype)


def _hyena_filters(feats, w1, b1, w2, b2, w3, b3, freq, decay):
    length = feats.shape[0]
    n = w3.shape[1]
    ch = decay.shape[1]
    full = lambda a: pl.BlockSpec(a.shape, lambda j: (0, 0))
    return pl.pallas_call(
        functools.partial(_hyena_filter_kernel, n_forward=HYENA_ORDER),
        grid=(n // ch,),
        in_specs=[full(feats), full(w1), full(b1), full(w2), full(b2),
                  pl.BlockSpec((w3.shape[0], ch), lambda j: (0, j)), pl.BlockSpec((1, ch), lambda j: (0, j)),
                  full(freq), full(decay)],
        out_specs=pl.BlockSpec((length, ch), lambda j: (0, j)),
        out_shape=jax.ShapeDtypeStruct((length, n), BF16),
        compiler_params=_params("arbitrary"),
        name="hyena_filters",
    )(feats, w1, b1, w2, b2, w3, b3, freq, decay)


def _spectrum_kernel(f_ref, h_ref, re_ref, im_ref):
    j = pl.program_id(0)
    tk = re_ref.shape[0]
    n = re_ref.shape[1]
    a = _dot(f_ref[0], h_ref[...])
    re_ref[...] = a[:tk, :n] + a[:tk, n:]
    row = lax.broadcasted_iota(jnp.int32, (tk, n), 0)
    sign = jnp.where((row == 0) & (j == 0), 1.0, -1.0)
    im_ref[...] = a[tk:, :n] + sign * a[tk:, n:]


def _filter_spectrum(f3, hcat):
    nt, tk2, length = f3.shape
    n = hcat.shape[1] // 2
    tk = tk2 // 2
    out = pl.BlockSpec((tk, n), lambda j: (j, 0))
    return pl.pallas_call(
        _spectrum_kernel,
        grid=(nt,),
        in_specs=[pl.BlockSpec((1, tk2, length), lambda j: (j, 0, 0)),
                  pl.BlockSpec(hcat.shape, lambda j: (0, 0))],
        out_specs=[out, out],
        out_shape=[jax.ShapeDtypeStruct((nt * tk, n), F32)] * 2,
        compiler_params=_params("arbitrary"),
        name="filter_spectrum",
    )(f3, hcat)


def _long_conv_kernel(z_ref, x_ref, f_ref, g_ref, hre_ref, him_ref, bias_ref, o_ref, acc_ref):
    j = pl.program_id(1)
    tk = hre_ref.shape[0]
    row = lax.broadcasted_iota(jnp.int32, hre_ref.shape, 0)
    packed = (row == 0) & (j == 0)
    bias = bias_ref[...]
    h_re = hre_ref[...] + bias
    h_im = him_ref[...] + jnp.where(packed, bias, 0.0)

    @pl.when(j == 0)
    def _():
        acc_ref[...] = jnp.zeros_like(acc_ref)

    spectra = [_dot(f_ref[0], z_ref[s]) for s in range(z_ref.shape[0])]
    for s, a in enumerate(spectra):
        a_re, a_im = a[:tk], a[tk:]
        ii = a_im * h_im
        p_re = a_re * h_re - jnp.where(packed, 0.0, ii)
        p_im = jnp.where(packed, ii, a_re * h_im + a_im * h_re)
        acc_ref[s] += _dot(g_ref[0], jnp.concatenate([p_re, p_im], axis=0).astype(BF16))

    @pl.when(j == pl.num_programs(1) - 1)
    def _():
        o_ref[...] = (x_ref[...].astype(F32) * acc_ref[...]).astype(o_ref.dtype)


def _long_conv(z_src, z_col, x_src, x_col, f3, g3, hre, him, h_col, bias):
    b, length, _ = z_src.shape
    nt, tk2, _ = f3.shape
    tk = tk2 // 2
    ch = HYENA_CH
    nb = CONV_SAMPLES
    return pl.pallas_call(
        _long_conv_kernel,
        grid=(b // nb, nt),
        in_specs=[
            pl.BlockSpec((nb, length, ch), lambda bi, j: (bi, 0, z_col)),
            pl.BlockSpec((nb, length, ch), lambda bi, j: (bi, 0, x_col)),
            pl.BlockSpec((1, tk2, length), lambda bi, j: (j, 0, 0)),
            pl.BlockSpec((1, length, tk2), lambda bi, j: (j, 0, 0)),
            pl.BlockSpec((tk, ch), lambda bi, j: (j, h_col)),
            pl.BlockSpec((tk, ch), lambda bi, j: (j, h_col)),
            pl.BlockSpec((1, ch), lambda bi, j: (0, 0)),
        ],
        out_specs=pl.BlockSpec((nb, length, ch), lambda bi, j: (bi, 0, 0)),
        out_shape=jax.ShapeDtypeStruct((b, length, ch), BF16),
        scratch_shapes=[pltpu.VMEM((nb, length, ch), F32)],
        compiler_params=_params("parallel", "arbitrary"),
        name="long_conv",
    )(z_src, x_src, f3, g3, hre, him, bias)


def _axial_rope(length, rot_dim):
    rows = length // GRID_W
    row = jnp.repeat(jnp.arange(rows), GRID_W).astype(F32)
    col = jnp.tile(jnp.arange(GRID_W), rows).astype(F32)
    quarter = rot_dim // 4
    inv = ROPE_BASE ** (-jnp.arange(quarter, dtype=F32) / quarter)
    ang = jnp.concatenate([row[:, None] * inv, col[:, None] * inv], axis=-1)
    return jnp.cos(ang), jnp.sin(ang)


def _rope_tables(length, n_ctx):
    cos_m, sin_m = _axial_rope(length, MLA_ROPE)
    cos_d, sin_d = _axial_rope(length, DIFF_HEAD_DIM)
    ones = lambda w: jnp.ones((length, w), F32)
    zeros = lambda w: jnp.zeros((length, w), F32)
    cm = jnp.concatenate([ones(MLA_NOPE), cos_m, cos_m, ones(LANES - MLA_NOPE - MLA_ROPE)], axis=1)
    sm = jnp.concatenate([zeros(MLA_NOPE), -sin_m, sin_m, zeros(LANES - MLA_NOPE - MLA_ROPE)], axis=1)
    cd = jnp.concatenate([cos_d] * 4, axis=1)
    sd = jnp.concatenate([-sin_d, sin_d] * 2, axis=1)
    pad = lambda t, v: jnp.concatenate([t, jnp.full((n_ctx, LANES), v, F32)], axis=0)
    return pad(cm, 1.0), pad(sm, 0.0), pad(cd, 1.0), pad(sd, 0.0)


def _dft_matrices(length):
    n = 2 * length
    k = np.arange(length, dtype=np.int64)
    ang = 2.0 * np.pi * ((k[:, None] * k[None, :]) % n).astype(np.float64) / n
    alt = np.where(k % 2 == 0, 1.0, -1.0)
    f_re = np.cos(ang)
    f_im = -np.sin(ang)
    f_im[0, :] = alt
    g_re = (2.0 / n) * np.cos(ang.T)
    g_im = -(2.0 / n) * np.sin(ang.T)
    g_re[:, 0] = 1.0 / n
    g_im[:, 0] = alt / n
    nt = length // FREQ_TILE
    f3 = np.concatenate([f_re.reshape(nt, FREQ_TILE, length), f_im.reshape(nt, FREQ_TILE, length)], axis=1)
    g3 = np.concatenate([g_re.reshape(length, nt, FREQ_TILE), g_im.reshape(length, nt, FREQ_TILE)], axis=2)
    g3 = np.transpose(g3, (1, 0, 2))
    return jnp.asarray(f3, dtype=BF16), jnp.asarray(g3, dtype=BF16)


def _hyena_features(length):
    t = jnp.arange(length, dtype=F32)
    tn = t / max(length - 1, 1)
    bands = jnp.linspace(1e-4, HYENA_BANDS - 1, HYENA_BANDS, dtype=F32)
    ang = 2.0 * math.pi * bands[None, :] * t[:, None] / length
    feats = jnp.concatenate([tn[:, None], jnp.cos(ang), -jnp.sin(ang)], axis=-1)
    min_decay = math.log(HYENA_DECAY_TARGET) / HYENA_SLOW_PCT
    max_decay = math.log(HYENA_DECAY_TARGET) / HYENA_FAST_PCT
    deltas = jnp.abs(jnp.linspace(min_decay, max_decay, HYENA_CH, dtype=F32))
    decay = jnp.exp(-tn[:, None] * deltas[None, :])
    return feats, decay


def _pad_cols(w, left, right):
    return jnp.pad(w, ((0, 0),) * (w.ndim - 1) + ((left, right),))


def _even_weights(w_in, w_uq, w_ukv):
    d = w_in.shape[0]
    o = MLA_Q_RANK + MLA_KV_RANK
    w_kr = _pad_cols(w_in[:, o:o + MLA_ROPE], MLA_NOPE, LANES - MLA_NOPE - MLA_ROPE)
    w_in_ext = jnp.concatenate([w_in[:, :o], w_kr, w_in[:, o + MLA_ROPE:]], axis=1)
    uq = _pad_cols(w_uq.reshape(MLA_Q_RANK, MLA_HEADS, MLA_NOPE + MLA_ROPE), 0, LANES - MLA_NOPE - MLA_ROPE)
    ukv = w_ukv.reshape(MLA_KV_RANK, MLA_HEADS, MLA_NOPE + MLA_V)
    uk = _pad_cols(ukv[:, :, :MLA_NOPE], 0, LANES - MLA_NOPE)
    uv = ukv[:, :, MLA_NOPE:].reshape(MLA_KV_RANK, MLA_HEADS // 2, 2, MLA_V)
    uv = jnp.stack([_pad_cols(uv[:, :, 0], 0, LANES - MLA_V), _pad_cols(uv[:, :, 1], LANES - MLA_V, 0)], axis=2)
    w_ukv_ext = jnp.concatenate([uk.reshape(MLA_KV_RANK, -1), uv.reshape(MLA_KV_RANK, -1)], axis=1)
    return w_in_ext.astype(BF16), uq.reshape(MLA_Q_RANK, -1).astype(BF16), w_ukv_ext.astype(BF16)


def _odd_weights(w_in):
    d = w_in.shape[0]
    nq = WIN_Q_HEADS * WIN_HEAD_DIM
    nkv = WIN_KV_HEADS * WIN_HEAD_DIM
    dup = lambda w: jnp.concatenate([w.reshape(d, WIN_KV_HEADS, WIN_HEAD_DIM)] * 2, axis=-1).reshape(d, -1)
    wk = dup(w_in[:, nq:nq + nkv])
    wv = dup(w_in[:, nq + nkv:nq + 2 * nkv])
    wqkv = jnp.concatenate([w_in[:, :nq], wk, wv], axis=1).astype(BF16)
    wkv = jnp.concatenate([wk, wv], axis=1).astype(BF16)
    wu = w_in[:, nq + 2 * nkv:].astype(BF16)
    return wqkv, wkv, wu


def _ffn_weights(w_up, conv_w, conv_b, w_down):
    d = w_up.shape[0]
    nc = D_FF // FFN_CHUNK
    chunks = lambda w: jnp.transpose(w.reshape(w.shape[0], nc, FFN_CHUNK), (1, 0, 2))
    cw = jnp.concatenate([conv_w, conv_b[None, :]], axis=0)
    return (chunks(w_up[:, :D_FF]).astype(BF16), chunks(w_up[:, D_FF:]).astype(BF16),
            chunks(cw[:, :D_FF]), chunks(cw[:, D_FF:]),
            w_down.astype(BF16))


def kernel(x, c, ctx, c_ctx, ada_w, ada_b, norm_g, mix_w_out, ffn_w_up, ffn_conv_w, ffn_conv_b, ffn_w_down, even_w_in, mla_q_norm_g, mla_w_uq, mla_kv_norm_g, mla_w_ukv, diff_lambda, diff_subln_g, odd_w_in, win_sink, hy_conv_w, hy_conv_b, hy_f_w1, hy_f_b1, hy_f_w2, hy_f_b2, hy_f_w3, hy_f_b3, hy_f_freq, hy_bias):
    b, seq, d = x.shape
    n_ctx = ctx.shape[1]
    depth = ada_w.shape[0]
    assert depth == 2 and d == D_MODEL and seq % ROW_TILE == 0 and n_ctx == ROW_TILE
    n_lat_tiles = seq // ROW_TILE
    n_tiles = n_lat_tiles + 1

    pad_rows = (-(b + 1)) % 8
    cond = jnp.concatenate([c, c_ctx[None, :], jnp.zeros((pad_rows, d), F32)], axis=0)
    mod = _ada_table(cond, ada_w, ada_b).reshape(depth, b + 1 + pad_rows, 6, d)[:, :b + 1]
    g = norm_g.reshape(depth, 4, 1, d)
    cm, sm, cd, sd = _rope_tables(seq, n_ctx)

    lambda_init = 0.8 - 0.6 * math.exp(-0.3 * 0)
    w_in_e, w_uq_e, w_ukv_e = _even_weights(even_w_in[0], mla_w_uq[0], mla_w_ukv[0])
    rows = seq + n_ctx
    even_w = (g[0, 0], w_in_e, mla_q_norm_g[0][None, :], w_uq_e, mla_kv_norm_g[0][None, :], w_ukv_e)
    proj = _proj_even(x, mod[0], even_w, (cm, sm, cd, sd), rows, PROJ_ROW_TILE, 0, False)
    qm, km, vm, dq, dk, dv = _proj_even(ctx, mod[0], even_w, (cm, sm, cd, sd), rows, n_ctx, seq, True, prior=proj)
    lam_args = (diff_lambda[0], diff_subln_g[0][None, :], lambda_init)
    w_out0 = mix_w_out[0].astype(BF16)
    ffn0 = _ffn_weights(ffn_w_up[0], ffn_conv_w[0], ffn_conv_b[0], ffn_w_down[0])
    o_mla = _mla_attention(qm, km, vm, ATTN_Q_TILE, 0, seq, 0, rows)
    o_diff = _diff_attention(dq, dk, dv, *lam_args, ATTN_Q_TILE, 0, seq, 0, rows)
    x_lat = _mix_ffn(o_mla, o_diff, x, mod[0], w_out0, g[0, 1], g[0, 2], *ffn0, g[0, 3], FFN_ROW_TILE, False)
    o_mla = _mla_attention(qm, km, vm, n_ctx, seq, n_ctx, seq, n_ctx)
    o_diff = _diff_attention(dq, dk, dv, *lam_args, n_ctx, seq, n_ctx, seq, n_ctx)
    x_ctx = _mix_ffn(o_mla, o_diff, ctx, mod[0], w_out0, g[0, 1], g[0, 2], *ffn0, g[0, 3], n_ctx, True)

    wqkv, wkv, wu = _odd_weights(odd_w_in[0])
    cw_hy = jnp.concatenate([hy_conv_w[0], hy_conv_b[0][None, :]], axis=0)
    q, kd, vd, u = _proj_odd(x_lat, mod[1], g[1, 0], wqkv, wu, cw_hy, cd, sd, PROJ_ROW_TILE)
    kc, vc = _proj_ctx_kv(x_ctx, mod[1], g[1, 0], wkv, 0, n_ctx)
    o_win = _window_attention(q, kd, vd, kc, vc, win_sink[0])

    feats, decay = _hyena_features(seq)
    emb = feats.shape[1]
    feats = _pad_cols(feats, 0, LANES - emb)
    w1 = jnp.pad(hy_f_w1[0], ((0, LANES - emb), (0, 0)))
    hcat = _hyena_filters(feats, w1, hy_f_b1[0][None, :], hy_f_w2[0], hy_f_b2[0][None, :], hy_f_w3[0],
                          hy_f_b3[0][None, :], hy_f_freq[0][None, :], decay)
    f3, g3 = _dft_matrices(seq)
    hre, him = _filter_spectrum(f3, hcat)
    z = _long_conv(u, 0, u, 1, f3, g3, hre, him, 0, hy_bias[0, 0][None, :])
    z = _long_conv(z, 0, u, 2, f3, g3, hre, him, 1, hy_bias[0, 1][None, :])

    w_out1 = mix_w_out[1].astype(BF16)
    ffn1 = _ffn_weights(ffn_w_up[1], ffn_conv_w[1], ffn_conv_b[1], ffn_w_down[1])
    return _mix_ffn(o_win, z, x_lat, mod[1], w_out1, g[1, 1], g[1, 2], *ffn1, g[1, 3], FFN_ROW_TILE, False)
```

```python
import functools
import math

import numpy as np
import jax
import jax.numpy as jnp
from jax import lax
from jax.experimental import pallas as pl
from jax.experimental.pallas import tpu as pltpu

D_MODEL = 1024
GRID_W = 64
MLA_HEADS = 8
MLA_NOPE = 64
MLA_ROPE = 32
MLA_V = 64
MLA_Q_RANK = 384
MLA_KV_RANK = 256
DIFF_HEADS = 4
DIFF_HEAD_DIM = 64
DIFF_V_DIM = 128
WIN_Q_HEADS = 8
WIN_KV_HEADS = 2
WIN_HEAD_DIM = 64
WINDOW = 128
HYENA_CH = 512
HYENA_ORDER = 2
HYENA_BANDS = 16
HYENA_DECAY_TARGET = 1e-2
HYENA_FAST_PCT = 0.3
HYENA_SLOW_PCT = 1.5
D_FF = 2816
ROPE_BASE = 10000.0
NORM_EPS = 1e-6
NEG_INF = -1e30

LANES = 128
HALO_ROWS = 16
ROW_TILE = 256
ATTN_Q_TILE = 2048
ATTN_SUB_TILE = 512
ATTN_KEY_CHUNK = 256
LOG2E = math.log2(math.e)
FFN_ROW_TILE = 512
PROJ_ROW_TILE = 512
FFN_CHUNK = 256
FFN_DOWN_GROUP = 4
FREQ_TILE = 256
CONV_SAMPLES = 2
VMEM_LIMIT = 48 * 1024 * 1024

BF16 = jnp.bfloat16
F32 = jnp.float32


def _params(*semantics):
    return pltpu.CompilerParams(dimension_semantics=semantics, vmem_limit_bytes=VMEM_LIMIT)


def _rms(x, g):
    return x * lax.rsqrt(jnp.mean(x * x, axis=-1, keepdims=True) + NORM_EPS) * g


def _dot(a, b):
    return jnp.dot(a, b, preferred_element_type=F32)


def _lane_iota(shape):
    return lax.broadcasted_iota(jnp.int32, shape, len(shape) - 1)


def _rope_chunk(x, cos, sin_signed, half):
    first = (_lane_iota(x.shape) % (2 * half)) < half
    partner = jnp.where(first, pltpu.roll(x, LANES - half, 1), pltpu.roll(x, half, 1))
    return x * cos + partner * sin_signed


def _rope_rows(xt, cos_t, sin_signed_t, half):
    first = (lax.broadcasted_iota(jnp.int32, xt.shape, 0) % (2 * half)) < half
    partner = jnp.where(first, pltpu.roll(xt, LANES - half, 0), pltpu.roll(xt, half, 0))
    return xt * cos_t + partner * sin_signed_t


def _shift_rows(u, prev_row, next_row):
    rows = u.shape[0]
    r = lax.broadcasted_iota(jnp.int32, u.shape, 0)
    down = jnp.where(r == 0, prev_row, pltpu.roll(u, 1, 0))
    up = jnp.where(r == rows - 1, next_row, pltpu.roll(u, rows - 1, 0))
    return down, up


def _ada_kernel(c_ref, w_ref, b_ref, o_ref):
    c = c_ref[...]
    sc = c * jax.nn.sigmoid(c)
    o_ref[0] = _dot(sc.astype(BF16), w_ref[0].astype(BF16)) + b_ref[0]


def _ada_table(cond, ada_w, ada_b):
    depth, d, n = ada_w.shape
    rows = cond.shape[0]
    tn = 1536
    return pl.pallas_call(
        _ada_kernel,
        grid=(depth, n // tn),
        in_specs=[
            pl.BlockSpec((rows, d), lambda l, j: (0, 0)),
            pl.BlockSpec((1, d, tn), lambda l, j: (l, 0, j)),
            pl.BlockSpec((1, 1, tn), lambda l, j: (l, 0, j)),
        ],
        out_specs=pl.BlockSpec((1, rows, tn), lambda l, j: (l, 0, j)),
        out_shape=jax.ShapeDtypeStruct((depth, rows, n), F32),
        compiler_params=_params("arbitrary", "arbitrary"),
        name="ada_table",
    )(cond, ada_w, ada_b.reshape(depth, 1, n))


def _proj_even_kernel(x_ref, mod_ref, g_ref, win_ref, qg_ref, wuq_ref, kvg_ref, wukv_ref,
                      cm_ref, sm_ref, cd_ref, sd_ref, cmt_ref, smt_ref, cdt_ref, sdt_ref, *rest):
    qm_ref, km_ref, vm_ref, dq_ref, dk_ref, dv_ref = rest[-6:]
    x = x_ref[0]
    mod = mod_ref[0]
    h = _rms(x, g_ref[...]) * (1.0 + mod[1:2]) + mod[0:1]
    p = _dot(h.astype(BF16), win_ref[...])
    o = 0
    cq = p[:, o:o + MLA_Q_RANK]; o += MLA_Q_RANK
    ckv = p[:, o:o + MLA_KV_RANK]; o += MLA_KV_RANK
    kr = p[:, o:o + LANES]; o += LANES
    dq = p[:, o:o + 512]; o += 512
    dk = p[:, o:o + 512]; o += 512
    dv = p[:, o:o + 512]
    q = _dot(_rms(cq, qg_ref[...]).astype(BF16), wuq_ref[...])
    kv = _dot(_rms(ckv, kvg_ref[...]).astype(BF16), wukv_ref[...])
    cm, sm, cd, sd = cm_ref[...], sm_ref[...], cd_ref[...], sd_ref[...]
    cmt, smt, cdt, sdt = cmt_ref[...], smt_ref[...], cdt_ref[...], sdt_ref[...]
    mla_scale = (MLA_NOPE + MLA_ROPE) ** -0.5 * LOG2E
    diff_scale = DIFF_HEAD_DIM ** -0.5 * LOG2E
    kr = _rope_chunk(kr, cm, sm, MLA_ROPE // 2)
    for hd in range(MLA_HEADS):
        c = slice(hd * LANES, (hd + 1) * LANES)
        qm_ref[0, c, :] = (_rope_rows(q[:, c].T, cmt, smt, MLA_ROPE // 2) * mla_scale).astype(BF16)
        km_ref[0, :, c] = (kv[:, c] + kr).astype(BF16)
        vm_ref[0, c, :] = kv[:, MLA_HEADS * LANES + hd * LANES:MLA_HEADS * LANES + (hd + 1) * LANES].astype(BF16).T
    for hd in range(DIFF_HEADS):
        c = slice(hd * LANES, (hd + 1) * LANES)
        dq_ref[0, c, :] = (_rope_rows(dq[:, c].T, cdt, sdt, DIFF_HEAD_DIM // 2) * diff_scale).astype(BF16)
        dk_ref[0, :, c] = _rope_chunk(dk[:, c], cd, sd, DIFF_HEAD_DIM // 2).astype(BF16)
        dv_ref[0, c, :] = dv[:, c].astype(BF16).T


def _proj_even(x, mod, weights, tables, rows, tm, row_off, ctx_mod, prior=None):
    b, n, d = x.shape
    off = row_off // tm
    ctx_row = mod.shape[0] - 1
    const = lambda a: pl.BlockSpec(a.shape, lambda bi, i: (0,) * a.ndim)
    tab = pl.BlockSpec((tm, LANES), lambda bi, i: (off + i, 0))
    out = lambda w: pl.BlockSpec((1, tm, w), lambda bi, i: (bi, off + i, 0))
    shp = lambda w: jax.ShapeDtypeStruct((b, rows, w), BF16)
    out_t = lambda w: pl.BlockSpec((1, w, tm), lambda bi, i: (bi, 0, off + i))
    shp_t = lambda w: jax.ShapeDtypeStruct((b, w, rows), BF16)
    tab_t = pl.BlockSpec((LANES, tm), lambda bi, i: (0, off + i))
    tables_t = [t.T for t in tables]
    n_in = 2 + len(weights) + 2 * len(tables)
    prior = () if prior is None else tuple(prior)
    return pl.pallas_call(
        _proj_even_kernel,
        grid=(b, n // tm),
        in_specs=[
            pl.BlockSpec((1, tm, d), lambda bi, i: (bi, i, 0)),
            pl.BlockSpec((1, 6, d), lambda bi, i: (ctx_row if ctx_mod else bi, 0, 0)),
            *[const(w) for w in weights], *[tab for _ in tables], *[tab_t for _ in tables],
            *[pl.BlockSpec(memory_space=pl.ANY) for _ in prior],
        ],
        out_specs=[out_t(1024), out(1024), out_t(1024), out_t(512), out(512), out_t(512)],
        out_shape=[shp_t(1024), shp(1024), shp_t(1024), shp_t(512), shp(512), shp_t(512)],
        input_output_aliases={n_in + k: k for k in range(len(prior))},
        compiler_params=_params("parallel", "arbitrary"),
        name="proj_even",
    )(x, mod, *weights, *tables, *tables_t, *prior)


SOFTMAX_FLOOR = 2.0 ** -90


def _logits_t(qt, k):
    nk = k.shape[0]
    step = min(nk, ATTN_KEY_CHUNK)
    return [_dot(k[a:a + step], qt) for a in range(0, nk, step)]


def _softmax_pv_t(s, vt, shift=None):
    step = s[0].shape[0]
    if shift is None:
        shift = functools.reduce(jnp.maximum, [jnp.max(si, axis=0, keepdims=True) for si in s])
    p = [jnp.exp2(si - shift) for si in s]
    l = sum(jnp.sum(pi, axis=0, keepdims=True) for pi in p)
    o = sum(_dot(vt[:, j * step:(j + 1) * step], pi.astype(BF16)) for j, pi in enumerate(p))
    return o, l


def _max_key_norm(k, lanes=None):
    kf = k.astype(F32)
    sq = kf * kf if lanes is None else jnp.where(lanes, kf * kf, 0.0)
    return jnp.sqrt(jnp.max(jnp.sum(sq, axis=-1, keepdims=True), axis=0, keepdims=True))


def _query_norms(qt):
    qf = qt.astype(F32)
    return jnp.sqrt(jnp.sum(qf * qf, axis=0, keepdims=True))


def _mla_kernel(qt_ref, k_ref, vt_ref, o_ref):
    sub = min(qt_ref.shape[2], ATTN_SUB_TILE)
    items = [(r, hd) for r in range(qt_ref.shape[2] // sub) for hd in range(2)]

    def query(item):
        r, hd = item
        return qt_ref[0, hd * LANES:(hd + 1) * LANES, r * sub:(r + 1) * sub]

    def logits(item):
        return _logits_t(query(item), k_ref[0, :, item[1] * LANES:(item[1] + 1) * LANES])

    def run(bounded):
        if bounded:
            k_norm = [_max_key_norm(k_ref[0, :, hd * LANES:(hd + 1) * LANES]) for hd in range(2)]
        s_next = logits(items[0])
        acc, floor = None, None
        for j, (r, hd) in enumerate(items):
            s = s_next
            if j + 1 < len(items):
                s_next = logits(items[j + 1])
            shift = _query_norms(query((r, hd))) * k_norm[hd] if bounded else None
            o, l = _softmax_pv_t(s, vt_ref[0, hd * LANES:(hd + 1) * LANES, :], shift)
            floor = l if floor is None else jnp.minimum(floor, l)
            acc = o / l if hd == 0 else acc + o / l
            if hd == 1:
                o_ref[0, r * sub:(r + 1) * sub, :] = acc.T.astype(o_ref.dtype)
        return jnp.min(floor)

    smallest = run(bounded=True)

    @pl.when(smallest < SOFTMAX_FLOOR)
    def _():
        run(bounded=False)


def _mla_attention(qmt, km, vmt, tq, q_off, n_q, k_off, n_k):
    b = km.shape[0]
    pairs = MLA_HEADS // 2
    qo, ko = q_off // tq, k_off // n_k
    return pl.pallas_call(
        _mla_kernel,
        grid=(b, pairs, n_q // tq),
        in_specs=[
            pl.BlockSpec((1, 2 * LANES, tq), lambda bi, p, i: (bi, p, qo + i)),
            pl.BlockSpec((1, n_k, 2 * LANES), lambda bi, p, i: (bi, ko, p)),
            pl.BlockSpec((1, 2 * LANES, n_k), lambda bi, p, i: (bi, p, ko)),
        ],
        out_specs=pl.BlockSpec((1, tq, LANES), lambda bi, p, i: (bi, i, p)),
        out_shape=jax.ShapeDtypeStruct((b, n_q, pairs * LANES), BF16),
        compiler_params=_params("parallel", "arbitrary", "arbitrary"),
        name="mla_attention",
    )(qmt, km, vmt)


def _diff_kernel(qt_ref, k_ref, vt_ref, lam_ref, g_ref, o_ref, *, lambda_init):
    lv = lam_ref[...]
    lam = (jnp.exp(jnp.sum(lv[0:1] * lv[1:2], axis=-1, keepdims=True))
           - jnp.exp(jnp.sum(lv[2:3] * lv[3:4], axis=-1, keepdims=True)) + lambda_init)
    vt = vt_ref[0]
    sub = min(qt_ref.shape[2], ATTN_SUB_TILE)
    items = [(r, which) for r in range(qt_ref.shape[2] // sub) for which in range(2)]

    def query(item):
        r, which = item
        qt = qt_ref[0, :, r * sub:(r + 1) * sub]
        first = lax.broadcasted_iota(jnp.int32, qt.shape, 0) < DIFF_HEAD_DIM
        zero = jnp.zeros_like(qt)
        return jnp.where(first, qt, zero) if which == 0 else jnp.where(first, zero, qt)

    def logits(item):
        return _logits_t(query(item), k_ref[0])

    def run(bounded):
        if bounded:
            first_k = _lane_iota(k_ref.shape[1:]) < DIFF_HEAD_DIM
            k_norm = [_max_key_norm(k_ref[0], first_k), _max_key_norm(k_ref[0], ~first_k)]
        s_next = logits(items[0])
        o1, floor = None, None
        for j, (r, which) in enumerate(items):
            s = s_next
            if j + 1 < len(items):
                s_next = logits(items[j + 1])
            shift = _query_norms(query((r, which))) * k_norm[which] if bounded else None
            o, l = _softmax_pv_t(s, vt, shift)
            floor = l if floor is None else jnp.minimum(floor, l)
            if which == 0:
                o1 = o / l
            else:
                o = (o1 - lam * (o / l)).T
                o_ref[0, r * sub:(r + 1) * sub, :] = (_rms(o, g_ref[...]) * (1.0 - lambda_init)).astype(o_ref.dtype)
        return jnp.min(floor)

    smallest = run(bounded=True)

    @pl.when(smallest < SOFTMAX_FLOOR)
    def _():
        run(bounded=False)


def _diff_attention(dq, dk, dv, diff_lambda, subln_g, lambda_init, tq, q_off, n_q, k_off, n_k):
    b = dq.shape[0]
    qo, ko = q_off // tq, k_off // n_k
    return pl.pallas_call(
        functools.partial(_diff_kernel, lambda_init=lambda_init),
        grid=(b, DIFF_HEADS, n_q // tq),
        in_specs=[
            pl.BlockSpec((1, LANES, tq), lambda bi, h, i: (bi, h, qo + i)),
            pl.BlockSpec((1, n_k, LANES), lambda bi, h, i: (bi, ko, h)),
            pl.BlockSpec((1, LANES, n_k), lambda bi, h, i: (bi, h, ko)),
            pl.BlockSpec((4, DIFF_HEAD_DIM), lambda bi, h, i: (0, 0)),
            pl.BlockSpec((1, DIFF_V_DIM), lambda bi, h, i: (0, 0)),
        ],
        out_specs=pl.BlockSpec((1, tq, LANES), lambda bi, h, i: (bi, i, h)),
        out_shape=jax.ShapeDtypeStruct((b, n_q, DIFF_HEADS * LANES), BF16),
        compiler_params=_params("parallel", "arbitrary", "arbitrary"),
        name="diff_attention",
    )(dq, dk, dv, diff_lambda, subln_g)


def _mix_ffn_kernel(a_ref, ap_ref, an_ref, b_ref, bp_ref, bn_ref, x_ref, xp_ref, xn_ref, mod_ref,
                    wo_ref, g1_ref, g2_ref, wa_ref, wg_ref, cwa_ref, cwg_ref, wd_ref, g3_ref,
                    o_ref, lhs_ref, x1_ref, ua0_ref, ug0_ref, ua1_ref, ug1_ref, act0_ref, act1_ref, acc_ref):
    i = pl.program_id(1)
    tm = a_ref.shape[1]
    half = a_ref.shape[2]
    n_chunks = wa_ref.shape[0]
    rows = tm + 2 * HALO_ROWS
    mid = slice(HALO_ROWS, HALO_ROWS + tm)
    mod = mod_ref[0]

    ext = lambda p_ref, m_ref, n_ref: jnp.concatenate([p_ref[0], m_ref[0], n_ref[0]], axis=0)
    y = _dot(ext(ap_ref, a_ref, an_ref), wo_ref[0:half, :]) + _dot(ext(bp_ref, b_ref, bn_ref), wo_ref[half:, :])
    x1 = ext(xp_ref, x_ref, xn_ref) + mod[2:3] * _rms(y, g1_ref[...])
    h2 = _rms(x1, g2_ref[...]) * (1.0 + mod[4:5]) + mod[3:4]
    r = lax.broadcasted_iota(jnp.int32, h2.shape, 0)
    pad = ((r < HALO_ROWS) & (i == 0)) | ((r >= HALO_ROWS + tm) & (i == pl.num_programs(1) - 1))
    lhs_ref[...] = jnp.where(pad, 0.0, h2).astype(BF16)
    x1_ref[...] = x1[mid]
    slots = ((ua0_ref, ug0_ref), (ua1_ref, ug1_ref))

    def up(j, slot):
        lhs = lhs_ref[...]
        slots[slot][0][...] = _dot(lhs, wa_ref[j])
        slots[slot][1][...] = _dot(lhs, wg_ref[j])

    def conv(u, cw):
        down = pltpu.roll(u, 1, 0)[mid]
        above = pltpu.roll(u, rows - 1, 0)[mid]
        return down * cw[0:1] + u[mid] * cw[1:2] + above * cw[2:3] + cw[3:4]

    acts = (act0_ref, act1_ref)
    width = wa_ref.shape[2]

    def activate(j):
        a = conv(slots[j % 2][0][...], cwa_ref[j])
        g = conv(slots[j % 2][1][...], cwg_ref[j])
        k = j % FFN_DOWN_GROUP
        acts[(j // FFN_DOWN_GROUP) % 2][:, k * width:(k + 1) * width] = ((g * jax.nn.sigmoid(g)) * a).astype(BF16)

    def down(j_last):
        j0 = j_last - j_last % FFN_DOWN_GROUP
        k = (j_last - j0 + 1) * width
        return _dot(acts[(j_last // FFN_DOWN_GROUP) % 2][:, :k], wd_ref[j0 * width:(j_last + 1) * width, :])

    up(0, 0)
    up(1, 1)
    y = None
    for j in range(n_chunks):
        activate(j)
        if j + 2 < n_chunks:
            up(j + 2, j % 2)
        if j % FFN_DOWN_GROUP == FFN_DOWN_GROUP - 1 or j == n_chunks - 1:
            y = down(j)
            if j == n_chunks - 1:
                break
            if j < FFN_DOWN_GROUP:
                acc_ref[...] = y
            else:
                acc_ref[...] += y
    total = y if n_chunks <= FFN_DOWN_GROUP else acc_ref[...] + y
    o_ref[0] = x1_ref[...] + mod[5:6] * _rms(total, g3_ref[...])


def _mix_ffn(a, bb, x, mod, w_out, g1, g2, wa, wg, cwa, cwg, wd, g3, tm, ctx_mod):
    b, n_rows, half = a.shape
    d = x.shape[-1]
    ctx_row = mod.shape[0] - 1
    per = tm // HALO_ROWS
    last_halo = n_rows // HALO_ROWS - 1
    const = lambda *shape: pl.BlockSpec(shape, lambda bi, i: (0,) * len(shape), pipeline_mode=pl.Buffered(1))
    tile = lambda w: pl.BlockSpec((1, tm, w), lambda bi, i: (bi, i, 0))
    prev = lambda w: pl.BlockSpec((1, HALO_ROWS, w), lambda bi, i: (bi, jnp.maximum(i * per - 1, 0), 0))
    nxt = lambda w: pl.BlockSpec((1, HALO_ROWS, w), lambda bi, i: (bi, jnp.minimum((i + 1) * per, last_halo), 0))
    rows = tm + 2 * HALO_ROWS
    return pl.pallas_call(
        _mix_ffn_kernel,
        grid=(b, n_rows // tm),
        in_specs=[
            tile(half), prev(half), nxt(half), tile(half), prev(half), nxt(half), tile(d), prev(d), nxt(d),
            pl.BlockSpec((1, 6, d), lambda bi, i: (ctx_row if ctx_mod else bi, 0, 0)),
            const(*w_out.shape), const(1, d), const(1, d),
            const(*wa.shape), const(*wg.shape), const(*cwa.shape), const(*cwg.shape), const(*wd.shape),
            const(1, d),
        ],
        out_specs=tile(d),
        out_shape=jax.ShapeDtypeStruct((b, n_rows, d), F32),
        scratch_shapes=[pltpu.VMEM((rows, d), BF16), pltpu.VMEM((tm, d), F32)]
        + [pltpu.VMEM((rows, FFN_CHUNK), F32)] * 4
        + [pltpu.VMEM((tm, FFN_DOWN_GROUP * FFN_CHUNK), BF16)] * 2 + [pltpu.VMEM((tm, d), F32)],
        compiler_params=_params("parallel", "arbitrary"),
        name="mix_ffn",
    )(a, a, a, bb, bb, bb, x, x, x, mod, w_out, g1, g2, wa, wg, cwa, cwg, wd, g3)


def _proj_odd_kernel(x_ref, xp_ref, xn_ref, mod_ref, g_ref, wqkv_ref, wu_ref, cw_ref, cd_ref, sd_ref,
                     q_ref, k_ref, v_ref, u_ref, *, n_tiles):
    i = pl.program_id(1)
    mod = mod_ref[0]

    def normed(x):
        return (_rms(x, g_ref[...]) * (1.0 + mod[1:2]) + mod[0:1]).astype(BF16)

    h = normed(x_ref[0])
    p = _dot(h, wqkv_ref[...])
    cd, sd = cd_ref[...], sd_ref[...]
    scale = WIN_HEAD_DIM ** -0.5 * LOG2E
    for c in range(4):
        cs = slice(c * LANES, (c + 1) * LANES)
        q_ref[0, :, cs] = (_rope_chunk(p[:, cs], cd, sd, WIN_HEAD_DIM // 2) * scale).astype(BF16)
    for c in range(2):
        cs = slice(512 + c * LANES, 512 + (c + 1) * LANES)
        k_ref[0, c * LANES:(c + 1) * LANES, :] = _rope_chunk(p[:, cs], cd, sd, WIN_HEAD_DIM // 2).astype(BF16).T
    v_ref[0] = p[:, 768:1024].astype(BF16)
    prev_ok = jnp.where(i == 0, 0.0, 1.0)
    next_ok = jnp.where(i == n_tiles - 1, 0.0, 1.0)
    wu = wu_ref[...]
    u = _dot(h, wu)
    up = _dot(normed(xp_ref[0]), wu)[HALO_ROWS - 1:HALO_ROWS] * prev_ok
    un = _dot(normed(xn_ref[0]), wu)[0:1] * next_ok
    down, upw = _shift_rows(u, up, un)
    cw = cw_ref[...]
    u_ref[0] = (down * cw[0:1] + u * cw[1:2] + upw * cw[2:3] + cw[3:4]).astype(u_ref.dtype)


def _proj_odd(xcat, mod, g, wqkv, wu, cw, cd, sd, tm):
    b, rows, d = xcat.shape
    n_tiles = rows // tm
    per = tm // HALO_ROWS
    last_halo = n_tiles * per - 1
    const = lambda *shape: pl.BlockSpec(shape, lambda bi, i: (0,) * len(shape))
    tab = pl.BlockSpec((tm, LANES), lambda bi, i: (i, 0))
    out = lambda w: pl.BlockSpec((1, tm, w), lambda bi, i: (bi, i, 0))
    return pl.pallas_call(
        functools.partial(_proj_odd_kernel, n_tiles=n_tiles),
        grid=(b, n_tiles),
        in_specs=[
            pl.BlockSpec((1, tm, d), lambda bi, i: (bi, i, 0)),
            pl.BlockSpec((1, HALO_ROWS, d), lambda bi, i: (bi, jnp.maximum(i * per - 1, 0), 0)),
            pl.BlockSpec((1, HALO_ROWS, d), lambda bi, i: (bi, jnp.minimum((i + 1) * per, last_halo), 0)),
            pl.BlockSpec((1, 6, d), lambda bi, i: (bi, 0, 0)),
            const(1, d), const(*wqkv.shape), const(*wu.shape), const(*cw.shape), tab, tab,
        ],
        out_specs=[out(512), pl.BlockSpec((1, 2 * LANES, tm), lambda bi, i: (bi, 0, i)), out(256),
                   out(3 * HYENA_CH)],
        out_shape=[jax.ShapeDtypeStruct((b, rows, 512), BF16), jax.ShapeDtypeStruct((b, 2 * LANES, rows), BF16),
                   jax.ShapeDtypeStruct((b, rows, 256), BF16), jax.ShapeDtypeStruct((b, rows, 3 * HYENA_CH), BF16)],
        compiler_params=_params("parallel", "arbitrary"),
        name="proj_odd",
    )(xcat, xcat, xcat, mod, g, wqkv, wu, cw, cd, sd)


def _proj_ctx_kv_kernel(x_ref, mod_ref, g_ref, w_ref, k_ref, v_ref):
    mod = mod_ref[0]
    h = (_rms(x_ref[0], g_ref[...]) * (1.0 + mod[1:2]) + mod[0:1]).astype(BF16)
    p = _dot(h, w_ref[...])
    k_ref[0] = p[:, :2 * LANES].astype(BF16).T
    v_ref[0] = p[:, 2 * LANES:].astype(BF16)


def _proj_ctx_kv(xcat, mod, g, wkv, ctx_tile, n_ctx):
    b, _, d = xcat.shape
    ctx_row = mod.shape[0] - 1
    out = pl.BlockSpec((1, n_ctx, 2 * LANES), lambda bi: (bi, 0, 0))
    return pl.pallas_call(
        _proj_ctx_kv_kernel,
        grid=(b,),
        in_specs=[
            pl.BlockSpec((1, n_ctx, d), lambda bi: (bi, ctx_tile, 0)),
            pl.BlockSpec((1, 6, d), lambda bi: (ctx_row, 0, 0)),
            pl.BlockSpec((1, d), lambda bi: (0, 0)),
            pl.BlockSpec(wkv.shape, lambda bi: (0, 0)),
        ],
        out_specs=[pl.BlockSpec((1, 2 * LANES, n_ctx), lambda bi: (bi, 0, 0)), out],
        out_shape=[jax.ShapeDtypeStruct((b, 2 * LANES, n_ctx), BF16), jax.ShapeDtypeStruct((b, n_ctx, 2 * LANES), BF16)],
        compiler_params=_params("parallel"),
        name="proj_ctx_kv",
    )(xcat, mod, g, wkv)


def _window_kernel(sink_ref, q_ref, kc_ref, vc_ref, kp_ref, km_ref, kn_ref, vp_ref, vm_ref, vn_ref, o_ref,
                   *, n_lat):
    i = pl.program_id(1)
    tq = q_ref.shape[1]
    n_ctx = vc_ref.shape[1]
    nk = n_ctx + tq + 2 * WINDOW
    col = lax.broadcasted_iota(jnp.int32, (tq, nk), 1)
    row = lax.broadcasted_iota(jnp.int32, (tq, nk), 0)
    k_pos = i * tq - WINDOW + (col - n_ctx)
    q_pos = i * tq + row
    visible = (col < n_ctx) | ((jnp.abs(q_pos - k_pos) <= WINDOW) & (k_pos >= 0) & (k_pos < n_lat))
    bias = jnp.where(visible, 0.0, NEG_INF)
    group = WIN_Q_HEADS // WIN_KV_HEADS
    bias = jnp.concatenate([bias, bias], axis=0)
    second_head = lax.broadcasted_iota(jnp.int32, (2 * tq, 1), 0) >= tq
    first = _lane_iota((tq, LANES)) < WIN_HEAD_DIM
    kts, vs = [], []
    for hk in range(WIN_KV_HEADS):
        c = slice(hk * LANES, (hk + 1) * LANES)
        kts.append(jnp.concatenate([kc_ref[0, c, :], kp_ref[0, c, :], km_ref[0, c, :], kn_ref[0, c, :]], axis=1))
        vs.append(jnp.concatenate([vc_ref[0, :, c], vp_ref[0, :, c], vm_ref[0, :, c], vn_ref[0, :, c]], axis=0))

    def logits(chunk):
        q = q_ref[0, :, chunk * LANES:(chunk + 1) * LANES]
        zero = jnp.zeros_like(q)
        stacked = jnp.concatenate([jnp.where(first, q, zero), jnp.where(first, zero, q)], axis=0)
        return _dot(stacked, kts[chunk // (group // 2)])

    n_chunks = WIN_Q_HEADS // 2
    s_next = logits(0)
    for chunk in range(n_chunks):
        s = s_next + bias
        if chunk + 1 < n_chunks:
            s_next = logits(chunk + 1)
        sink = jnp.where(second_head, sink_ref[2 * chunk + 1], sink_ref[2 * chunk]) * LOG2E
        m = jnp.maximum(jnp.max(s, axis=-1, keepdims=True), sink)
        p = jnp.exp2(s - m)
        l = jnp.sum(p, axis=-1, keepdims=True) + jnp.exp2(sink - m)
        o = _dot(p.astype(BF16), vs[chunk // (group // 2)]) / l
        o_ref[0, :, chunk * LANES:(chunk + 1) * LANES] = jnp.where(first, o[:tq], o[tq:]).astype(o_ref.dtype)


def _window_attention(q, kdt, vd, kct, vc, sink):
    b, n_lat, _ = q.shape
    tq = ROW_TILE
    n_ctx = vc.shape[1]
    per = tq // WINDOW
    last = n_lat // WINDOW - 1
    prev_i = lambda i: jnp.maximum(i * per - 1, 0)
    next_i = lambda i: jnp.minimum((i + 1) * per, last)
    w = 2 * LANES
    return pl.pallas_call(
        functools.partial(_window_kernel, n_lat=n_lat),
        grid=(b, n_lat // tq),
        in_specs=[
            pl.BlockSpec(memory_space=pltpu.SMEM),
            pl.BlockSpec((1, tq, 4 * LANES), lambda bi, i: (bi, i, 0)),
            pl.BlockSpec((1, w, n_ctx), lambda bi, i: (bi, 0, 0)),
            pl.BlockSpec((1, n_ctx, w), lambda bi, i: (bi, 0, 0)),
            pl.BlockSpec((1, w, WINDOW), lambda bi, i: (bi, 0, prev_i(i))),
            pl.BlockSpec((1, w, tq), lambda bi, i: (bi, 0, i)),
            pl.BlockSpec((1, w, WINDOW), lambda bi, i: (bi, 0, next_i(i))),
            pl.BlockSpec((1, WINDOW, w), lambda bi, i: (bi, prev_i(i), 0)),
            pl.BlockSpec((1, tq, w), lambda bi, i: (bi, i, 0)),
            pl.BlockSpec((1, WINDOW, w), lambda bi, i: (bi, next_i(i), 0)),
        ],
        out_specs=pl.BlockSpec((1, tq, 4 * LANES), lambda bi, i: (bi, i, 0)),
        out_shape=jax.ShapeDtypeStruct((b, n_lat, 4 * LANES), BF16),
        compiler_params=_params("parallel", "arbitrary"),
        name="window_attention",
    )(sink, q, kct, vc, kdt, kdt, kdt, vd, vd, vd)


def _hyena_filter_kernel(feat_ref, w1_ref, b1_ref, w2_ref, b2_ref, w3_ref, b3_ref, freq_ref, decay_ref, h_ref,
                         hid_ref, *, n_forward):
    j = pl.program_id(0)
    hp = lax.Precision.HIGHEST

    @pl.when(j == 0)
    def _():
        freq = freq_ref[...]
        hid = jnp.sin(freq * (jnp.dot(feat_ref[...], w1_ref[...], precision=hp, preferred_element_type=F32) + b1_ref[...]))
        hid_ref[...] = jnp.sin(freq * (jnp.dot(hid, w2_ref[...], precision=hp, preferred_element_type=F32) + b2_ref[...]))

    h = jnp.dot(hid_ref[...], w3_ref[...], precision=hp, preferred_element_type=F32) + b3_ref[...]
    h = h * decay_ref[...]
    row = lax.broadcasted_iota(jnp.int32, h.shape, 0)
    h_ref[...] = jnp.where((row == 0) & (j >= n_forward), 0.0, h).astype(h_ref.dtype)


def _hyena_filters(feats, w1, b1, w2, b2, w3, b3, freq, decay):
    length = feats.shape[0]
    n = w3.shape[1]
    ch = decay.shape[1]
    full = lambda a: pl.BlockSpec(a.shape, lambda j: (0, 0))
    return pl.pallas_call(
        functools.partial(_hyena_filter_kernel, n_forward=HYENA_ORDER),
        grid=(n // ch,),
        in_specs=[full(feats), full(w1), full(b1), full(w2), full(b2),
                  pl.BlockSpec((w3.shape[0], ch), lambda j: (0, j)), pl.BlockSpec((1, ch), lambda j: (0, j)),
                  full(freq), full(decay)],
        out_specs=pl.BlockSpec((length, ch), lambda j: (0, j)),
        out_shape=jax.ShapeDtypeStruct((length, n), BF16),
        scratch_shapes=[pltpu.VMEM((length, w2.shape[1]), F32)],
        compiler_params=_params("arbitrary"),
        name="hyena_filters",
    )(feats, w1, b1, w2, b2, w3, b3, freq, decay)


def _spectrum_kernel(f_ref, h_ref, re_ref, im_ref):
    j = pl.program_id(0)
    tk = re_ref.shape[0]
    n = re_ref.shape[1]
    a = _dot(f_ref[0], h_ref[...])
    re_ref[...] = a[:tk, :n] + a[:tk, n:]
    row = lax.broadcasted_iota(jnp.int32, (tk, n), 0)
    sign = jnp.where((row == 0) & (j == 0), 1.0, -1.0)
    im_ref[...] = a[tk:, :n] + sign * a[tk:, n:]


def _filter_spectrum(f3, hcat):
    nt, tk2, length = f3.shape
    n = hcat.shape[1] // 2
    tk = tk2 // 2
    out = pl.BlockSpec((tk, n), lambda j: (j, 0))
    return pl.pallas_call(
        _spectrum_kernel,
        grid=(nt,),
        in_specs=[pl.BlockSpec((1, tk2, length), lambda j: (j, 0, 0)),
                  pl.BlockSpec(hcat.shape, lambda j: (0, 0))],
        out_specs=[out, out],
        out_shape=[jax.ShapeDtypeStruct((nt * tk, n), F32)] * 2,
        compiler_params=_params("arbitrary"),
        name="filter_spectrum",
    )(f3, hcat)


def _long_conv_kernel(z_ref, x_ref, f_ref, g_ref, hre_ref, him_ref, bias_ref, o_ref, acc_ref):
    j = pl.program_id(1)
    tk = hre_ref.shape[0]
    row = lax.broadcasted_iota(jnp.int32, hre_ref.shape, 0)
    packed = (row == 0) & (j == 0)
    bias = bias_ref[...]
    h_re = hre_ref[...] + bias
    h_im = him_ref[...] + jnp.where(packed, bias, 0.0)

    @pl.when(j == 0)
    def _():
        acc_ref[...] = jnp.zeros_like(acc_ref)

    spectra = [_dot(f_ref[0], z_ref[s]) for s in range(z_ref.shape[0])]
    for s, a in enumerate(spectra):
        a_re, a_im = a[:tk], a[tk:]
        ii = a_im * h_im
        p_re = a_re * h_re - jnp.where(packed, 0.0, ii)
        p_im = jnp.where(packed, ii, a_re * h_im + a_im * h_re)
        acc_ref[s] += _dot(g_ref[0], jnp.concatenate([p_re, p_im], axis=0).astype(BF16))

    @pl.when(j == pl.num_programs(1) - 1)
    def _():
        o_ref[...] = (x_ref[...].astype(F32) * acc_ref[...]).astype(o_ref.dtype)


def _long_conv(z_src, z_col, x_src, x_col, f3, g3, hre, him, h_col, bias):
    b, length, _ = z_src.shape
    nt, tk2, _ = f3.shape
    tk = tk2 // 2
    ch = HYENA_CH
    nb = CONV_SAMPLES
    return pl.pallas_call(
        _long_conv_kernel,
        grid=(b // nb, nt),
        in_specs=[
            pl.BlockSpec((nb, length, ch), lambda bi, j: (bi, 0, z_col)),
            pl.BlockSpec((nb, length, ch), lambda bi, j: (bi, 0, x_col)),
            pl.BlockSpec((1, tk2, length), lambda bi, j: (j, 0, 0)),
            pl.BlockSpec((1, length, tk2), lambda bi, j: (j, 0, 0)),
            pl.BlockSpec((tk, ch), lambda bi, j: (j, h_col)),
            pl.BlockSpec((tk, ch), lambda bi, j: (j, h_col)),
            pl.BlockSpec((1, ch), lambda bi, j: (0, 0)),
        ],
        out_specs=pl.BlockSpec((nb, length, ch), lambda bi, j: (bi, 0, 0)),
        out_shape=jax.ShapeDtypeStruct((b, length, ch), BF16),
        scratch_shapes=[pltpu.VMEM((nb, length, ch), F32)],
        compiler_params=_params("parallel", "arbitrary"),
        name="long_conv",
    )(z_src, x_src, f3, g3, hre, him, bias)


def _axial_rope(length, rot_dim):
    rows = length // GRID_W
    row = jnp.repeat(jnp.arange(rows), GRID_W).astype(F32)
    col = jnp.tile(jnp.arange(GRID_W), rows).astype(F32)
    quarter = rot_dim // 4
    inv = ROPE_BASE ** (-jnp.arange(quarter, dtype=F32) / quarter)
    ang = jnp.concatenate([row[:, None] * inv, col[:, None] * inv], axis=-1)
    return jnp.cos(ang), jnp.sin(ang)


def _rope_tables(length, n_ctx):
    cos_m, sin_m = _axial_rope(length, MLA_ROPE)
    cos_d, sin_d = _axial_rope(length, DIFF_HEAD_DIM)
    ones = lambda w: jnp.ones((length, w), F32)
    zeros = lambda w: jnp.zeros((length, w), F32)
    cm = jnp.concatenate([ones(MLA_NOPE), cos_m, cos_m, ones(LANES - MLA_NOPE - MLA_ROPE)], axis=1)
    sm = jnp.concatenate([zeros(MLA_NOPE), -sin_m, sin_m, zeros(LANES - MLA_NOPE - MLA_ROPE)], axis=1)
    cd = jnp.concatenate([cos_d] * 4, axis=1)
    sd = jnp.concatenate([-sin_d, sin_d] * 2, axis=1)
    pad = lambda t, v: jnp.concatenate([t, jnp.full((n_ctx, LANES), v, F32)], axis=0)
    return pad(cm, 1.0), pad(sm, 0.0), pad(cd, 1.0), pad(sd, 0.0)


def _dft_matrices(length):
    n = 2 * length
    k = np.arange(length, dtype=np.int64)
    ang = 2.0 * np.pi * ((k[:, None] * k[None, :]) % n).astype(np.float64) / n
    alt = np.where(k % 2 == 0, 1.0, -1.0)
    f_re = np.cos(ang)
    f_im = -np.sin(ang)
    f_im[0, :] = alt
    g_re = (2.0 / n) * np.cos(ang.T)
    g_im = -(2.0 / n) * np.sin(ang.T)
    g_re[:, 0] = 1.0 / n
    g_im[:, 0] = alt / n
    nt = length // FREQ_TILE
    f3 = np.concatenate([f_re.reshape(nt, FREQ_TILE, length), f_im.reshape(nt, FREQ_TILE, length)], axis=1)
    g3 = np.concatenate([g_re.reshape(length, nt, FREQ_TILE), g_im.reshape(length, nt, FREQ_TILE)], axis=2)
    g3 = np.transpose(g3, (1, 0, 2))
    return jnp.asarray(f3, dtype=BF16), jnp.asarray(g3, dtype=BF16)


def _hyena_features(length):
    t = jnp.arange(length, dtype=F32)
    tn = t / max(length - 1, 1)
    bands = jnp.linspace(1e-4, HYENA_BANDS - 1, HYENA_BANDS, dtype=F32)
    ang = 2.0 * math.pi * bands[None, :] * t[:, None] / length
    feats = jnp.concatenate([tn[:, None], jnp.cos(ang), -jnp.sin(ang)], axis=-1)
    min_decay = math.log(HYENA_DECAY_TARGET) / HYENA_SLOW_PCT
    max_decay = math.log(HYENA_DECAY_TARGET) / HYENA_FAST_PCT
    deltas = jnp.abs(jnp.linspace(min_decay, max_decay, HYENA_CH, dtype=F32))
    decay = jnp.exp(-tn[:, None] * deltas[None, :])
    return feats, decay


def _pad_cols(w, left, right):
    return jnp.pad(w, ((0, 0),) * (w.ndim - 1) + ((left, right),))


def _even_weights(w_in, w_uq, w_ukv):
    d = w_in.shape[0]
    o = MLA_Q_RANK + MLA_KV_RANK
    w_kr = _pad_cols(w_in[:, o:o + MLA_ROPE], MLA_NOPE, LANES - MLA_NOPE - MLA_ROPE)
    w_in_ext = jnp.concatenate([w_in[:, :o], w_kr, w_in[:, o + MLA_ROPE:]], axis=1)
    uq = _pad_cols(w_uq.reshape(MLA_Q_RANK, MLA_HEADS, MLA_NOPE + MLA_ROPE), 0, LANES - MLA_NOPE - MLA_ROPE)
    ukv = w_ukv.reshape(MLA_KV_RANK, MLA_HEADS, MLA_NOPE + MLA_V)
    uk = _pad_cols(ukv[:, :, :MLA_NOPE], 0, LANES - MLA_NOPE)
    uv = ukv[:, :, MLA_NOPE:].reshape(MLA_KV_RANK, MLA_HEADS // 2, 2, MLA_V)
    uv = jnp.stack([_pad_cols(uv[:, :, 0], 0, LANES - MLA_V), _pad_cols(uv[:, :, 1], LANES - MLA_V, 0)], axis=2)
    w_ukv_ext = jnp.concatenate([uk.reshape(MLA_KV_RANK, -1), uv.reshape(MLA_KV_RANK, -1)], axis=1)
    return w_in_ext.astype(BF16), uq.reshape(MLA_Q_RANK, -1).astype(BF16), w_ukv_ext.astype(BF16)


def _odd_weights(w_in):
    d = w_in.shape[0]
    nq = WIN_Q_HEADS * WIN_HEAD_DIM
    nkv = WIN_KV_HEADS * WIN_HEAD_DIM
    dup = lambda w: jnp.concatenate([w.reshape(d, WIN_KV_HEADS, WIN_HEAD_DIM)] * 2, axis=-1).reshape(d, -1)
    wk = dup(w_in[:, nq:nq + nkv])
    wv = dup(w_in[:, nq + nkv:nq + 2 * nkv])
    wqkv = jnp.concatenate([w_in[:, :nq], wk, wv], axis=1).astype(BF16)
    wkv = jnp.concatenate([wk, wv], axis=1).astype(BF16)
    wu = w_in[:, nq + 2 * nkv:].astype(BF16)
    return wqkv, wkv, wu


def _ffn_weights(w_up, conv_w, conv_b, w_down):
    d = w_up.shape[0]
    nc = D_FF // FFN_CHUNK
    chunks = lambda w: jnp.transpose(w.reshape(w.shape[0], nc, FFN_CHUNK), (1, 0, 2))
    cw = jnp.concatenate([conv_w, conv_b[None, :]], axis=0)
    return (chunks(w_up[:, :D_FF]).astype(BF16), chunks(w_up[:, D_FF:]).astype(BF16),
            chunks(cw[:, :D_FF]), chunks(cw[:, D_FF:]),
            w_down.astype(BF16))


def kernel(x, c, ctx, c_ctx, ada_w, ada_b, norm_g, mix_w_out, ffn_w_up, ffn_conv_w, ffn_conv_b, ffn_w_down, even_w_in, mla_q_norm_g, mla_w_uq, mla_kv_norm_g, mla_w_ukv, diff_lambda, diff_subln_g, odd_w_in, win_sink, hy_conv_w, hy_conv_b, hy_f_w1, hy_f_b1, hy_f_w2, hy_f_b2, hy_f_w3, hy_f_b3, hy_f_freq, hy_bias):
    b, seq, d = x.shape
    n_ctx = ctx.shape[1]
    depth = ada_w.shape[0]
    assert depth == 2 and d == D_MODEL and seq % ROW_TILE == 0 and n_ctx == ROW_TILE
    n_lat_tiles = seq // ROW_TILE
    n_tiles = n_lat_tiles + 1

    pad_rows = (-(b + 1)) % 8
    cond = jnp.concatenate([c, c_ctx[None, :], jnp.zeros((pad_rows, d), F32)], axis=0)
    mod = _ada_table(cond, ada_w, ada_b).reshape(depth, b + 1 + pad_rows, 6, d)[:, :b + 1]
    g = norm_g.reshape(depth, 4, 1, d)
    cm, sm, cd, sd = _rope_tables(seq, n_ctx)

    lambda_init = 0.8 - 0.6 * math.exp(-0.3 * 0)
    w_in_e, w_uq_e, w_ukv_e = _even_weights(even_w_in[0], mla_w_uq[0], mla_w_ukv[0])
    rows = seq + n_ctx
    even_w = (g[0, 0], w_in_e, mla_q_norm_g[0][None, :], w_uq_e, mla_kv_norm_g[0][None, :], w_ukv_e)
    proj = _proj_even(x, mod[0], even_w, (cm, sm, cd, sd), rows, PROJ_ROW_TILE, 0, False)
    qm, km, vm, dq, dk, dv = _proj_even(ctx, mod[0], even_w, (cm, sm, cd, sd), rows, n_ctx, seq, True, prior=proj)
    lam_args = (diff_lambda[0], diff_subln_g[0][None, :], lambda_init)
    w_out0 = mix_w_out[0].astype(BF16)
    ffn0 = _ffn_weights(ffn_w_up[0], ffn_conv_w[0], ffn_conv_b[0], ffn_w_down[0])
    o_mla = _mla_attention(qm, km, vm, ATTN_Q_TILE, 0, seq, 0, rows)
    o_diff = _diff_attention(dq, dk, dv, *lam_args, ATTN_Q_TILE, 0, seq, 0, rows)
    x_lat = _mix_ffn(o_mla, o_diff, x, mod[0], w_out0, g[0, 1], g[0, 2], *ffn0, g[0, 3], FFN_ROW_TILE, False)
    o_mla = _mla_attention(qm, km, vm, n_ctx, seq, n_ctx, seq, n_ctx)
    o_diff = _diff_attention(dq, dk, dv, *lam_args, n_ctx, seq, n_ctx, seq, n_ctx)
    x_ctx = _mix_ffn(o_mla, o_diff, ctx, mod[0], w_out0, g[0, 1], g[0, 2], *ffn0, g[0, 3], n_ctx, True)

    wqkv, wkv, wu = _odd_weights(odd_w_in[0])
    cw_hy = jnp.concatenate([hy_conv_w[0], hy_conv_b[0][None, :]], axis=0)
    q, kd, vd, u = _proj_odd(x_lat, mod[1], g[1, 0], wqkv, wu, cw_hy, cd, sd, PROJ_ROW_TILE)
    kc, vc = _proj_ctx_kv(x_ctx, mod[1], g[1, 0], wkv, 0, n_ctx)
    o_win = _window_attention(q, kd, vd, kc, vc, win_sink[0])

    feats, decay = _hyena_features(seq)
    emb = feats.shape[1]
    feats = _pad_cols(feats, 0, LANES - emb)
    w1 = jnp.pad(hy_f_w1[0], ((0, LANES - emb), (0, 0)))
    hcat = _hyena_filters(feats, w1, hy_f_b1[0][None, :], hy_f_w2[0], hy_f_b2[0][None, :], hy_f_w3[0],
                          hy_f_b3[0][None, :], hy_f_freq[0][None, :], decay)
    f3, g3 = _dft_matrices(seq)
    hre, him = _filter_spectrum(f3, hcat)
    z = _long_conv(u, 0, u, 1, f3, g3, hre, him, 0, hy_bias[0, 0][None, :])
    z = _long_conv(z, 0, u, 2, f3, g3, hre, him, 1, hy_bias[0, 1][None, :])

    w_out1 = mix_w_out[1].astype(BF16)
    ffn1 = _ffn_weights(ffn_w_up[1], ffn_conv_w[1], ffn_conv_b[1], ffn_w_down[1])
    return _mix_ffn(o_win, z, x_lat, mod[1], w_out1, g[1, 1], g[1, 2], *ffn1, g[1, 3], FFN_ROW_TILE, False)
```

```python
import functools
import math

import numpy as np
import jax
import jax.numpy as jnp
from jax import lax
from jax.experimental import pallas as pl
from jax.experimental.pallas import tpu as pltpu

D_MODEL = 1024
GRID_W = 64
MLA_HEADS = 8
MLA_NOPE = 64
MLA_ROPE = 32
MLA_V = 64
MLA_Q_RANK = 384
MLA_KV_RANK = 256
DIFF_HEADS = 4
DIFF_HEAD_DIM = 64
DIFF_V_DIM = 128
WIN_Q_HEADS = 8
WIN_KV_HEADS = 2
WIN_HEAD_DIM = 64
WINDOW = 128
HYENA_CH = 512
HYENA_ORDER = 2
HYENA_BANDS = 16
HYENA_DECAY_TARGET = 1e-2
HYENA_FAST_PCT = 0.3
HYENA_SLOW_PCT = 1.5
D_FF = 2816
ROPE_BASE = 10000.0
NORM_EPS = 1e-6
NEG_INF = -1e30

LANES = 128
HALO_ROWS = 16
ROW_TILE = 256
ATTN_Q_TILE = 2048
ATTN_SUB_TILE = 512
ATTN_KEY_CHUNK = 256
LOG2E = math.log2(math.e)
FFN_ROW_TILE = 512
PROJ_ROW_TILE = 512
FFN_CHUNK = 256
FFN_DOWN_GROUP = 4
FREQ_TILE = 256
CONV_SAMPLES = 2
VMEM_LIMIT = 48 * 1024 * 1024

BF16 = jnp.bfloat16
F32 = jnp.float32


def _params(*semantics):
    return pltpu.CompilerParams(dimension_semantics=semantics, vmem_limit_bytes=VMEM_LIMIT)


def _rms(x, g):
    return x * lax.rsqrt(jnp.mean(x * x, axis=-1, keepdims=True) + NORM_EPS) * g


def _dot(a, b):
    return jnp.dot(a, b, preferred_element_type=F32)


def _lane_iota(shape):
    return lax.broadcasted_iota(jnp.int32, shape, len(shape) - 1)


def _rope_chunk(x, cos, sin_signed, half):
    first = (_lane_iota(x.shape) % (2 * half)) < half
    partner = jnp.where(first, pltpu.roll(x, LANES - half, 1), pltpu.roll(x, half, 1))
    return x * cos + partner * sin_signed


def _rope_rows(xt, cos_t, sin_signed_t, half):
    first = (lax.broadcasted_iota(jnp.int32, xt.shape, 0) % (2 * half)) < half
    partner = jnp.where(first, pltpu.roll(xt, LANES - half, 0), pltpu.roll(xt, half, 0))
    return xt * cos_t + partner * sin_signed_t


def _shift_rows(u, prev_row, next_row):
    rows = u.shape[0]
    r = lax.broadcasted_iota(jnp.int32, u.shape, 0)
    down = jnp.where(r == 0, prev_row, pltpu.roll(u, 1, 0))
    up = jnp.where(r == rows - 1, next_row, pltpu.roll(u, rows - 1, 0))
    return down, up


def _ada_kernel(c_ref, w_ref, b_ref, o_ref):
    c = c_ref[...]
    sc = c * jax.nn.sigmoid(c)
    o_ref[0] = _dot(sc.astype(BF16), w_ref[0].astype(BF16)) + b_ref[0]


def _ada_table(cond, ada_w, ada_b):
    depth, d, n = ada_w.shape
    rows = cond.shape[0]
    tn = 1536
    return pl.pallas_call(
        _ada_kernel,
        grid=(depth, n // tn),
        in_specs=[
            pl.BlockSpec((rows, d), lambda l, j: (0, 0)),
            pl.BlockSpec((1, d, tn), lambda l, j: (l, 0, j)),
            pl.BlockSpec((1, 1, tn), lambda l, j: (l, 0, j)),
        ],
        out_specs=pl.BlockSpec((1, rows, tn), lambda l, j: (l, 0, j)),
        out_shape=jax.ShapeDtypeStruct((depth, rows, n), F32),
        compiler_params=_params("arbitrary", "arbitrary"),
        name="ada_table",
    )(cond, ada_w, ada_b.reshape(depth, 1, n))


def _proj_even_kernel(x_ref, mod_ref, g_ref, win_ref, qg_ref, wuq_ref, kvg_ref, wukv_ref,
                      cm_ref, sm_ref, cd_ref, sd_ref, cmt_ref, smt_ref, cdt_ref, sdt_ref, *rest):
    qm_ref, km_ref, vm_ref, dq_ref, dk_ref, dv_ref = rest[-6:]
    x = x_ref[0]
    mod = mod_ref[0]
    h = _rms(x, g_ref[...]) * (1.0 + mod[1:2]) + mod[0:1]
    p = _dot(h.astype(BF16), win_ref[...])
    o = 0
    cq = p[:, o:o + MLA_Q_RANK]; o += MLA_Q_RANK
    ckv = p[:, o:o + MLA_KV_RANK]; o += MLA_KV_RANK
    kr = p[:, o:o + LANES]; o += LANES
    dq = p[:, o:o + 512]; o += 512
    dk = p[:, o:o + 512]; o += 512
    dv = p[:, o:o + 512]
    q = _dot(_rms(cq, qg_ref[...]).astype(BF16), wuq_ref[...])
    kv = _dot(_rms(ckv, kvg_ref[...]).astype(BF16), wukv_ref[...])
    cm, sm, cd, sd = cm_ref[...], sm_ref[...], cd_ref[...], sd_ref[...]
    cmt, smt, cdt, sdt = cmt_ref[...], smt_ref[...], cdt_ref[...], sdt_ref[...]
    mla_scale = (MLA_NOPE + MLA_ROPE) ** -0.5 * LOG2E
    diff_scale = DIFF_HEAD_DIM ** -0.5 * LOG2E
    kr = _rope_chunk(kr, cm, sm, MLA_ROPE // 2)
    for hd in range(MLA_HEADS):
        c = slice(hd * LANES, (hd + 1) * LANES)
        qm_ref[0, c, :] = (_rope_rows(q[:, c].T, cmt, smt, MLA_ROPE // 2) * mla_scale).astype(BF16)
        km_ref[0, :, c] = (kv[:, c] + kr).astype(BF16)
        vm_ref[0, c, :] = kv[:, MLA_HEADS * LANES + hd * LANES:MLA_HEADS * LANES + (hd + 1) * LANES].astype(BF16).T
    for hd in range(DIFF_HEADS):
        c = slice(hd * LANES, (hd + 1) * LANES)
        dq_ref[0, c, :] = (_rope_rows(dq[:, c].T, cdt, sdt, DIFF_HEAD_DIM // 2) * diff_scale).astype(BF16)
        dk_ref[0, :, c] = _rope_chunk(dk[:, c], cd, sd, DIFF_HEAD_DIM // 2).astype(BF16)
        dv_ref[0, c, :] = dv[:, c].astype(BF16).T


def _proj_even(x, mod, weights, tables, rows, tm, row_off, ctx_mod, prior=None):
    b, n, d = x.shape
    off = row_off // tm
    ctx_row = mod.shape[0] - 1
    const = lambda a: pl.BlockSpec(a.shape, lambda bi, i: (0,) * a.ndim)
    tab = pl.BlockSpec((tm, LANES), lambda bi, i: (off + i, 0))
    out = lambda w: pl.BlockSpec((1, tm, w), lambda bi, i: (bi, off + i, 0))
    shp = lambda w: jax.ShapeDtypeStruct((b, rows, w), BF16)
    out_t = lambda w: pl.BlockSpec((1, w, tm), lambda bi, i: (bi, 0, off + i))
    shp_t = lambda w: jax.ShapeDtypeStruct((b, w, rows), BF16)
    tab_t = pl.BlockSpec((LANES, tm), lambda bi, i: (0, off + i))
    tables_t = [t.T for t in tables]
    n_in = 2 + len(weights) + 2 * len(tables)
    prior = () if prior is None else tuple(prior)
    return pl.pallas_call(
        _proj_even_kernel,
        grid=(b, n // tm),
        in_specs=[
            pl.BlockSpec((1, tm, d), lambda bi, i: (bi, i, 0)),
            pl.BlockSpec((1, 6, d), lambda bi, i: (ctx_row if ctx_mod else bi, 0, 0)),
            *[const(w) for w in weights], *[tab for _ in tables], *[tab_t for _ in tables],
            *[pl.BlockSpec(memory_space=pl.ANY) for _ in prior],
        ],
        out_specs=[out_t(1024), out(1024), out_t(1024), out_t(512), out(512), out_t(512)],
        out_shape=[shp_t(1024), shp(1024), shp_t(1024), shp_t(512), shp(512), shp_t(512)],
        input_output_aliases={n_in + k: k for k in range(len(prior))},
        compiler_params=_params("parallel", "arbitrary"),
        name="proj_even",
    )(x, mod, *weights, *tables, *tables_t, *prior)


SOFTMAX_FLOOR = 2.0 ** -90


def _logits_t(qt, k):
    nk = k.shape[0]
    step = min(nk, ATTN_KEY_CHUNK)
    return [_dot(k[a:a + step], qt) for a in range(0, nk, step)]


def _softmax_pv_t(s, vt, shift=None):
    step = s[0].shape[0]
    if shift is None:
        shift = functools.reduce(jnp.maximum, [jnp.max(si, axis=0, keepdims=True) for si in s])
    p = [jnp.exp2(si - shift) for si in s]
    l = sum(jnp.sum(pi, axis=0, keepdims=True) for pi in p)
    o = sum(_dot(vt[:, j * step:(j + 1) * step], pi.astype(BF16)) for j, pi in enumerate(p))
    return o, l


def _max_key_norm(k, lanes=None):
    kf = k.astype(F32)
    sq = kf * kf if lanes is None else jnp.where(lanes, kf * kf, 0.0)
    return jnp.sqrt(jnp.max(jnp.sum(sq, axis=-1, keepdims=True), axis=0, keepdims=True))


def _query_norms(qt):
    qf = qt.astype(F32)
    return jnp.sqrt(jnp.sum(qf * qf, axis=0, keepdims=True))


def _mla_kernel(qt_ref, k_ref, vt_ref, o_ref):
    sub = min(qt_ref.shape[2], ATTN_SUB_TILE)
    items = [(r, hd) for r in range(qt_ref.shape[2] // sub) for hd in range(2)]

    def query(item):
        r, hd = item
        return qt_ref[0, hd * LANES:(hd + 1) * LANES, r * sub:(r + 1) * sub]

    def logits(item):
        return _logits_t(query(item), k_ref[0, :, item[1] * LANES:(item[1] + 1) * LANES])

    def run(bounded):
        if bounded:
            k_norm = [_max_key_norm(k_ref[0, :, hd * LANES:(hd + 1) * LANES]) for hd in range(2)]
        s_next = logits(items[0])
        acc, floor = None, None
        for j, (r, hd) in enumerate(items):
            s = s_next
            if j + 1 < len(items):
                s_next = logits(items[j + 1])
            shift = _query_norms(query((r, hd))) * k_norm[hd] if bounded else None
            o, l = _softmax_pv_t(s, vt_ref[0, hd * LANES:(hd + 1) * LANES, :], shift)
            floor = l if floor is None else jnp.minimum(floor, l)
            acc = o / l if hd == 0 else acc + o / l
            if hd == 1:
                o_ref[0, r * sub:(r + 1) * sub, :] = acc.T.astype(o_ref.dtype)
        return jnp.min(floor)

    smallest = run(bounded=True)

    @pl.when(smallest < SOFTMAX_FLOOR)
    def _():
        run(bounded=False)


def _mla_attention(qmt, km, vmt, tq, q_off, n_q, k_off, n_k):
    b = km.shape[0]
    pairs = MLA_HEADS // 2
    qo, ko = q_off // tq, k_off // n_k
    return pl.pallas_call(
        _mla_kernel,
        grid=(b, pairs, n_q // tq),
        in_specs=[
            pl.BlockSpec((1, 2 * LANES, tq), lambda bi, p, i: (bi, p, qo + i)),
            pl.BlockSpec((1, n_k, 2 * LANES), lambda bi, p, i: (bi, ko, p)),
            pl.BlockSpec((1, 2 * LANES, n_k), lambda bi, p, i: (bi, p, ko)),
        ],
        out_specs=pl.BlockSpec((1, tq, LANES), lambda bi, p, i: (bi, i, p)),
        out_shape=jax.ShapeDtypeStruct((b, n_q, pairs * LANES), BF16),
        compiler_params=_params("parallel", "arbitrary", "arbitrary"),
        name="mla_attention",
    )(qmt, km, vmt)


def _diff_kernel(qt_ref, k_ref, vt_ref, lam_ref, g_ref, o_ref, *, lambda_init):
    lv = lam_ref[...]
    lam = (jnp.exp(jnp.sum(lv[0:1] * lv[1:2], axis=-1, keepdims=True))
           - jnp.exp(jnp.sum(lv[2:3] * lv[3:4], axis=-1, keepdims=True)) + lambda_init)
    vt = vt_ref[0]
    sub = min(qt_ref.shape[2], ATTN_SUB_TILE)
    items = [(r, which) for r in range(qt_ref.shape[2] // sub) for which in range(2)]

    def query(item):
        r, which = item
        qt = qt_ref[0, :, r * sub:(r + 1) * sub]
        first = lax.broadcasted_iota(jnp.int32, qt.shape, 0) < DIFF_HEAD_DIM
        zero = jnp.zeros_like(qt)
        return jnp.where(first, qt, zero) if which == 0 else jnp.where(first, zero, qt)

    def logits(item):
        return _logits_t(query(item), k_ref[0])

    def run(bounded):
        if bounded:
            first_k = _lane_iota(k_ref.shape[1:]) < DIFF_HEAD_DIM
            k_norm = [_max_key_norm(k_ref[0], first_k), _max_key_norm(k_ref[0], ~first_k)]
        s_next = logits(items[0])
        o1, floor = None, None
        for j, (r, which) in enumerate(items):
            s = s_next
            if j + 1 < len(items):
                s_next = logits(items[j + 1])
            shift = _query_norms(query((r, which))) * k_norm[which] if bounded else None
            o, l = _softmax_pv_t(s, vt, shift)
            floor = l if floor is None else jnp.minimum(floor, l)
            if which == 0:
                o1 = o / l
            else:
                o = (o1 - lam * (o / l)).T
                o_ref[0, r * sub:(r + 1) * sub, :] = (_rms(o, g_ref[...]) * (1.0 - lambda_init)).astype(o_ref.dtype)
        return jnp.min(floor)

    smallest = run(bounded=True)

    @pl.when(smallest < SOFTMAX_FLOOR)
    def _():
        run(bounded=False)


def _diff_attention(dq, dk, dv, diff_lambda, subln_g, lambda_init, tq, q_off, n_q, k_off, n_k):
    b = dq.shape[0]
    qo, ko = q_off // tq, k_off // n_k
    return pl.pallas_call(
        functools.partial(_diff_kernel, lambda_init=lambda_init),
        grid=(b, DIFF_HEADS, n_q // tq),
        in_specs=[
            pl.BlockSpec((1, LANES, tq), lambda bi, h, i: (bi, h, qo + i)),
            pl.BlockSpec((1, n_k, LANES), lambda bi, h, i: (bi, ko, h)),
            pl.BlockSpec((1, LANES, n_k), lambda bi, h, i: (bi, h, ko)),
            pl.BlockSpec((4, DIFF_HEAD_DIM), lambda bi, h, i: (0, 0)),
            pl.BlockSpec((1, DIFF_V_DIM), lambda bi, h, i: (0, 0)),
        ],
        out_specs=pl.BlockSpec((1, tq, LANES), lambda bi, h, i: (bi, i, h)),
        out_shape=jax.ShapeDtypeStruct((b, n_q, DIFF_HEADS * LANES), BF16),
        compiler_params=_params("parallel", "arbitrary", "arbitrary"),
        name="diff_attention",
    )(dq, dk, dv, diff_lambda, subln_g)


def _mix_ffn_kernel(a_ref, ap_ref, an_ref, b_ref, bp_ref, bn_ref, x_ref, xp_ref, xn_ref, mod_ref,
                    wo_ref, g1_ref, g2_ref, wa_ref, wg_ref, cwa_ref, cwg_ref, wd_ref, g3_ref,
                    o_ref, lhs_ref, x1_ref, ua0_ref, ug0_ref, ua1_ref, ug1_ref, act0_ref, act1_ref, acc_ref):
    i = pl.program_id(1)
    tm = a_ref.shape[1]
    half = a_ref.shape[2]
    n_chunks = wa_ref.shape[0]
    rows = tm + 2 * HALO_ROWS
    mid = slice(HALO_ROWS, HALO_ROWS + tm)
    mod = mod_ref[0]

    ext = lambda p_ref, m_ref, n_ref: jnp.concatenate([p_ref[0], m_ref[0], n_ref[0]], axis=0)
    y = _dot(ext(ap_ref, a_ref, an_ref), wo_ref[0:half, :]) + _dot(ext(bp_ref, b_ref, bn_ref), wo_ref[half:, :])
    x1 = ext(xp_ref, x_ref, xn_ref) + mod[2:3] * _rms(y, g1_ref[...])
    h2 = _rms(x1, g2_ref[...]) * (1.0 + mod[4:5]) + mod[3:4]
    r = lax.broadcasted_iota(jnp.int32, h2.shape, 0)
    pad = ((r < HALO_ROWS) & (i == 0)) | ((r >= HALO_ROWS + tm) & (i == pl.num_programs(1) - 1))
    lhs_ref[...] = jnp.where(pad, 0.0, h2).astype(BF16)
    x1_ref[...] = x1[mid]
    slots = ((ua0_ref, ug0_ref), (ua1_ref, ug1_ref))

    def up(j, slot):
        lhs = lhs_ref[...]
        slots[slot][0][...] = _dot(lhs, wa_ref[j])
        slots[slot][1][...] = _dot(lhs, wg_ref[j])

    def conv(u, cw):
        down = pltpu.roll(u, 1, 0)[mid]
        above = pltpu.roll(u, rows - 1, 0)[mid]
        return down * cw[0:1] + u[mid] * cw[1:2] + above * cw[2:3] + cw[3:4]

    acts = (act0_ref, act1_ref)
    width = wa_ref.shape[2]

    def activate(j):
        a = conv(slots[j % 2][0][...], cwa_ref[j])
        g = conv(slots[j % 2][1][...], cwg_ref[j])
        k = j % FFN_DOWN_GROUP
        acts[(j // FFN_DOWN_GROUP) % 2][:, k * width:(k + 1) * width] = ((g * jax.nn.sigmoid(g)) * a).astype(BF16)

    def down(j_last):
        j0 = j_last - j_last % FFN_DOWN_GROUP
        k = (j_last - j0 + 1) * width
        return _dot(acts[(j_last // FFN_DOWN_GROUP) % 2][:, :k], wd_ref[j0 * width:(j_last + 1) * width, :])

    up(0, 0)
    up(1, 1)
    y = None
    for j in range(n_chunks):
        activate(j)
        if j + 2 < n_chunks:
            up(j + 2, j % 2)
        if j % FFN_DOWN_GROUP == FFN_DOWN_GROUP - 1 or j == n_chunks - 1:
            y = down(j)
            if j == n_chunks - 1:
                break
            if j < FFN_DOWN_GROUP:
                acc_ref[...] = y
            else:
                acc_ref[...] += y
    total = y if n_chunks <= FFN_DOWN_GROUP else acc_ref[...] + y
    o_ref[0] = x1_ref[...] + mod[5:6] * _rms(total, g3_ref[...])


def _mix_ffn(a, bb, x, mod, w_out, g1, g2, wa, wg, cwa, cwg, wd, g3, tm, ctx_mod):
    b, n_rows, half = a.shape
    d = x.shape[-1]
    ctx_row = mod.shape[0] - 1
    per = tm // HALO_ROWS
    last_halo = n_rows // HALO_ROWS - 1
    const = lambda *shape: pl.BlockSpec(shape, lambda bi, i: (0,) * len(shape), pipeline_mode=pl.Buffered(1))
    tile = lambda w: pl.BlockSpec((1, tm, w), lambda bi, i: (bi, i, 0))
    prev = lambda w: pl.BlockSpec((1, HALO_ROWS, w), lambda bi, i: (bi, jnp.maximum(i * per - 1, 0), 0))
    nxt = lambda w: pl.BlockSpec((1, HALO_ROWS, w), lambda bi, i: (bi, jnp.minimum((i + 1) * per, last_halo), 0))
    rows = tm + 2 * HALO_ROWS
    return pl.pallas_call(
        _mix_ffn_kernel,
        grid=(b, n_rows // tm),
        in_specs=[
            tile(half), prev(half), nxt(half), tile(half), prev(half), nxt(half), tile(d), prev(d), nxt(d),
            pl.BlockSpec((1, 6, d), lambda bi, i: (ctx_row if ctx_mod else bi, 0, 0)),
            const(*w_out.shape), const(1, d), const(1, d),
            const(*wa.shape), const(*wg.shape), const(*cwa.shape), const(*cwg.shape), const(*wd.shape),
            const(1, d),
        ],
        out_specs=tile(d),
        out_shape=jax.ShapeDtypeStruct((b, n_rows, d), F32),
        scratch_shapes=[pltpu.VMEM((rows, d), BF16), pltpu.VMEM((tm, d), F32)]
        + [pltpu.VMEM((rows, FFN_CHUNK), F32)] * 4
        + [pltpu.VMEM((tm, FFN_DOWN_GROUP * FFN_CHUNK), BF16)] * 2 + [pltpu.VMEM((tm, d), F32)],
        compiler_params=_params("parallel", "arbitrary"),
        name="mix_ffn",
    )(a, a, a, bb, bb, bb, x, x, x, mod, w_out, g1, g2, wa, wg, cwa, cwg, wd, g3)


def _proj_odd_kernel(x_ref, xp_ref, xn_ref, mod_ref, g_ref, wqkv_ref, wu_ref, cw_ref, cd_ref, sd_ref,
                     q_ref, k_ref, v_ref, u_ref, *, n_tiles):
    i = pl.program_id(1)
    mod = mod_ref[0]

    def normed(x):
        return (_rms(x, g_ref[...]) * (1.0 + mod[1:2]) + mod[0:1]).astype(BF16)

    h = normed(x_ref[0])
    p = _dot(h, wqkv_ref[...])
    cd, sd = cd_ref[...], sd_ref[...]
    scale = WIN_HEAD_DIM ** -0.5 * LOG2E
    for c in range(4):
        cs = slice(c * LANES, (c + 1) * LANES)
        q_ref[0, :, cs] = (_rope_chunk(p[:, cs], cd, sd, WIN_HEAD_DIM // 2) * scale).astype(BF16)
    for c in range(2):
        cs = slice(512 + c * LANES, 512 + (c + 1) * LANES)
        k_ref[0, c * LANES:(c + 1) * LANES, :] = _rope_chunk(p[:, cs], cd, sd, WIN_HEAD_DIM // 2).astype(BF16).T
    v_ref[0] = p[:, 768:1024].astype(BF16)
    prev_ok = jnp.where(i == 0, 0.0, 1.0)
    next_ok = jnp.where(i == n_tiles - 1, 0.0, 1.0)
    wu = wu_ref[...]
    u = _dot(h, wu)
    up = _dot(normed(xp_ref[0]), wu)[HALO_ROWS - 1:HALO_ROWS] * prev_ok
    un = _dot(normed(xn_ref[0]), wu)[0:1] * next_ok
    down, upw = _shift_rows(u, up, un)
    cw = cw_ref[...]
    u_ref[0] = (down * cw[0:1] + u * cw[1:2] + upw * cw[2:3] + cw[3:4]).astype(u_ref.dtype)


def _proj_odd(xcat, mod, g, wqkv, wu, cw, cd, sd, tm):
    b, rows, d = xcat.shape
    n_tiles = rows // tm
    per = tm // HALO_ROWS
    last_halo = n_tiles * per - 1
    const = lambda *shape: pl.BlockSpec(shape, lambda bi, i: (0,) * len(shape))
    tab = pl.BlockSpec((tm, LANES), lambda bi, i: (i, 0))
    out = lambda w: pl.BlockSpec((1, tm, w), lambda bi, i: (bi, i, 0))
    return pl.pallas_call(
        functools.partial(_proj_odd_kernel, n_tiles=n_tiles),
        grid=(b, n_tiles),
        in_specs=[
            pl.BlockSpec((1, tm, d), lambda bi, i: (bi, i, 0)),
            pl.BlockSpec((1, HALO_ROWS, d), lambda bi, i: (bi, jnp.maximum(i * per - 1, 0), 0)),
            pl.BlockSpec((1, HALO_ROWS, d), lambda bi, i: (bi, jnp.minimum((i + 1) * per, last_halo), 0)),
            pl.BlockSpec((1, 6, d), lambda bi, i: (bi, 0, 0)),
            const(1, d), const(*wqkv.shape), const(*wu.shape), const(*cw.shape), tab, tab,
        ],
        out_specs=[out(512), pl.BlockSpec((1, 2 * LANES, tm), lambda bi, i: (bi, 0, i)), out(256),
                   out(3 * HYENA_CH)],
        out_shape=[jax.ShapeDtypeStruct((b, rows, 512), BF16), jax.ShapeDtypeStruct((b, 2 * LANES, rows), BF16),
                   jax.ShapeDtypeStruct((b, rows, 256), BF16), jax.ShapeDtypeStruct((b, rows, 3 * HYENA_CH), BF16)],
        compiler_params=_params("parallel", "arbitrary"),
        name="proj_odd",
    )(xcat, xcat, xcat, mod, g, wqkv, wu, cw, cd, sd)


def _proj_ctx_kv_kernel(x_ref, mod_ref, g_ref, w_ref, k_ref, v_ref):
    mod = mod_ref[0]
    h = (_rms(x_ref[0], g_ref[...]) * (1.0 + mod[1:2]) + mod[0:1]).astype(BF16)
    p = _dot(h, w_ref[...])
    k_ref[0] = p[:, :2 * LANES].astype(BF16).T
    v_ref[0] = p[:, 2 * LANES:].astype(BF16)


def _proj_ctx_kv(xcat, mod, g, wkv, ctx_tile, n_ctx):
    b, _, d = xcat.shape
    ctx_row = mod.shape[0] - 1
    out = pl.BlockSpec((1, n_ctx, 2 * LANES), lambda bi: (bi, 0, 0))
    return pl.pallas_call(
        _proj_ctx_kv_kernel,
        grid=(b,),
        in_specs=[
            pl.BlockSpec((1, n_ctx, d), lambda bi: (bi, ctx_tile, 0)),
            pl.BlockSpec((1, 6, d), lambda bi: (ctx_row, 0, 0)),
            pl.BlockSpec((1, d), lambda bi: (0, 0)),
            pl.BlockSpec(wkv.shape, lambda bi: (0, 0)),
        ],
        out_specs=[pl.BlockSpec((1, 2 * LANES, n_ctx), lambda bi: (bi, 0, 0)), out],
        out_shape=[jax.ShapeDtypeStruct((b, 2 * LANES, n_ctx), BF16), jax.ShapeDtypeStruct((b, n_ctx, 2 * LANES), BF16)],
        compiler_params=_params("parallel"),
        name="proj_ctx_kv",
    )(xcat, mod, g, wkv)


def _window_kernel(sink_ref, q_ref, kc_ref, vc_ref, kp_ref, km_ref, kn_ref, vp_ref, vm_ref, vn_ref, o_ref,
                   *, n_lat):
    i = pl.program_id(1)
    tq = q_ref.shape[1]
    n_ctx = vc_ref.shape[1]
    nk = n_ctx + tq + 2 * WINDOW
    col = lax.broadcasted_iota(jnp.int32, (tq, nk), 1)
    row = lax.broadcasted_iota(jnp.int32, (tq, nk), 0)
    k_pos = i * tq - WINDOW + (col - n_ctx)
    q_pos = i * tq + row
    visible = (col < n_ctx) | ((jnp.abs(q_pos - k_pos) <= WINDOW) & (k_pos >= 0) & (k_pos < n_lat))
    bias = jnp.where(visible, 0.0, NEG_INF)
    group = WIN_Q_HEADS // WIN_KV_HEADS
    bias = jnp.concatenate([bias, bias], axis=0)
    second_head = lax.broadcasted_iota(jnp.int32, (2 * tq, 1), 0) >= tq
    first = _lane_iota((tq, LANES)) < WIN_HEAD_DIM
    kts, vs = [], []
    for hk in range(WIN_KV_HEADS):
        c = slice(hk * LANES, (hk + 1) * LANES)
        kts.append(jnp.concatenate([kc_ref[0, c, :], kp_ref[0, c, :], km_ref[0, c, :], kn_ref[0, c, :]], axis=1))
        vs.append(jnp.concatenate([vc_ref[0, :, c], vp_ref[0, :, c], vm_ref[0, :, c], vn_ref[0, :, c]], axis=0))

    def stacked_queries(chunk):
        q = q_ref[0, :, chunk * LANES:(chunk + 1) * LANES]
        zero = jnp.zeros_like(q)
        return jnp.concatenate([jnp.where(first, q, zero), jnp.where(first, zero, q)], axis=0)

    def logits(chunk):
        return _dot(stacked_queries(chunk), kts[chunk // (group // 2)])

    n_chunks = WIN_Q_HEADS // 2

    def run(bounded):
        if bounded:
            k_norm = []
            for kt in kts:
                kf = kt[:WIN_HEAD_DIM].astype(F32)
                k_norm.append(jnp.sqrt(jnp.max(jnp.sum(kf * kf, axis=0, keepdims=True), axis=1, keepdims=True)))
        s_next = logits(0)
        floor = None
        for chunk in range(n_chunks):
            s = s_next + bias
            if chunk + 1 < n_chunks:
                s_next = logits(chunk + 1)
            sink = jnp.where(second_head, sink_ref[2 * chunk + 1], sink_ref[2 * chunk]) * LOG2E
            if bounded:
                qf = stacked_queries(chunk).astype(F32)
                m = jnp.sqrt(jnp.sum(qf * qf, axis=-1, keepdims=True)) * k_norm[chunk // (group // 2)]
            else:
                m = jnp.max(s, axis=-1, keepdims=True)
            m = jnp.maximum(m, sink)
            p = jnp.exp2(s - m)
            l = jnp.sum(p, axis=-1, keepdims=True) + jnp.exp2(sink - m)
            floor = l if floor is None else jnp.minimum(floor, l)
            o = _dot(p.astype(BF16), vs[chunk // (group // 2)]) / l
            o_ref[0, :, chunk * LANES:(chunk + 1) * LANES] = jnp.where(first, o[:tq], o[tq:]).astype(o_ref.dtype)
        return jnp.min(floor)

    smallest = run(bounded=True)

    @pl.when(smallest < SOFTMAX_FLOOR)
    def _():
        run(bounded=False)


def _window_attention(q, kdt, vd, kct, vc, sink):
    b, n_lat, _ = q.shape
    tq = ROW_TILE
    n_ctx = vc.shape[1]
    per = tq // WINDOW
    last = n_lat // WINDOW - 1
    prev_i = lambda i: jnp.maximum(i * per - 1, 0)
    next_i = lambda i: jnp.minimum((i + 1) * per, last)
    w = 2 * LANES
    return pl.pallas_call(
        functools.partial(_window_kernel, n_lat=n_lat),
        grid=(b, n_lat // tq),
        in_specs=[
            pl.BlockSpec(memory_space=pltpu.SMEM),
            pl.BlockSpec((1, tq, 4 * LANES), lambda bi, i: (bi, i, 0)),
            pl.BlockSpec((1, w, n_ctx), lambda bi, i: (bi, 0, 0)),
            pl.BlockSpec((1, n_ctx, w), lambda bi, i: (bi, 0, 0)),
            pl.BlockSpec((1, w, WINDOW), lambda bi, i: (bi, 0, prev_i(i))),
            pl.BlockSpec((1, w, tq), lambda bi, i: (bi, 0, i)),
            pl.BlockSpec((1, w, WINDOW), lambda bi, i: (bi, 0, next_i(i))),
            pl.BlockSpec((1, WINDOW, w), lambda bi, i: (bi, prev_i(i), 0)),
            pl.BlockSpec((1, tq, w), lambda bi, i: (bi, i, 0)),
            pl.BlockSpec((1, WINDOW, w), lambda bi, i: (bi, next_i(i), 0)),
        ],
        out_specs=pl.BlockSpec((1, tq, 4 * LANES), lambda bi, i: (bi, i, 0)),
        out_shape=jax.ShapeDtypeStruct((b, n_lat, 4 * LANES), BF16),
        compiler_params=_params("parallel", "arbitrary"),
        name="window_attention",
    )(sink, q, kct, vc, kdt, kdt, kdt, vd, vd, vd)


def _hyena_filter_kernel(feat_ref, w1_ref, b1_ref, w2_ref, b2_ref, w3_ref, b3_ref, freq_ref, decay_ref, h_ref,
                         hid_ref, *, n_forward):
    j = pl.program_id(0)
    hp = lax.Precision.HIGHEST

    @pl.when(j == 0)
    def _():
        freq = freq_ref[...]
        hid = jnp.sin(freq * (jnp.dot(feat_ref[...], w1_ref[...], precision=hp, preferred_element_type=F32) + b1_ref[...]))
        hid_ref[...] = jnp.sin(freq * (jnp.dot(hid, w2_ref[...], precision=hp, preferred_element_type=F32) + b2_ref[...]))

    h = jnp.dot(hid_ref[...], w3_ref[...], precision=hp, preferred_element_type=F32) + b3_ref[...]
    h = h * decay_ref[...]
    row = lax.broadcasted_iota(jnp.int32, h.shape, 0)
    h_ref[...] = jnp.where((row == 0) & (j >= n_forward), 0.0, h).astype(h_ref.dtype)


def _hyena_filters(feats, w1, b1, w2, b2, w3, b3, freq, decay):
    length = feats.shape[0]
    n = w3.shape[1]
    ch = decay.shape[1]
    full = lambda a: pl.BlockSpec(a.shape, lambda j: (0, 0))
    return pl.pallas_call(
        functools.partial(_hyena_filter_kernel, n_forward=HYENA_ORDER),
        grid=(n // ch,),
        in_specs=[full(feats), full(w1), full(b1), full(w2), full(b2),
                  pl.BlockSpec((w3.shape[0], ch), lambda j: (0, j)), pl.BlockSpec((1, ch), lambda j: (0, j)),
                  full(freq), full(decay)],
        out_specs=pl.BlockSpec((length, ch), lambda j: (0, j)),
        out_shape=jax.ShapeDtypeStruct((length, n), BF16),
        scratch_shapes=[pltpu.VMEM((length, w2.shape[1]), F32)],
        compiler_params=_params("arbitrary"),
        name="hyena_filters",
    )(feats, w1, b1, w2, b2, w3, b3, freq, decay)


def _spectrum_kernel(f_ref, h_ref, re_ref, im_ref):
    j = pl.program_id(0)
    tk = re_ref.shape[0]
    n = re_ref.shape[1]
    a = _dot(f_ref[0], h_ref[...])
    re_ref[...] = a[:tk, :n] + a[:tk, n:]
    row = lax.broadcasted_iota(jnp.int32, (tk, n), 0)
    sign = jnp.where((row == 0) & (j == 0), 1.0, -1.0)
    im_ref[...] = a[tk:, :n] + sign * a[tk:, n:]


def _filter_spectrum(f3, hcat):
    nt, tk2, length = f3.shape
    n = hcat.shape[1] // 2
    tk = tk2 // 2
    out = pl.BlockSpec((tk, n), lambda j: (j, 0))
    return pl.pallas_call(
        _spectrum_kernel,
        grid=(nt,),
        in_specs=[pl.BlockSpec((1, tk2, length), lambda j: (j, 0, 0)),
                  pl.BlockSpec(hcat.shape, lambda j: (0, 0))],
        out_specs=[out, out],
        out_shape=[jax.ShapeDtypeStruct((nt * tk, n), F32)] * 2,
        compiler_params=_params("arbitrary"),
        name="filter_spectrum",
    )(f3, hcat)


def _long_conv_kernel(z_ref, x_ref, f_ref, g_ref, hre_ref, him_ref, bias_ref, o_ref, acc_ref):
    j = pl.program_id(1)
    tk = hre_ref.shape[0]
    row = lax.broadcasted_iota(jnp.int32, hre_ref.shape, 0)
    packed = (row == 0) & (j == 0)
    bias = bias_ref[...]
    h_re = hre_ref[...] + bias
    h_im = him_ref[...] + jnp.where(packed, bias, 0.0)

    @pl.when(j == 0)
    def _():
        acc_ref[...] = jnp.zeros_like(acc_ref)

    spectra = [_dot(f_ref[0], z_ref[s]) for s in range(z_ref.shape[0])]
    for s, a in enumerate(spectra):
        a_re, a_im = a[:tk], a[tk:]
        ii = a_im * h_im
        p_re = a_re * h_re - jnp.where(packed, 0.0, ii)
        p_im = jnp.where(packed, ii, a_re * h_im + a_im * h_re)
        acc_ref[s] += _dot(g_ref[0], jnp.concatenate([p_re, p_im], axis=0).astype(BF16))

    @pl.when(j == pl.num_programs(1) - 1)
    def _():
        o_ref[...] = (x_ref[...].astype(F32) * acc_ref[...]).astype(o_ref.dtype)


def _long_conv(z_src, z_col, x_src, x_col, f3, g3, hre, him, h_col, bias):
    b, length, _ = z_src.shape
    nt, tk2, _ = f3.shape
    tk = tk2 // 2
    ch = HYENA_CH
    nb = CONV_SAMPLES
    return pl.pallas_call(
        _long_conv_kernel,
        grid=(b // nb, nt),
        in_specs=[
            pl.BlockSpec((nb, length, ch), lambda bi, j: (bi, 0, z_col)),
            pl.BlockSpec((nb, length, ch), lambda bi, j: (bi, 0, x_col)),
            pl.BlockSpec((1, tk2, length), lambda bi, j: (j, 0, 0)),
            pl.BlockSpec((1, length, tk2), lambda bi, j: (j, 0, 0)),
            pl.BlockSpec((tk, ch), lambda bi, j: (j, h_col)),
            pl.BlockSpec((tk, ch), lambda bi, j: (j, h_col)),
            pl.BlockSpec((1, ch), lambda bi, j: (0, 0)),
        ],
        out_specs=pl.BlockSpec((nb, length, ch), lambda bi, j: (bi, 0, 0)),
        out_shape=jax.ShapeDtypeStruct((b, length, ch), BF16),
        scratch_shapes=[pltpu.VMEM((nb, length, ch), F32)],
        compiler_params=_params("parallel", "arbitrary"),
        name="long_conv",
    )(z_src, x_src, f3, g3, hre, him, bias)


def _axial_rope(length, rot_dim):
    rows = length // GRID_W
    row = jnp.repeat(jnp.arange(rows), GRID_W).astype(F32)
    col = jnp.tile(jnp.arange(GRID_W), rows).astype(F32)
    quarter = rot_dim // 4
    inv = ROPE_BASE ** (-jnp.arange(quarter, dtype=F32) / quarter)
    ang = jnp.concatenate([row[:, None] * inv, col[:, None] * inv], axis=-1)
    return jnp.cos(ang), jnp.sin(ang)


def _rope_tables(length, n_ctx):
    cos_m, sin_m = _axial_rope(length, MLA_ROPE)
    cos_d, sin_d = _axial_rope(length, DIFF_HEAD_DIM)
    ones = lambda w: jnp.ones((length, w), F32)
    zeros = lambda w: jnp.zeros((length, w), F32)
    cm = jnp.concatenate([ones(MLA_NOPE), cos_m, cos_m, ones(LANES - MLA_NOPE - MLA_ROPE)], axis=1)
    sm = jnp.concatenate([zeros(MLA_NOPE), -sin_m, sin_m, zeros(LANES - MLA_NOPE - MLA_ROPE)], axis=1)
    cd = jnp.concatenate([cos_d] * 4, axis=1)
    sd = jnp.concatenate([-sin_d, sin_d] * 2, axis=1)
    pad = lambda t, v: jnp.concatenate([t, jnp.full((n_ctx, LANES), v, F32)], axis=0)
    return pad(cm, 1.0), pad(sm, 0.0), pad(cd, 1.0), pad(sd, 0.0)


def _dft_matrices(length):
    n = 2 * length
    k = np.arange(length, dtype=np.int64)
    ang = 2.0 * np.pi * ((k[:, None] * k[None, :]) % n).astype(np.float64) / n
    alt = np.where(k % 2 == 0, 1.0, -1.0)
    f_re = np.cos(ang)
    f_im = -np.sin(ang)
    f_im[0, :] = alt
    g_re = (2.0 / n) * np.cos(ang.T)
    g_im = -(2.0 / n) * np.sin(ang.T)
    g_re[:, 0] = 1.0 / n
    g_im[:, 0] = alt / n
    nt = length // FREQ_TILE
    f3 = np.concatenate([f_re.reshape(nt, FREQ_TILE, length), f_im.reshape(nt, FREQ_TILE, length)], axis=1)
    g3 = np.concatenate([g_re.reshape(length, nt, FREQ_TILE), g_im.reshape(length, nt, FREQ_TILE)], axis=2)
    g3 = np.transpose(g3, (1, 0, 2))
    return jnp.asarray(f3, dtype=BF16), jnp.asarray(g3, dtype=BF16)


def _hyena_features(length):
    t = jnp.arange(length, dtype=F32)
    tn = t / max(length - 1, 1)
    bands = jnp.linspace(1e-4, HYENA_BANDS - 1, HYENA_BANDS, dtype=F32)
    ang = 2.0 * math.pi * bands[None, :] * t[:, None] / length
    feats = jnp.concatenate([tn[:, None], jnp.cos(ang), -jnp.sin(ang)], axis=-1)
    min_decay = math.log(HYENA_DECAY_TARGET) / HYENA_SLOW_PCT
    max_decay = math.log(HYENA_DECAY_TARGET) / HYENA_FAST_PCT
    deltas = jnp.abs(jnp.linspace(min_decay, max_decay, HYENA_CH, dtype=F32))
    decay = jnp.exp(-tn[:, None] * deltas[None, :])
    return feats, decay


def _pad_cols(w, left, right):
    return jnp.pad(w, ((0, 0),) * (w.ndim - 1) + ((left, right),))


def _even_weights(w_in, w_uq, w_ukv):
    d = w_in.shape[0]
    o = MLA_Q_RANK + MLA_KV_RANK
    w_kr = _pad_cols(w_in[:, o:o + MLA_ROPE], MLA_NOPE, LANES - MLA_NOPE - MLA_ROPE)
    w_in_ext = jnp.concatenate([w_in[:, :o], w_kr, w_in[:, o + MLA_ROPE:]], axis=1)
    uq = _pad_cols(w_uq.reshape(MLA_Q_RANK, MLA_HEADS, MLA_NOPE + MLA_ROPE), 0, LANES - MLA_NOPE - MLA_ROPE)
    ukv = w_ukv.reshape(MLA_KV_RANK, MLA_HEADS, MLA_NOPE + MLA_V)
    uk = _pad_cols(ukv[:, :, :MLA_NOPE], 0, LANES - MLA_NOPE)
    uv = ukv[:, :, MLA_NOPE:].reshape(MLA_KV_RANK, MLA_HEADS // 2, 2, MLA_V)
    uv = jnp.stack([_pad_cols(uv[:, :, 0], 0, LANES - MLA_V), _pad_cols(uv[:, :, 1], LANES - MLA_V, 0)], axis=2)
    w_ukv_ext = jnp.concatenate([uk.reshape(MLA_KV_RANK, -1), uv.reshape(MLA_KV_RANK, -1)], axis=1)
    return w_in_ext.astype(BF16), uq.reshape(MLA_Q_RANK, -1).astype(BF16), w_ukv_ext.astype(BF16)


def _odd_weights(w_in):
    d = w_in.shape[0]
    nq = WIN_Q_HEADS * WIN_HEAD_DIM
    nkv = WIN_KV_HEADS * WIN_HEAD_DIM
    dup = lambda w: jnp.concatenate([w.reshape(d, WIN_KV_HEADS, WIN_HEAD_DIM)] * 2, axis=-1).reshape(d, -1)
    wk = dup(w_in[:, nq:nq + nkv])
    wv = dup(w_in[:, nq + nkv:nq + 2 * nkv])
    wqkv = jnp.concatenate([w_in[:, :nq], wk, wv], axis=1).astype(BF16)
    wkv = jnp.concatenate([wk, wv], axis=1).astype(BF16)
    wu = w_in[:, nq + 2 * nkv:].astype(BF16)
    return wqkv, wkv, wu


def _ffn_weights(w_up, conv_w, conv_b, w_down):
    d = w_up.shape[0]
    nc = D_FF // FFN_CHUNK
    chunks = lambda w: jnp.transpose(w.reshape(w.shape[0], nc, FFN_CHUNK), (1, 0, 2))
    cw = jnp.concatenate([conv_w, conv_b[None, :]], axis=0)
    return (chunks(w_up[:, :D_FF]).astype(BF16), chunks(w_up[:, D_FF:]).astype(BF16),
            chunks(cw[:, :D_FF]), chunks(cw[:, D_FF:]),
            w_down.astype(BF16))


def kernel(x, c, ctx, c_ctx, ada_w, ada_b, norm_g, mix_w_out, ffn_w_up, ffn_conv_w, ffn_conv_b, ffn_w_down, even_w_in, mla_q_norm_g, mla_w_uq, mla_kv_norm_g, mla_w_ukv, diff_lambda, diff_subln_g, odd_w_in, win_sink, hy_conv_w, hy_conv_b, hy_f_w1, hy_f_b1, hy_f_w2, hy_f_b2, hy_f_w3, hy_f_b3, hy_f_freq, hy_bias):
    b, seq, d = x.shape
    n_ctx = ctx.shape[1]
    depth = ada_w.shape[0]
    assert depth == 2 and d == D_MODEL and seq % ROW_TILE == 0 and n_ctx == ROW_TILE
    n_lat_tiles = seq // ROW_TILE
    n_tiles = n_lat_tiles + 1

    pad_rows = (-(b + 1)) % 8
    cond = jnp.concatenate([c, c_ctx[None, :], jnp.zeros((pad_rows, d), F32)], axis=0)
    mod = _ada_table(cond, ada_w, ada_b).reshape(depth, b + 1 + pad_rows, 6, d)[:, :b + 1]
    g = norm_g.reshape(depth, 4, 1, d)
    cm, sm, cd, sd = _rope_tables(seq, n_ctx)

    lambda_init = 0.8 - 0.6 * math.exp(-0.3 * 0)
    w_in_e, w_uq_e, w_ukv_e = _even_weights(even_w_in[0], mla_w_uq[0], mla_w_ukv[0])
    rows = seq + n_ctx
    even_w = (g[0, 0], w_in_e, mla_q_norm_g[0][None, :], w_uq_e, mla_kv_norm_g[0][None, :], w_ukv_e)
    proj = _proj_even(x, mod[0], even_w, (cm, sm, cd, sd), rows, PROJ_ROW_TILE, 0, False)
    qm, km, vm, dq, dk, dv = _proj_even(ctx, mod[0], even_w, (cm, sm, cd, sd), rows, n_ctx, seq, True, prior=proj)
    lam_args = (diff_lambda[0], diff_subln_g[0][None, :], lambda_init)
    w_out0 = mix_w_out[0].astype(BF16)
    ffn0 = _ffn_weights(ffn_w_up[0], ffn_conv_w[0], ffn_conv_b[0], ffn_w_down[0])
    o_mla = _mla_attention(qm, km, vm, ATTN_Q_TILE, 0, seq, 0, rows)
    o_diff = _diff_attention(dq, dk, dv, *lam_args, ATTN_Q_TILE, 0, seq, 0, rows)
    x_lat = _mix_ffn(o_mla, o_diff, x, mod[0], w_out0, g[0, 1], g[0, 2], *ffn0, g[0, 3], FFN_ROW_TILE, False)
    o_mla = _mla_attention(qm, km, vm, n_ctx, seq, n_ctx, seq, n_ctx)
    o_diff = _diff_attention(dq, dk, dv, *lam_args, n_ctx, seq, n_ctx, seq, n_ctx)
    x_ctx = _mix_ffn(o_mla, o_diff, ctx, mod[0], w_out0, g[0, 1], g[0, 2], *ffn0, g[0, 3], n_ctx, True)

    wqkv, wkv, wu = _odd_weights(odd_w_in[0])
    cw_hy = jnp.concatenate([hy_conv_w[0], hy_conv_b[0][None, :]], axis=0)
    q, kd, vd, u = _proj_odd(x_lat, mod[1], g[1, 0], wqkv, wu, cw_hy, cd, sd, PROJ_ROW_TILE)
    kc, vc = _proj_ctx_kv(x_ctx, mod[1], g[1, 0], wkv, 0, n_ctx)
    o_win = _window_attention(q, kd, vd, kc, vc, win_sink[0])

    feats, decay = _hyena_features(seq)
    emb = feats.shape[1]
    feats = _pad_cols(feats, 0, LANES - emb)
    w1 = jnp.pad(hy_f_w1[0], ((0, LANES - emb), (0, 0)))
    hcat = _hyena_filters(feats, w1, hy_f_b1[0][None, :], hy_f_w2[0], hy_f_b2[0][None, :], hy_f_w3[0],
                          hy_f_b3[0][None, :], hy_f_freq[0][None, :], decay)
    f3, g3 = _dft_matrices(seq)
    hre, him = _filter_spectrum(f3, hcat)
    z = _long_conv(u, 0, u, 1, f3, g3, hre, him, 0, hy_bias[0, 0][None, :])
    z = _long_conv(z, 0, u, 2, f3, g3, hre, him, 1, hy_bias[0, 1][None, :])

    w_out1 = mix_w_out[1].astype(BF16)
    ffn1 = _ffn_weights(ffn_w_up[1], ffn_conv_w[1], ffn_conv_b[1], ffn_w_down[1])
    return _mix_ffn(o_win, z, x_lat, mod[1], w_out1, g[1, 1], g[1, 2], *ffn1, g[1, 3], FFN_ROW_TILE, False)
```

```python
import functools
import math

import numpy as np
import jax
import jax.numpy as jnp
from jax import lax
from jax.experimental import pallas as pl
from jax.experimental.pallas import tpu as pltpu

D_MODEL = 1024
GRID_W = 64
MLA_HEADS = 8
MLA_NOPE = 64
MLA_ROPE = 32
MLA_V = 64
MLA_Q_RANK = 384
MLA_KV_RANK = 256
DIFF_HEADS = 4
DIFF_HEAD_DIM = 64
DIFF_V_DIM = 128
WIN_Q_HEADS = 8
WIN_KV_HEADS = 2
WIN_HEAD_DIM = 64
WINDOW = 128
HYENA_CH = 512
HYENA_ORDER = 2
HYENA_BANDS = 16
HYENA_DECAY_TARGET = 1e-2
HYENA_FAST_PCT = 0.3
HYENA_SLOW_PCT = 1.5
D_FF = 2816
ROPE_BASE = 10000.0
NORM_EPS = 1e-6
NEG_INF = -1e30

LANES = 128
HALO_ROWS = 16
ROW_TILE = 256
ATTN_Q_TILE = 2048
ATTN_SUB_TILE = 512
ATTN_KEY_CHUNK = 256
LOG2E = math.log2(math.e)
FFN_ROW_TILE = 512
PROJ_ROW_TILE = 512
FFN_CHUNK = 256
FREQ_TILE = 256
CONV_SAMPLES = 2
VMEM_LIMIT = 48 * 1024 * 1024

BF16 = jnp.bfloat16
F32 = jnp.float32


def _params(*semantics):
    return pltpu.CompilerParams(dimension_semantics=semantics, vmem_limit_bytes=VMEM_LIMIT)


def _rms(x, g):
    return x * lax.rsqrt(jnp.mean(x * x, axis=-1, keepdims=True) + NORM_EPS) * g


def _dot(a, b):
    return jnp.dot(a, b, preferred_element_type=F32)


def _lane_iota(shape):
    return lax.broadcasted_iota(jnp.int32, shape, len(shape) - 1)


def _rope_chunk(x, cos, sin_signed, half):
    first = (_lane_iota(x.shape) % (2 * half)) < half
    partner = jnp.where(first, pltpu.roll(x, LANES - half, 1), pltpu.roll(x, half, 1))
    return x * cos + partner * sin_signed


def _rope_rows(xt, cos_t, sin_signed_t, half):
    first = (lax.broadcasted_iota(jnp.int32, xt.shape, 0) % (2 * half)) < half
    partner = jnp.where(first, pltpu.roll(xt, LANES - half, 0), pltpu.roll(xt, half, 0))
    return xt * cos_t + partner * sin_signed_t


def _ada_kernel(c_ref, w_ref, b_ref, o_ref):
    c = c_ref[...]
    sc = c * jax.nn.sigmoid(c)
    o_ref[0] = _dot(sc.astype(BF16), w_ref[0].astype(BF16)) + b_ref[0]


def _ada_table(cond, ada_w, ada_b):
    depth, d, n = ada_w.shape
    rows = cond.shape[0]
    tn = 1536
    return pl.pallas_call(
        _ada_kernel,
        grid=(depth, n // tn),
        in_specs=[
            pl.BlockSpec((rows, d), lambda l, j: (0, 0)),
            pl.BlockSpec((1, d, tn), lambda l, j: (l, 0, j)),
            pl.BlockSpec((1, 1, tn), lambda l, j: (l, 0, j)),
        ],
        out_specs=pl.BlockSpec((1, rows, tn), lambda l, j: (l, 0, j)),
        out_shape=jax.ShapeDtypeStruct((depth, rows, n), F32),
        compiler_params=_params("arbitrary", "arbitrary"),
        name="ada_table",
    )(cond, ada_w, ada_b.reshape(depth, 1, n))


def _proj_even_kernel(x_ref, mod_ref, g_ref, win_ref, qg_ref, wuq_ref, kvg_ref, wukv_ref,
                      cm_ref, sm_ref, cd_ref, sd_ref, cmt_ref, smt_ref, cdt_ref, sdt_ref, *rest):
    qm_ref, km_ref, vm_ref, dq_ref, dk_ref, dv_ref = rest[-6:]
    x = x_ref[0]
    mod = mod_ref[0]
    h = _rms(x, g_ref[...]) * (1.0 + mod[1:2]) + mod[0:1]
    p = _dot(h.astype(BF16), win_ref[...])
    o = 0
    cq = p[:, o:o + MLA_Q_RANK]; o += MLA_Q_RANK
    ckv = p[:, o:o + MLA_KV_RANK]; o += MLA_KV_RANK
    kr = p[:, o:o + LANES]; o += LANES
    dq = p[:, o:o + 512]; o += 512
    dk = p[:, o:o + 512]; o += 512
    dv = p[:, o:o + 512]
    q = _dot(_rms(cq, qg_ref[...]).astype(BF16), wuq_ref[...])
    kv = _dot(_rms(ckv, kvg_ref[...]).astype(BF16), wukv_ref[...])
    cm, sm, cd, sd = cm_ref[...], sm_ref[...], cd_ref[...], sd_ref[...]
    cmt, smt, cdt, sdt = cmt_ref[...], smt_ref[...], cdt_ref[...], sdt_ref[...]
    mla_scale = (MLA_NOPE + MLA_ROPE) ** -0.5 * LOG2E
    diff_scale = DIFF_HEAD_DIM ** -0.5 * LOG2E
    kr = _rope_chunk(kr, cm, sm, MLA_ROPE // 2)
    for hd in range(MLA_HEADS):
        c = slice(hd * LANES, (hd + 1) * LANES)
        qm_ref[0, c, :] = (_rope_rows(q[:, c].T, cmt, smt, MLA_ROPE // 2) * mla_scale).astype(BF16)
        km_ref[0, :, c] = (kv[:, c] + kr).astype(BF16)
        vm_ref[0, c, :] = kv[:, MLA_HEADS * LANES + hd * LANES:MLA_HEADS * LANES + (hd + 1) * LANES].astype(BF16).T
    for hd in range(DIFF_HEADS):
        c = slice(hd * LANES, (hd + 1) * LANES)
        dq_ref[0, c, :] = (_rope_rows(dq[:, c].T, cdt, sdt, DIFF_HEAD_DIM // 2) * diff_scale).astype(BF16)
        dk_ref[0, :, c] = _rope_chunk(dk[:, c], cd, sd, DIFF_HEAD_DIM // 2).astype(BF16)
        dv_ref[0, c, :] = dv[:, c].astype(BF16).T


def _proj_even(x, mod, weights, tables, rows, tm, row_off, ctx_mod, prior=None):
    b, n, d = x.shape
    off = row_off // tm
    ctx_row = mod.shape[0] - 1
    const = lambda a: pl.BlockSpec(a.shape, lambda bi, i: (0,) * a.ndim)
    tab = pl.BlockSpec((tm, LANES), lambda bi, i: (off + i, 0))
    out = lambda w: pl.BlockSpec((1, tm, w), lambda bi, i: (bi, off + i, 0))
    shp = lambda w: jax.ShapeDtypeStruct((b, rows, w), BF16)
    out_t = lambda w: pl.BlockSpec((1, w, tm), lambda bi, i: (bi, 0, off + i))
    shp_t = lambda w: jax.ShapeDtypeStruct((b, w, rows), BF16)
    tab_t = pl.BlockSpec((LANES, tm), lambda bi, i: (0, off + i))
    tables_t = [t.T for t in tables]
    n_in = 2 + len(weights) + 2 * len(tables)
    prior = () if prior is None else tuple(prior)
    return pl.pallas_call(
        _proj_even_kernel,
        grid=(b, n // tm),
        in_specs=[
            pl.BlockSpec((1, tm, d), lambda bi, i: (bi, i, 0)),
            pl.BlockSpec((1, 6, d), lambda bi, i: (ctx_row if ctx_mod else bi, 0, 0)),
            *[const(w) for w in weights], *[tab for _ in tables], *[tab_t for _ in tables],
            *[pl.BlockSpec(memory_space=pl.ANY) for _ in prior],
        ],
        out_specs=[out_t(1024), out(1024), out_t(1024), out_t(512), out(512), out_t(512)],
        out_shape=[shp_t(1024), shp(1024), shp_t(1024), shp_t(512), shp(512), shp_t(512)],
        input_output_aliases={n_in + k: k for k in range(len(prior))},
        compiler_params=_params("parallel", "arbitrary"),
        name="proj_even",
    )(x, mod, *weights, *tables, *tables_t, *prior)


SOFTMAX_FLOOR = 2.0 ** -90


def _logits_t(qt, k):
    nk = k.shape[0]
    step = min(nk, ATTN_KEY_CHUNK)
    return [_dot(k[a:a + step], qt) for a in range(0, nk, step)]


def _softmax_pv_t(s, vt, shift=None):
    step = s[0].shape[0]
    if shift is None:
        shift = functools.reduce(jnp.maximum, [jnp.max(si, axis=0, keepdims=True) for si in s])
    p = [jnp.exp2(si - shift) for si in s]
    l = sum(jnp.sum(pi, axis=0, keepdims=True) for pi in p)
    o = sum(_dot(vt[:, j * step:(j + 1) * step], pi.astype(BF16)) for j, pi in enumerate(p))
    return o, l


def _max_key_norm(k, lanes=None):
    kf = k.astype(F32)
    sq = kf * kf if lanes is None else jnp.where(lanes, kf * kf, 0.0)
    return jnp.sqrt(jnp.max(jnp.sum(sq, axis=-1, keepdims=True), axis=0, keepdims=True))


def _query_norms(qt):
    qf = qt.astype(F32)
    return jnp.sqrt(jnp.sum(qf * qf, axis=0, keepdims=True))


def _mla_kernel(qt_ref, k_ref, vt_ref, o_ref):
    sub = min(qt_ref.shape[2], ATTN_SUB_TILE)
    items = [(r, hd) for r in range(qt_ref.shape[2] // sub) for hd in range(2)]

    def query(item):
        r, hd = item
        return qt_ref[0, hd * LANES:(hd + 1) * LANES, r * sub:(r + 1) * sub]

    def logits(item):
        return _logits_t(query(item), k_ref[0, :, item[1] * LANES:(item[1] + 1) * LANES])

    def run(bounded):
        if bounded:
            k_norm = [_max_key_norm(k_ref[0, :, hd * LANES:(hd + 1) * LANES]) for hd in range(2)]
        s_next = logits(items[0])
        acc, floor = None, None
        for j, (r, hd) in enumerate(items):
            s = s_next
            if j + 1 < len(items):
                s_next = logits(items[j + 1])
            shift = _query_norms(query((r, hd))) * k_norm[hd] if bounded else None
            o, l = _softmax_pv_t(s, vt_ref[0, hd * LANES:(hd + 1) * LANES, :], shift)
            floor = l if floor is None else jnp.minimum(floor, l)
            acc = o / l if hd == 0 else acc + o / l
            if hd == 1:
                o_ref[0, r * sub:(r + 1) * sub, :] = acc.T.astype(o_ref.dtype)
        return jnp.min(floor)

    smallest = run(bounded=True)

    @pl.when(smallest < SOFTMAX_FLOOR)
    def _():
        run(bounded=False)


def _mla_attention(qmt, km, vmt, tq, q_off, n_q, k_off, n_k):
    b = km.shape[0]
    pairs = MLA_HEADS // 2
    qo, ko = q_off // tq, k_off // n_k
    return pl.pallas_call(
        _mla_kernel,
        grid=(b, pairs, n_q // tq),
        in_specs=[
            pl.BlockSpec((1, 2 * LANES, tq), lambda bi, p, i: (bi, p, qo + i)),
            pl.BlockSpec((1, n_k, 2 * LANES), lambda bi, p, i: (bi, ko, p)),
            pl.BlockSpec((1, 2 * LANES, n_k), lambda bi, p, i: (bi, p, ko)),
        ],
        out_specs=pl.BlockSpec((1, tq, LANES), lambda bi, p, i: (bi, i, p)),
        out_shape=jax.ShapeDtypeStruct((b, n_q, pairs * LANES), BF16),
        compiler_params=_params("parallel", "arbitrary", "arbitrary"),
        name="mla_attention",
    )(qmt, km, vmt)


def _diff_kernel(qt_ref, k_ref, vt_ref, lam_ref, g_ref, o_ref, *, lambda_init):
    lv = lam_ref[...]
    lam = (jnp.exp(jnp.sum(lv[0:1] * lv[1:2], axis=-1, keepdims=True))
           - jnp.exp(jnp.sum(lv[2:3] * lv[3:4], axis=-1, keepdims=True)) + lambda_init)
    vt = vt_ref[0]
    sub = min(qt_ref.shape[2], ATTN_SUB_TILE)
    items = [(r, which) for r in range(qt_ref.shape[2] // sub) for which in range(2)]

    def query(item):
        r, which = item
        qt = qt_ref[0, :, r * sub:(r + 1) * sub]
        first = lax.broadcasted_iota(jnp.int32, qt.shape, 0) < DIFF_HEAD_DIM
        zero = jnp.zeros_like(qt)
        return jnp.where(first, qt, zero) if which == 0 else jnp.where(first, zero, qt)

    def logits(item):
        return _logits_t(query(item), k_ref[0])

    def run(bounded):
        if bounded:
            first_k = _lane_iota(k_ref.shape[1:]) < DIFF_HEAD_DIM
            k_norm = [_max_key_norm(k_ref[0], first_k), _max_key_norm(k_ref[0], ~first_k)]
        s_next = logits(items[0])
        o1, floor = None, None
        for j, (r, which) in enumerate(items):
            s = s_next
            if j + 1 < len(items):
                s_next = logits(items[j + 1])
            shift = _query_norms(query((r, which))) * k_norm[which] if bounded else None
            o, l = _softmax_pv_t(s, vt, shift)
            floor = l if floor is None else jnp.minimum(floor, l)
            if which == 0:
                o1 = o / l
            else:
                o = (o1 - lam * (o / l)).T
                o_ref[0, r * sub:(r + 1) * sub, :] = (_rms(o, g_ref[...]) * (1.0 - lambda_init)).astype(o_ref.dtype)
        return jnp.min(floor)

    smallest = run(bounded=True)

    @pl.when(smallest < SOFTMAX_FLOOR)
    def _():
        run(bounded=False)


def _diff_attention(dq, dk, dv, diff_lambda, subln_g, lambda_init, tq, q_off, n_q, k_off, n_k):
    b = dq.shape[0]
    qo, ko = q_off // tq, k_off // n_k
    return pl.pallas_call(
        functools.partial(_diff_kernel, lambda_init=lambda_init),
        grid=(b, DIFF_HEADS, n_q // tq),
        in_specs=[
            pl.BlockSpec((1, LANES, tq), lambda bi, h, i: (bi, h, qo + i)),
            pl.BlockSpec((1, n_k, LANES), lambda bi, h, i: (bi, ko, h)),
            pl.BlockSpec((1, LANES, n_k), lambda bi, h, i: (bi, h, ko)),
            pl.BlockSpec((4, DIFF_HEAD_DIM), lambda bi, h, i: (0, 0)),
            pl.BlockSpec((1, DIFF_V_DIM), lambda bi, h, i: (0, 0)),
        ],
        out_specs=pl.BlockSpec((1, tq, LANES), lambda bi, h, i: (bi, i, h)),
        out_shape=jax.ShapeDtypeStruct((b, n_q, DIFF_HEADS * LANES), BF16),
        compiler_params=_params("parallel", "arbitrary", "arbitrary"),
        name="diff_attention",
    )(dq, dk, dv, diff_lambda, subln_g)


def _mix_ffn_kernel(a_ref, ap_ref, an_ref, b_ref, bp_ref, bn_ref, x_ref, xp_ref, xn_ref, mod_ref,
                    wo_ref, g1_ref, g2_ref, wa_ref, wg_ref, cwa_ref, cwg_ref, wd_ref, g3_ref,
                    o_ref, lhs_ref, x1_ref, ua0_ref, ug0_ref, ua1_ref, ug1_ref, act_ref):
    i = pl.program_id(1)
    tm = a_ref.shape[1]
    half = a_ref.shape[2]
    n_chunks = wa_ref.shape[0]
    rows = tm + 2 * HALO_ROWS
    mid = slice(HALO_ROWS, HALO_ROWS + tm)
    mod = mod_ref[0]

    ext = lambda p_ref, m_ref, n_ref: jnp.concatenate([p_ref[0], m_ref[0], n_ref[0]], axis=0)
    y = _dot(ext(ap_ref, a_ref, an_ref), wo_ref[0:half, :]) + _dot(ext(bp_ref, b_ref, bn_ref), wo_ref[half:, :])
    x1 = ext(xp_ref, x_ref, xn_ref) + mod[2:3] * _rms(y, g1_ref[...])
    h2 = _rms(x1, g2_ref[...]) * (1.0 + mod[4:5]) + mod[3:4]
    r = lax.broadcasted_iota(jnp.int32, h2.shape, 0)
    pad = ((r < HALO_ROWS) & (i == 0)) | ((r >= HALO_ROWS + tm) & (i == pl.num_programs(1) - 1))
    lhs_ref[...] = jnp.where(pad, 0.0, h2).astype(BF16)
    x1_ref[...] = x1[mid]
    slots = ((ua0_ref, ug0_ref), (ua1_ref, ug1_ref))

    def up(j, slot):
        lhs = lhs_ref[...]
        slots[slot][0][...] = _dot(lhs, wa_ref[j])
        slots[slot][1][...] = _dot(lhs, wg_ref[j])

    def conv(u, cw):
        down = pltpu.roll(u, 1, 0)[mid]
        above = pltpu.roll(u, rows - 1, 0)[mid]
        return down * cw[0:1] + u[mid] * cw[1:2] + above * cw[2:3] + cw[3:4]

    width = wa_ref.shape[2]

    def activate(j):
        a = conv(slots[j % 2][0][...], cwa_ref[j])
        g = conv(slots[j % 2][1][...], cwg_ref[j])
        act_ref[:, j * width:(j + 1) * width] = ((g * jax.nn.sigmoid(g)) * a).astype(BF16)

    up(0, 0)
    up(1, 1)
    for j in range(n_chunks):
        activate(j)
        if j + 2 < n_chunks:
            up(j + 2, j % 2)
    o_ref[0] = x1_ref[...] + mod[5:6] * _rms(_dot(act_ref[...], wd_ref[...]), g3_ref[...])


def _mix_ffn(a, bb, x, mod, w_out, g1, g2, wa, wg, cwa, cwg, wd, g3, tm, ctx_mod):
    b, n_rows, half = a.shape
    d = x.shape[-1]
    ctx_row = mod.shape[0] - 1
    per = tm // HALO_ROWS
    last_halo = n_rows // HALO_ROWS - 1
    const = lambda *shape: pl.BlockSpec(shape, lambda bi, i: (0,) * len(shape), pipeline_mode=pl.Buffered(1))
    tile = lambda w: pl.BlockSpec((1, tm, w), lambda bi, i: (bi, i, 0))
    prev = lambda w: pl.BlockSpec((1, HALO_ROWS, w), lambda bi, i: (bi, jnp.maximum(i * per - 1, 0), 0))
    nxt = lambda w: pl.BlockSpec((1, HALO_ROWS, w), lambda bi, i: (bi, jnp.minimum((i + 1) * per, last_halo), 0))
    rows = tm + 2 * HALO_ROWS
    return pl.pallas_call(
        _mix_ffn_kernel,
        grid=(b, n_rows // tm),
        in_specs=[
            tile(half), prev(half), nxt(half), tile(half), prev(half), nxt(half), tile(d), prev(d), nxt(d),
            pl.BlockSpec((1, 6, d), lambda bi, i: (ctx_row if ctx_mod else bi, 0, 0)),
            const(*w_out.shape), const(1, d), const(1, d),
            const(*wa.shape), const(*wg.shape), const(*cwa.shape), const(*cwg.shape), const(*wd.shape),
            const(1, d),
        ],
        out_specs=tile(d),
        out_shape=jax.ShapeDtypeStruct((b, n_rows, d), F32),
        scratch_shapes=[pltpu.VMEM((rows, d), BF16), pltpu.VMEM((tm, d), F32)]
        + [pltpu.VMEM((rows, FFN_CHUNK), F32)] * 4
        + [pltpu.VMEM((tm, wd.shape[0]), BF16)],
        compiler_params=_params("parallel", "arbitrary"),
        name="mix_ffn",
    )(a, a, a, bb, bb, bb, x, x, x, mod, w_out, g1, g2, wa, wg, cwa, cwg, wd, g3)


def _proj_odd_kernel(x_ref, xp_ref, xn_ref, mod_ref, g_ref, wqkv_ref, wu_ref, cw_ref, cd_ref, sd_ref,
                     q_ref, k_ref, v_ref, u_ref, *, n_tiles):
    i = pl.program_id(1)
    mod = mod_ref[0]

    def normed(x):
        return (_rms(x, g_ref[...]) * (1.0 + mod[1:2]) + mod[0:1]).astype(BF16)

    h = normed(x_ref[0])
    p = _dot(h, wqkv_ref[...])
    cd, sd = cd_ref[...], sd_ref[...]
    scale = WIN_HEAD_DIM ** -0.5 * LOG2E
    for c in range(4):
        cs = slice(c * LANES, (c + 1) * LANES)
        q_ref[0, :, cs] = (_rope_chunk(p[:, cs], cd, sd, WIN_HEAD_DIM // 2) * scale).astype(BF16)
    for c in range(2):
        cs = slice(512 + c * LANES, 512 + (c + 1) * LANES)
        k_ref[0, c * LANES:(c + 1) * LANES, :] = _rope_chunk(p[:, cs], cd, sd, WIN_HEAD_DIM // 2).astype(BF16).T
    v_ref[0] = p[:, 768:1024].astype(BF16)
    hp = normed(xp_ref[0])
    hn = normed(xn_ref[0])
    ext = jnp.concatenate([jnp.where(i == 0, jnp.zeros_like(hp), hp), h,
                           jnp.where(i == n_tiles - 1, jnp.zeros_like(hn), hn)], axis=0)
    u = _dot(ext, wu_ref[...])
    rows = u.shape[0]
    mid = slice(HALO_ROWS, rows - HALO_ROWS)
    cw = cw_ref[...]
    conv = pltpu.roll(u, 1, 0)[mid] * cw[0:1] + u[mid] * cw[1:2] + pltpu.roll(u, rows - 1, 0)[mid] * cw[2:3] + cw[3:4]
    u_ref[0] = conv.astype(u_ref.dtype)


def _proj_odd(xcat, mod, g, wqkv, wu, cw, cd, sd, tm):
    b, rows, d = xcat.shape
    n_tiles = rows // tm
    per = tm // HALO_ROWS
    last_halo = n_tiles * per - 1
    const = lambda *shape: pl.BlockSpec(shape, lambda bi, i: (0,) * len(shape))
    tab = pl.BlockSpec((tm, LANES), lambda bi, i: (i, 0))
    out = lambda w: pl.BlockSpec((1, tm, w), lambda bi, i: (bi, i, 0))
    return pl.pallas_call(
        functools.partial(_proj_odd_kernel, n_tiles=n_tiles),
        grid=(b, n_tiles),
        in_specs=[
            pl.BlockSpec((1, tm, d), lambda bi, i: (bi, i, 0)),
            pl.BlockSpec((1, HALO_ROWS, d), lambda bi, i: (bi, jnp.maximum(i * per - 1, 0), 0)),
            pl.BlockSpec((1, HALO_ROWS, d), lambda bi, i: (bi, jnp.minimum((i + 1) * per, last_halo), 0)),
            pl.BlockSpec((1, 6, d), lambda bi, i: (bi, 0, 0)),
            const(1, d), const(*wqkv.shape), const(*wu.shape), const(*cw.shape), tab, tab,
        ],
        out_specs=[out(512), pl.BlockSpec((1, 2 * LANES, tm), lambda bi, i: (bi, 0, i)), out(256),
                   out(3 * HYENA_CH)],
        out_shape=[jax.ShapeDtypeStruct((b, rows, 512), BF16), jax.ShapeDtypeStruct((b, 2 * LANES, rows), BF16),
                   jax.ShapeDtypeStruct((b, rows, 256), BF16), jax.ShapeDtypeStruct((b, rows, 3 * HYENA_CH), BF16)],
        compiler_params=_params("parallel", "arbitrary"),
        name="proj_odd",
    )(xcat, xcat, xcat, mod, g, wqkv, wu, cw, cd, sd)


def _proj_ctx_kv_kernel(x_ref, mod_ref, g_ref, w_ref, k_ref, v_ref):
    mod = mod_ref[0]
    h = (_rms(x_ref[0], g_ref[...]) * (1.0 + mod[1:2]) + mod[0:1]).astype(BF16)
    p = _dot(h, w_ref[...])
    k_ref[0] = p[:, :2 * LANES].astype(BF16).T
    v_ref[0] = p[:, 2 * LANES:].astype(BF16)


def _proj_ctx_kv(xcat, mod, g, wkv, ctx_tile, n_ctx):
    b, _, d = xcat.shape
    ctx_row = mod.shape[0] - 1
    out = pl.BlockSpec((1, n_ctx, 2 * LANES), lambda bi: (bi, 0, 0))
    return pl.pallas_call(
        _proj_ctx_kv_kernel,
        grid=(b,),
        in_specs=[
            pl.BlockSpec((1, n_ctx, d), lambda bi: (bi, ctx_tile, 0)),
            pl.BlockSpec((1, 6, d), lambda bi: (ctx_row, 0, 0)),
            pl.BlockSpec((1, d), lambda bi: (0, 0)),
            pl.BlockSpec(wkv.shape, lambda bi: (0, 0)),
        ],
        out_specs=[pl.BlockSpec((1, 2 * LANES, n_ctx), lambda bi: (bi, 0, 0)), out],
        out_shape=[jax.ShapeDtypeStruct((b, 2 * LANES, n_ctx), BF16), jax.ShapeDtypeStruct((b, n_ctx, 2 * LANES), BF16)],
        compiler_params=_params("parallel"),
        name="proj_ctx_kv",
    )(xcat, mod, g, wkv)


def _window_kernel(sink_ref, q_ref, kc_ref, vc_ref, kp_ref, km_ref, kn_ref, vp_ref, vm_ref, vn_ref, o_ref,
                   *, n_lat):
    i = pl.program_id(1)
    tq = q_ref.shape[1]
    n_ctx = vc_ref.shape[1]
    nk = n_ctx + tq + 2 * WINDOW
    col = lax.broadcasted_iota(jnp.int32, (tq, nk), 1)
    row = lax.broadcasted_iota(jnp.int32, (tq, nk), 0)
    k_pos = i * tq - WINDOW + (col - n_ctx)
    q_pos = i * tq + row
    visible = (col < n_ctx) | ((jnp.abs(q_pos - k_pos) <= WINDOW) & (k_pos >= 0) & (k_pos < n_lat))
    bias = jnp.where(visible, 0.0, NEG_INF)
    group = WIN_Q_HEADS // WIN_KV_HEADS
    bias = jnp.concatenate([bias, bias], axis=0)
    second_head = lax.broadcasted_iota(jnp.int32, (2 * tq, 1), 0) >= tq
    first = _lane_iota((tq, LANES)) < WIN_HEAD_DIM
    kts, vs = [], []
    for hk in range(WIN_KV_HEADS):
        c = slice(hk * LANES, (hk + 1) * LANES)
        kts.append(jnp.concatenate([kc_ref[0, c, :], kp_ref[0, c, :], km_ref[0, c, :], kn_ref[0, c, :]], axis=1))
        vs.append(jnp.concatenate([vc_ref[0, :, c], vp_ref[0, :, c], vm_ref[0, :, c], vn_ref[0, :, c]], axis=0))

    def stacked_queries(chunk):
        q = q_ref[0, :, chunk * LANES:(chunk + 1) * LANES]
        zero = jnp.zeros_like(q)
        return jnp.concatenate([jnp.where(first, q, zero), jnp.where(first, zero, q)], axis=0)

    def logits(chunk):
        return _dot(stacked_queries(chunk), kts[chunk // (group // 2)])

    n_chunks = WIN_Q_HEADS // 2

    def run(bounded):
        if bounded:
            k_norm = []
            for kt in kts:
                kf = kt[:WIN_HEAD_DIM].astype(F32)
                k_norm.append(jnp.sqrt(jnp.max(jnp.sum(kf * kf, axis=0, keepdims=True), axis=1, keepdims=True)))
        s_next = logits(0)
        floor = None
        for chunk in range(n_chunks):
            s = s_next + bias
            if chunk + 1 < n_chunks:
                s_next = logits(chunk + 1)
            sink = jnp.where(second_head, sink_ref[2 * chunk + 1], sink_ref[2 * chunk]) * LOG2E
            if bounded:
                qf = stacked_queries(chunk).astype(F32)
                m = jnp.sqrt(jnp.sum(qf * qf, axis=-1, keepdims=True)) * k_norm[chunk // (group // 2)]
            else:
                m = jnp.max(s, axis=-1, keepdims=True)
            m = jnp.maximum(m, sink)
            p = jnp.exp2(s - m)
            l = jnp.sum(p, axis=-1, keepdims=True) + jnp.exp2(sink - m)
            floor = l if floor is None else jnp.minimum(floor, l)
            o = _dot(p.astype(BF16), vs[chunk // (group // 2)]) / l
            o_ref[0, :, chunk * LANES:(chunk + 1) * LANES] = jnp.where(first, o[:tq], o[tq:]).astype(o_ref.dtype)
        return jnp.min(floor)

    smallest = run(bounded=True)

    @pl.when(smallest < SOFTMAX_FLOOR)
    def _():
        run(bounded=False)


def _window_attention(q, kdt, vd, kct, vc, sink):
    b, n_lat, _ = q.shape
    tq = ROW_TILE
    n_ctx = vc.shape[1]
    per = tq // WINDOW
    last = n_lat // WINDOW - 1
    prev_i = lambda i: jnp.maximum(i * per - 1, 0)
    next_i = lambda i: jnp.minimum((i + 1) * per, last)
    w = 2 * LANES
    return pl.pallas_call(
        functools.partial(_window_kernel, n_lat=n_lat),
        grid=(b, n_lat // tq),
        in_specs=[
            pl.BlockSpec(memory_space=pltpu.SMEM),
            pl.BlockSpec((1, tq, 4 * LANES), lambda bi, i: (bi, i, 0)),
            pl.BlockSpec((1, w, n_ctx), lambda bi, i: (bi, 0, 0)),
            pl.BlockSpec((1, n_ctx, w), lambda bi, i: (bi, 0, 0)),
            pl.BlockSpec((1, w, WINDOW), lambda bi, i: (bi, 0, prev_i(i))),
            pl.BlockSpec((1, w, tq), lambda bi, i: (bi, 0, i)),
            pl.BlockSpec((1, w, WINDOW), lambda bi, i: (bi, 0, next_i(i))),
            pl.BlockSpec((1, WINDOW, w), lambda bi, i: (bi, prev_i(i), 0)),
            pl.BlockSpec((1, tq, w), lambda bi, i: (bi, i, 0)),
            pl.BlockSpec((1, WINDOW, w), lambda bi, i: (bi, next_i(i), 0)),
        ],
        out_specs=pl.BlockSpec((1, tq, 4 * LANES), lambda bi, i: (bi, i, 0)),
        out_shape=jax.ShapeDtypeStruct((b, n_lat, 4 * LANES), BF16),
        compiler_params=_params("parallel", "arbitrary"),
        name="window_attention",
    )(sink, q, kct, vc, kdt, kdt, kdt, vd, vd, vd)


def _hyena_filter_kernel(feat_ref, w1_ref, b1_ref, w2_ref, b2_ref, w3_ref, b3_ref, freq_ref, decay_ref, h_ref,
                         hid_ref, *, n_forward):
    j = pl.program_id(0)
    hp = lax.Precision.HIGHEST

    @pl.when(j == 0)
    def _():
        freq = freq_ref[...]
        hid = jnp.sin(freq * (jnp.dot(feat_ref[...], w1_ref[...], precision=hp, preferred_element_type=F32) + b1_ref[...]))
        hid_ref[...] = jnp.sin(freq * (jnp.dot(hid, w2_ref[...], precision=hp, preferred_element_type=F32) + b2_ref[...]))

    h = jnp.dot(hid_ref[...], w3_ref[...], precision=hp, preferred_element_type=F32) + b3_ref[...]
    h = h * decay_ref[...]
    row = lax.broadcasted_iota(jnp.int32, h.shape, 0)
    h_ref[...] = jnp.where((row == 0) & (j >= n_forward), 0.0, h).astype(h_ref.dtype)


def _hyena_filters(feats, w1, b1, w2, b2, w3, b3, freq, decay):
    length = feats.shape[0]
    n = w3.shape[1]
    ch = decay.shape[1]
    full = lambda a: pl.BlockSpec(a.shape, lambda j: (0, 0))
    return pl.pallas_call(
        functools.partial(_hyena_filter_kernel, n_forward=HYENA_ORDER),
        grid=(n // ch,),
        in_specs=[full(feats), full(w1), full(b1), full(w2), full(b2),
                  pl.BlockSpec((w3.shape[0], ch), lambda j: (0, j)), pl.BlockSpec((1, ch), lambda j: (0, j)),
                  full(freq), full(decay)],
        out_specs=pl.BlockSpec((length, ch), lambda j: (0, j)),
        out_shape=jax.ShapeDtypeStruct((length, n), BF16),
        scratch_shapes=[pltpu.VMEM((length, w2.shape[1]), F32)],
        compiler_params=_params("arbitrary"),
        name="hyena_filters",
    )(feats, w1, b1, w2, b2, w3, b3, freq, decay)


def _spectrum_kernel(f_ref, h_ref, re_ref, im_ref):
    j = pl.program_id(0)
    tk = re_ref.shape[0]
    n = re_ref.shape[1]
    a = _dot(f_ref[0], h_ref[...])
    re_ref[...] = a[:tk, :n] + a[:tk, n:]
    row = lax.broadcasted_iota(jnp.int32, (tk, n), 0)
    sign = jnp.where((row == 0) & (j == 0), 1.0, -1.0)
    im_ref[...] = a[tk:, :n] + sign * a[tk:, n:]


def _filter_spectrum(f3, hcat):
    nt, tk2, length = f3.shape
    n = hcat.shape[1] // 2
    tk = tk2 // 2
    out = pl.BlockSpec((tk, n), lambda j: (j, 0))
    return pl.pallas_call(
        _spectrum_kernel,
        grid=(nt,),
        in_specs=[pl.BlockSpec((1, tk2, length), lambda j: (j, 0, 0)),
                  pl.BlockSpec(hcat.shape, lambda j: (0, 0))],
        out_specs=[out, out],
        out_shape=[jax.ShapeDtypeStruct((nt * tk, n), F32)] * 2,
        compiler_params=_params("arbitrary"),
        name="filter_spectrum",
    )(f3, hcat)


def _long_conv_kernel(z_ref, x_ref, f_ref, g_ref, hre_ref, him_ref, bias_ref, o_ref, acc_ref):
    j = pl.program_id(1)
    tk = hre_ref.shape[0]
    row = lax.broadcasted_iota(jnp.int32, hre_ref.shape, 0)
    packed = (row == 0) & (j == 0)
    bias = bias_ref[...]
    h_re = hre_ref[...] + bias
    h_im = him_ref[...] + jnp.where(packed, bias, 0.0)

    @pl.when(j == 0)
    def _():
        acc_ref[...] = jnp.zeros_like(acc_ref)

    spectra = [_dot(f_ref[0], z_ref[s]) for s in range(z_ref.shape[0])]
    for s, a in enumerate(spectra):
        a_re, a_im = a[:tk], a[tk:]
        ii = a_im * h_im
        p_re = a_re * h_re - jnp.where(packed, 0.0, ii)
        p_im = jnp.where(packed, ii, a_re * h_im + a_im * h_re)
        acc_ref[s] += _dot(g_ref[0], jnp.concatenate([p_re, p_im], axis=0).astype(BF16))

    @pl.when(j == pl.num_programs(1) - 1)
    def _():
        o_ref[...] = (x_ref[...].astype(F32) * acc_ref[...]).astype(o_ref.dtype)


def _long_conv(z_src, z_col, x_src, x_col, f3, g3, hre, him, h_col, bias):
    b, length, _ = z_src.shape
    nt, tk2, _ = f3.shape
    tk = tk2 // 2
    ch = HYENA_CH
    nb = CONV_SAMPLES
    return pl.pallas_call(
        _long_conv_kernel,
        grid=(b // nb, nt),
        in_specs=[
            pl.BlockSpec((nb, length, ch), lambda bi, j: (bi, 0, z_col)),
            pl.BlockSpec((nb, length, ch), lambda bi, j: (bi, 0, x_col)),
            pl.BlockSpec((1, tk2, length), lambda bi, j: (j, 0, 0)),
            pl.BlockSpec((1, length, tk2), lambda bi, j: (j, 0, 0)),
            pl.BlockSpec((tk, ch), lambda bi, j: (j, h_col)),
            pl.BlockSpec((tk, ch), lambda bi, j: (j, h_col)),
            pl.BlockSpec((1, ch), lambda bi, j: (0, 0)),
        ],
        out_specs=pl.BlockSpec((nb, length, ch), lambda bi, j: (bi, 0, 0)),
        out_shape=jax.ShapeDtypeStruct((b, length, ch), BF16),
        scratch_shapes=[pltpu.VMEM((nb, length, ch), F32)],
        compiler_params=_params("parallel", "arbitrary"),
        name="long_conv",
    )(z_src, x_src, f3, g3, hre, him, bias)


def _axial_rope(length, rot_dim):
    rows = length // GRID_W
    row = jnp.repeat(jnp.arange(rows), GRID_W).astype(F32)
    col = jnp.tile(jnp.arange(GRID_W), rows).astype(F32)
    quarter = rot_dim // 4
    inv = ROPE_BASE ** (-jnp.arange(quarter, dtype=F32) / quarter)
    ang = jnp.concatenate([row[:, None] * inv, col[:, None] * inv], axis=-1)
    return jnp.cos(ang), jnp.sin(ang)


def _rope_tables(length, n_ctx):
    cos_m, sin_m = _axial_rope(length, MLA_ROPE)
    cos_d, sin_d = _axial_rope(length, DIFF_HEAD_DIM)
    ones = lambda w: jnp.ones((length, w), F32)
    zeros = lambda w: jnp.zeros((length, w), F32)
    cm = jnp.concatenate([ones(MLA_NOPE), cos_m, cos_m, ones(LANES - MLA_NOPE - MLA_ROPE)], axis=1)
    sm = jnp.concatenate([zeros(MLA_NOPE), -sin_m, sin_m, zeros(LANES - MLA_NOPE - MLA_ROPE)], axis=1)
    cd = jnp.concatenate([cos_d] * 4, axis=1)
    sd = jnp.concatenate([-sin_d, sin_d] * 2, axis=1)
    pad = lambda t, v: jnp.concatenate([t, jnp.full((n_ctx, LANES), v, F32)], axis=0)
    return pad(cm, 1.0), pad(sm, 0.0), pad(cd, 1.0), pad(sd, 0.0)


def _dft_matrices(length):
    n = 2 * length
    k = np.arange(length, dtype=np.int64)
    ang = 2.0 * np.pi * ((k[:, None] * k[None, :]) % n).astype(np.float64) / n
    alt = np.where(k % 2 == 0, 1.0, -1.0)
    f_re = np.cos(ang)
    f_im = -np.sin(ang)
    f_im[0, :] = alt
    g_re = (2.0 / n) * np.cos(ang.T)
    g_im = -(2.0 / n) * np.sin(ang.T)
    g_re[:, 0] = 1.0 / n
    g_im[:, 0] = alt / n
    nt = length // FREQ_TILE
    f3 = np.concatenate([f_re.reshape(nt, FREQ_TILE, length), f_im.reshape(nt, FREQ_TILE, length)], axis=1)
    g3 = np.concatenate([g_re.reshape(length, nt, FREQ_TILE), g_im.reshape(length, nt, FREQ_TILE)], axis=2)
    g3 = np.transpose(g3, (1, 0, 2))
    return jnp.asarray(f3, dtype=BF16), jnp.asarray(g3, dtype=BF16)


def _hyena_features(length):
    t = jnp.arange(length, dtype=F32)
    tn = t / max(length - 1, 1)
    bands = jnp.linspace(1e-4, HYENA_BANDS - 1, HYENA_BANDS, dtype=F32)
    ang = 2.0 * math.pi * bands[None, :] * t[:, None] / length
    feats = jnp.concatenate([tn[:, None], jnp.cos(ang), -jnp.sin(ang)], axis=-1)
    min_decay = math.log(HYENA_DECAY_TARGET) / HYENA_SLOW_PCT
    max_decay = math.log(HYENA_DECAY_TARGET) / HYENA_FAST_PCT
    deltas = jnp.abs(jnp.linspace(min_decay, max_decay, HYENA_CH, dtype=F32))
    decay = jnp.exp(-tn[:, None] * deltas[None, :])
    return feats, decay


def _pad_cols(w, left, right):
    return jnp.pad(w, ((0, 0),) * (w.ndim - 1) + ((left, right),))


def _even_weights(w_in, w_uq, w_ukv):
    d = w_in.shape[0]
    o = MLA_Q_RANK + MLA_KV_RANK
    w_kr = _pad_cols(w_in[:, o:o + MLA_ROPE], MLA_NOPE, LANES - MLA_NOPE - MLA_ROPE)
    w_in_ext = jnp.concatenate([w_in[:, :o], w_kr, w_in[:, o + MLA_ROPE:]], axis=1)
    uq = _pad_cols(w_uq.reshape(MLA_Q_RANK, MLA_HEADS, MLA_NOPE + MLA_ROPE), 0, LANES - MLA_NOPE - MLA_ROPE)
    ukv = w_ukv.reshape(MLA_KV_RANK, MLA_HEADS, MLA_NOPE + MLA_V)
    uk = _pad_cols(ukv[:, :, :MLA_NOPE], 0, LANES - MLA_NOPE)
    uv = ukv[:, :, MLA_NOPE:].reshape(MLA_KV_RANK, MLA_HEADS // 2, 2, MLA_V)
    uv = jnp.stack([_pad_cols(uv[:, :, 0], 0, LANES - MLA_V), _pad_cols(uv[:, :, 1], LANES - MLA_V, 0)], axis=2)
    w_ukv_ext = jnp.concatenate([uk.reshape(MLA_KV_RANK, -1), uv.reshape(MLA_KV_RANK, -1)], axis=1)
    return w_in_ext.astype(BF16), uq.reshape(MLA_Q_RANK, -1).astype(BF16), w_ukv_ext.astype(BF16)


def _odd_weights(w_in):
    d = w_in.shape[0]
    nq = WIN_Q_HEADS * WIN_HEAD_DIM
    nkv = WIN_KV_HEADS * WIN_HEAD_DIM
    dup = lambda w: jnp.concatenate([w.reshape(d, WIN_KV_HEADS, WIN_HEAD_DIM)] * 2, axis=-1).reshape(d, -1)
    wk = dup(w_in[:, nq:nq + nkv])
    wv = dup(w_in[:, nq + nkv:nq + 2 * nkv])
    wqkv = jnp.concatenate([w_in[:, :nq], wk, wv], axis=1).astype(BF16)
    wkv = jnp.concatenate([wk, wv], axis=1).astype(BF16)
    wu = w_in[:, nq + 2 * nkv:].astype(BF16)
    return wqkv, wkv, wu


def _ffn_weights(w_up, conv_w, conv_b, w_down):
    d = w_up.shape[0]
    nc = D_FF // FFN_CHUNK
    chunks = lambda w: jnp.transpose(w.reshape(w.shape[0], nc, FFN_CHUNK), (1, 0, 2))
    cw = jnp.concatenate([conv_w, conv_b[None, :]], axis=0)
    return (chunks(w_up[:, :D_FF]).astype(BF16), chunks(w_up[:, D_FF:]).astype(BF16),
            chunks(cw[:, :D_FF]), chunks(cw[:, D_FF:]),
            w_down.astype(BF16))


def kernel(x, c, ctx, c_ctx, ada_w, ada_b, norm_g, mix_w_out, ffn_w_up, ffn_conv_w, ffn_conv_b, ffn_w_down, even_w_in, mla_q_norm_g, mla_w_uq, mla_kv_norm_g, mla_w_ukv, diff_lambda, diff_subln_g, odd_w_in, win_sink, hy_conv_w, hy_conv_b, hy_f_w1, hy_f_b1, hy_f_w2, hy_f_b2, hy_f_w3, hy_f_b3, hy_f_freq, hy_bias):
    b, seq, d = x.shape
    n_ctx = ctx.shape[1]
    depth = ada_w.shape[0]
    assert depth == 2 and d == D_MODEL and seq % ROW_TILE == 0 and n_ctx == ROW_TILE
    n_lat_tiles = seq // ROW_TILE
    n_tiles = n_lat_tiles + 1

    pad_rows = (-(b + 1)) % 8
    cond = jnp.concatenate([c, c_ctx[None, :], jnp.zeros((pad_rows, d), F32)], axis=0)
    mod = _ada_table(cond, ada_w, ada_b).reshape(depth, b + 1 + pad_rows, 6, d)[:, :b + 1]
    g = norm_g.reshape(depth, 4, 1, d)
    cm, sm, cd, sd = _rope_tables(seq, n_ctx)

    lambda_init = 0.8 - 0.6 * math.exp(-0.3 * 0)
    w_in_e, w_uq_e, w_ukv_e = _even_weights(even_w_in[0], mla_w_uq[0], mla_w_ukv[0])
    rows = seq + n_ctx
    even_w = (g[0, 0], w_in_e, mla_q_norm_g[0][None, :], w_uq_e, mla_kv_norm_g[0][None, :], w_ukv_e)
    proj = _proj_even(x, mod[0], even_w, (cm, sm, cd, sd), rows, PROJ_ROW_TILE, 0, False)
    qm, km, vm, dq, dk, dv = _proj_even(ctx, mod[0], even_w, (cm, sm, cd, sd), rows, n_ctx, seq, True, prior=proj)
    lam_args = (diff_lambda[0], diff_subln_g[0][None, :], lambda_init)
    w_out0 = mix_w_out[0].astype(BF16)
    ffn0 = _ffn_weights(ffn_w_up[0], ffn_conv_w[0], ffn_conv_b[0], ffn_w_down[0])
    o_mla = _mla_attention(qm, km, vm, ATTN_Q_TILE, 0, seq, 0, rows)
    o_diff = _diff_attention(dq, dk, dv, *lam_args, ATTN_Q_TILE, 0, seq, 0, rows)
    x_lat = _mix_ffn(o_mla, o_diff, x, mod[0], w_out0, g[0, 1], g[0, 2], *ffn0, g[0, 3], FFN_ROW_TILE, False)
    o_mla = _mla_attention(qm, km, vm, n_ctx, seq, n_ctx, seq, n_ctx)
    o_diff = _diff_attention(dq, dk, dv, *lam_args, n_ctx, seq, n_ctx, seq, n_ctx)
    x_ctx = _mix_ffn(o_mla, o_diff, ctx, mod[0], w_out0, g[0, 1], g[0, 2], *ffn0, g[0, 3], n_ctx, True)

    wqkv, wkv, wu = _odd_weights(odd_w_in[0])
    cw_hy = jnp.concatenate([hy_conv_w[0], hy_conv_b[0][None, :]], axis=0)
    q, kd, vd, u = _proj_odd(x_lat, mod[1], g[1, 0], wqkv, wu, cw_hy, cd, sd, PROJ_ROW_TILE)
    kc, vc = _proj_ctx_kv(x_ctx, mod[1], g[1, 0], wkv, 0, n_ctx)
    o_win = _window_attention(q, kd, vd, kc, vc, win_sink[0])

    feats, decay = _hyena_features(seq)
    emb = feats.shape[1]
    feats = _pad_cols(feats, 0, LANES - emb)
    w1 = jnp.pad(hy_f_w1[0], ((0, LANES - emb), (0, 0)))
    hcat = _hyena_filters(feats, w1, hy_f_b1[0][None, :], hy_f_w2[0], hy_f_b2[0][None, :], hy_f_w3[0],
                          hy_f_b3[0][None, :], hy_f_freq[0][None, :], decay)
    f3, g3 = _dft_matrices(seq)
    hre, him = _filter_spectrum(f3, hcat)
    z = _long_conv(u, 0, u, 1, f3, g3, hre, him, 0, hy_bias[0, 0][None, :])
    z = _long_conv(z, 0, u, 2, f3, g3, hre, him, 1, hy_bias[0, 1][None, :])

    w_out1 = mix_w_out[1].astype(BF16)
    ffn1 = _ffn_weights(ffn_w_up[1], ffn_conv_w[1], ffn_conv_b[1], ffn_w_down[1])
    return _mix_ffn(o_win, z, x_lat, mod[1], w_out1, g[1, 1], g[1, 2], *ffn1, g[1, 3], FFN_ROW_TILE, False)
```

```python
import functools
import math

import numpy as np
import jax
import jax.numpy as jnp
from jax import lax
from jax.experimental import pallas as pl
from jax.experimental.pallas import tpu as pltpu

D_MODEL = 1024
GRID_W = 64
MLA_HEADS = 8
MLA_NOPE = 64
MLA_ROPE = 32
MLA_V = 64
MLA_Q_RANK = 384
MLA_KV_RANK = 256
DIFF_HEADS = 4
DIFF_HEAD_DIM = 64
DIFF_V_DIM = 128
WIN_Q_HEADS = 8
WIN_KV_HEADS = 2
WIN_HEAD_DIM = 64
WINDOW = 128
HYENA_CH = 512
HYENA_ORDER = 2
HYENA_BANDS = 16
HYENA_DECAY_TARGET = 1e-2
HYENA_FAST_PCT = 0.3
HYENA_SLOW_PCT = 1.5
D_FF = 2816
ROPE_BASE = 10000.0
NORM_EPS = 1e-6
NEG_INF = -1e30

LANES = 128
HALO_ROWS = 16
ROW_TILE = 256
ATTN_Q_TILE = 2048
ATTN_SUB_TILE = 512
ATTN_KEY_CHUNK = 256
LOG2E = math.log2(math.e)
FFN_ROW_TILE = 512
PROJ_ROW_TILE = 512
FFN_CHUNK = 256
FREQ_TILE = 256
CONV_SAMPLES = 2
VMEM_LIMIT = 48 * 1024 * 1024

BF16 = jnp.bfloat16
F32 = jnp.float32


def _params(*semantics):
    return pltpu.CompilerParams(dimension_semantics=semantics, vmem_limit_bytes=VMEM_LIMIT)


def _rms(x, g):
    return x * lax.rsqrt(jnp.mean(x * x, axis=-1, keepdims=True) + NORM_EPS) * g


def _dot(a, b):
    return jnp.dot(a, b, preferred_element_type=F32)


def _lane_iota(shape):
    return lax.broadcasted_iota(jnp.int32, shape, len(shape) - 1)


def _rope_chunk(x, cos, sin_signed, half):
    first = (_lane_iota(x.shape) % (2 * half)) < half
    partner = jnp.where(first, pltpu.roll(x, LANES - half, 1), pltpu.roll(x, half, 1))
    return x * cos + partner * sin_signed


def _rope_rows(xt, cos_t, sin_signed_t, half):
    first = (lax.broadcasted_iota(jnp.int32, xt.shape, 0) % (2 * half)) < half
    partner = jnp.where(first, pltpu.roll(xt, LANES - half, 0), pltpu.roll(xt, half, 0))
    return xt * cos_t + partner * sin_signed_t


def _ada_kernel(c_ref, w_ref, b_ref, o_ref):
    c = c_ref[...]
    sc = c * jax.nn.sigmoid(c)
    o_ref[0] = _dot(sc.astype(BF16), w_ref[0].astype(BF16)) + b_ref[0]


def _ada_table(cond, ada_w, ada_b):
    depth, d, n = ada_w.shape
    rows = cond.shape[0]
    tn = 1536
    return pl.pallas_call(
        _ada_kernel,
        grid=(depth, n // tn),
        in_specs=[
            pl.BlockSpec((rows, d), lambda l, j: (0, 0)),
            pl.BlockSpec((1, d, tn), lambda l, j: (l, 0, j)),
            pl.BlockSpec((1, 1, tn), lambda l, j: (l, 0, j)),
        ],
        out_specs=pl.BlockSpec((1, rows, tn), lambda l, j: (l, 0, j)),
        out_shape=jax.ShapeDtypeStruct((depth, rows, n), F32),
        compiler_params=_params("arbitrary", "arbitrary"),
        name="ada_table",
    )(cond, ada_w, ada_b.reshape(depth, 1, n))


def _proj_even_kernel(x_ref, mod_ref, g_ref, win_ref, qg_ref, wuq_ref, kvg_ref, wukv_ref,
                      cm_ref, sm_ref, cd_ref, sd_ref, cmt_ref, smt_ref, cdt_ref, sdt_ref, *rest):
    qm_ref, km_ref, vm_ref, dq_ref, dk_ref, dv_ref = rest[-6:]
    x = x_ref[0]
    mod = mod_ref[0]
    h = _rms(x, g_ref[...]) * (1.0 + mod[1:2]) + mod[0:1]
    p = _dot(h.astype(BF16), win_ref[...])
    o = 0
    cq = p[:, o:o + MLA_Q_RANK]; o += MLA_Q_RANK
    ckv = p[:, o:o + MLA_KV_RANK]; o += MLA_KV_RANK
    kr = p[:, o:o + LANES]; o += LANES
    dq = p[:, o:o + 512]; o += 512
    dk = p[:, o:o + 512]; o += 512
    dv = p[:, o:o + 512]
    q = _dot(_rms(cq, qg_ref[...]).astype(BF16), wuq_ref[...])
    kv = _dot(_rms(ckv, kvg_ref[...]).astype(BF16), wukv_ref[...])
    cm, sm, cd, sd = cm_ref[...], sm_ref[...], cd_ref[...], sd_ref[...]
    cmt, smt, cdt, sdt = cmt_ref[...], smt_ref[...], cdt_ref[...], sdt_ref[...]
    mla_scale = (MLA_NOPE + MLA_ROPE) ** -0.5 * LOG2E
    diff_scale = DIFF_HEAD_DIM ** -0.5 * LOG2E
    kr = _rope_chunk(kr, cm, sm, MLA_ROPE // 2)
    for hd in range(MLA_HEADS):
        c = slice(hd * LANES, (hd + 1) * LANES)
        qm_ref[0, c, :] = (_rope_rows(q[:, c].T, cmt, smt, MLA_ROPE // 2) * mla_scale).astype(BF16)
        km_ref[0, :, c] = (kv[:, c] + kr).astype(BF16)
        vm_ref[0, c, :] = kv[:, MLA_HEADS * LANES + hd * LANES:MLA_HEADS * LANES + (hd + 1) * LANES].astype(BF16).T
    for hd in range(DIFF_HEADS):
        c = slice(hd * LANES, (hd + 1) * LANES)
        dq_ref[0, c, :] = (_rope_rows(dq[:, c].T, cdt, sdt, DIFF_HEAD_DIM // 2) * diff_scale).astype(BF16)
        dk_ref[0, :, c] = _rope_chunk(dk[:, c], cd, sd, DIFF_HEAD_DIM // 2).astype(BF16)
        dv_ref[0, c, :] = dv[:, c].astype(BF16).T


def _proj_even(x, mod, weights, tables, rows, tm, row_off, ctx_mod, prior=None):
    b, n, d = x.shape
    off = row_off // tm
    ctx_row = mod.shape[0] - 1
    const = lambda a: pl.BlockSpec(a.shape, lambda bi, i: (0,) * a.ndim)
    tab = pl.BlockSpec((tm, LANES), lambda bi, i: (off + i, 0))
    out = lambda w: pl.BlockSpec((1, tm, w), lambda bi, i: (bi, off + i, 0))
    shp = lambda w: jax.ShapeDtypeStruct((b, rows, w), BF16)
    out_t = lambda w: pl.BlockSpec((1, w, tm), lambda bi, i: (bi, 0, off + i))
    shp_t = lambda w: jax.ShapeDtypeStruct((b, w, rows), BF16)
    tab_t = pl.BlockSpec((LANES, tm), lambda bi, i: (0, off + i))
    tables_t = [t.T for t in tables]
    n_in = 2 + len(weights) + 2 * len(tables)
    prior = () if prior is None else tuple(prior)
    return pl.pallas_call(
        _proj_even_kernel,
        grid=(b, n // tm),
        in_specs=[
            pl.BlockSpec((1, tm, d), lambda bi, i: (bi, i, 0)),
            pl.BlockSpec((1, 6, d), lambda bi, i: (ctx_row if ctx_mod else bi, 0, 0)),
            *[const(w) for w in weights], *[tab for _ in tables], *[tab_t for _ in tables],
            *[pl.BlockSpec(memory_space=pl.ANY) for _ in prior],
        ],
        out_specs=[out_t(1024), out(1024), out_t(1024), out_t(512), out(512), out_t(512)],
        out_shape=[shp_t(1024), shp(1024), shp_t(1024), shp_t(512), shp(512), shp_t(512)],
        input_output_aliases={n_in + k: k for k in range(len(prior))},
        compiler_params=_params("parallel", "arbitrary"),
        name="proj_even",
    )(x, mod, *weights, *tables, *tables_t, *prior)


SOFTMAX_FLOOR = 2.0 ** -90


def _logits_t(qt, k):
    nk = k.shape[0]
    step = min(nk, ATTN_KEY_CHUNK)
    return [_dot(k[a:a + step], qt) for a in range(0, nk, step)]


def _softmax_pv_t(s, vt, shift=None):
    step = s[0].shape[0]
    if shift is None:
        shift = functools.reduce(jnp.maximum, [jnp.max(si, axis=0, keepdims=True) for si in s])
    p = [jnp.exp2(si - shift) for si in s]
    l = sum(jnp.sum(pi, axis=0, keepdims=True) for pi in p)
    o = sum(_dot(vt[:, j * step:(j + 1) * step], pi.astype(BF16)) for j, pi in enumerate(p))
    return o, l


def _max_key_norm(k, lanes=None):
    kf = k.astype(F32)
    sq = kf * kf if lanes is None else jnp.where(lanes, kf * kf, 0.0)
    return jnp.sqrt(jnp.max(jnp.sum(sq, axis=-1, keepdims=True), axis=0, keepdims=True))


def _query_norms(qt):
    qf = qt.astype(F32)
    return jnp.sqrt(jnp.sum(qf * qf, axis=0, keepdims=True))


def _mla_kernel(qt_ref, k_ref, vt_ref, o_ref):
    sub = min(qt_ref.shape[2], ATTN_SUB_TILE)
    items = [(r, hd) for r in range(qt_ref.shape[2] // sub) for hd in range(2)]

    def query(item):
        r, hd = item
        return qt_ref[0, hd * LANES:(hd + 1) * LANES, r * sub:(r + 1) * sub]

    def logits(item):
        return _logits_t(query(item), k_ref[0, :, item[1] * LANES:(item[1] + 1) * LANES])

    def run(bounded):
        if bounded:
            k_norm = [_max_key_norm(k_ref[0, :, hd * LANES:(hd + 1) * LANES]) for hd in range(2)]
        s_next = logits(items[0])
        acc, floor = None, None
        for j, (r, hd) in enumerate(items):
            s = s_next
            if j + 1 < len(items):
                s_next = logits(items[j + 1])
            shift = _query_norms(query((r, hd))) * k_norm[hd] if bounded else None
            o, l = _softmax_pv_t(s, vt_ref[0, hd * LANES:(hd + 1) * LANES, :], shift)
            floor = l if floor is None else jnp.minimum(floor, l)
            acc = o / l if hd == 0 else acc + o / l
            if hd == 1:
                o_ref[0, r * sub:(r + 1) * sub, :] = acc.T.astype(o_ref.dtype)
        return jnp.min(floor)

    smallest = run(bounded=True)

    @pl.when(smallest < SOFTMAX_FLOOR)
    def _():
        run(bounded=False)


def _mla_attention(qmt, km, vmt, tq, q_off, n_q, k_off, n_k):
    b = km.shape[0]
    pairs = MLA_HEADS // 2
    qo, ko = q_off // tq, k_off // n_k
    return pl.pallas_call(
        _mla_kernel,
        grid=(b, pairs, n_q // tq),
        in_specs=[
            pl.BlockSpec((1, 2 * LANES, tq), lambda bi, p, i: (bi, p, qo + i)),
            pl.BlockSpec((1, n_k, 2 * LANES), lambda bi, p, i: (bi, ko, p)),
            pl.BlockSpec((1, 2 * LANES, n_k), lambda bi, p, i: (bi, p, ko)),
        ],
        out_specs=pl.BlockSpec((1, tq, LANES), lambda bi, p, i: (bi, i, p)),
        out_shape=jax.ShapeDtypeStruct((b, n_q, pairs * LANES), BF16),
        compiler_params=_params("parallel", "arbitrary", "arbitrary"),
        name="mla_attention",
    )(qmt, km, vmt)


def _diff_kernel(qt_ref, k_ref, vt_ref, lam_ref, g_ref, o_ref, *, lambda_init):
    lv = lam_ref[...]
    lam = (jnp.exp(jnp.sum(lv[0:1] * lv[1:2], axis=-1, keepdims=True))
           - jnp.exp(jnp.sum(lv[2:3] * lv[3:4], axis=-1, keepdims=True)) + lambda_init)
    vt = vt_ref[0]
    sub = min(qt_ref.shape[2], ATTN_SUB_TILE)
    items = [(r, which) for r in range(qt_ref.shape[2] // sub) for which in range(2)]

    def query(item):
        r, which = item
        qt = qt_ref[0, :, r * sub:(r + 1) * sub]
        first = lax.broadcasted_iota(jnp.int32, qt.shape, 0) < DIFF_HEAD_DIM
        zero = jnp.zeros_like(qt)
        return jnp.where(first, qt, zero) if which == 0 else jnp.where(first, zero, qt)

    def logits(item):
        return _logits_t(query(item), k_ref[0])

    def run(bounded):
        if bounded:
            first_k = _lane_iota(k_ref.shape[1:]) < DIFF_HEAD_DIM
            k_norm = [_max_key_norm(k_ref[0], first_k), _max_key_norm(k_ref[0], ~first_k)]
        s_next = logits(items[0])
        o1, floor = None, None
        for j, (r, which) in enumerate(items):
            s = s_next
            if j + 1 < len(items):
                s_next = logits(items[j + 1])
            shift = _query_norms(query((r, which))) * k_norm[which] if bounded else None
            o, l = _softmax_pv_t(s, vt, shift)
            floor = l if floor is None else jnp.minimum(floor, l)
            if which == 0:
                o1 = o / l
            else:
                o = (o1 - lam * (o / l)).T
                o_ref[0, r * sub:(r + 1) * sub, :] = (_rms(o, g_ref[...]) * (1.0 - lambda_init)).astype(o_ref.dtype)
        return jnp.min(floor)

    smallest = run(bounded=True)

    @pl.when(smallest < SOFTMAX_FLOOR)
    def _():
        run(bounded=False)


def _diff_attention(dq, dk, dv, diff_lambda, subln_g, lambda_init, tq, q_off, n_q, k_off, n_k):
    b = dq.shape[0]
    qo, ko = q_off // tq, k_off // n_k
    return pl.pallas_call(
        functools.partial(_diff_kernel, lambda_init=lambda_init),
        grid=(b, DIFF_HEADS, n_q // tq),
        in_specs=[
            pl.BlockSpec((1, LANES, tq), lambda bi, h, i: (bi, h, qo + i)),
            pl.BlockSpec((1, n_k, LANES), lambda bi, h, i: (bi, ko, h)),
            pl.BlockSpec((1, LANES, n_k), lambda bi, h, i: (bi, h, ko)),
            pl.BlockSpec((4, DIFF_HEAD_DIM), lambda bi, h, i: (0, 0)),
            pl.BlockSpec((1, DIFF_V_DIM), lambda bi, h, i: (0, 0)),
        ],
        out_specs=pl.BlockSpec((1, tq, LANES), lambda bi, h, i: (bi, i, h)),
        out_shape=jax.ShapeDtypeStruct((b, n_q, DIFF_HEADS * LANES), BF16),
        compiler_params=_params("parallel", "arbitrary", "arbitrary"),
        name="diff_attention",
    )(dq, dk, dv, diff_lambda, subln_g)


def _mix_ffn_kernel(a_ref, ap_ref, an_ref, b_ref, bp_ref, bn_ref, x_ref, xp_ref, xn_ref, mod_ref,
                    wo_ref, g1_ref, g2_ref, wa_ref, wg_ref, cwa_ref, cwg_ref, wd_ref, g3_ref,
                    o_ref, lhs_ref, x1_ref, ua0_ref, ug0_ref, ua1_ref, ug1_ref, act_ref):
    i = pl.program_id(1)
    tm = a_ref.shape[1]
    half = a_ref.shape[2]
    n_chunks = wa_ref.shape[0]
    rows = tm + 2 * HALO_ROWS
    mid = slice(HALO_ROWS, HALO_ROWS + tm)
    mod = mod_ref[0]

    ext = lambda p_ref, m_ref, n_ref: jnp.concatenate([p_ref[0], m_ref[0], n_ref[0]], axis=0)
    a_ext, b_ext, x_ext = ext(ap_ref, a_ref, an_ref), ext(bp_ref, b_ref, bn_ref), ext(xp_ref, x_ref, xn_ref)
    cut = rows // 2
    halves = (slice(0, cut), slice(cut, rows))
    ys = [_dot(a_ext[h], wo_ref[0:half, :]) + _dot(b_ext[h], wo_ref[half:, :]) for h in halves]
    for h, y in zip(halves, ys):
        x1 = x_ext[h] + mod[2:3] * _rms(y, g1_ref[...])
        h2 = _rms(x1, g2_ref[...]) * (1.0 + mod[4:5]) + mod[3:4]
        r = h.start + lax.broadcasted_iota(jnp.int32, h2.shape, 0)
        pad = ((r < HALO_ROWS) & (i == 0)) | ((r >= HALO_ROWS + tm) & (i == pl.num_programs(1) - 1))
        lhs_ref[h] = jnp.where(pad, 0.0, h2).astype(BF16)
        lo, hi = max(h.start, HALO_ROWS), min(h.stop, HALO_ROWS + tm)
        x1_ref[lo - HALO_ROWS:hi - HALO_ROWS] = x1[lo - h.start:hi - h.start]
    slots = ((ua0_ref, ug0_ref), (ua1_ref, ug1_ref))

    def up(j, slot):
        lhs = lhs_ref[...]
        slots[slot][0][...] = _dot(lhs, wa_ref[j])
        slots[slot][1][...] = _dot(lhs, wg_ref[j])

    def conv(u, cw):
        down = pltpu.roll(u, 1, 0)[mid]
        above = pltpu.roll(u, rows - 1, 0)[mid]
        return down * cw[0:1] + u[mid] * cw[1:2] + above * cw[2:3] + cw[3:4]

    width = wa_ref.shape[2]

    def activate(j):
        a = conv(slots[j % 2][0][...], cwa_ref[j])
        g = conv(slots[j % 2][1][...], cwg_ref[j])
        act_ref[:, j * width:(j + 1) * width] = ((g * jax.nn.sigmoid(g)) * a).astype(BF16)

    up(0, 0)
    up(1, 1)
    for j in range(n_chunks):
        activate(j)
        if j + 2 < n_chunks:
            up(j + 2, j % 2)
    o_ref[0] = x1_ref[...] + mod[5:6] * _rms(_dot(act_ref[...], wd_ref[...]), g3_ref[...])


def _mix_ffn(a, bb, x, mod, w_out, g1, g2, wa, wg, cwa, cwg, wd, g3, tm, ctx_mod):
    b, n_rows, half = a.shape
    d = x.shape[-1]
    ctx_row = mod.shape[0] - 1
    per = tm // HALO_ROWS
    last_halo = n_rows // HALO_ROWS - 1
    const = lambda *shape: pl.BlockSpec(shape, lambda bi, i: (0,) * len(shape), pipeline_mode=pl.Buffered(1))
    tile = lambda w: pl.BlockSpec((1, tm, w), lambda bi, i: (bi, i, 0))
    prev = lambda w: pl.BlockSpec((1, HALO_ROWS, w), lambda bi, i: (bi, jnp.maximum(i * per - 1, 0), 0))
    nxt = lambda w: pl.BlockSpec((1, HALO_ROWS, w), lambda bi, i: (bi, jnp.minimum((i + 1) * per, last_halo), 0))
    rows = tm + 2 * HALO_ROWS
    return pl.pallas_call(
        _mix_ffn_kernel,
        grid=(b, n_rows // tm),
        in_specs=[
            tile(half), prev(half), nxt(half), tile(half), prev(half), nxt(half), tile(d), prev(d), nxt(d),
            pl.BlockSpec((1, 6, d), lambda bi, i: (ctx_row if ctx_mod else bi, 0, 0)),
            const(*w_out.shape), const(1, d), const(1, d),
            const(*wa.shape), const(*wg.shape), const(*cwa.shape), const(*cwg.shape), const(*wd.shape),
            const(1, d),
        ],
        out_specs=tile(d),
        out_shape=jax.ShapeDtypeStruct((b, n_rows, d), F32),
        scratch_shapes=[pltpu.VMEM((rows, d), BF16), pltpu.VMEM((tm, d), F32)]
        + [pltpu.VMEM((rows, FFN_CHUNK), F32)] * 4
        + [pltpu.VMEM((tm, wd.shape[0]), BF16)],
        compiler_params=_params("parallel", "arbitrary"),
        name="mix_ffn",
    )(a, a, a, bb, bb, bb, x, x, x, mod, w_out, g1, g2, wa, wg, cwa, cwg, wd, g3)


def _proj_odd_kernel(x_ref, xp_ref, xn_ref, mod_ref, g_ref, wqkv_ref, wu_ref, cw_ref, cd_ref, sd_ref,
                     q_ref, k_ref, v_ref, u_ref, *, n_tiles):
    i = pl.program_id(1)
    mod = mod_ref[0]

    def normed(x):
        return (_rms(x, g_ref[...]) * (1.0 + mod[1:2]) + mod[0:1]).astype(BF16)

    h = normed(x_ref[0])
    p = _dot(h, wqkv_ref[...])
    cd, sd = cd_ref[...], sd_ref[...]
    scale = WIN_HEAD_DIM ** -0.5 * LOG2E
    for c in range(4):
        cs = slice(c * LANES, (c + 1) * LANES)
        q_ref[0, :, cs] = (_rope_chunk(p[:, cs], cd, sd, WIN_HEAD_DIM // 2) * scale).astype(BF16)
    for c in range(2):
        cs = slice(512 + c * LANES, 512 + (c + 1) * LANES)
        k_ref[0, c * LANES:(c + 1) * LANES, :] = _rope_chunk(p[:, cs], cd, sd, WIN_HEAD_DIM // 2).astype(BF16).T
    v_ref[0] = p[:, 768:1024].astype(BF16)
    hp = normed(xp_ref[0])
    hn = normed(xn_ref[0])
    ext = jnp.concatenate([jnp.where(i == 0, jnp.zeros_like(hp), hp), h,
                           jnp.where(i == n_tiles - 1, jnp.zeros_like(hn), hn)], axis=0)
    u = _dot(ext, wu_ref[...])
    rows = u.shape[0]
    mid = slice(HALO_ROWS, rows - HALO_ROWS)
    cw = cw_ref[...]
    conv = pltpu.roll(u, 1, 0)[mid] * cw[0:1] + u[mid] * cw[1:2] + pltpu.roll(u, rows - 1, 0)[mid] * cw[2:3] + cw[3:4]
    u_ref[0] = conv.astype(u_ref.dtype)


def _proj_odd(xcat, mod, g, wqkv, wu, cw, cd, sd, tm):
    b, rows, d = xcat.shape
    n_tiles = rows // tm
    per = tm // HALO_ROWS
    last_halo = n_tiles * per - 1
    const = lambda *shape: pl.BlockSpec(shape, lambda bi, i: (0,) * len(shape))
    tab = pl.BlockSpec((tm, LANES), lambda bi, i: (i, 0))
    out = lambda w: pl.BlockSpec((1, tm, w), lambda bi, i: (bi, i, 0))
    return pl.pallas_call(
        functools.partial(_proj_odd_kernel, n_tiles=n_tiles),
        grid=(b, n_tiles),
        in_specs=[
            pl.BlockSpec((1, tm, d), lambda bi, i: (bi, i, 0)),
            pl.BlockSpec((1, HALO_ROWS, d), lambda bi, i: (bi, jnp.maximum(i * per - 1, 0), 0)),
            pl.BlockSpec((1, HALO_ROWS, d), lambda bi, i: (bi, jnp.minimum((i + 1) * per, last_halo), 0)),
            pl.BlockSpec((1, 6, d), lambda bi, i: (bi, 0, 0)),
            const(1, d), const(*wqkv.shape), const(*wu.shape), const(*cw.shape), tab, tab,
        ],
        out_specs=[out(512), pl.BlockSpec((1, 2 * LANES, tm), lambda bi, i: (bi, 0, i)), out(256),
                   out(3 * HYENA_CH)],
        out_shape=[jax.ShapeDtypeStruct((b, rows, 512), BF16), jax.ShapeDtypeStruct((b, 2 * LANES, rows), BF16),
                   jax.ShapeDtypeStruct((b, rows, 256), BF16), jax.ShapeDtypeStruct((b, rows, 3 * HYENA_CH), BF16)],
        compiler_params=_params("parallel", "arbitrary"),
        name="proj_odd",
    )(xcat, xcat, xcat, mod, g, wqkv, wu, cw, cd, sd)


def _proj_ctx_kv_kernel(x_ref, mod_ref, g_ref, w_ref, k_ref, v_ref):
    mod = mod_ref[0]
    h = (_rms(x_ref[0], g_ref[...]) * (1.0 + mod[1:2]) + mod[0:1]).astype(BF16)
    p = _dot(h, w_ref[...])
    k_ref[0] = p[:, :2 * LANES].astype(BF16).T
    v_ref[0] = p[:, 2 * LANES:].astype(BF16)


def _proj_ctx_kv(xcat, mod, g, wkv, ctx_tile, n_ctx):
    b, _, d = xcat.shape
    ctx_row = mod.shape[0] - 1
    out = pl.BlockSpec((1, n_ctx, 2 * LANES), lambda bi: (bi, 0, 0))
    return pl.pallas_call(
        _proj_ctx_kv_kernel,
        grid=(b,),
        in_specs=[
            pl.BlockSpec((1, n_ctx, d), lambda bi: (bi, ctx_tile, 0)),
            pl.BlockSpec((1, 6, d), lambda bi: (ctx_row, 0, 0)),
            pl.BlockSpec((1, d), lambda bi: (0, 0)),
            pl.BlockSpec(wkv.shape, lambda bi: (0, 0)),
        ],
        out_specs=[pl.BlockSpec((1, 2 * LANES, n_ctx), lambda bi: (bi, 0, 0)), out],
        out_shape=[jax.ShapeDtypeStruct((b, 2 * LANES, n_ctx), BF16), jax.ShapeDtypeStruct((b, n_ctx, 2 * LANES), BF16)],
        compiler_params=_params("parallel"),
        name="proj_ctx_kv",
    )(xcat, mod, g, wkv)


def _window_kernel(sink_ref, q_ref, kc_ref, vc_ref, kp_ref, km_ref, kn_ref, vp_ref, vm_ref, vn_ref, o_ref,
                   *, n_lat):
    i = pl.program_id(1)
    tq = q_ref.shape[1]
    n_ctx = vc_ref.shape[1]
    nk = n_ctx + tq + 2 * WINDOW
    col = lax.broadcasted_iota(jnp.int32, (tq, nk), 1)
    row = lax.broadcasted_iota(jnp.int32, (tq, nk), 0)
    k_pos = i * tq - WINDOW + (col - n_ctx)
    q_pos = i * tq + row
    visible = (col < n_ctx) | ((jnp.abs(q_pos - k_pos) <= WINDOW) & (k_pos >= 0) & (k_pos < n_lat))
    bias = jnp.where(visible, 0.0, NEG_INF)
    group = WIN_Q_HEADS // WIN_KV_HEADS
    bias = jnp.concatenate([bias, bias], axis=0)
    second_head = lax.broadcasted_iota(jnp.int32, (2 * tq, 1), 0) >= tq
    first = _lane_iota((tq, LANES)) < WIN_HEAD_DIM
    kts, vs = [], []
    for hk in range(WIN_KV_HEADS):
        c = slice(hk * LANES, (hk + 1) * LANES)
        kts.append(jnp.concatenate([kc_ref[0, c, :], kp_ref[0, c, :], km_ref[0, c, :], kn_ref[0, c, :]], axis=1))
        vs.append(jnp.concatenate([vc_ref[0, :, c], vp_ref[0, :, c], vm_ref[0, :, c], vn_ref[0, :, c]], axis=0))

    def stacked_queries(chunk):
        q = q_ref[0, :, chunk * LANES:(chunk + 1) * LANES]
        zero = jnp.zeros_like(q)
        return jnp.concatenate([jnp.where(first, q, zero), jnp.where(first, zero, q)], axis=0)

    def logits(chunk):
        return _dot(stacked_queries(chunk), kts[chunk // (group // 2)])

    n_chunks = WIN_Q_HEADS // 2

    def run(bounded):
        if bounded:
            k_norm = []
            for kt in kts:
                kf = kt[:WIN_HEAD_DIM].astype(F32)
                k_norm.append(jnp.sqrt(jnp.max(jnp.sum(kf * kf, axis=0, keepdims=True), axis=1, keepdims=True)))
        s_next = logits(0)
        floor = None
        for chunk in range(n_chunks):
            s = s_next + bias
            if chunk + 1 < n_chunks:
                s_next = logits(chunk + 1)
            sink = jnp.where(second_head, sink_ref[2 * chunk + 1], sink_ref[2 * chunk]) * LOG2E
            if bounded:
                qf = stacked_queries(chunk).astype(F32)
                m = jnp.sqrt(jnp.sum(qf * qf, axis=-1, keepdims=True)) * k_norm[chunk // (group // 2)]
            else:
                m = jnp.max(s, axis=-1, keepdims=True)
            m = jnp.maximum(m, sink)
            p = jnp.exp2(s - m)
            l = jnp.sum(p, axis=-1, keepdims=True) + jnp.exp2(sink - m)
            floor = l if floor is None else jnp.minimum(floor, l)
            o = _dot(p.astype(BF16), vs[chunk // (group // 2)]) / l
            o_ref[0, :, chunk * LANES:(chunk + 1) * LANES] = jnp.where(first, o[:tq], o[tq:]).astype(o_ref.dtype)
        return jnp.min(floor)

    smallest = run(bounded=True)

    @pl.when(smallest < SOFTMAX_FLOOR)
    def _():
        run(bounded=False)


def _window_attention(q, kdt, vd, kct, vc, sink):
    b, n_lat, _ = q.shape
    tq = ROW_TILE
    n_ctx = vc.shape[1]
    per = tq // WINDOW
    last = n_lat // WINDOW - 1
    prev_i = lambda i: jnp.maximum(i * per - 1, 0)
    next_i = lambda i: jnp.minimum((i + 1) * per, last)
    w = 2 * LANES
    return pl.pallas_call(
        functools.partial(_window_kernel, n_lat=n_lat),
        grid=(b, n_lat // tq),
        in_specs=[
            pl.BlockSpec(memory_space=pltpu.SMEM),
            pl.BlockSpec((1, tq, 4 * LANES), lambda bi, i: (bi, i, 0)),
            pl.BlockSpec((1, w, n_ctx), lambda bi, i: (bi, 0, 0)),
            pl.BlockSpec((1, n_ctx, w), lambda bi, i: (bi, 0, 0)),
            pl.BlockSpec((1, w, WINDOW), lambda bi, i: (bi, 0, prev_i(i))),
            pl.BlockSpec((1, w, tq), lambda bi, i: (bi, 0, i)),
            pl.BlockSpec((1, w, WINDOW), lambda bi, i: (bi, 0, next_i(i))),
            pl.BlockSpec((1, WINDOW, w), lambda bi, i: (bi, prev_i(i), 0)),
            pl.BlockSpec((1, tq, w), lambda bi, i: (bi, i, 0)),
            pl.BlockSpec((1, WINDOW, w), lambda bi, i: (bi, next_i(i), 0)),
        ],
        out_specs=pl.BlockSpec((1, tq, 4 * LANES), lambda bi, i: (bi, i, 0)),
        out_shape=jax.ShapeDtypeStruct((b, n_lat, 4 * LANES), BF16),
        compiler_params=_params("parallel", "arbitrary"),
        name="window_attention",
    )(sink, q, kct, vc, kdt, kdt, kdt, vd, vd, vd)


def _hyena_filter_kernel(feat_ref, w1_ref, b1_ref, w2_ref, b2_ref, w3_ref, b3_ref, freq_ref, decay_ref, h_ref,
                         hid_ref, *, n_forward):
    j = pl.program_id(0)
    hp = lax.Precision.HIGHEST

    @pl.when(j == 0)
    def _():
        freq = freq_ref[...]
        hid = jnp.sin(freq * (jnp.dot(feat_ref[...], w1_ref[...], precision=hp, preferred_element_type=F32) + b1_ref[...]))
        hid_ref[...] = jnp.sin(freq * (jnp.dot(hid, w2_ref[...], precision=hp, preferred_element_type=F32) + b2_ref[...]))

    h = jnp.dot(hid_ref[...], w3_ref[...], precision=hp, preferred_element_type=F32) + b3_ref[...]
    h = h * decay_ref[...]
    row = lax.broadcasted_iota(jnp.int32, h.shape, 0)
    h_ref[...] = jnp.where((row == 0) & (j >= n_forward), 0.0, h).astype(h_ref.dtype)


def _hyena_filters(feats, w1, b1, w2, b2, w3, b3, freq, decay):
    length = feats.shape[0]
    n = w3.shape[1]
    ch = decay.shape[1]
    full = lambda a: pl.BlockSpec(a.shape, lambda j: (0, 0))
    return pl.pallas_call(
        functools.partial(_hyena_filter_kernel, n_forward=HYENA_ORDER),
        grid=(n // ch,),
        in_specs=[full(feats), full(w1), full(b1), full(w2), full(b2),
                  pl.BlockSpec((w3.shape[0], ch), lambda j: (0, j)), pl.BlockSpec((1, ch), lambda j: (0, j)),
                  full(freq), full(decay)],
        out_specs=pl.BlockSpec((length, ch), lambda j: (0, j)),
        out_shape=jax.ShapeDtypeStruct((length, n), BF16),
        scratch_shapes=[pltpu.VMEM((length, w2.shape[1]), F32)],
        compiler_params=_params("arbitrary"),
        name="hyena_filters",
    )(feats, w1, b1, w2, b2, w3, b3, freq, decay)


def _spectrum_kernel(f_ref, h_ref, re_ref, im_ref):
    j = pl.program_id(0)
    tk = re_ref.shape[0]
    n = re_ref.shape[1]
    a = _dot(f_ref[0], h_ref[...])
    re_ref[...] = a[:tk, :n] + a[:tk, n:]
    row = lax.broadcasted_iota(jnp.int32, (tk, n), 0)
    sign = jnp.where((row == 0) & (j == 0), 1.0, -1.0)
    im_ref[...] = a[tk:, :n] + sign * a[tk:, n:]


def _filter_spectrum(f3, hcat):
    nt, tk2, length = f3.shape
    n = hcat.shape[1] // 2
    tk = tk2 // 2
    out = pl.BlockSpec((tk, n), lambda j: (j, 0))
    return pl.pallas_call(
        _spectrum_kernel,
        grid=(nt,),
        in_specs=[pl.BlockSpec((1, tk2, length), lambda j: (j, 0, 0)),
                  pl.BlockSpec(hcat.shape, lambda j: (0, 0))],
        out_specs=[out, out],
        out_shape=[jax.ShapeDtypeStruct((nt * tk, n), F32)] * 2,
        compiler_params=_params("arbitrary"),
        name="filter_spectrum",
    )(f3, hcat)


def _long_conv_kernel(z_ref, x_ref, f_ref, g_ref, hre_ref, him_ref, bias_ref, o_ref, acc_ref):
    j = pl.program_id(1)
    tk = hre_ref.shape[0]
    row = lax.broadcasted_iota(jnp.int32, hre_ref.shape, 0)
    packed = (row == 0) & (j == 0)
    bias = bias_ref[...]
    h_re = hre_ref[...] + bias
    h_im = him_ref[...] + jnp.where(packed, bias, 0.0)

    @pl.when(j == 0)
    def _():
        acc_ref[...] = jnp.zeros_like(acc_ref)

    spectra = [_dot(f_ref[0], z_ref[s]) for s in range(z_ref.shape[0])]
    for s, a in enumerate(spectra):
        a_re, a_im = a[:tk], a[tk:]
        ii = a_im * h_im
        p_re = a_re * h_re - jnp.where(packed, 0.0, ii)
        p_im = jnp.where(packed, ii, a_re * h_im + a_im * h_re)
        acc_ref[s] += _dot(g_ref[0], jnp.concatenate([p_re, p_im], axis=0).astype(BF16))

    @pl.when(j == pl.num_programs(1) - 1)
    def _():
        o_ref[...] = (x_ref[...].astype(F32) * acc_ref[...]).astype(o_ref.dtype)


def _long_conv(z_src, z_col, x_src, x_col, f3, g3, hre, him, h_col, bias):
    b, length, _ = z_src.shape
    nt, tk2, _ = f3.shape
    tk = tk2 // 2
    ch = HYENA_CH
    nb = CONV_SAMPLES
    return pl.pallas_call(
        _long_conv_kernel,
        grid=(b // nb, nt),
        in_specs=[
            pl.BlockSpec((nb, length, ch), lambda bi, j: (bi, 0, z_col)),
            pl.BlockSpec((nb, length, ch), lambda bi, j: (bi, 0, x_col)),
            pl.BlockSpec((1, tk2, length), lambda bi, j: (j, 0, 0)),
            pl.BlockSpec((1, length, tk2), lambda bi, j: (j, 0, 0)),
            pl.BlockSpec((tk, ch), lambda bi, j: (j, h_col)),
            pl.BlockSpec((tk, ch), lambda bi, j: (j, h_col)),
            pl.BlockSpec((1, ch), lambda bi, j: (0, 0)),
        ],
        out_specs=pl.BlockSpec((nb, length, ch), lambda bi, j: (bi, 0, 0)),
        out_shape=jax.ShapeDtypeStruct((b, length, ch), BF16),
        scratch_shapes=[pltpu.VMEM((nb, length, ch), F32)],
        compiler_params=_params("parallel", "arbitrary"),
        name="long_conv",
    )(z_src, x_src, f3, g3, hre, him, bias)


def _axial_rope(length, rot_dim):
    rows = length // GRID_W
    row = jnp.repeat(jnp.arange(rows), GRID_W).astype(F32)
    col = jnp.tile(jnp.arange(GRID_W), rows).astype(F32)
    quarter = rot_dim // 4
    inv = ROPE_BASE ** (-jnp.arange(quarter, dtype=F32) / quarter)
    ang = jnp.concatenate([row[:, None] * inv, col[:, None] * inv], axis=-1)
    return jnp.cos(ang), jnp.sin(ang)


def _rope_tables(length, n_ctx):
    cos_m, sin_m = _axial_rope(length, MLA_ROPE)
    cos_d, sin_d = _axial_rope(length, DIFF_HEAD_DIM)
    ones = lambda w: jnp.ones((length, w), F32)
    zeros = lambda w: jnp.zeros((length, w), F32)
    cm = jnp.concatenate([ones(MLA_NOPE), cos_m, cos_m, ones(LANES - MLA_NOPE - MLA_ROPE)], axis=1)
    sm = jnp.concatenate([zeros(MLA_NOPE), -sin_m, sin_m, zeros(LANES - MLA_NOPE - MLA_ROPE)], axis=1)
    cd = jnp.concatenate([cos_d] * 4, axis=1)
    sd = jnp.concatenate([-sin_d, sin_d] * 2, axis=1)
    pad = lambda t, v: jnp.concatenate([t, jnp.full((n_ctx, LANES), v, F32)], axis=0)
    return pad(cm, 1.0), pad(sm, 0.0), pad(cd, 1.0), pad(sd, 0.0)


def _dft_matrices(length):
    n = 2 * length
    k = np.arange(length, dtype=np.int64)
    ang = 2.0 * np.pi * ((k[:, None] * k[None, :]) % n).astype(np.float64) / n
    alt = np.where(k % 2 == 0, 1.0, -1.0)
    f_re = np.cos(ang)
    f_im = -np.sin(ang)
    f_im[0, :] = alt
    g_re = (2.0 / n) * np.cos(ang.T)
    g_im = -(2.0 / n) * np.sin(ang.T)
    g_re[:, 0] = 1.0 / n
    g_im[:, 0] = alt / n
    nt = length // FREQ_TILE
    f3 = np.concatenate([f_re.reshape(nt, FREQ_TILE, length), f_im.reshape(nt, FREQ_TILE, length)], axis=1)
    g3 = np.concatenate([g_re.reshape(length, nt, FREQ_TILE), g_im.reshape(length, nt, FREQ_TILE)], axis=2)
    g3 = np.transpose(g3, (1, 0, 2))
    return jnp.asarray(f3, dtype=BF16), jnp.asarray(g3, dtype=BF16)


def _hyena_features(length):
    t = jnp.arange(length, dtype=F32)
    tn = t / max(length - 1, 1)
    bands = jnp.linspace(1e-4, HYENA_BANDS - 1, HYENA_BANDS, dtype=F32)
    ang = 2.0 * math.pi * bands[None, :] * t[:, None] / length
    feats = jnp.concatenate([tn[:, None], jnp.cos(ang), -jnp.sin(ang)], axis=-1)
    min_decay = math.log(HYENA_DECAY_TARGET) / HYENA_SLOW_PCT
    max_decay = math.log(HYENA_DECAY_TARGET) / HYENA_FAST_PCT
    deltas = jnp.abs(jnp.linspace(min_decay, max_decay, HYENA_CH, dtype=F32))
    decay = jnp.exp(-tn[:, None] * deltas[None, :])
    return feats, decay


def _pad_cols(w, left, right):
    return jnp.pad(w, ((0, 0),) * (w.ndim - 1) + ((left, right),))


def _even_weights(w_in, w_uq, w_ukv):
    d = w_in.shape[0]
    o = MLA_Q_RANK + MLA_KV_RANK
    w_kr = _pad_cols(w_in[:, o:o + MLA_ROPE], MLA_NOPE, LANES - MLA_NOPE - MLA_ROPE)
    w_in_ext = jnp.concatenate([w_in[:, :o], w_kr, w_in[:, o + MLA_ROPE:]], axis=1)
    uq = _pad_cols(w_uq.reshape(MLA_Q_RANK, MLA_HEADS, MLA_NOPE + MLA_ROPE), 0, LANES - MLA_NOPE - MLA_ROPE)
    ukv = w_ukv.reshape(MLA_KV_RANK, MLA_HEADS, MLA_NOPE + MLA_V)
    uk = _pad_cols(ukv[:, :, :MLA_NOPE], 0, LANES - MLA_NOPE)
    uv = ukv[:, :, MLA_NOPE:].reshape(MLA_KV_RANK, MLA_HEADS // 2, 2, MLA_V)
    uv = jnp.stack([_pad_cols(uv[:, :, 0], 0, LANES - MLA_V), _pad_cols(uv[:, :, 1], LANES - MLA_V, 0)], axis=2)
    w_ukv_ext = jnp.concatenate([uk.reshape(MLA_KV_RANK, -1), uv.reshape(MLA_KV_RANK, -1)], axis=1)
    return w_in_ext.astype(BF16), uq.reshape(MLA_Q_RANK, -1).astype(BF16), w_ukv_ext.astype(BF16)


def _odd_weights(w_in):
    d = w_in.shape[0]
    nq = WIN_Q_HEADS * WIN_HEAD_DIM
    nkv = WIN_KV_HEADS * WIN_HEAD_DIM
    dup = lambda w: jnp.concatenate([w.reshape(d, WIN_KV_HEADS, WIN_HEAD_DIM)] * 2, axis=-1).reshape(d, -1)
    wk = dup(w_in[:, nq:nq + nkv])
    wv = dup(w_in[:, nq + nkv:nq + 2 * nkv])
    wqkv = jnp.concatenate([w_in[:, :nq], wk, wv], axis=1).astype(BF16)
    wkv = jnp.concatenate([wk, wv], axis=1).astype(BF16)
    wu = w_in[:, nq + 2 * nkv:].astype(BF16)
    return wqkv, wkv, wu


def _ffn_weights(w_up, conv_w, conv_b, w_down):
    d = w_up.shape[0]
    nc = D_FF // FFN_CHUNK
    chunks = lambda w: jnp.transpose(w.reshape(w.shape[0], nc, FFN_CHUNK), (1, 0, 2))
    cw = jnp.concatenate([conv_w, conv_b[None, :]], axis=0)
    return (chunks(w_up[:, :D_FF]).astype(BF16), chunks(w_up[:, D_FF:]).astype(BF16),
            chunks(cw[:, :D_FF]), chunks(cw[:, D_FF:]),
            w_down.astype(BF16))


def kernel(x, c, ctx, c_ctx, ada_w, ada_b, norm_g, mix_w_out, ffn_w_up, ffn_conv_w, ffn_conv_b, ffn_w_down, even_w_in, mla_q_norm_g, mla_w_uq, mla_kv_norm_g, mla_w_ukv, diff_lambda, diff_subln_g, odd_w_in, win_sink, hy_conv_w, hy_conv_b, hy_f_w1, hy_f_b1, hy_f_w2, hy_f_b2, hy_f_w3, hy_f_b3, hy_f_freq, hy_bias):
    b, seq, d = x.shape
    n_ctx = ctx.shape[1]
    depth = ada_w.shape[0]
    assert depth == 2 and d == D_MODEL and seq % ROW_TILE == 0 and n_ctx == ROW_TILE
    n_lat_tiles = seq // ROW_TILE
    n_tiles = n_lat_tiles + 1

    pad_rows = (-(b + 1)) % 8
    cond = jnp.concatenate([c, c_ctx[None, :], jnp.zeros((pad_rows, d), F32)], axis=0)
    mod = _ada_table(cond, ada_w, ada_b).reshape(depth, b + 1 + pad_rows, 6, d)[:, :b + 1]
    g = norm_g.reshape(depth, 4, 1, d)
    cm, sm, cd, sd = _rope_tables(seq, n_ctx)

    lambda_init = 0.8 - 0.6 * math.exp(-0.3 * 0)
    w_in_e, w_uq_e, w_ukv_e = _even_weights(even_w_in[0], mla_w_uq[0], mla_w_ukv[0])
    rows = seq + n_ctx
    even_w = (g[0, 0], w_in_e, mla_q_norm_g[0][None, :], w_uq_e, mla_kv_norm_g[0][None, :], w_ukv_e)
    proj = _proj_even(x, mod[0], even_w, (cm, sm, cd, sd), rows, PROJ_ROW_TILE, 0, False)
    qm, km, vm, dq, dk, dv = _proj_even(ctx, mod[0], even_w, (cm, sm, cd, sd), rows, n_ctx, seq, True, prior=proj)
    lam_args = (diff_lambda[0], diff_subln_g[0][None, :], lambda_init)
    w_out0 = mix_w_out[0].astype(BF16)
    ffn0 = _ffn_weights(ffn_w_up[0], ffn_conv_w[0], ffn_conv_b[0], ffn_w_down[0])
    o_mla = _mla_attention(qm, km, vm, ATTN_Q_TILE, 0, seq, 0, rows)
    o_diff = _diff_attention(dq, dk, dv, *lam_args, ATTN_Q_TILE, 0, seq, 0, rows)
    x_lat = _mix_ffn(o_mla, o_diff, x, mod[0], w_out0, g[0, 1], g[0, 2], *ffn0, g[0, 3], FFN_ROW_TILE, False)
    o_mla = _mla_attention(qm, km, vm, n_ctx, seq, n_ctx, seq, n_ctx)
    o_diff = _diff_attention(dq, dk, dv, *lam_args, n_ctx, seq, n_ctx, seq, n_ctx)
    x_ctx = _mix_ffn(o_mla, o_diff, ctx, mod[0], w_out0, g[0, 1], g[0, 2], *ffn0, g[0, 3], n_ctx, True)

    wqkv, wkv, wu = _odd_weights(odd_w_in[0])
    cw_hy = jnp.concatenate([hy_conv_w[0], hy_conv_b[0][None, :]], axis=0)
    q, kd, vd, u = _proj_odd(x_lat, mod[1], g[1, 0], wqkv, wu, cw_hy, cd, sd, PROJ_ROW_TILE)
    kc, vc = _proj_ctx_kv(x_ctx, mod[1], g[1, 0], wkv, 0, n_ctx)
    o_win = _window_attention(q, kd, vd, kc, vc, win_sink[0])

    feats, decay = _hyena_features(seq)
    emb = feats.shape[1]
    feats = _pad_cols(feats, 0, LANES - emb)
    w1 = jnp.pad(hy_f_w1[0], ((0, LANES - emb), (0, 0)))
    hcat = _hyena_filters(feats, w1, hy_f_b1[0][None, :], hy_f_w2[0], hy_f_b2[0][None, :], hy_f_w3[0],
                          hy_f_b3[0][None, :], hy_f_freq[0][None, :], decay)
    f3, g3 = _dft_matrices(seq)
    hre, him = _filter_spectrum(f3, hcat)
    z = _long_conv(u, 0, u, 1, f3, g3, hre, him, 0, hy_bias[0, 0][None, :])
    z = _long_conv(z, 0, u, 2, f3, g3, hre, him, 1, hy_bias[0, 1][None, :])

    w_out1 = mix_w_out[1].astype(BF16)
    ffn1 = _ffn_weights(ffn_w_up[1], ffn_conv_w[1], ffn_conv_b[1], ffn_w_down[1])
    return _mix_ffn(o_win, z, x_lat, mod[1], w_out1, g[1, 1], g[1, 2], *ffn1, g[1, 3], FFN_ROW_TILE, False)
```

```python
import functools
import math

import numpy as np
import jax
import jax.numpy as jnp
from jax import lax
from jax.experimental import pallas as pl
from jax.experimental.pallas import tpu as pltpu

D_MODEL = 1024
GRID_W = 64
MLA_HEADS = 8
MLA_NOPE = 64
MLA_ROPE = 32
MLA_V = 64
MLA_Q_RANK = 384
MLA_KV_RANK = 256
DIFF_HEADS = 4
DIFF_HEAD_DIM = 64
DIFF_V_DIM = 128
WIN_Q_HEADS = 8
WIN_KV_HEADS = 2
WIN_HEAD_DIM = 64
WINDOW = 128
HYENA_CH = 512
HYENA_ORDER = 2
HYENA_BANDS = 16
HYENA_DECAY_TARGET = 1e-2
HYENA_FAST_PCT = 0.3
HYENA_SLOW_PCT = 1.5
D_FF = 2816
ROPE_BASE = 10000.0
NORM_EPS = 1e-6
NEG_INF = -1e30

LANES = 128
HALO_ROWS = 16
ROW_TILE = 256
ATTN_SUB_TILE = 512
ATTN_KEY_CHUNK = 256
LOG2E = math.log2(math.e)
FFN_ROW_TILE = 512
PROJ_ROW_TILE = 512
FFN_CHUNK = 256
FREQ_TILE = 256
CONV_SAMPLES = 2
VMEM_LIMIT = 48 * 1024 * 1024

BF16 = jnp.bfloat16
F32 = jnp.float32


def _params(*semantics):
    return pltpu.CompilerParams(dimension_semantics=semantics, vmem_limit_bytes=VMEM_LIMIT)


def _rms(x, g):
    return x * lax.rsqrt(jnp.mean(x * x, axis=-1, keepdims=True) + NORM_EPS) * g


def _dot(a, b):
    return jnp.dot(a, b, preferred_element_type=F32)


def _lane_iota(shape):
    return lax.broadcasted_iota(jnp.int32, shape, len(shape) - 1)


def _rope_chunk(x, cos, sin_signed, half):
    first = (_lane_iota(x.shape) % (2 * half)) < half
    partner = jnp.where(first, pltpu.roll(x, LANES - half, 1), pltpu.roll(x, half, 1))
    return x * cos + partner * sin_signed


def _rope_rows(xt, cos_t, sin_signed_t, half):
    first = (lax.broadcasted_iota(jnp.int32, xt.shape, 0) % (2 * half)) < half
    partner = jnp.where(first, pltpu.roll(xt, LANES - half, 0), pltpu.roll(xt, half, 0))
    return xt * cos_t + partner * sin_signed_t


def _ada_kernel(c_ref, w_ref, b_ref, o_ref):
    c = c_ref[...]
    sc = c * jax.nn.sigmoid(c)
    o_ref[0] = _dot(sc.astype(BF16), w_ref[0].astype(BF16)) + b_ref[0]


def _ada_table(cond, ada_w, ada_b):
    depth, d, n = ada_w.shape
    rows = cond.shape[0]
    tn = 1536
    return pl.pallas_call(
        _ada_kernel,
        grid=(depth, n // tn),
        in_specs=[
            pl.BlockSpec((rows, d), lambda l, j: (0, 0)),
            pl.BlockSpec((1, d, tn), lambda l, j: (l, 0, j)),
            pl.BlockSpec((1, 1, tn), lambda l, j: (l, 0, j)),
        ],
        out_specs=pl.BlockSpec((1, rows, tn), lambda l, j: (l, 0, j)),
        out_shape=jax.ShapeDtypeStruct((depth, rows, n), F32),
        compiler_params=_params("arbitrary", "arbitrary"),
        name="ada_table",
    )(cond, ada_w, ada_b.reshape(depth, 1, n))


def _proj_even_kernel(x_ref, mod_ref, g_ref, win_ref, qg_ref, wuq_ref, kvg_ref, wukv_ref,
                      cm_ref, sm_ref, cd_ref, sd_ref, cmt_ref, smt_ref, cdt_ref, sdt_ref, *rest):
    qm_ref, km_ref, vm_ref, dq_ref, dk_ref, dv_ref = rest[-6:]
    x = x_ref[0]
    mod = mod_ref[0]
    h = _rms(x, g_ref[...]) * (1.0 + mod[1:2]) + mod[0:1]
    p = _dot(h.astype(BF16), win_ref[...])
    o = 0
    cq = p[:, o:o + MLA_Q_RANK]; o += MLA_Q_RANK
    ckv = p[:, o:o + MLA_KV_RANK]; o += MLA_KV_RANK
    kr = p[:, o:o + LANES]; o += LANES
    dq = p[:, o:o + 512]; o += 512
    dk = p[:, o:o + 512]; o += 512
    dv = p[:, o:o + 512]
    q = _dot(_rms(cq, qg_ref[...]).astype(BF16), wuq_ref[...])
    kv = _dot(_rms(ckv, kvg_ref[...]).astype(BF16), wukv_ref[...])
    cm, sm, cd, sd = cm_ref[...], sm_ref[...], cd_ref[...], sd_ref[...]
    cmt, smt, cdt, sdt = cmt_ref[...], smt_ref[...], cdt_ref[...], sdt_ref[...]
    mla_scale = (MLA_NOPE + MLA_ROPE) ** -0.5 * LOG2E
    diff_scale = DIFF_HEAD_DIM ** -0.5 * LOG2E
    kr = _rope_chunk(kr, cm, sm, MLA_ROPE // 2)
    for hd in range(MLA_HEADS):
        c = slice(hd * LANES, (hd + 1) * LANES)
        qm_ref[0, c, :] = (_rope_rows(q[:, c].T, cmt, smt, MLA_ROPE // 2) * mla_scale).astype(BF16)
        km_ref[0, :, c] = (kv[:, c] + kr).astype(BF16)
        vm_ref[0, c, :] = kv[:, MLA_HEADS * LANES + hd * LANES:MLA_HEADS * LANES + (hd + 1) * LANES].astype(BF16).T
    for hd in range(DIFF_HEADS):
        c = slice(hd * LANES, (hd + 1) * LANES)
        dq_ref[0, c, :] = (_rope_rows(dq[:, c].T, cdt, sdt, DIFF_HEAD_DIM // 2) * diff_scale).astype(BF16)
        dk_ref[0, :, c] = _rope_chunk(dk[:, c], cd, sd, DIFF_HEAD_DIM // 2).astype(BF16)
        dv_ref[0, c, :] = dv[:, c].astype(BF16).T


def _proj_even(x, mod, weights, tables, rows, tm, row_off, ctx_mod, prior=None):
    b, n, d = x.shape
    off = row_off // tm
    ctx_row = mod.shape[0] - 1
    const = lambda a: pl.BlockSpec(a.shape, lambda bi, i: (0,) * a.ndim)
    tab = pl.BlockSpec((tm, LANES), lambda bi, i: (off + i, 0))
    out = lambda w: pl.BlockSpec((1, tm, w), lambda bi, i: (bi, off + i, 0))
    shp = lambda w: jax.ShapeDtypeStruct((b, rows, w), BF16)
    out_t = lambda w: pl.BlockSpec((1, w, tm), lambda bi, i: (bi, 0, off + i))
    shp_t = lambda w: jax.ShapeDtypeStruct((b, w, rows), BF16)
    tab_t = pl.BlockSpec((LANES, tm), lambda bi, i: (0, off + i))
    tables_t = [t.T for t in tables]
    n_in = 2 + len(weights) + 2 * len(tables)
    prior = () if prior is None else tuple(prior)
    return pl.pallas_call(
        _proj_even_kernel,
        grid=(b, n // tm),
        in_specs=[
            pl.BlockSpec((1, tm, d), lambda bi, i: (bi, i, 0)),
            pl.BlockSpec((1, 6, d), lambda bi, i: (ctx_row if ctx_mod else bi, 0, 0)),
            *[const(w) for w in weights], *[tab for _ in tables], *[tab_t for _ in tables],
            *[pl.BlockSpec(memory_space=pl.ANY) for _ in prior],
        ],
        out_specs=[out_t(1024), out(1024), out_t(1024), out_t(512), out(512), out_t(512)],
        out_shape=[shp_t(1024), shp(1024), shp_t(1024), shp_t(512), shp(512), shp_t(512)],
        input_output_aliases={n_in + k: k for k in range(len(prior))},
        compiler_params=_params("parallel", "arbitrary"),
        name="proj_even",
    )(x, mod, *weights, *tables, *tables_t, *prior)


SOFTMAX_FLOOR = 2.0 ** -90


def _logits_t(qt, k):
    nk = k.shape[0]
    step = min(nk, ATTN_KEY_CHUNK)
    return [_dot(k[a:a + step], qt) for a in range(0, nk, step)]


def _softmax_pv_t(s, vt, shift=None):
    step = s[0].shape[0]
    if shift is None:
        shift = functools.reduce(jnp.maximum, [jnp.max(si, axis=0, keepdims=True) for si in s])
    p = [jnp.exp2(si - shift) for si in s]
    l = sum(jnp.sum(pi, axis=0, keepdims=True) for pi in p)
    o = sum(_dot(vt[:, j * step:(j + 1) * step], pi.astype(BF16)) for j, pi in enumerate(p))
    return o, l


def _max_key_norm(k, lanes=None):
    kf = k.astype(F32)
    sq = kf * kf if lanes is None else jnp.where(lanes, kf * kf, 0.0)
    return jnp.sqrt(jnp.max(jnp.sum(sq, axis=-1, keepdims=True), axis=0, keepdims=True))


def _query_norms(qt):
    qf = qt.astype(F32)
    return jnp.sqrt(jnp.sum(qf * qf, axis=0, keepdims=True))


def _mla_kernel(qt_ref, qc_ref, k_ref, vt_ref, o_ref, oc_ref):
    n_lat = qt_ref.shape[2]
    acc = None
    for hd in range(2):
        c = slice(hd * LANES, (hd + 1) * LANES)
        o, l = _softmax_pv_t(_logits_t(qc_ref[0, c, :], k_ref[0, n_lat:, c]), vt_ref[0, c, n_lat:])
        acc = o / l if acc is None else acc + o / l
    oc_ref[0] = acc.T.astype(oc_ref.dtype)

    sub = min(n_lat, ATTN_SUB_TILE)
    items = [(r, hd) for r in range(n_lat // sub) for hd in range(2)]

    def query(item):
        r, hd = item
        return qt_ref[0, hd * LANES:(hd + 1) * LANES, r * sub:(r + 1) * sub]

    def logits(item):
        return _logits_t(query(item), k_ref[0, :, item[1] * LANES:(item[1] + 1) * LANES])

    def run(bounded):
        if bounded:
            k_norm = [_max_key_norm(k_ref[0, :, hd * LANES:(hd + 1) * LANES]) for hd in range(2)]
        s_next = logits(items[0])
        acc, floor = None, None
        for j, (r, hd) in enumerate(items):
            s = s_next
            if j + 1 < len(items):
                s_next = logits(items[j + 1])
            shift = _query_norms(query((r, hd))) * k_norm[hd] if bounded else None
            o, l = _softmax_pv_t(s, vt_ref[0, hd * LANES:(hd + 1) * LANES, :], shift)
            floor = l if floor is None else jnp.minimum(floor, l)
            acc = o / l if hd == 0 else acc + o / l
            if hd == 1:
                o_ref[0, r * sub:(r + 1) * sub, :] = acc.T.astype(o_ref.dtype)
        return jnp.min(floor)

    smallest = run(bounded=True)

    @pl.when(smallest < SOFTMAX_FLOOR)
    def _():
        run(bounded=False)


def _mla_attention(qmt, km, vmt, n_lat):
    b, rows, _ = km.shape
    n_ctx = rows - n_lat
    pairs = MLA_HEADS // 2
    return pl.pallas_call(
        _mla_kernel,
        grid=(b, pairs),
        in_specs=[
            pl.BlockSpec((1, 2 * LANES, n_lat), lambda bi, p: (bi, p, 0)),
            pl.BlockSpec((1, 2 * LANES, n_ctx), lambda bi, p: (bi, p, n_lat // n_ctx)),
            pl.BlockSpec((1, rows, 2 * LANES), lambda bi, p: (bi, 0, p)),
            pl.BlockSpec((1, 2 * LANES, rows), lambda bi, p: (bi, p, 0)),
        ],
        out_specs=[pl.BlockSpec((1, n_lat, LANES), lambda bi, p: (bi, 0, p)),
                   pl.BlockSpec((1, n_ctx, LANES), lambda bi, p: (bi, 0, p))],
        out_shape=[jax.ShapeDtypeStruct((b, n_lat, pairs * LANES), BF16),
                   jax.ShapeDtypeStruct((b, n_ctx, pairs * LANES), BF16)],
        compiler_params=_params("parallel", "arbitrary"),
        name="mla_attention",
    )(qmt, qmt, km, vmt)


def _diff_kernel(qt_ref, qc_ref, k_ref, vt_ref, lam_ref, g_ref, o_ref, oc_ref, *, lambda_init):
    lv = lam_ref[...]
    lam = (jnp.exp(jnp.sum(lv[0:1] * lv[1:2], axis=-1, keepdims=True))
           - jnp.exp(jnp.sum(lv[2:3] * lv[3:4], axis=-1, keepdims=True)) + lambda_init)
    vt = vt_ref[0]
    n_lat = qt_ref.shape[2]
    sub = min(n_lat, ATTN_SUB_TILE)
    items = [(r, which) for r in range(n_lat // sub) for which in range(2)]

    def split(qt, which):
        first = lax.broadcasted_iota(jnp.int32, qt.shape, 0) < DIFF_HEAD_DIM
        zero = jnp.zeros_like(qt)
        return jnp.where(first, qt, zero) if which == 0 else jnp.where(first, zero, qt)

    def finish(o1, o2):
        o = (o1 - lam * o2).T
        return (_rms(o, g_ref[...]) * (1.0 - lambda_init)).astype(o_ref.dtype)

    ctx = []
    for which in range(2):
        o, l = _softmax_pv_t(_logits_t(split(qc_ref[0], which), k_ref[0, n_lat:, :]), vt[:, n_lat:])
        ctx.append(o / l)
    oc_ref[0] = finish(*ctx)

    def query(item):
        r, which = item
        return split(qt_ref[0, :, r * sub:(r + 1) * sub], which)

    def logits(item):
        return _logits_t(query(item), k_ref[0])

    def run(bounded):
        if bounded:
            first_k = _lane_iota(k_ref.shape[1:]) < DIFF_HEAD_DIM
            k_norm = [_max_key_norm(k_ref[0], first_k), _max_key_norm(k_ref[0], ~first_k)]
        s_next = logits(items[0])
        o1, floor = None, None
        for j, (r, which) in enumerate(items):
            s = s_next
            if j + 1 < len(items):
                s_next = logits(items[j + 1])
            shift = _query_norms(query((r, which))) * k_norm[which] if bounded else None
            o, l = _softmax_pv_t(s, vt, shift)
            floor = l if floor is None else jnp.minimum(floor, l)
            if which == 0:
                o1 = o / l
            else:
                o_ref[0, r * sub:(r + 1) * sub, :] = finish(o1, o / l)
        return jnp.min(floor)

    smallest = run(bounded=True)

    @pl.when(smallest < SOFTMAX_FLOOR)
    def _():
        run(bounded=False)


def _diff_attention(dqt, dk, dvt, diff_lambda, subln_g, lambda_init, n_lat):
    b, rows, _ = dk.shape
    n_ctx = rows - n_lat
    return pl.pallas_call(
        functools.partial(_diff_kernel, lambda_init=lambda_init),
        grid=(b, DIFF_HEADS),
        in_specs=[
            pl.BlockSpec((1, LANES, n_lat), lambda bi, h: (bi, h, 0)),
            pl.BlockSpec((1, LANES, n_ctx), lambda bi, h: (bi, h, n_lat // n_ctx)),
            pl.BlockSpec((1, rows, LANES), lambda bi, h: (bi, 0, h)),
            pl.BlockSpec((1, LANES, rows), lambda bi, h: (bi, h, 0)),
            pl.BlockSpec((4, DIFF_HEAD_DIM), lambda bi, h: (0, 0)),
            pl.BlockSpec((1, DIFF_V_DIM), lambda bi, h: (0, 0)),
        ],
        out_specs=[pl.BlockSpec((1, n_lat, LANES), lambda bi, h: (bi, 0, h)),
                   pl.BlockSpec((1, n_ctx, LANES), lambda bi, h: (bi, 0, h))],
        out_shape=[jax.ShapeDtypeStruct((b, n_lat, DIFF_HEADS * LANES), BF16),
                   jax.ShapeDtypeStruct((b, n_ctx, DIFF_HEADS * LANES), BF16)],
        compiler_params=_params("parallel", "arbitrary"),
        name="diff_attention",
    )(dqt, dqt, dk, dvt, diff_lambda, subln_g)


def _mix_ffn_kernel(a_ref, ap_ref, an_ref, b_ref, bp_ref, bn_ref, x_ref, xp_ref, xn_ref, mod_ref,
                    wo_ref, g1_ref, g2_ref, wa_ref, wg_ref, cwa_ref, cwg_ref, wd_ref, g3_ref,
                    o_ref, lhs_ref, x1_ref, ua0_ref, ug0_ref, ua1_ref, ug1_ref, act_ref):
    i = pl.program_id(1)
    tm = a_ref.shape[1]
    half = a_ref.shape[2]
    n_chunks = wa_ref.shape[0]
    rows = tm + 2 * HALO_ROWS
    mid = slice(HALO_ROWS, HALO_ROWS + tm)
    mod = mod_ref[0]

    ext = lambda p_ref, m_ref, n_ref: jnp.concatenate([p_ref[0], m_ref[0], n_ref[0]], axis=0)
    a_ext, b_ext, x_ext = ext(ap_ref, a_ref, an_ref), ext(bp_ref, b_ref, bn_ref), ext(xp_ref, x_ref, xn_ref)
    cut = rows // 2
    halves = (slice(0, cut), slice(cut, rows))
    ys = [_dot(a_ext[h], wo_ref[0:half, :]) + _dot(b_ext[h], wo_ref[half:, :]) for h in halves]
    for h, y in zip(halves, ys):
        x1 = x_ext[h] + mod[2:3] * _rms(y, g1_ref[...])
        h2 = _rms(x1, g2_ref[...]) * (1.0 + mod[4:5]) + mod[3:4]
        r = h.start + lax.broadcasted_iota(jnp.int32, h2.shape, 0)
        pad = ((r < HALO_ROWS) & (i == 0)) | ((r >= HALO_ROWS + tm) & (i == pl.num_programs(1) - 1))
        lhs_ref[h] = jnp.where(pad, 0.0, h2).astype(BF16)
        lo, hi = max(h.start, HALO_ROWS), min(h.stop, HALO_ROWS + tm)
        x1_ref[lo - HALO_ROWS:hi - HALO_ROWS] = x1[lo - h.start:hi - h.start]
    slots = ((ua0_ref, ug0_ref), (ua1_ref, ug1_ref))

    def up(j, slot):
        lhs = lhs_ref[...]
        slots[slot][0][...] = _dot(lhs, wa_ref[j])
        slots[slot][1][...] = _dot(lhs, wg_ref[j])

    def conv(u, cw):
        down = pltpu.roll(u, 1, 0)[mid]
        above = pltpu.roll(u, rows - 1, 0)[mid]
        return down * cw[0:1] + u[mid] * cw[1:2] + above * cw[2:3] + cw[3:4]

    width = wa_ref.shape[2]

    def activate(j):
        a = conv(slots[j % 2][0][...], cwa_ref[j])
        g = conv(slots[j % 2][1][...], cwg_ref[j])
        act_ref[:, j * width:(j + 1) * width] = ((g * jax.nn.sigmoid(g)) * a).astype(BF16)

    up(0, 0)
    up(1, 1)
    for j in range(n_chunks):
        activate(j)
        if j + 2 < n_chunks:
            up(j + 2, j % 2)
    o_ref[0] = x1_ref[...] + mod[5:6] * _rms(_dot(act_ref[...], wd_ref[...]), g3_ref[...])


def _mix_ffn(a, bb, x, mod, w_out, g1, g2, wa, wg, cwa, cwg, wd, g3, tm, ctx_mod):
    b, n_rows, half = a.shape
    d = x.shape[-1]
    ctx_row = mod.shape[0] - 1
    per = tm // HALO_ROWS
    last_halo = n_rows // HALO_ROWS - 1
    const = lambda *shape: pl.BlockSpec(shape, lambda bi, i: (0,) * len(shape), pipeline_mode=pl.Buffered(1))
    tile = lambda w: pl.BlockSpec((1, tm, w), lambda bi, i: (bi, i, 0))
    prev = lambda w: pl.BlockSpec((1, HALO_ROWS, w), lambda bi, i: (bi, jnp.maximum(i * per - 1, 0), 0))
    nxt = lambda w: pl.BlockSpec((1, HALO_ROWS, w), lambda bi, i: (bi, jnp.minimum((i + 1) * per, last_halo), 0))
    rows = tm + 2 * HALO_ROWS
    return pl.pallas_call(
        _mix_ffn_kernel,
        grid=(b, n_rows // tm),
        in_specs=[
            tile(half), prev(half), nxt(half), tile(half), prev(half), nxt(half), tile(d), prev(d), nxt(d),
            pl.BlockSpec((1, 6, d), lambda bi, i: (ctx_row if ctx_mod else bi, 0, 0)),
            const(*w_out.shape), const(1, d), const(1, d),
            const(*wa.shape), const(*wg.shape), const(*cwa.shape), const(*cwg.shape), const(*wd.shape),
            const(1, d),
        ],
        out_specs=tile(d),
        out_shape=jax.ShapeDtypeStruct((b, n_rows, d), F32),
        scratch_shapes=[pltpu.VMEM((rows, d), BF16), pltpu.VMEM((tm, d), F32)]
        + [pltpu.VMEM((rows, FFN_CHUNK), F32)] * 4
        + [pltpu.VMEM((tm, wd.shape[0]), BF16)],
        compiler_params=_params("parallel", "arbitrary"),
        name="mix_ffn",
    )(a, a, a, bb, bb, bb, x, x, x, mod, w_out, g1, g2, wa, wg, cwa, cwg, wd, g3)


def _proj_odd_kernel(x_ref, xp_ref, xn_ref, mod_ref, g_ref, wqkv_ref, wu_ref, cw_ref, cd_ref, sd_ref,
                     q_ref, k_ref, v_ref, u_ref, *, n_tiles):
    i = pl.program_id(1)
    mod = mod_ref[0]

    def normed(x):
        return (_rms(x, g_ref[...]) * (1.0 + mod[1:2]) + mod[0:1]).astype(BF16)

    h = normed(x_ref[0])
    p = _dot(h, wqkv_ref[...])
    cd, sd = cd_ref[...], sd_ref[...]
    scale = WIN_HEAD_DIM ** -0.5 * LOG2E
    for c in range(4):
        cs = slice(c * LANES, (c + 1) * LANES)
        q_ref[0, :, cs] = (_rope_chunk(p[:, cs], cd, sd, WIN_HEAD_DIM // 2) * scale).astype(BF16)
    for c in range(2):
        cs = slice(512 + c * LANES, 512 + (c + 1) * LANES)
        k_ref[0, c * LANES:(c + 1) * LANES, :] = _rope_chunk(p[:, cs], cd, sd, WIN_HEAD_DIM // 2).astype(BF16).T
    v_ref[0] = p[:, 768:1024].astype(BF16)
    hp = normed(xp_ref[0])
    hn = normed(xn_ref[0])
    ext = jnp.concatenate([jnp.where(i == 0, jnp.zeros_like(hp), hp), h,
                           jnp.where(i == n_tiles - 1, jnp.zeros_like(hn), hn)], axis=0)
    u = _dot(ext, wu_ref[...])
    rows = u.shape[0]
    mid = slice(HALO_ROWS, rows - HALO_ROWS)
    cw = cw_ref[...]
    conv = pltpu.roll(u, 1, 0)[mid] * cw[0:1] + u[mid] * cw[1:2] + pltpu.roll(u, rows - 1, 0)[mid] * cw[2:3] + cw[3:4]
    u_ref[0] = conv.astype(u_ref.dtype)


def _proj_odd(xcat, mod, g, wqkv, wu, cw, cd, sd, tm):
    b, rows, d = xcat.shape
    n_tiles = rows // tm
    per = tm // HALO_ROWS
    last_halo = n_tiles * per - 1
    const = lambda *shape: pl.BlockSpec(shape, lambda bi, i: (0,) * len(shape))
    tab = pl.BlockSpec((tm, LANES), lambda bi, i: (i, 0))
    out = lambda w: pl.BlockSpec((1, tm, w), lambda bi, i: (bi, i, 0))
    return pl.pallas_call(
        functools.partial(_proj_odd_kernel, n_tiles=n_tiles),
        grid=(b, n_tiles),
        in_specs=[
            pl.BlockSpec((1, tm, d), lambda bi, i: (bi, i, 0)),
            pl.BlockSpec((1, HALO_ROWS, d), lambda bi, i: (bi, jnp.maximum(i * per - 1, 0), 0)),
            pl.BlockSpec((1, HALO_ROWS, d), lambda bi, i: (bi, jnp.minimum((i + 1) * per, last_halo), 0)),
            pl.BlockSpec((1, 6, d), lambda bi, i: (bi, 0, 0)),
            const(1, d), const(*wqkv.shape), const(*wu.shape), const(*cw.shape), tab, tab,
        ],
        out_specs=[out(512), pl.BlockSpec((1, 2 * LANES, tm), lambda bi, i: (bi, 0, i)), out(256),
                   out(3 * HYENA_CH)],
        out_shape=[jax.ShapeDtypeStruct((b, rows, 512), BF16), jax.ShapeDtypeStruct((b, 2 * LANES, rows), BF16),
                   jax.ShapeDtypeStruct((b, rows, 256), BF16), jax.ShapeDtypeStruct((b, rows, 3 * HYENA_CH), BF16)],
        compiler_params=_params("parallel", "arbitrary"),
        name="proj_odd",
    )(xcat, xcat, xcat, mod, g, wqkv, wu, cw, cd, sd)


def _proj_ctx_kv_kernel(x_ref, mod_ref, g_ref, w_ref, k_ref, v_ref):
    mod = mod_ref[0]
    h = (_rms(x_ref[0], g_ref[...]) * (1.0 + mod[1:2]) + mod[0:1]).astype(BF16)
    p = _dot(h, w_ref[...])
    k_ref[0] = p[:, :2 * LANES].astype(BF16).T
    v_ref[0] = p[:, 2 * LANES:].astype(BF16)


def _proj_ctx_kv(xcat, mod, g, wkv, ctx_tile, n_ctx):
    b, _, d = xcat.shape
    ctx_row = mod.shape[0] - 1
    out = pl.BlockSpec((1, n_ctx, 2 * LANES), lambda bi: (bi, 0, 0))
    return pl.pallas_call(
        _proj_ctx_kv_kernel,
        grid=(b,),
        in_specs=[
            pl.BlockSpec((1, n_ctx, d), lambda bi: (bi, ctx_tile, 0)),
            pl.BlockSpec((1, 6, d), lambda bi: (ctx_row, 0, 0)),
            pl.BlockSpec((1, d), lambda bi: (0, 0)),
            pl.BlockSpec(wkv.shape, lambda bi: (0, 0)),
        ],
        out_specs=[pl.BlockSpec((1, 2 * LANES, n_ctx), lambda bi: (bi, 0, 0)), out],
        out_shape=[jax.ShapeDtypeStruct((b, 2 * LANES, n_ctx), BF16), jax.ShapeDtypeStruct((b, n_ctx, 2 * LANES), BF16)],
        compiler_params=_params("parallel"),
        name="proj_ctx_kv",
    )(xcat, mod, g, wkv)


def _window_kernel(sink_ref, q_ref, kc_ref, vc_ref, kp_ref, km_ref, kn_ref, vp_ref, vm_ref, vn_ref, o_ref,
                   *, n_lat):
    i = pl.program_id(1)
    tq = q_ref.shape[1]
    n_ctx = vc_ref.shape[1]
    nk = n_ctx + tq + 2 * WINDOW
    col = lax.broadcasted_iota(jnp.int32, (tq, nk), 1)
    row = lax.broadcasted_iota(jnp.int32, (tq, nk), 0)
    k_pos = i * tq - WINDOW + (col - n_ctx)
    q_pos = i * tq + row
    visible = (col < n_ctx) | ((jnp.abs(q_pos - k_pos) <= WINDOW) & (k_pos >= 0) & (k_pos < n_lat))
    bias = jnp.where(visible, 0.0, NEG_INF)
    group = WIN_Q_HEADS // WIN_KV_HEADS
    bias = jnp.concatenate([bias, bias], axis=0)
    second_head = lax.broadcasted_iota(jnp.int32, (2 * tq, 1), 0) >= tq
    first = _lane_iota((tq, LANES)) < WIN_HEAD_DIM
    kts, vs = [], []
    for hk in range(WIN_KV_HEADS):
        c = slice(hk * LANES, (hk + 1) * LANES)
        kts.append(jnp.concatenate([kc_ref[0, c, :], kp_ref[0, c, :], km_ref[0, c, :], kn_ref[0, c, :]], axis=1))
        vs.append(jnp.concatenate([vc_ref[0, :, c], vp_ref[0, :, c], vm_ref[0, :, c], vn_ref[0, :, c]], axis=0))

    def stacked_queries(chunk):
        q = q_ref[0, :, chunk * LANES:(chunk + 1) * LANES]
        zero = jnp.zeros_like(q)
        return jnp.concatenate([jnp.where(first, q, zero), jnp.where(first, zero, q)], axis=0)

    def logits(chunk):
        return _dot(stacked_queries(chunk), kts[chunk // (group // 2)])

    n_chunks = WIN_Q_HEADS // 2

    def run(bounded):
        if bounded:
            k_norm = []
            for kt in kts:
                kf = kt[:WIN_HEAD_DIM].astype(F32)
                k_norm.append(jnp.sqrt(jnp.max(jnp.sum(kf * kf, axis=0, keepdims=True), axis=1, keepdims=True)))
        s_next = logits(0)
        floor = None
        for chunk in range(n_chunks):
            s = s_next + bias
            if chunk + 1 < n_chunks:
                s_next = logits(chunk + 1)
            sink = jnp.where(second_head, sink_ref[2 * chunk + 1], sink_ref[2 * chunk]) * LOG2E
            if bounded:
                qf = stacked_queries(chunk).astype(F32)
                m = jnp.sqrt(jnp.sum(qf * qf, axis=-1, keepdims=True)) * k_norm[chunk // (group // 2)]
            else:
                m = jnp.max(s, axis=-1, keepdims=True)
            m = jnp.maximum(m, sink)
            p = jnp.exp2(s - m)
            l = jnp.sum(p, axis=-1, keepdims=True) + jnp.exp2(sink - m)
            floor = l if floor is None else jnp.minimum(floor, l)
            o = _dot(p.astype(BF16), vs[chunk // (group // 2)]) / l
            o_ref[0, :, chunk * LANES:(chunk + 1) * LANES] = jnp.where(first, o[:tq], o[tq:]).astype(o_ref.dtype)
        return jnp.min(floor)

    smallest = run(bounded=True)

    @pl.when(smallest < SOFTMAX_FLOOR)
    def _():
        run(bounded=False)


def _window_attention(q, kdt, vd, kct, vc, sink):
    b, n_lat, _ = q.shape
    tq = ROW_TILE
    n_ctx = vc.shape[1]
    per = tq // WINDOW
    last = n_lat // WINDOW - 1
    prev_i = lambda i: jnp.maximum(i * per - 1, 0)
    next_i = lambda i: jnp.minimum((i + 1) * per, last)
    w = 2 * LANES
    return pl.pallas_call(
        functools.partial(_window_kernel, n_lat=n_lat),
        grid=(b, n_lat // tq),
        in_specs=[
            pl.BlockSpec(memory_space=pltpu.SMEM),
            pl.BlockSpec((1, tq, 4 * LANES), lambda bi, i: (bi, i, 0)),
            pl.BlockSpec((1, w, n_ctx), lambda bi, i: (bi, 0, 0)),
            pl.BlockSpec((1, n_ctx, w), lambda bi, i: (bi, 0, 0)),
            pl.BlockSpec((1, w, WINDOW), lambda bi, i: (bi, 0, prev_i(i))),
            pl.BlockSpec((1, w, tq), lambda bi, i: (bi, 0, i)),
            pl.BlockSpec((1, w, WINDOW), lambda bi, i: (bi, 0, next_i(i))),
            pl.BlockSpec((1, WINDOW, w), lambda bi, i: (bi, prev_i(i), 0)),
            pl.BlockSpec((1, tq, w), lambda bi, i: (bi, i, 0)),
            pl.BlockSpec((1, WINDOW, w), lambda bi, i: (bi, next_i(i), 0)),
        ],
        out_specs=pl.BlockSpec((1, tq, 4 * LANES), lambda bi, i: (bi, i, 0)),
        out_shape=jax.ShapeDtypeStruct((b, n_lat, 4 * LANES), BF16),
        compiler_params=_params("parallel", "arbitrary"),
        name="window_attention",
    )(sink, q, kct, vc, kdt, kdt, kdt, vd, vd, vd)


def _hyena_filter_kernel(feat_ref, w1_ref, b1_ref, w2_ref, b2_ref, w3_ref, b3_ref, freq_ref, decay_ref, h_ref,
                         hid_ref, *, n_forward):
    j = pl.program_id(0)
    hp = lax.Precision.HIGHEST

    @pl.when(j == 0)
    def _():
        freq = freq_ref[...]
        hid = jnp.sin(freq * (jnp.dot(feat_ref[...], w1_ref[...], precision=hp, preferred_element_type=F32) + b1_ref[...]))
        hid_ref[...] = jnp.sin(freq * (jnp.dot(hid, w2_ref[...], precision=hp, preferred_element_type=F32) + b2_ref[...]))

    h = jnp.dot(hid_ref[...], w3_ref[...], precision=hp, preferred_element_type=F32) + b3_ref[...]
    h = h * decay_ref[...]
    row = lax.broadcasted_iota(jnp.int32, h.shape, 0)
    h_ref[...] = jnp.where((row == 0) & (j >= n_forward), 0.0, h).astype(h_ref.dtype)


def _hyena_filters(feats, w1, b1, w2, b2, w3, b3, freq, decay):
    length = feats.shape[0]
    n = w3.shape[1]
    ch = decay.shape[1]
    full = lambda a: pl.BlockSpec(a.shape, lambda j: (0, 0))
    return pl.pallas_call(
        functools.partial(_hyena_filter_kernel, n_forward=HYENA_ORDER),
        grid=(n // ch,),
        in_specs=[full(feats), full(w1), full(b1), full(w2), full(b2),
                  pl.BlockSpec((w3.shape[0], ch), lambda j: (0, j)), pl.BlockSpec((1, ch), lambda j: (0, j)),
                  full(freq), full(decay)],
        out_specs=pl.BlockSpec((length, ch), lambda j: (0, j)),
        out_shape=jax.ShapeDtypeStruct((length, n), BF16),
        scratch_shapes=[pltpu.VMEM((length, w2.shape[1]), F32)],
        compiler_params=_params("arbitrary"),
        name="hyena_filters",
    )(feats, w1, b1, w2, b2, w3, b3, freq, decay)


def _spectrum_kernel(f_ref, h_ref, re_ref, im_ref):
    j = pl.program_id(0)
    tk = re_ref.shape[0]
    n = re_ref.shape[1]
    a = _dot(f_ref[0], h_ref[...])
    re_ref[...] = a[:tk, :n] + a[:tk, n:]
    row = lax.broadcasted_iota(jnp.int32, (tk, n), 0)
    sign = jnp.where((row == 0) & (j == 0), 1.0, -1.0)
    im_ref[...] = a[tk:, :n] + sign * a[tk:, n:]


def _filter_spectrum(f3, hcat):
    nt, tk2, length = f3.shape
    n = hcat.shape[1] // 2
    tk = tk2 // 2
    out = pl.BlockSpec((tk, n), lambda j: (j, 0))
    return pl.pallas_call(
        _spectrum_kernel,
        grid=(nt,),
        in_specs=[pl.BlockSpec((1, tk2, length), lambda j: (j, 0, 0)),
                  pl.BlockSpec(hcat.shape, lambda j: (0, 0))],
        out_specs=[out, out],
        out_shape=[jax.ShapeDtypeStruct((nt * tk, n), F32)] * 2,
        compiler_params=_params("arbitrary"),
        name="filter_spectrum",
    )(f3, hcat)


def _long_conv_kernel(z_ref, x_ref, f_ref, g_ref, hre_ref, him_ref, bias_ref, o_ref, acc_ref):
    j = pl.program_id(1)
    tk = hre_ref.shape[0]
    row = lax.broadcasted_iota(jnp.int32, hre_ref.shape, 0)
    packed = (row == 0) & (j == 0)
    bias = bias_ref[...]
    h_re = hre_ref[...] + bias
    h_im = him_ref[...] + jnp.where(packed, bias, 0.0)

    @pl.when(j == 0)
    def _():
        acc_ref[...] = jnp.zeros_like(acc_ref)

    spectra = [_dot(f_ref[0], z_ref[s]) for s in range(z_ref.shape[0])]
    for s, a in enumerate(spectra):
        a_re, a_im = a[:tk], a[tk:]
        ii = a_im * h_im
        p_re = a_re * h_re - jnp.where(packed, 0.0, ii)
        p_im = jnp.where(packed, ii, a_re * h_im + a_im * h_re)
        acc_ref[s] += _dot(g_ref[0], jnp.concatenate([p_re, p_im], axis=0).astype(BF16))

    @pl.when(j == pl.num_programs(1) - 1)
    def _():
        o_ref[...] = (x_ref[...].astype(F32) * acc_ref[...]).astype(o_ref.dtype)


def _long_conv(z_src, z_col, x_src, x_col, f3, g3, hre, him, h_col, bias):
    b, length, _ = z_src.shape
    nt, tk2, _ = f3.shape
    tk = tk2 // 2
    ch = HYENA_CH
    nb = CONV_SAMPLES
    return pl.pallas_call(
        _long_conv_kernel,
        grid=(b // nb, nt),
        in_specs=[
            pl.BlockSpec((nb, length, ch), lambda bi, j: (bi, 0, z_col)),
            pl.BlockSpec((nb, length, ch), lambda bi, j: (bi, 0, x_col)),
            pl.BlockSpec((1, tk2, length), lambda bi, j: (j, 0, 0)),
            pl.BlockSpec((1, length, tk2), lambda bi, j: (j, 0, 0)),
            pl.BlockSpec((tk, ch), lambda bi, j: (j, h_col)),
            pl.BlockSpec((tk, ch), lambda bi, j: (j, h_col)),
            pl.BlockSpec((1, ch), lambda bi, j: (0, 0)),
        ],
        out_specs=pl.BlockSpec((nb, length, ch), lambda bi, j: (bi, 0, 0)),
        out_shape=jax.ShapeDtypeStruct((b, length, ch), BF16),
        scratch_shapes=[pltpu.VMEM((nb, length, ch), F32)],
        compiler_params=_params("parallel", "arbitrary"),
        name="long_conv",
    )(z_src, x_src, f3, g3, hre, him, bias)


def _axial_rope(length, rot_dim):
    rows = length // GRID_W
    row = jnp.repeat(jnp.arange(rows), GRID_W).astype(F32)
    col = jnp.tile(jnp.arange(GRID_W), rows).astype(F32)
    quarter = rot_dim // 4
    inv = ROPE_BASE ** (-jnp.arange(quarter, dtype=F32) / quarter)
    ang = jnp.concatenate([row[:, None] * inv, col[:, None] * inv], axis=-1)
    return jnp.cos(ang), jnp.sin(ang)


def _rope_tables(length, n_ctx):
    cos_m, sin_m = _axial_rope(length, MLA_ROPE)
    cos_d, sin_d = _axial_rope(length, DIFF_HEAD_DIM)
    ones = lambda w: jnp.ones((length, w), F32)
    zeros = lambda w: jnp.zeros((length, w), F32)
    cm = jnp.concatenate([ones(MLA_NOPE), cos_m, cos_m, ones(LANES - MLA_NOPE - MLA_ROPE)], axis=1)
    sm = jnp.concatenate([zeros(MLA_NOPE), -sin_m, sin_m, zeros(LANES - MLA_NOPE - MLA_ROPE)], axis=1)
    cd = jnp.concatenate([cos_d] * 4, axis=1)
    sd = jnp.concatenate([-sin_d, sin_d] * 2, axis=1)
    pad = lambda t, v: jnp.concatenate([t, jnp.full((n_ctx, LANES), v, F32)], axis=0)
    return pad(cm, 1.0), pad(sm, 0.0), pad(cd, 1.0), pad(sd, 0.0)


def _dft_matrices(length):
    n = 2 * length
    k = np.arange(length, dtype=np.int64)
    ang = 2.0 * np.pi * ((k[:, None] * k[None, :]) % n).astype(np.float64) / n
    alt = np.where(k % 2 == 0, 1.0, -1.0)
    f_re = np.cos(ang)
    f_im = -np.sin(ang)
    f_im[0, :] = alt
    g_re = (2.0 / n) * np.cos(ang.T)
    g_im = -(2.0 / n) * np.sin(ang.T)
    g_re[:, 0] = 1.0 / n
    g_im[:, 0] = alt / n
    nt = length // FREQ_TILE
    f3 = np.concatenate([f_re.reshape(nt, FREQ_TILE, length), f_im.reshape(nt, FREQ_TILE, length)], axis=1)
    g3 = np.concatenate([g_re.reshape(length, nt, FREQ_TILE), g_im.reshape(length, nt, FREQ_TILE)], axis=2)
    g3 = np.transpose(g3, (1, 0, 2))
    return jnp.asarray(f3, dtype=BF16), jnp.asarray(g3, dtype=BF16)


def _hyena_features(length):
    t = jnp.arange(length, dtype=F32)
    tn = t / max(length - 1, 1)
    bands = jnp.linspace(1e-4, HYENA_BANDS - 1, HYENA_BANDS, dtype=F32)
    ang = 2.0 * math.pi * bands[None, :] * t[:, None] / length
    feats = jnp.concatenate([tn[:, None], jnp.cos(ang), -jnp.sin(ang)], axis=-1)
    min_decay = math.log(HYENA_DECAY_TARGET) / HYENA_SLOW_PCT
    max_decay = math.log(HYENA_DECAY_TARGET) / HYENA_FAST_PCT
    deltas = jnp.abs(jnp.linspace(min_decay, max_decay, HYENA_CH, dtype=F32))
    decay = jnp.exp(-tn[:, None] * deltas[None, :])
    return feats, decay


def _pad_cols(w, left, right):
    return jnp.pad(w, ((0, 0),) * (w.ndim - 1) + ((left, right),))


def _even_weights(w_in, w_uq, w_ukv):
    d = w_in.shape[0]
    o = MLA_Q_RANK + MLA_KV_RANK
    w_kr = _pad_cols(w_in[:, o:o + MLA_ROPE], MLA_NOPE, LANES - MLA_NOPE - MLA_ROPE)
    w_in_ext = jnp.concatenate([w_in[:, :o], w_kr, w_in[:, o + MLA_ROPE:]], axis=1)
    uq = _pad_cols(w_uq.reshape(MLA_Q_RANK, MLA_HEADS, MLA_NOPE + MLA_ROPE), 0, LANES - MLA_NOPE - MLA_ROPE)
    ukv = w_ukv.reshape(MLA_KV_RANK, MLA_HEADS, MLA_NOPE + MLA_V)
    uk = _pad_cols(ukv[:, :, :MLA_NOPE], 0, LANES - MLA_NOPE)
    uv = ukv[:, :, MLA_NOPE:].reshape(MLA_KV_RANK, MLA_HEADS // 2, 2, MLA_V)
    uv = jnp.stack([_pad_cols(uv[:, :, 0], 0, LANES - MLA_V), _pad_cols(uv[:, :, 1], LANES - MLA_V, 0)], axis=2)
    w_ukv_ext = jnp.concatenate([uk.reshape(MLA_KV_RANK, -1), uv.reshape(MLA_KV_RANK, -1)], axis=1)
    return w_in_ext.astype(BF16), uq.reshape(MLA_Q_RANK, -1).astype(BF16), w_ukv_ext.astype(BF16)


def _odd_weights(w_in):
    d = w_in.shape[0]
    nq = WIN_Q_HEADS * WIN_HEAD_DIM
    nkv = WIN_KV_HEADS * WIN_HEAD_DIM
    dup = lambda w: jnp.concatenate([w.reshape(d, WIN_KV_HEADS, WIN_HEAD_DIM)] * 2, axis=-1).reshape(d, -1)
    wk = dup(w_in[:, nq:nq + nkv])
    wv = dup(w_in[:, nq + nkv:nq + 2 * nkv])
    wqkv = jnp.concatenate([w_in[:, :nq], wk, wv], axis=1).astype(BF16)
    wkv = jnp.concatenate([wk, wv], axis=1).astype(BF16)
    wu = w_in[:, nq + 2 * nkv:].astype(BF16)
    return wqkv, wkv, wu


def _ffn_weights(w_up, conv_w, conv_b, w_down):
    d = w_up.shape[0]
    nc = D_FF // FFN_CHUNK
    chunks = lambda w: jnp.transpose(w.reshape(w.shape[0], nc, FFN_CHUNK), (1, 0, 2))
    cw = jnp.concatenate([conv_w, conv_b[None, :]], axis=0)
    return (chunks(w_up[:, :D_FF]).astype(BF16), chunks(w_up[:, D_FF:]).astype(BF16),
            chunks(cw[:, :D_FF]), chunks(cw[:, D_FF:]),
            w_down.astype(BF16))


def kernel(x, c, ctx, c_ctx, ada_w, ada_b, norm_g, mix_w_out, ffn_w_up, ffn_conv_w, ffn_conv_b, ffn_w_down, even_w_in, mla_q_norm_g, mla_w_uq, mla_kv_norm_g, mla_w_ukv, diff_lambda, diff_subln_g, odd_w_in, win_sink, hy_conv_w, hy_conv_b, hy_f_w1, hy_f_b1, hy_f_w2, hy_f_b2, hy_f_w3, hy_f_b3, hy_f_freq, hy_bias):
    b, seq, d = x.shape
    n_ctx = ctx.shape[1]
    depth = ada_w.shape[0]
    assert depth == 2 and d == D_MODEL and seq % ROW_TILE == 0 and n_ctx == ROW_TILE
    n_lat_tiles = seq // ROW_TILE
    n_tiles = n_lat_tiles + 1

    pad_rows = (-(b + 1)) % 8
    cond = jnp.concatenate([c, c_ctx[None, :], jnp.zeros((pad_rows, d), F32)], axis=0)
    mod = _ada_table(cond, ada_w, ada_b).reshape(depth, b + 1 + pad_rows, 6, d)[:, :b + 1]
    g = norm_g.reshape(depth, 4, 1, d)
    cm, sm, cd, sd = _rope_tables(seq, n_ctx)

    lambda_init = 0.8 - 0.6 * math.exp(-0.3 * 0)
    w_in_e, w_uq_e, w_ukv_e = _even_weights(even_w_in[0], mla_w_uq[0], mla_w_ukv[0])
    rows = seq + n_ctx
    even_w = (g[0, 0], w_in_e, mla_q_norm_g[0][None, :], w_uq_e, mla_kv_norm_g[0][None, :], w_ukv_e)
    proj = _proj_even(x, mod[0], even_w, (cm, sm, cd, sd), rows, PROJ_ROW_TILE, 0, False)
    qm, km, vm, dq, dk, dv = _proj_even(ctx, mod[0], even_w, (cm, sm, cd, sd), rows, n_ctx, seq, True, prior=proj)
    lam_args = (diff_lambda[0], diff_subln_g[0][None, :], lambda_init)
    w_out0 = mix_w_out[0].astype(BF16)
    ffn0 = _ffn_weights(ffn_w_up[0], ffn_conv_w[0], ffn_conv_b[0], ffn_w_down[0])
    o_mla, o_mla_ctx = _mla_attention(qm, km, vm, seq)
    o_diff, o_diff_ctx = _diff_attention(dq, dk, dv, *lam_args, seq)
    x_lat = _mix_ffn(o_mla, o_diff, x, mod[0], w_out0, g[0, 1], g[0, 2], *ffn0, g[0, 3], FFN_ROW_TILE, False)
    x_ctx = _mix_ffn(o_mla_ctx, o_diff_ctx, ctx, mod[0], w_out0, g[0, 1], g[0, 2], *ffn0, g[0, 3], n_ctx, True)

    wqkv, wkv, wu = _odd_weights(odd_w_in[0])
    cw_hy = jnp.concatenate([hy_conv_w[0], hy_conv_b[0][None, :]], axis=0)
    q, kd, vd, u = _proj_odd(x_lat, mod[1], g[1, 0], wqkv, wu, cw_hy, cd, sd, PROJ_ROW_TILE)
    kc, vc = _proj_ctx_kv(x_ctx, mod[1], g[1, 0], wkv, 0, n_ctx)
    o_win = _window_attention(q, kd, vd, kc, vc, win_sink[0])

    feats, decay = _hyena_features(seq)
    emb = feats.shape[1]
    feats = _pad_cols(feats, 0, LANES - emb)
    w1 = jnp.pad(hy_f_w1[0], ((0, LANES - emb), (0, 0)))
    hcat = _hyena_filters(feats, w1, hy_f_b1[0][None, :], hy_f_w2[0], hy_f_b2[0][None, :], hy_f_w3[0],
                          hy_f_b3[0][None, :], hy_f_freq[0][None, :], decay)
    f3, g3 = _dft_matrices(seq)
    hre, him = _filter_spectrum(f3, hcat)
    z = _long_conv(u, 0, u, 1, f3, g3, hre, him, 0, hy_bias[0, 0][None, :])
    z = _long_conv(z, 0, u, 2, f3, g3, hre, him, 1, hy_bias[0, 1][None, :])

    w_out1 = mix_w_out[1].astype(BF16)
    ffn1 = _ffn_weights(ffn_w_up[1], ffn_conv_w[1], ffn_conv_b[1], ffn_w_down[1])
    return _mix_ffn(o_win, z, x_lat, mod[1], w_out1, g[1, 1], g[1, 2], *ffn1, g[1, 3], FFN_ROW_TILE, False)
```
